```python
import jax, jax.numpy as jnp
from jax import lax
import numpy as np

D_MODEL = 2048
BATCH = 4
SEQ = 2048
DEPTH = 1
DEC_BATCH = 128
DEC_SEQ = 8
PAST_LEN = 16384
PAGE_SIZE = 128

POOL_WIDTH = D_MODEL // 2
POOL_WINDOWS = (2, 4, 8, 16)
POOL_GROUPS = len(POOL_WINDOWS)
POOL_GROUP_DIM = POOL_WIDTH // POOL_GROUPS
POOL_BUF = max(POOL_WINDOWS) - 1
LRU_WIDTH = D_MODEL
LRU_BLOCKS = 8
LRU_BLOCK_DIM = LRU_WIDTH // LRU_BLOCKS
LRU_CONV = 4
LRU_C = 8.0
D_FF = 3 * D_MODEL
FFN_CONV = 3
IN_WIDTH = POOL_WIDTH + LRU_WIDTH + 2 * D_MODEL
N_ADA = 6
EPS = 1e-6

kernel_name = 'hybrid_pool_rglru_convffn_step'


def rmsnorm(x, g):
    xf = x.astype(jnp.float32)
    y = xf * lax.rsqrt(jnp.mean(xf * xf, axis=-1, keepdims=True) + EPS)
    return (y * g.astype(jnp.float32)).astype(x.dtype)


def causal_dwconv(buf, u, w, b):
    K = w.shape[0]
    T = u.shape[1]
    ext = jnp.concatenate([buf.astype(u.dtype), u], axis=1)
    out = ext[:, 0:T] * w[0]
    for k in range(1, K):
        out = out + ext[:, k:k + T] * w[k]
    return out + b, ext[:, -(K - 1):]


def pool_mix(buf, u, start, w_grp, scale):
    B, T, P = u.shape
    ext = jnp.concatenate([buf.astype(u.dtype), u], axis=1)
    cs = jnp.cumsum(ext.astype(jnp.float32), axis=1)
    cs = jnp.pad(cs, ((0, 0), (1, 0), (0, 0)))
    hi = cs[:, POOL_BUF + 1:]
    pos = start + jnp.arange(T, dtype=jnp.int32)
    means = []
    for k, w in enumerate(POOL_WINDOWS):
        sl = slice(k * POOL_GROUP_DIM, (k + 1) * POOL_GROUP_DIM)
        lo = cs[:, POOL_BUF + 1 - w:POOL_BUF + 1 - w + T, sl]
        cnt = jnp.minimum(w, pos + 1).astype(jnp.float32)[None, :, None]
        means.append((hi[..., sl] - lo) / cnt)
    mean = jnp.concatenate(means, axis=-1).astype(u.dtype)
    d = (mean - u).reshape(B, T, POOL_GROUPS, POOL_GROUP_DIM)
    y = jnp.einsum('btgc,gcd->btgd', d, w_grp).reshape(B, T, P) * scale
    return y, ext[:, -POOL_BUF:]


def rglru(h0, xc, w_rg, b_rg, w_ig, b_ig, lam):
    B, T, R = xc.shape
    xb = xc.reshape(B, T, LRU_BLOCKS, LRU_BLOCK_DIM)
    r = jax.nn.sigmoid((jnp.einsum('btnc,ncd->btnd', xb, w_rg).reshape(B, T, R) + b_rg).astype(jnp.float32))
    i = jax.nn.sigmoid((jnp.einsum('btnc,ncd->btnd', xb, w_ig).reshape(B, T, R) + b_ig).astype(jnp.float32))
    log_a = -LRU_C * r * jax.nn.softplus(-lam.astype(jnp.float32))
    a = jnp.exp(log_a)
    u = jnp.sqrt(-jnp.expm1(2.0 * log_a)) * (i * xc.astype(jnp.float32))

    def step(h, inp):
        a_t, u_t = inp
        h = a_t * h + u_t
        return h, h

    hT, hs = lax.scan(step, h0.astype(jnp.float32), (jnp.swapaxes(a, 0, 1), jnp.swapaxes(u, 0, 1)))
    return jnp.swapaxes(hs, 0, 1).astype(xc.dtype), hT


def _layer(x, c, pool_buf, lru_buf, lru_h, ffn_buf, start,
           w_ada, b_ada, g_pre1, g_post1, g_pre2, g_post2, w_in, w_pool_grp, pool_scale,
           w_lru_conv, b_lru_conv, w_rg, b_rg, w_ig, b_ig, lru_lambda,
           w_pool_up, w_lru_up, w_out, w_ffn_up, w_ffn_conv, b_ffn_conv, w_ffn_down):
    ada = (jax.nn.silu(c) @ w_ada + b_ada)[:, None, :]
    shift1, scale1, gate1, shift2, scale2, gate2 = jnp.split(ada, N_ADA, axis=-1)

    h = rmsnorm(x, g_pre1) * (1.0 + scale1) + shift1
    z = h @ w_in
    u_pool, u_lru, g_pool, g_lru = jnp.split(
        z, [POOL_WIDTH, POOL_WIDTH + LRU_WIDTH, POOL_WIDTH + LRU_WIDTH + D_MODEL], axis=-1)
    y_pool, new_pool = pool_mix(pool_buf, u_pool, start, w_pool_grp, pool_scale)
    xc, new_lru_buf = causal_dwconv(lru_buf, u_lru, w_lru_conv, b_lru_conv)
    y_lru, new_h = rglru(lru_h, xc, w_rg, b_rg, w_ig, b_ig, lru_lambda)
    merged = jax.nn.sigmoid(g_pool) * (y_pool @ w_pool_up) + jax.nn.sigmoid(g_lru) * (y_lru @ w_lru_up)
    x = x + gate1 * rmsnorm(merged @ w_out, g_post1)

    h = rmsnorm(x, g_pre2) * (1.0 + scale2) + shift2
    up = h @ w_ffn_up
    upc, new_ffn_buf = causal_dwconv(ffn_buf, up, w_ffn_conv, b_ffn_conv)
    gt, val = jnp.split(upc, 2, axis=-1)
    f = jax.nn.gelu(gt, approximate=True) * val
    x = x + gate2 * rmsnorm(f @ w_ffn_down, g_post2)
    return x, new_pool, new_lru_buf, new_h, new_ffn_buf


def setup_inputs(seed: int = 0) -> dict:
    key = jax.random.key(seed)
    ks = iter(jax.random.split(key, 40))
    f32 = jnp.float32

    def nrm(shape, scale):
        return jax.random.normal(next(ks), shape, f32) * scale

    def gain(shape):
        return 1.0 + 0.05 * jax.random.normal(next(ks), shape, f32)

    a0 = jax.random.uniform(next(ks), (DEPTH, LRU_WIDTH), f32, 0.9, 0.999)
    p = a0 ** (1.0 / LRU_C)
    lru_lambda = jnp.log(p) - jnp.log1p(-p)

    return {
        'x_prompt': nrm((BATCH, SEQ, D_MODEL), 1.0),
        'x_sample': nrm((DEC_BATCH, DEC_SEQ, D_MODEL), 1.0),
        'c_prompt': nrm((BATCH, D_MODEL), 1.0),
        'c_sample': nrm((DEC_BATCH, D_MODEL), 1.0),
        'state_pool': nrm((DEPTH, DEC_BATCH, POOL_BUF, POOL_WIDTH), 1.0),
        'state_lru_conv': nrm((DEPTH, DEC_BATCH, LRU_CONV - 1, LRU_WIDTH), 1.0),
        'state_lru_h': nrm((DEPTH, DEC_BATCH, LRU_WIDTH), 0.5),
        'state_ffn_conv': nrm((DEPTH, DEC_BATCH, FFN_CONV - 1, 2 * D_FF), 1.0),
        'w_ada': nrm((DEPTH, D_MODEL, N_ADA * D_MODEL), 0.5 * D_MODEL ** -0.5),
        'b_ada': nrm((DEPTH, N_ADA * D_MODEL), 0.02),
        'g_pre1': gain((DEPTH, D_MODEL)),
        'g_post1': gain((DEPTH, D_MODEL)),
        'g_pre2': gain((DEPTH, D_MODEL)),
        'g_post2': gain((DEPTH, D_MODEL)),
        'w_in': nrm((DEPTH, D_MODEL, IN_WIDTH), D_MODEL ** -0.5),
        'w_pool_grp': nrm((DEPTH, POOL_GROUPS, POOL_GROUP_DIM, POOL_GROUP_DIM), POOL_GROUP_DIM ** -0.5),
        'pool_scale': gain((DEPTH, POOL_WIDTH)),
        'w_lru_conv': nrm((DEPTH, LRU_CONV, LRU_WIDTH), LRU_CONV ** -0.5),
        'b_lru_conv': nrm((DEPTH, LRU_WIDTH), 0.02),
        'w_rg': nrm((DEPTH, LRU_BLOCKS, LRU_BLOCK_DIM, LRU_BLOCK_DIM), LRU_BLOCK_DIM ** -0.5),
        'b_rg': nrm((DEPTH, LRU_WIDTH), 0.02),
        'w_ig': nrm((DEPTH, LRU_BLOCKS, LRU_BLOCK_DIM, LRU_BLOCK_DIM), LRU_BLOCK_DIM ** -0.5),
        'b_ig': nrm((DEPTH, LRU_WIDTH), 0.02),
        'lru_lambda': lru_lambda,
        'w_pool_up': nrm((DEPTH, POOL_WIDTH, D_MODEL), POOL_WIDTH ** -0.5),
        'w_lru_up': nrm((DEPTH, LRU_WIDTH, D_MODEL), LRU_WIDTH ** -0.5),
        'w_out': nrm((DEPTH, D_MODEL, D_MODEL), D_MODEL ** -0.5),
        'w_ffn_up': nrm((DEPTH, D_MODEL, 2 * D_FF), D_MODEL ** -0.5),
        'w_ffn_conv': nrm((DEPTH, FFN_CONV, 2 * D_FF), FFN_CONV ** -0.5),
        'b_ffn_conv': nrm((DEPTH, 2 * D_FF), 0.02),
        'w_ffn_down': nrm((DEPTH, D_FF, D_MODEL), D_FF ** -0.5),
    }


def reference(x_prompt, x_sample, c_prompt, c_sample, state_pool, state_lru_conv, state_lru_h, state_ffn_conv,
              w_ada, b_ada, g_pre1, g_post1, g_pre2, g_post2, w_in, w_pool_grp, pool_scale,
              w_lru_conv, b_lru_conv, w_rg, b_rg, w_ig, b_ig, lru_lambda,
              w_pool_up, w_lru_up, w_out, w_ffn_up, w_ffn_conv, b_ffn_conv, w_ffn_down):
    weights = (w_ada, b_ada, g_pre1, g_post1, g_pre2, g_post2, w_in, w_pool_grp, pool_scale,
               w_lru_conv, b_lru_conv, w_rg, b_rg, w_ig, b_ig, lru_lambda,
               w_pool_up, w_lru_up, w_out, w_ffn_up, w_ffn_conv, b_ffn_conv, w_ffn_down)
    dt = x_prompt.dtype
    yp, ys = x_prompt, x_sample
    pp, plc, plh, pfc = [], [], [], []
    sp, slc, slh, sfc = [], [], [], []
    for l in range(DEPTH):
        params = [w[l] for w in weights]
        yp, a1, a2, a3, a4 = _layer(
            yp, c_prompt,
            jnp.zeros((BATCH, POOL_BUF, POOL_WIDTH), dt),
            jnp.zeros((BATCH, LRU_CONV - 1, LRU_WIDTH), dt),
            jnp.zeros((BATCH, LRU_WIDTH), jnp.float32),
            jnp.zeros((BATCH, FFN_CONV - 1, 2 * D_FF), dt),
            0, *params)
        pp.append(a1); plc.append(a2); plh.append(a3); pfc.append(a4)
        ys, b1, b2, b3, b4 = _layer(
            ys, c_sample, state_pool[l], state_lru_conv[l], state_lru_h[l], state_ffn_conv[l],
            PAST_LEN, *params)
        sp.append(b1); slc.append(b2); slh.append(b3); sfc.append(b4)
    return (yp, ys,
            jnp.stack(pp), jnp.stack(plc), jnp.stack(plh), jnp.stack(pfc),
            jnp.stack(sp), jnp.stack(slc), jnp.stack(slh), jnp.stack(sfc))
```

```python
import functools

import jax
import jax.numpy as jnp
from jax import lax
from jax.experimental import pallas as pl
from jax.experimental.pallas import tpu as pltpu

F32 = jnp.float32
BF16 = jnp.bfloat16

D_MODEL = 2048
POOL_WINDOWS = (2, 4, 8, 16)
POOL_GROUPS = len(POOL_WINDOWS)
POOL_BUF = max(POOL_WINDOWS) - 1
LRU_CONV = 4
LRU_C = 8.0
PAST_LEN = 16384
FFN_CONV = 3
N_ADA = 6
EPS = 1e-6

CB = 256
POOL_WIDTH = POOL_GROUPS * CB
LRU_BLOCKS = 8
LRU_WIDTH = LRU_BLOCKS * CB
N_MIX_BLOCKS = POOL_GROUPS + LRU_BLOCKS
HALO = 16
SUBLANES = 8
VMEM_LIMIT = 60 * 1024 * 1024


def _cparams(n_axes):
    return pltpu.CompilerParams(
        dimension_semantics=("arbitrary",) * n_axes, vmem_limit_bytes=VMEM_LIMIT)


def _dot(a, b):
    return jnp.dot(a, b, preferred_element_type=F32)


def _rms(x, g):
    ms = jnp.mean(x * x, axis=-1, keepdims=True)
    return (x * lax.rsqrt(ms + EPS)) * g


def _softplus(z):
    return jnp.maximum(z, 0.0) + jnp.log1p(jnp.exp(-jnp.abs(z)))


def _lru_coeffs(xc, r_pre, i_pre, neg_c_sp):
    r = jax.nn.sigmoid(r_pre)
    i = jax.nn.sigmoid(i_pre)
    log_a = r * neg_c_sp
    a = jnp.exp(log_a)
    b = jnp.sqrt(-jnp.tanh(log_a) * (a * a + 1.0)) * (i * xc)
    return a, b


def _ada_kernel(c_ref, w_ref, b_ref, o_ref):
    c = c_ref[...]
    s = (c * jax.nn.sigmoid(c)).astype(BF16)
    o_ref[...] = _dot(s, w_ref[...].astype(BF16)) + b_ref[...]


def _ada(c_all, w_ada, b_ada):
    rows = c_all.shape[0]
    tn = 1024
    return pl.pallas_call(
        _ada_kernel,
        grid=(N_ADA * D_MODEL // tn,),
        in_specs=[pl.BlockSpec((rows, D_MODEL), lambda n: (0, 0)),
                  pl.BlockSpec((D_MODEL, tn), lambda n: (0, n)),
                  pl.BlockSpec((1, tn), lambda n: (0, n))],
        out_specs=pl.BlockSpec((rows, tn), lambda n: (0, n)),
        out_shape=jax.ShapeDtypeStruct((rows, N_ADA * D_MODEL), F32),
        compiler_params=_cparams(1),
        name="ada",
    )(c_all, w_ada, b_ada)


NORM_ROWS = 32


def _build_h_prompt(x_ref, h_ref, g, scale1p, shift, n_rows):
    def body(i, carry):
        r0 = pl.multiple_of(i * NORM_ROWS, NORM_ROWS)
        x = x_ref[pl.ds(r0, NORM_ROWS), :]
        h_ref[pl.ds(r0, NORM_ROWS), :] = (_rms(x, g) * scale1p + shift).astype(BF16)
        return carry
    lax.fori_loop(0, n_rows // NORM_ROWS, body, 0)


def _build_h_sample(x_ref, h_ref, g, sc_ref, sh_ref, n_seq, n_t):
    for s0 in range(0, n_seq, NORM_ROWS):
        scale1p = 1.0 + sc_ref[s0:s0 + NORM_ROWS, :]
        shift = sh_ref[s0:s0 + NORM_ROWS, :]
        for t in range(n_t):
            sl = slice(t * n_seq + s0, t * n_seq + s0 + NORM_ROWS)
            h_ref[sl, :] = (_rms(x_ref[sl, :], g) * scale1p + shift).astype(BF16)


MIX_ROWS = 128


def _mix_prompt_kernel(x_ref, sh_ref, sc_ref, g_ref, win_ref, wgrp_ref, pscale_ref,
                       wconv_ref, bconv_ref, wrg_ref, brg_ref, wig_ref, big_ref, lam_ref,
                       y_ref, npool_ref, nconv_ref, nh_ref,
                       h_ref, ext_ref, xb_ref, r_ref, i_ref, a_ref, u_ref, *, seq):
    q = pl.program_id(0)
    j = pl.program_id(1)

    @pl.when(j == 0)
    def _():
        _build_h_prompt(x_ref, h_ref, g_ref[...], 1.0 + sc_ref[pl.ds(q, 1), :],
                        sh_ref[pl.ds(q, 1), :], seq)
        ext_ref[0:HALO, :] = jnp.zeros((HALO, CB), F32)

    ext_ref[HALO:HALO + seq, :] = _dot(h_ref[...], win_ref[...].astype(BF16))

    def pool_branch(w):
        npool_ref[0] = ext_ref[pl.ds(HALO + seq - POOL_BUF, POOL_BUF), :]
        for r0 in range(0, seq, MIX_ROWS):
            u = ext_ref[HALO + r0:HALO + r0 + MIX_ROWS, :]
            s = u
            for k in range(1, w):
                s = s + ext_ref[HALO + r0 - k:HALO + r0 - k + MIX_ROWS, :]
            if r0 < w:
                pos = r0 + lax.broadcasted_iota(jnp.int32, (MIX_ROWS, 1), 0)
                cnt = jnp.minimum(w, pos + 1).astype(F32)
            else:
                cnt = float(w)
            xb_ref[r0:r0 + MIX_ROWS, :] = (s / cnt - u).astype(BF16)
        y = _dot(xb_ref[...], wgrp_ref[0].astype(BF16)) * pscale_ref[...]
        y_ref[...] = y.astype(BF16)

    for g, w in enumerate(POOL_WINDOWS):
        pl.when(j == g)(functools.partial(pool_branch, w))

    @pl.when(j >= POOL_GROUPS)
    def _():
        nconv_ref[0] = ext_ref[pl.ds(HALO + seq - (LRU_CONV - 1), LRU_CONV - 1), :]
        wc = wconv_ref[...]
        bc = bconv_ref[...]
        for r0 in range(0, seq, MIX_ROWS):
            xc = bc
            for k in range(LRU_CONV):
                o = HALO + r0 - (LRU_CONV - 1) + k
                xc = xc + wc[k:k + 1, :] * ext_ref[o:o + MIX_ROWS, :]
            a_ref[r0:r0 + MIX_ROWS, :] = xc
            xb_ref[r0:r0 + MIX_ROWS, :] = xc.astype(BF16)
        r_ref[...] = _dot(xb_ref[...], wrg_ref[0].astype(BF16))
        i_ref[...] = _dot(xb_ref[...], wig_ref[0].astype(BF16))
        neg_c_sp = (-LRU_C) * _softplus(-lam_ref[...])
        brg = brg_ref[...]
        big = big_ref[...]
        nb = MIX_ROWS // SUBLANES
        row = lax.broadcasted_iota(jnp.int32, (nb, SUBLANES, CB), 1)
        for r0 in range(0, seq, MIX_ROWS):
            sl = slice(r0, r0 + MIX_ROWS)
            a, b = _lru_coeffs(a_ref[sl, :], r_ref[sl, :] + brg, i_ref[sl, :] + big, neg_c_sp)
            a = a.reshape(nb, SUBLANES, CB)
            b = b.reshape(nb, SUBLANES, CB)
            for k in (1, 2, 4):
                a_sh = jnp.where(row >= k, pltpu.roll(a, k, 1), 1.0)
                b_sh = jnp.where(row >= k, pltpu.roll(b, k, 1), 0.0)
                b = b + a * b_sh
                a = a * a_sh
            a_ref[sl, :] = a.reshape(MIX_ROWS, CB)
            u_ref[sl, :] = b.reshape(MIX_ROWS, CB)

        def carry_body(blk, h_prev):
            r0 = pl.multiple_of(blk * SUBLANES, SUBLANES)
            h = a_ref[pl.ds(r0, SUBLANES), :] * h_prev + u_ref[pl.ds(r0, SUBLANES), :]
            u_ref[pl.ds(r0, SUBLANES), :] = h
            return h[SUBLANES - 1:SUBLANES, :]

        h_last = lax.fori_loop(0, seq // SUBLANES, carry_body, jnp.zeros((1, CB), F32), unroll=8)
        nh_ref[0] = h_last
        y_ref[...] = u_ref[...].astype(BF16)


def _mix_sample_kernel(x_ref, sh_ref, sc_ref, g_ref, win_ref, wgrp_ref, pscale_ref,
                       wconv_ref, bconv_ref, wrg_ref, brg_ref, wig_ref, big_ref, lam_ref,
                       spool_ref, sconv_ref, sh0_ref,
                       y_ref, npool_ref, nconv_ref, nh_ref,
                       h_ref, u_ref, d_ref, *, n_seq, n_t, start):
    j = pl.program_id(1)

    @pl.when(j == 0)
    def _():
        _build_h_sample(x_ref, h_ref, g_ref[...], sc_ref, sh_ref, n_seq, n_t)

    u_ref[...] = _dot(h_ref[...], win_ref[...].astype(BF16))

    def u_slab(t):
        return u_ref[t * n_seq:(t + 1) * n_seq, :]

    def pool_branch(w):
        e = [spool_ref[k * n_seq:(k + 1) * n_seq, :] for k in range(POOL_BUF)]
        e += [u_slab(t) for t in range(n_t)]
        for k in range(POOL_BUF):
            npool_ref[k * n_seq:(k + 1) * n_seq, :] = e[n_t + k]
        for t in range(n_t):
            s = e[POOL_BUF + t]
            for k in range(1, w):
                s = s + e[POOL_BUF + t - k]
            cnt = float(min(w, start + t + 1))
            d_ref[t * n_seq:(t + 1) * n_seq, :] = (s / cnt - e[POOL_BUF + t]).astype(BF16)
        y = _dot(d_ref[...], wgrp_ref[0].astype(BF16)) * pscale_ref[...]
        y_ref[...] = y.astype(BF16)

    for g, w in enumerate(POOL_WINDOWS):
        pl.when(j == g)(functools.partial(pool_branch, w))

    @pl.when(j >= POOL_GROUPS)
    def _():
        nk = LRU_CONV - 1
        e = [sconv_ref[k * n_seq:(k + 1) * n_seq, :] for k in range(nk)]
        e += [u_slab(t) for t in range(n_t)]
        for k in range(nk):
            nconv_ref[k * n_seq:(k + 1) * n_seq, :] = e[n_t + k]
        wc = wconv_ref[...]
        bc = bconv_ref[...]
        wrg = wrg_ref[0].astype(BF16)
        wig = wig_ref[0].astype(BF16)
        neg_c_sp = (-LRU_C) * _softplus(-lam_ref[...])
        h = sh0_ref[...]
        for t in range(n_t):
            xc = bc
            for k in range(LRU_CONV):
                xc = xc + wc[k:k + 1, :] * e[t + k]
            xb = xc.astype(BF16)
            a, b = _lru_coeffs(xc, _dot(xb, wrg) + brg_ref[...], _dot(xb, wig) + big_ref[...],
                               neg_c_sp)
            h = a * h + b
            y_ref[t * n_seq:(t + 1) * n_seq, :] = h.astype(BF16)
        nh_ref[...] = h


def _mix_weight_specs():
    pj = lambda j: jnp.minimum(j, POOL_GROUPS - 1)
    lj = lambda j: jnp.maximum(j - POOL_GROUPS, 0)
    return [
        pl.BlockSpec((1, D_MODEL), lambda q, j: (0, 0)),
        pl.BlockSpec((D_MODEL, CB), lambda q, j: (0, j)),
        pl.BlockSpec((1, CB, CB), lambda q, j: (pj(j), 0, 0)),
        pl.BlockSpec((1, CB), lambda q, j: (0, pj(j))),
        pl.BlockSpec((LRU_CONV, CB), lambda q, j: (0, lj(j))),
        pl.BlockSpec((1, CB), lambda q, j: (0, lj(j))),
        pl.BlockSpec((1, CB, CB), lambda q, j: (lj(j), 0, 0)),
        pl.BlockSpec((1, CB), lambda q, j: (0, lj(j))),
        pl.BlockSpec((1, CB, CB), lambda q, j: (lj(j), 0, 0)),
        pl.BlockSpec((1, CB), lambda q, j: (0, lj(j))),
        pl.BlockSpec((1, CB), lambda q, j: (0, lj(j))),
    ], pj, lj


def _mix_prompt(x2, ada, p, n_seq, seq, ada_row_block):
    wspecs, pj, lj = _mix_weight_specs()
    in_specs = [
        pl.BlockSpec((seq, D_MODEL), lambda q, j: (q, 0), pipeline_mode=pl.Buffered(1)),
        pl.BlockSpec((SUBLANES, D_MODEL), lambda q, j: (ada_row_block, 0)),
        pl.BlockSpec((SUBLANES, D_MODEL), lambda q, j: (ada_row_block, 1)),
    ] + wspecs
    out_specs = [
        pl.BlockSpec((seq, CB), lambda q, j: (q, j)),
        pl.BlockSpec((1, POOL_BUF, CB), lambda q, j: (q, 0, pj(j))),
        pl.BlockSpec((1, LRU_CONV - 1, CB), lambda q, j: (q, 0, lj(j))),
        pl.BlockSpec((1, 1, CB), lambda q, j: (q, 0, lj(j))),
    ]
    out_shape = [
        jax.ShapeDtypeStruct((n_seq * seq, N_MIX_BLOCKS * CB), BF16),
        jax.ShapeDtypeStruct((n_seq, POOL_BUF, POOL_WIDTH), F32),
        jax.ShapeDtypeStruct((n_seq, LRU_CONV - 1, LRU_WIDTH), F32),
        jax.ShapeDtypeStruct((n_seq, 1, LRU_WIDTH), F32),
    ]
    scratch = [
        pltpu.VMEM((seq, D_MODEL), BF16),
        pltpu.VMEM((HALO + seq, CB), F32),
        pltpu.VMEM((seq, CB), BF16),
        pltpu.VMEM((seq, CB), F32),
        pltpu.VMEM((seq, CB), F32),
        pltpu.VMEM((seq, CB), F32),
        pltpu.VMEM((seq, CB), F32),
    ]
    return pl.pallas_call(
        functools.partial(_mix_prompt_kernel, seq=seq),
        grid=(n_seq, N_MIX_BLOCKS),
        in_specs=in_specs, out_specs=out_specs, out_shape=out_shape,
        scratch_shapes=scratch, compiler_params=_cparams(2), name="mix_prompt",
    )(x2, ada, ada, p["g_pre1"], p["w_in"], p["w_pool_grp"], p["pool_scale"],
      p["w_lru_conv"], p["b_lru_conv"], p["w_rg"], p["b_rg"], p["w_ig"], p["b_ig"],
      p["lru_lambda"])


def _mix_sample(x2, ada, p, spool2, sconv2, sh0, n_seq, n_t, start):
    wspecs, pj, lj = _mix_weight_specs()
    rows = n_seq * n_t
    in_specs = [
        pl.BlockSpec((rows, D_MODEL), lambda q, j: (0, 0), pipeline_mode=pl.Buffered(1)),
        pl.BlockSpec((n_seq, D_MODEL), lambda q, j: (0, 0)),
        pl.BlockSpec((n_seq, D_MODEL), lambda q, j: (0, 1)),
    ] + wspecs + [
        pl.BlockSpec((n_seq * POOL_BUF, CB), lambda q, j: (0, pj(j))),
        pl.BlockSpec((n_seq * (LRU_CONV - 1), CB), lambda q, j: (0, lj(j))),
        pl.BlockSpec((n_seq, CB), lambda q, j: (0, lj(j))),
    ]
    out_specs = [
        pl.BlockSpec((rows, CB), lambda q, j: (0, j)),
        pl.BlockSpec((n_seq * POOL_BUF, CB), lambda q, j: (0, pj(j))),
        pl.BlockSpec((n_seq * (LRU_CONV - 1), CB), lambda q, j: (0, lj(j))),
        pl.BlockSpec((n_seq, CB), lambda q, j: (0, lj(j))),
    ]
    out_shape = [
        jax.ShapeDtypeStruct((rows, N_MIX_BLOCKS * CB), BF16),
        jax.ShapeDtypeStruct((n_seq * POOL_BUF, POOL_WIDTH), F32),
        jax.ShapeDtypeStruct((n_seq * (LRU_CONV - 1), LRU_WIDTH), F32),
        jax.ShapeDtypeStruct((n_seq, LRU_WIDTH), F32),
    ]
    scratch = [
        pltpu.VMEM((rows, D_MODEL), BF16),
        pltpu.VMEM((rows, CB), F32),
        pltpu.VMEM((rows, CB), BF16),
    ]
    return pl.pallas_call(
        functools.partial(_mix_sample_kernel, n_seq=n_seq, n_t=n_t, start=start),
        grid=(1, N_MIX_BLOCKS),
        in_specs=in_specs, out_specs=out_specs, out_shape=out_shape,
        scratch_shapes=scratch, compiler_params=_cparams(2), name="mix_sample",
    )(x2, ada, ada, p["g_pre1"], p["w_in"], p["w_pool_grp"], p["pool_scale"],
      p["w_lru_conv"], p["b_lru_conv"], p["w_rg"], p["b_rg"], p["w_ig"], p["b_ig"],
      p["lru_lambda"], spool2, sconv2, sh0)


MERGE_ROWS = 64


def _merge_kernel(x_ref, sh_ref, sc_ref, g_ref, y_ref, wgp_ref, wgl_ref, wpu_ref, wlu_ref,
                  m_ref, h_ref, gp_ref, gl_ref, pu_ref, lu_ref, *, tm, tiles_per_seq, sample_t):
    i = pl.program_id(0)
    c = pl.program_id(1)

    @pl.when(c == 0)
    def _():
        if sample_t:
            _build_h_sample(x_ref, h_ref, g_ref[...], sc_ref, sh_ref, tm // sample_t, sample_t)
        else:
            q = i // tiles_per_seq
            _build_h_prompt(x_ref, h_ref, g_ref[...], 1.0 + sc_ref[pl.ds(q, 1), :],
                            sh_ref[pl.ds(q, 1), :], tm)

    gp_ref[...] = _dot(h_ref[...], wgp_ref[...].astype(BF16))
    gl_ref[...] = _dot(h_ref[...], wgl_ref[...].astype(BF16))
    pu_ref[...] = _dot(y_ref[:, 0:POOL_WIDTH], wpu_ref[...].astype(BF16))
    lu_ref[...] = _dot(y_ref[:, POOL_WIDTH:POOL_WIDTH + LRU_WIDTH], wlu_ref[...].astype(BF16))
    for r0 in range(0, tm, MERGE_ROWS):
        sl = slice(r0, r0 + MERGE_ROWS)
        m = (jax.nn.sigmoid(gp_ref[sl, :]) * pu_ref[sl, :]
             + jax.nn.sigmoid(gl_ref[sl, :]) * lu_ref[sl, :])
        m_ref[sl, :] = m.astype(BF16)


def _merge(x2, ada, y, p, tm, tiles_per_seq, ada_row_block, sample_t):
    rows = x2.shape[0]
    n_gate0 = N_MIX_BLOCKS
    n_gate1 = N_MIX_BLOCKS + D_MODEL // CB
    if sample_t:
        ada_spec = lambda k: pl.BlockSpec((tm // sample_t, D_MODEL), lambda i, c: (0, k))
    else:
        ada_spec = lambda k: pl.BlockSpec((SUBLANES, D_MODEL), lambda i, c: (ada_row_block, k))
    in_specs = [
        pl.BlockSpec((tm, D_MODEL), lambda i, c: (i, 0), pipeline_mode=pl.Buffered(1)),
        ada_spec(0), ada_spec(1),
        pl.BlockSpec((1, D_MODEL), lambda i, c: (0, 0)),
        pl.BlockSpec((tm, N_MIX_BLOCKS * CB), lambda i, c: (i, 0), pipeline_mode=pl.Buffered(1)),
        pl.BlockSpec((D_MODEL, CB), lambda i, c: (0, n_gate0 + c)),
        pl.BlockSpec((D_MODEL, CB), lambda i, c: (0, n_gate1 + c)),
        pl.BlockSpec((POOL_WIDTH, CB), lambda i, c: (0, c)),
        pl.BlockSpec((LRU_WIDTH, CB), lambda i, c: (0, c)),
    ]
    scratch = [pltpu.VMEM((tm, D_MODEL), BF16)] + [pltpu.VMEM((tm, CB), F32)] * 4
    return pl.pallas_call(
        functools.partial(_merge_kernel, tm=tm, tiles_per_seq=tiles_per_seq, sample_t=sample_t),
        grid=(rows // tm, D_MODEL // CB),
        in_specs=in_specs,
        out_specs=pl.BlockSpec((tm, CB), lambda i, c: (i, c)),
        out_shape=jax.ShapeDtypeStruct((rows, D_MODEL), BF16),
        scratch_shapes=scratch, compiler_params=_cparams(2),
        name="merge_sample" if sample_t else "merge_prompt",
    )(x2, ada, ada, p["g_pre1"], y, p["w_in"], p["w_in"], p["w_pool_up"], p["w_lru_up"])


def _cast_weight(w_ref, wb_ref):
    rows = w_ref.shape[0]
    for r0 in range(0, rows, 256):
        wb_ref[r0:r0 + 256, :] = w_ref[r0:r0 + 256, :].astype(BF16)


def _outproj_kernel(m_ref, x_ref, gate_ref, g_ref, w_ref, o_ref, wb_ref, acc_ref,
                    *, tm, tiles_per_seq, sample_t):
    i = pl.program_id(0)

    @pl.when(i == 0)
    def _():
        _cast_weight(w_ref, wb_ref)

    acc_ref[...] = _dot(m_ref[...], wb_ref[...])
    g = g_ref[...]
    if sample_t:
        n_seq = tm // sample_t
        for s0 in range(0, n_seq, NORM_ROWS):
            gate = gate_ref[s0:s0 + NORM_ROWS, :]
            for t in range(sample_t):
                sl = slice(t * n_seq + s0, t * n_seq + s0 + NORM_ROWS)
                o_ref[sl, :] = x_ref[sl, :] + gate * _rms(acc_ref[sl, :], g)
    else:
        gate = gate_ref[pl.ds(i // tiles_per_seq, 1), :]

        def body(k, carry):
            r0 = pl.multiple_of(k * NORM_ROWS, NORM_ROWS)
            sl = pl.ds(r0, NORM_ROWS)
            o_ref[sl, :] = x_ref[sl, :] + gate * _rms(acc_ref[sl, :], g)
            return carry
        lax.fori_loop(0, tm // NORM_ROWS, body, 0)


def _outproj(m, x2, ada, p, tm, tiles_per_seq, ada_row_block, sample_t):
    rows = x2.shape[0]
    if sample_t:
        gate_spec = pl.BlockSpec((tm // sample_t, D_MODEL), lambda i: (0, 2))
    else:
        gate_spec = pl.BlockSpec((SUBLANES, D_MODEL), lambda i: (ada_row_block, 2))
    return pl.pallas_call(
        functools.partial(_outproj_kernel, tm=tm, tiles_per_seq=tiles_per_seq, sample_t=sample_t),
        grid=(rows // tm,),
        in_specs=[pl.BlockSpec((tm, D_MODEL), lambda i: (i, 0)),
                  pl.BlockSpec((tm, D_MODEL), lambda i: (i, 0)),
                  gate_spec,
                  pl.BlockSpec((1, D_MODEL), lambda i: (0, 0)),
                  pl.BlockSpec((D_MODEL, D_MODEL), lambda i: (0, 0), pipeline_mode=pl.Buffered(1))],
        out_specs=pl.BlockSpec((tm, D_MODEL), lambda i: (i, 0)),
        out_shape=jax.ShapeDtypeStruct((rows, D_MODEL), F32),
        scratch_shapes=[pltpu.VMEM((D_MODEL, D_MODEL), BF16), pltpu.VMEM((tm, D_MODEL), F32)],
        compiler_params=_cparams(1),
        name="outproj_sample" if sample_t else "outproj_prompt",
    )(m, x2, ada, p["g_post1"], p["w_out"])


FFN_ROWS = 64


def _ffn_conv_gate(eg, ev, wcg, wcv, bcg, bcv):
    g = bcg
    v = bcv
    for k in range(FFN_CONV):
        g = g + wcg[k:k + 1, :] * eg[k]
        v = v + wcv[k:k + 1, :] * ev[k]
    return (jax.nn.gelu(g, approximate=True) * v).astype(BF16)


def _ffn_prompt_kernel(x_ref, sh_ref, sc_ref, gate_ref, gpre_ref, gpost_ref,
                       wgt_ref, wval_ref, wcg_ref, wcv_ref, bcg_ref, bcv_ref, wdn_ref,
                       o_ref, nst_ref,
                       h_ref, eg_ref, ev_ref, f_ref, carry_ref, *, tm, tiles_per_seq, n_blocks):
    i = pl.program_id(0)
    c = pl.program_id(1)
    q = i // tiles_per_seq
    first = (i % tiles_per_seq) == 0
    nk = FFN_CONV - 1

    @pl.when(c == 0)
    def _():
        _build_h_prompt(x_ref, h_ref, gpre_ref[...], 1.0 + sc_ref[pl.ds(q, 1), :],
                        sh_ref[pl.ds(q, 1), :], tm)

    eg_ref[SUBLANES:SUBLANES + tm, :] = _dot(h_ref[...], wgt_ref[...].astype(BF16))
    ev_ref[SUBLANES:SUBLANES + tm, :] = _dot(h_ref[...], wval_ref[...].astype(BF16))

    @pl.when(first)
    def _():
        eg_ref[0:SUBLANES, :] = jnp.zeros((SUBLANES, CB), F32)
        ev_ref[0:SUBLANES, :] = jnp.zeros((SUBLANES, CB), F32)

    @pl.when(jnp.logical_not(first))
    def _():
        eg_ref[0:SUBLANES, :] = carry_ref[c, :, 0:CB]
        ev_ref[0:SUBLANES, :] = carry_ref[c, :, CB:2 * CB]

    carry_ref[c, :, 0:CB] = eg_ref[tm:tm + SUBLANES, :]
    carry_ref[c, :, CB:2 * CB] = ev_ref[tm:tm + SUBLANES, :]
    nst_ref[0, :, 0, :] = eg_ref[pl.ds(SUBLANES + tm - nk, nk), :]
    nst_ref[0, :, 1, :] = ev_ref[pl.ds(SUBLANES + tm - nk, nk), :]

    wcg = wcg_ref[...]
    wcv = wcv_ref[...]
    bcg = bcg_ref[...]
    bcv = bcv_ref[...]
    for r0 in range(0, tm, FFN_ROWS):
        eg = [eg_ref[SUBLANES + r0 - nk + k:SUBLANES + r0 - nk + k + FFN_ROWS, :]
              for k in range(FFN_CONV)]
        ev = [ev_ref[SUBLANES + r0 - nk + k:SUBLANES + r0 - nk + k + FFN_ROWS, :]
              for k in range(FFN_CONV)]
        f_ref[r0:r0 + FFN_ROWS, :] = _ffn_conv_gate(eg, ev, wcg, wcv, bcg, bcv)

    down = _dot(f_ref[...], wdn_ref[...].astype(BF16))

    @pl.when(c == 0)
    def _():
        o_ref[...] = down

    @pl.when(c > 0)
    def _():
        o_ref[...] += down

    @pl.when(c == n_blocks - 1)
    def _():
        gate = gate_ref[pl.ds(q, 1), :]
        g = gpost_ref[...]

        def body(k, carry):
            sl = pl.ds(pl.multiple_of(k * NORM_ROWS, NORM_ROWS), NORM_ROWS)
            o_ref[sl, :] = x_ref[sl, :] + gate * _rms(o_ref[sl, :], g)
            return carry
        lax.fori_loop(0, tm // NORM_ROWS, body, 0)


def _ffn_sample_kernel(x_ref, sh_ref, sc_ref, gate_ref, gpre_ref, gpost_ref,
                       wgt_ref, wval_ref, wcg_ref, wcv_ref, bcg_ref, bcv_ref, wdn_ref,
                       sg_ref, sv_ref,
                       o_ref, ng_ref, nv_ref,
                       h_ref, eg_ref, ev_ref, f_ref, acc_ref, *, n_seq, n_t, n_blocks):
    c = pl.program_id(1)
    nk = FFN_CONV - 1

    @pl.when(c == 0)
    def _():
        g = gpre_ref[...]
        for s0 in range(0, n_seq, NORM_ROWS):
            scale1p = 1.0 + sc_ref[s0:s0 + NORM_ROWS, :]
            shift = sh_ref[s0:s0 + NORM_ROWS, :]
            for t in range(n_t):
                sl = slice(t * n_seq + s0, t * n_seq + s0 + NORM_ROWS)
                h_ref[sl, :] = (_rms(x_ref[sl, :], g) * scale1p + shift).astype(BF16)

    eg_ref[...] = _dot(h_ref[...], wgt_ref[...].astype(BF16))
    ev_ref[...] = _dot(h_ref[...], wval_ref[...].astype(BF16))
    wcg = wcg_ref[...]
    wcv = wcv_ref[...]
    bcg = bcg_ref[...]
    bcv = bcv_ref[...]
    for s0 in range(0, n_seq, FFN_ROWS):
        def views(state_ref, e_ref):
            e = [state_ref[k * n_seq + s0:k * n_seq + s0 + FFN_ROWS, :] for k in range(nk)]
            e += [e_ref[t * n_seq + s0:t * n_seq + s0 + FFN_ROWS, :] for t in range(n_t)]
            return e
        eg = views(sg_ref, eg_ref)
        ev = views(sv_ref, ev_ref)
        for k in range(nk):
            ng_ref[k * n_seq + s0:k * n_seq + s0 + FFN_ROWS, :] = eg[n_t + k]
            nv_ref[k * n_seq + s0:k * n_seq + s0 + FFN_ROWS, :] = ev[n_t + k]
        for t in range(n_t):
            f_ref[t * n_seq + s0:t * n_seq + s0 + FFN_ROWS, :] = _ffn_conv_gate(
                eg[t:t + FFN_CONV], ev[t:t + FFN_CONV], wcg, wcv, bcg, bcv)

    down = _dot(f_ref[...], wdn_ref[...].astype(BF16))

    @pl.when(c == 0)
    def _():
        acc_ref[...] = down

    @pl.when(c > 0)
    def _():
        acc_ref[...] += down

    @pl.when(c == n_blocks - 1)
    def _():
        g = gpost_ref[...]
        for s0 in range(0, n_seq, NORM_ROWS):
            gate = gate_ref[s0:s0 + NORM_ROWS, :]
            for t in range(n_t):
                sl = slice(t * n_seq + s0, t * n_seq + s0 + NORM_ROWS)
                o_ref[sl, :] = x_ref[sl, :] + gate * _rms(acc_ref[sl, :], g)


def _ffn_weight_specs(n_blocks):
    return [
        pl.BlockSpec((1, D_MODEL), lambda i, c: (0, 0)),
        pl.BlockSpec((1, D_MODEL), lambda i, c: (0, 0)),
        pl.BlockSpec((D_MODEL, CB), lambda i, c: (0, c)),
        pl.BlockSpec((D_MODEL, CB), lambda i, c: (0, n_blocks + c)),
        pl.BlockSpec((FFN_CONV, CB), lambda i, c: (0, c)),
        pl.BlockSpec((FFN_CONV, CB), lambda i, c: (0, n_blocks + c)),
        pl.BlockSpec((1, CB), lambda i, c: (0, c)),
        pl.BlockSpec((1, CB), lambda i, c: (0, n_blocks + c)),
        pl.BlockSpec((CB, D_MODEL), lambda i, c: (c, 0)),
    ]


def _ffn_prompt(x1, ada, p, n_seq, tm, tiles_per_seq, ada_row_block):
    rows = x1.shape[0]
    d_ff = p["w_ffn_down"].shape[0]
    n_blocks = d_ff // CB
    ada_spec = lambda k: pl.BlockSpec((SUBLANES, D_MODEL), lambda i, c: (ada_row_block, k))
    in_specs = [pl.BlockSpec((tm, D_MODEL), lambda i, c: (i, 0), pipeline_mode=pl.Buffered(1)),
                ada_spec(3), ada_spec(4), ada_spec(5)] + _ffn_weight_specs(n_blocks)
    out_specs = [
        pl.BlockSpec((tm, D_MODEL), lambda i, c: (i, 0)),
        pl.BlockSpec((1, FFN_CONV - 1, 2, CB), lambda i, c: (i, 0, 0, c)),
    ]
    out_shape = [jax.ShapeDtypeStruct((rows, D_MODEL), F32),
                 jax.ShapeDtypeStruct((rows // tm, FFN_CONV - 1, 2, d_ff), F32)]
    scratch = [
        pltpu.VMEM((tm, D_MODEL), BF16),
        pltpu.VMEM((SUBLANES + tm, CB), F32),
        pltpu.VMEM((SUBLANES + tm, CB), F32),
        pltpu.VMEM((tm, CB), BF16),
        pltpu.VMEM((n_blocks, SUBLANES, 2 * CB), F32),
    ]
    w = p["w_ffn_up"], p["w_ffn_conv"], p["b_ffn_conv"]
    return pl.pallas_call(
        functools.partial(_ffn_prompt_kernel, tm=tm, tiles_per_seq=tiles_per_seq,
                          n_blocks=n_blocks),
        grid=(rows // tm, n_blocks),
        in_specs=in_specs, out_specs=out_specs, out_shape=out_shape,
        scratch_shapes=scratch, compiler_params=_cparams(2), name="ffn_prompt",
    )(x1, ada, ada, ada, p["g_pre2"], p["g_post2"], w[0], w[0], w[1], w[1], w[2], w[2],
      p["w_ffn_down"])


def _ffn_sample(x1, ada, p, sffn2, n_seq, n_t):
    rows = n_seq * n_t
    d_ff = p["w_ffn_down"].shape[0]
    n_blocks = d_ff // CB
    nk = FFN_CONV - 1
    ada_spec = lambda k: pl.BlockSpec((n_seq, D_MODEL), lambda i, c: (0, k))
    in_specs = [pl.BlockSpec((rows, D_MODEL), lambda i, c: (0, 0), pipeline_mode=pl.Buffered(1)),
                ada_spec(3), ada_spec(4), ada_spec(5)] + _ffn_weight_specs(n_blocks) + [
        pl.BlockSpec((n_seq * nk, CB), lambda i, c: (0, c)),
        pl.BlockSpec((n_seq * nk, CB), lambda i, c: (0, n_blocks + c)),
    ]
    out_specs = [
        pl.BlockSpec((rows, D_MODEL), lambda i, c: (0, 0)),
        pl.BlockSpec((n_seq * nk, CB), lambda i, c: (0, c)),
        pl.BlockSpec((n_seq * nk, CB), lambda i, c: (0, c)),
    ]
    out_shape = [jax.ShapeDtypeStruct((rows, D_MODEL), F32),
                 jax.ShapeDtypeStruct((n_seq * nk, d_ff), F32),
                 jax.ShapeDtypeStruct((n_seq * nk, d_ff), F32)]
    scratch = [
        pltpu.VMEM((rows, D_MODEL), BF16),
        pltpu.VMEM((rows, CB), F32),
        pltpu.VMEM((rows, CB), F32),
        pltpu.VMEM((rows, CB), BF16),
        pltpu.VMEM((rows, D_MODEL), F32),
    ]
    w = p["w_ffn_up"], p["w_ffn_conv"], p["b_ffn_conv"]
    return pl.pallas_call(
        functools.partial(_ffn_sample_kernel, n_seq=n_seq, n_t=n_t, n_blocks=n_blocks),
        grid=(1, n_blocks),
        in_specs=in_specs, out_specs=out_specs, out_shape=out_shape,
        scratch_shapes=scratch, compiler_params=_cparams(2), name="ffn_sample",
    )(x1, ada, ada, ada, p["g_pre2"], p["g_post2"], w[0], w[0], w[1], w[1], w[2], w[2],
      p["w_ffn_down"], sffn2, sffn2)


TOKEN_TILE = 1024
OUTPROJ_TILE = 512


def kernel(x_prompt, x_sample, c_prompt, c_sample, state_pool, state_lru_conv, state_lru_h, state_ffn_conv, w_ada, b_ada, g_pre1, g_post1, g_pre2, g_post2, w_in, w_pool_grp, pool_scale, w_lru_conv, b_lru_conv, w_rg, b_rg, w_ig, b_ig, lru_lambda, w_pool_up, w_lru_up, w_out, w_ffn_up, w_ffn_conv, b_ffn_conv, w_ffn_down):
    batch, seq, d = x_prompt.shape
    dec_batch, dec_seq, _ = x_sample.shape
    depth = w_ada.shape[0]
    assert d == D_MODEL and dec_batch % SUBLANES == 0 and seq % TOKEN_TILE == 0
    assert w_in.shape[2] == N_MIX_BLOCKS * CB + 2 * D_MODEL

    pad = (-batch) % SUBLANES
    c_all = jnp.concatenate([c_sample, c_prompt, jnp.zeros((pad, d), c_prompt.dtype)], axis=0)
    prompt_row_block = dec_batch // SUBLANES

    vec_names = ("g_pre1", "g_post1", "g_pre2", "g_post2", "pool_scale", "b_lru_conv", "b_rg",
                 "b_ig", "lru_lambda", "b_ffn_conv")
    weights = dict(w_ada=w_ada, b_ada=b_ada, g_pre1=g_pre1, g_post1=g_post1, g_pre2=g_pre2,
                   g_post2=g_post2, w_in=w_in, w_pool_grp=w_pool_grp, pool_scale=pool_scale,
                   w_lru_conv=w_lru_conv, b_lru_conv=b_lru_conv, w_rg=w_rg, b_rg=b_rg, w_ig=w_ig,
                   b_ig=b_ig, lru_lambda=lru_lambda, w_pool_up=w_pool_up, w_lru_up=w_lru_up,
                   w_out=w_out, w_ffn_up=w_ffn_up, w_ffn_conv=w_ffn_conv, b_ffn_conv=b_ffn_conv,
                   w_ffn_down=w_ffn_down)

    def time_major(a):
        return jnp.swapaxes(a, 0, 1).reshape(-1, a.shape[-1])

    def seq_major(a2, n_rows):
        return jnp.swapaxes(a2.reshape(n_rows, dec_batch, -1), 0, 1)

    xp = x_prompt.reshape(batch * seq, d)
    xs = time_major(x_sample)
    tps = seq // TOKEN_TILE
    outs_p = ([], [], [], [])
    outs_s = ([], [], [], [])
    for l in range(depth):
        p = {k: v[l] for k, v in weights.items()}
        for k in vec_names + ("b_ada",):
            p[k] = p[k].reshape(1, -1)
        ada = _ada(c_all, p["w_ada"], p["b_ada"])

        y, npool, nconv, nh = _mix_prompt(xp, ada, p, batch, seq, prompt_row_block)
        m = _merge(xp, ada, y, p, TOKEN_TILE, tps, prompt_row_block, 0)
        x1 = _outproj(m, xp, ada, p, OUTPROJ_TILE, seq // OUTPROJ_TILE, prompt_row_block, 0)
        xp, nffn = _ffn_prompt(x1, ada, p, batch, TOKEN_TILE, tps, prompt_row_block)
        outs_p[0].append(npool)
        outs_p[1].append(nconv)
        outs_p[2].append(nh.reshape(batch, LRU_WIDTH))
        outs_p[3].append(nffn[tps - 1::tps].reshape(batch, FFN_CONV - 1, -1))

        rows_s = dec_batch * dec_seq
        y, npool, nconv, nh = _mix_sample(
            xs, ada, p, time_major(state_pool[l]), time_major(state_lru_conv[l]),
            state_lru_h[l], dec_batch, dec_seq, PAST_LEN)
        m = _merge(xs, ada, y, p, rows_s, 1, 0, dec_seq)
        x1 = _outproj(m, xs, ada, p, rows_s, 1, 0, dec_seq)
        xs, ng, nv = _ffn_sample(x1, ada, p, time_major(state_ffn_conv[l]), dec_batch, dec_seq)
        outs_s[0].append(seq_major(npool, POOL_BUF))
        outs_s[1].append(seq_major(nconv, LRU_CONV - 1))
        outs_s[2].append(nh)
        outs_s[3].append(jnp.concatenate([seq_major(ng, FFN_CONV - 1),
                                          seq_major(nv, FFN_CONV - 1)], axis=-1))

    return (xp.reshape(batch, seq, d), seq_major(xs, dec_seq),
            jnp.stack(outs_p[0]), jnp.stack(outs_p[1]), jnp.stack(outs_p[2]), jnp.stack(outs_p[3]),
            jnp.stack(outs_s[0]), jnp.stack(outs_s[1]), jnp.stack(outs_s[2]), jnp.stack(outs_s[3]))
```

```python
import functools

import jax
import jax.numpy as jnp
from jax import lax
from jax.experimental import pallas as pl
from jax.experimental.pallas import tpu as pltpu

F32 = jnp.float32
BF16 = jnp.bfloat16

D_MODEL = 2048
POOL_WINDOWS = (2, 4, 8, 16)
POOL_GROUPS = len(POOL_WINDOWS)
POOL_BUF = max(POOL_WINDOWS) - 1
LRU_CONV = 4
LRU_C = 8.0
PAST_LEN = 16384
FFN_CONV = 3
N_ADA = 6
EPS = 1e-6

CB = 256
POOL_WIDTH = POOL_GROUPS * CB
LRU_BLOCKS = 8
LRU_WIDTH = LRU_BLOCKS * CB
N_MIX_BLOCKS = POOL_GROUPS + LRU_BLOCKS
HALO = 16
SUBLANES = 8
VMEM_LIMIT = 60 * 1024 * 1024


def _cparams(n_axes):
    return pltpu.CompilerParams(
        dimension_semantics=("arbitrary",) * n_axes, vmem_limit_bytes=VMEM_LIMIT)


def _dot(a, b):
    return jnp.dot(a, b, preferred_element_type=F32)


def _softplus(z):
    return jnp.maximum(z, 0.0) + jnp.log1p(jnp.exp(-jnp.abs(z)))


def _lru_coeffs(xc, r_pre, i_pre, neg_c_sp):
    r = jax.nn.sigmoid(r_pre)
    i = jax.nn.sigmoid(i_pre)
    log_a = r * neg_c_sp
    a = jnp.exp(log_a)
    b = jnp.sqrt(-jnp.tanh(log_a) * (a * a + 1.0)) * (i * xc)
    return a, b


def _ada_kernel(c_ref, w_ref, b_ref, o_ref):
    c = c_ref[...]
    s = (c * jax.nn.sigmoid(c)).astype(BF16)
    o_ref[...] = _dot(s, w_ref[...].astype(BF16)) + b_ref[...]


def _ada(c_all, w_ada, b_ada):
    rows = c_all.shape[0]
    tn = 1024
    return pl.pallas_call(
        _ada_kernel,
        grid=(N_ADA * D_MODEL // tn,),
        in_specs=[pl.BlockSpec((rows, D_MODEL), lambda n: (0, 0)),
                  pl.BlockSpec((D_MODEL, tn), lambda n: (0, n)),
                  pl.BlockSpec((1, tn), lambda n: (0, n))],
        out_specs=pl.BlockSpec((rows, tn), lambda n: (0, n)),
        out_shape=jax.ShapeDtypeStruct((rows, N_ADA * D_MODEL), F32),
        compiler_params=_cparams(1),
        name="ada",
    )(c_all, w_ada, b_ada)


NORM_ROWS = 32
NORM_UNROLL = 4
OUT_SUBTILE = 256


def _unit_rms(x):
    return x * lax.rsqrt(jnp.mean(x * x, axis=-1, keepdims=True) + EPS)


def _build_h_prompt(x_ref, h_ref, g, scale1p, shift, n_rows):
    gs = g * scale1p

    def body(i, carry):
        r0 = pl.multiple_of(i * NORM_ROWS, NORM_ROWS)
        x = x_ref[pl.ds(r0, NORM_ROWS), :]
        h_ref[pl.ds(r0, NORM_ROWS), :] = (_unit_rms(x) * gs + shift).astype(BF16)
        return carry
    lax.fori_loop(0, n_rows // NORM_ROWS, body, 0, unroll=NORM_UNROLL)


def _residual_norm_prompt(x_ref, acc_ref, o_ref, g, gate, n_rows):
    gg = g * gate

    def body(i, carry):
        sl = pl.ds(pl.multiple_of(i * NORM_ROWS, NORM_ROWS), NORM_ROWS)
        o_ref[sl, :] = x_ref[sl, :] + _unit_rms(acc_ref[sl, :]) * gg
        return carry
    lax.fori_loop(0, n_rows // NORM_ROWS, body, 0, unroll=NORM_UNROLL)


def _build_h_sample(x_ref, h_ref, g, sc_ref, sh_ref, n_seq, n_t):
    for s0 in range(0, n_seq, NORM_ROWS):
        gs = g * (1.0 + sc_ref[s0:s0 + NORM_ROWS, :])
        shift = sh_ref[s0:s0 + NORM_ROWS, :]
        for t in range(n_t):
            sl = slice(t * n_seq + s0, t * n_seq + s0 + NORM_ROWS)
            h_ref[sl, :] = (_unit_rms(x_ref[sl, :]) * gs + shift).astype(BF16)


MIX_ROWS = 128


def _mix_prompt_kernel(x_ref, sh_ref, sc_ref, g_ref, win_ref, wgrp_ref, pscale_ref,
                       wconv_ref, bconv_ref, wrg_ref, brg_ref, wig_ref, big_ref, lam_ref,
                       y_ref, npool_ref, nconv_ref, nh_ref, h_ref,
                       ext_ref, xb_ref, r_ref, i_ref, a_ref, u_ref, *, seq):
    q = pl.program_id(0)
    j = pl.program_id(1)

    @pl.when(j == 0)
    def _():
        _build_h_prompt(x_ref, h_ref, g_ref[...], 1.0 + sc_ref[pl.ds(q, 1), :],
                        sh_ref[pl.ds(q, 1), :], seq)
        ext_ref[0:HALO, :] = jnp.zeros((HALO, CB), F32)

    ext_ref[HALO:HALO + seq, :] = _dot(h_ref[...], win_ref[...].astype(BF16))

    def pool_branch(w):
        npool_ref[0] = ext_ref[pl.ds(HALO + seq - POOL_BUF, POOL_BUF), :]
        for r0 in range(0, seq, MIX_ROWS):
            u = ext_ref[HALO + r0:HALO + r0 + MIX_ROWS, :]
            s = u
            for k in range(1, w):
                s = s + ext_ref[HALO + r0 - k:HALO + r0 - k + MIX_ROWS, :]
            if r0 < w:
                pos = r0 + lax.broadcasted_iota(jnp.int32, (MIX_ROWS, 1), 0)
                cnt = jnp.minimum(w, pos + 1).astype(F32)
            else:
                cnt = float(w)
            xb_ref[r0:r0 + MIX_ROWS, :] = (s / cnt - u).astype(BF16)
        y = _dot(xb_ref[...], wgrp_ref[0].astype(BF16)) * pscale_ref[...]
        y_ref[...] = y.astype(BF16)

    for g, w in enumerate(POOL_WINDOWS):
        pl.when(j == g)(functools.partial(pool_branch, w))

    @pl.when(j >= POOL_GROUPS)
    def _():
        nconv_ref[0] = ext_ref[pl.ds(HALO + seq - (LRU_CONV - 1), LRU_CONV - 1), :]
        wc = wconv_ref[...]
        bc = bconv_ref[...]
        for r0 in range(0, seq, MIX_ROWS):
            xc = bc
            for k in range(LRU_CONV):
                o = HALO + r0 - (LRU_CONV - 1) + k
                xc = xc + wc[k:k + 1, :] * ext_ref[o:o + MIX_ROWS, :]
            a_ref[r0:r0 + MIX_ROWS, :] = xc
            xb_ref[r0:r0 + MIX_ROWS, :] = xc.astype(BF16)
        r_ref[...] = _dot(xb_ref[...], wrg_ref[0].astype(BF16))
        i_ref[...] = _dot(xb_ref[...], wig_ref[0].astype(BF16))
        neg_c_sp = (-LRU_C) * _softplus(-lam_ref[...])
        brg = brg_ref[...]
        big = big_ref[...]
        nb = MIX_ROWS // SUBLANES
        row = lax.broadcasted_iota(jnp.int32, (nb, SUBLANES, CB), 1)
        for r0 in range(0, seq, MIX_ROWS):
            sl = slice(r0, r0 + MIX_ROWS)
            a, b = _lru_coeffs(a_ref[sl, :], r_ref[sl, :] + brg, i_ref[sl, :] + big, neg_c_sp)
            a = a.reshape(nb, SUBLANES, CB)
            b = b.reshape(nb, SUBLANES, CB)
            for k in (1, 2, 4):
                a_sh = jnp.where(row >= k, pltpu.roll(a, k, 1), 1.0)
                b_sh = jnp.where(row >= k, pltpu.roll(b, k, 1), 0.0)
                b = b + a * b_sh
                a = a * a_sh
            a_ref[sl, :] = a.reshape(MIX_ROWS, CB)
            u_ref[sl, :] = b.reshape(MIX_ROWS, CB)

        def carry_body(blk, h_prev):
            r0 = pl.multiple_of(blk * SUBLANES, SUBLANES)
            h = a_ref[pl.ds(r0, SUBLANES), :] * h_prev + u_ref[pl.ds(r0, SUBLANES), :]
            u_ref[pl.ds(r0, SUBLANES), :] = h
            return h[SUBLANES - 1:SUBLANES, :]

        h_last = lax.fori_loop(0, seq // SUBLANES, carry_body, jnp.zeros((1, CB), F32), unroll=8)
        nh_ref[0] = h_last
        y_ref[...] = u_ref[...].astype(BF16)


def _mix_sample_kernel(x_ref, sh_ref, sc_ref, g_ref, win_ref, wgrp_ref, pscale_ref,
                       wconv_ref, bconv_ref, wrg_ref, brg_ref, wig_ref, big_ref, lam_ref,
                       spool_ref, sconv_ref, sh0_ref,
                       y_ref, npool_ref, nconv_ref, nh_ref, h_ref,
                       u_ref, d_ref, *, n_seq, n_t, start):
    j = pl.program_id(1)

    @pl.when(j == 0)
    def _():
        _build_h_sample(x_ref, h_ref, g_ref[...], sc_ref, sh_ref, n_seq, n_t)

    u_ref[...] = _dot(h_ref[...], win_ref[...].astype(BF16))

    def u_slab(t):
        return u_ref[t * n_seq:(t + 1) * n_seq, :]

    def pool_branch(w):
        e = [spool_ref[k * n_seq:(k + 1) * n_seq, :] for k in range(POOL_BUF)]
        e += [u_slab(t) for t in range(n_t)]
        for k in range(POOL_BUF):
            npool_ref[k * n_seq:(k + 1) * n_seq, :] = e[n_t + k]
        for t in range(n_t):
            s = e[POOL_BUF + t]
            for k in range(1, w):
                s = s + e[POOL_BUF + t - k]
            cnt = float(min(w, start + t + 1))
            d_ref[t * n_seq:(t + 1) * n_seq, :] = (s / cnt - e[POOL_BUF + t]).astype(BF16)
        y = _dot(d_ref[...], wgrp_ref[0].astype(BF16)) * pscale_ref[...]
        y_ref[...] = y.astype(BF16)

    for g, w in enumerate(POOL_WINDOWS):
        pl.when(j == g)(functools.partial(pool_branch, w))

    @pl.when(j >= POOL_GROUPS)
    def _():
        nk = LRU_CONV - 1
        e = [sconv_ref[k * n_seq:(k + 1) * n_seq, :] for k in range(nk)]
        e += [u_slab(t) for t in range(n_t)]
        for k in range(nk):
            nconv_ref[k * n_seq:(k + 1) * n_seq, :] = e[n_t + k]
        wc = wconv_ref[...]
        bc = bconv_ref[...]
        wrg = wrg_ref[0].astype(BF16)
        wig = wig_ref[0].astype(BF16)
        neg_c_sp = (-LRU_C) * _softplus(-lam_ref[...])
        h = sh0_ref[...]
        for t in range(n_t):
            xc = bc
            for k in range(LRU_CONV):
                xc = xc + wc[k:k + 1, :] * e[t + k]
            xb = xc.astype(BF16)
            a, b = _lru_coeffs(xc, _dot(xb, wrg) + brg_ref[...], _dot(xb, wig) + big_ref[...],
                               neg_c_sp)
            h = a * h + b
            y_ref[t * n_seq:(t + 1) * n_seq, :] = h.astype(BF16)
        nh_ref[...] = h


def _mix_weight_specs():
    pj = lambda j: jnp.minimum(j, POOL_GROUPS - 1)
    lj = lambda j: jnp.maximum(j - POOL_GROUPS, 0)
    return [
        pl.BlockSpec((1, D_MODEL), lambda q, j: (0, 0)),
        pl.BlockSpec((D_MODEL, CB), lambda q, j: (0, j)),
        pl.BlockSpec((1, CB, CB), lambda q, j: (pj(j), 0, 0)),
        pl.BlockSpec((1, CB), lambda q, j: (0, pj(j))),
        pl.BlockSpec((LRU_CONV, CB), lambda q, j: (0, lj(j))),
        pl.BlockSpec((1, CB), lambda q, j: (0, lj(j))),
        pl.BlockSpec((1, CB, CB), lambda q, j: (lj(j), 0, 0)),
        pl.BlockSpec((1, CB), lambda q, j: (0, lj(j))),
        pl.BlockSpec((1, CB, CB), lambda q, j: (lj(j), 0, 0)),
        pl.BlockSpec((1, CB), lambda q, j: (0, lj(j))),
        pl.BlockSpec((1, CB), lambda q, j: (0, lj(j))),
    ], pj, lj


def _mix_prompt(x2, ada, p, n_seq, seq, ada_row_block):
    wspecs, pj, lj = _mix_weight_specs()
    in_specs = [
        pl.BlockSpec((seq, D_MODEL), lambda q, j: (q, 0), pipeline_mode=pl.Buffered(1)),
        pl.BlockSpec((SUBLANES, D_MODEL), lambda q, j: (ada_row_block, 0)),
        pl.BlockSpec((SUBLANES, D_MODEL), lambda q, j: (ada_row_block, 1)),
    ] + wspecs
    out_specs = [
        pl.BlockSpec((seq, CB), lambda q, j: (q, j)),
        pl.BlockSpec((1, POOL_BUF, CB), lambda q, j: (q, 0, pj(j))),
        pl.BlockSpec((1, LRU_CONV - 1, CB), lambda q, j: (q, 0, lj(j))),
        pl.BlockSpec((1, 1, CB), lambda q, j: (q, 0, lj(j))),
        pl.BlockSpec((seq, D_MODEL), lambda q, j: (q, 0)),
    ]
    out_shape = [
        jax.ShapeDtypeStruct((n_seq * seq, N_MIX_BLOCKS * CB), BF16),
        jax.ShapeDtypeStruct((n_seq, POOL_BUF, POOL_WIDTH), F32),
        jax.ShapeDtypeStruct((n_seq, LRU_CONV - 1, LRU_WIDTH), F32),
        jax.ShapeDtypeStruct((n_seq, 1, LRU_WIDTH), F32),
        jax.ShapeDtypeStruct((n_seq * seq, D_MODEL), BF16),
    ]
    scratch = [
        pltpu.VMEM((HALO + seq, CB), F32),
        pltpu.VMEM((seq, CB), BF16),
        pltpu.VMEM((seq, CB), F32),
        pltpu.VMEM((seq, CB), F32),
        pltpu.VMEM((seq, CB), F32),
        pltpu.VMEM((seq, CB), F32),
    ]
    return pl.pallas_call(
        functools.partial(_mix_prompt_kernel, seq=seq),
        grid=(n_seq, N_MIX_BLOCKS),
        in_specs=in_specs, out_specs=out_specs, out_shape=out_shape,
        scratch_shapes=scratch, compiler_params=_cparams(2), name="mix_prompt",
    )(x2, ada, ada, p["g_pre1"], p["w_in"], p["w_pool_grp"], p["pool_scale"],
      p["w_lru_conv"], p["b_lru_conv"], p["w_rg"], p["b_rg"], p["w_ig"], p["b_ig"],
      p["lru_lambda"])


def _mix_sample(x2, ada, p, spool2, sconv2, sh0, n_seq, n_t, start):
    wspecs, pj, lj = _mix_weight_specs()
    rows = n_seq * n_t
    in_specs = [
        pl.BlockSpec((rows, D_MODEL), lambda q, j: (0, 0), pipeline_mode=pl.Buffered(1)),
        pl.BlockSpec((n_seq, D_MODEL), lambda q, j: (0, 0)),
        pl.BlockSpec((n_seq, D_MODEL), lambda q, j: (0, 1)),
    ] + wspecs + [
        pl.BlockSpec((n_seq * POOL_BUF, CB), lambda q, j: (0, pj(j))),
        pl.BlockSpec((n_seq * (LRU_CONV - 1), CB), lambda q, j: (0, lj(j))),
        pl.BlockSpec((n_seq, CB), lambda q, j: (0, lj(j))),
    ]
    out_specs = [
        pl.BlockSpec((rows, CB), lambda q, j: (0, j)),
        pl.BlockSpec((n_seq * POOL_BUF, CB), lambda q, j: (0, pj(j))),
        pl.BlockSpec((n_seq * (LRU_CONV - 1), CB), lambda q, j: (0, lj(j))),
        pl.BlockSpec((n_seq, CB), lambda q, j: (0, lj(j))),
        pl.BlockSpec((rows, D_MODEL), lambda q, j: (0, 0)),
    ]
    out_shape = [
        jax.ShapeDtypeStruct((rows, N_MIX_BLOCKS * CB), BF16),
        jax.ShapeDtypeStruct((n_seq * POOL_BUF, POOL_WIDTH), F32),
        jax.ShapeDtypeStruct((n_seq * (LRU_CONV - 1), LRU_WIDTH), F32),
        jax.ShapeDtypeStruct((n_seq, LRU_WIDTH), F32),
        jax.ShapeDtypeStruct((rows, D_MODEL), BF16),
    ]
    scratch = [
        pltpu.VMEM((rows, CB), F32),
        pltpu.VMEM((rows, CB), BF16),
    ]
    return pl.pallas_call(
        functools.partial(_mix_sample_kernel, n_seq=n_seq, n_t=n_t, start=start),
        grid=(1, N_MIX_BLOCKS),
        in_specs=in_specs, out_specs=out_specs, out_shape=out_shape,
        scratch_shapes=scratch, compiler_params=_cparams(2), name="mix_sample",
    )(x2, ada, ada, p["g_pre1"], p["w_in"], p["w_pool_grp"], p["pool_scale"],
      p["w_lru_conv"], p["b_lru_conv"], p["w_rg"], p["b_rg"], p["w_ig"], p["b_ig"],
      p["lru_lambda"], spool2, sconv2, sh0)


MERGE_ROWS = 64
MERGE_SUBTILE = 256


def _merge_kernel(h_ref, y_ref, wgp_ref, wgl_ref, wpu_ref, wlu_ref,
                  m_ref, gp_ref, gl_ref, pu_ref, lu_ref, *, tm):
    wgp = wgp_ref[...].astype(BF16)
    wgl = wgl_ref[...].astype(BF16)
    wpu = wpu_ref[...].astype(BF16)
    wlu = wlu_ref[...].astype(BF16)

    def proj(m0):
        rows = slice(m0, m0 + MERGE_SUBTILE)
        gp_ref[rows, :] = _dot(h_ref[rows, :], wgp)
        gl_ref[rows, :] = _dot(h_ref[rows, :], wgl)
        pu_ref[rows, :] = _dot(y_ref[rows, 0:POOL_WIDTH], wpu)
        lu_ref[rows, :] = _dot(y_ref[rows, POOL_WIDTH:POOL_WIDTH + LRU_WIDTH], wlu)

    proj(0)
    for m0 in range(0, tm, MERGE_SUBTILE):
        if m0 + MERGE_SUBTILE < tm:
            proj(m0 + MERGE_SUBTILE)
        for r0 in range(m0, m0 + MERGE_SUBTILE, MERGE_ROWS):
            sl = slice(r0, r0 + MERGE_ROWS)
            m = (jax.nn.sigmoid(gp_ref[sl, :]) * pu_ref[sl, :]
                 + jax.nn.sigmoid(gl_ref[sl, :]) * lu_ref[sl, :])
            m_ref[sl, :] = m.astype(BF16)


def _merge(h, y, p, tm, name):
    rows = h.shape[0]
    n_gate0 = N_MIX_BLOCKS
    n_gate1 = N_MIX_BLOCKS + D_MODEL // CB
    in_specs = [
        pl.BlockSpec((tm, D_MODEL), lambda i, c: (i, 0), pipeline_mode=pl.Buffered(1)),
        pl.BlockSpec((tm, N_MIX_BLOCKS * CB), lambda i, c: (i, 0), pipeline_mode=pl.Buffered(1)),
        pl.BlockSpec((D_MODEL, CB), lambda i, c: (0, n_gate0 + c)),
        pl.BlockSpec((D_MODEL, CB), lambda i, c: (0, n_gate1 + c)),
        pl.BlockSpec((POOL_WIDTH, CB), lambda i, c: (0, c)),
        pl.BlockSpec((LRU_WIDTH, CB), lambda i, c: (0, c)),
    ]
    return pl.pallas_call(
        functools.partial(_merge_kernel, tm=tm),
        grid=(rows // tm, D_MODEL // CB),
        in_specs=in_specs,
        out_specs=pl.BlockSpec((tm, CB), lambda i, c: (i, c)),
        out_shape=jax.ShapeDtypeStruct((rows, D_MODEL), BF16),
        scratch_shapes=[pltpu.VMEM((tm, CB), F32)] * 4, compiler_params=_cparams(2),
        name=name,
    )(h, y, p["w_in"], p["w_in"], p["w_pool_up"], p["w_lru_up"])


def _cast_weight(w_ref, wb_ref):
    rows = w_ref.shape[0]
    for r0 in range(0, rows, 256):
        wb_ref[r0:r0 + 256, :] = w_ref[r0:r0 + 256, :].astype(BF16)


def _outproj_kernel(m_ref, x_ref, gate_ref, g_ref, w_ref, o_ref, wb_ref, acc_ref,
                    *, tm, tiles_per_seq, sample_t):
    i = pl.program_id(0)

    @pl.when(i == 0)
    def _():
        _cast_weight(w_ref, wb_ref)

    g = g_ref[...]
    if sample_t:
        n_seq = tm // sample_t
    else:
        gg_tile = g * gate_ref[pl.ds(i // tiles_per_seq, 1), :]

    def proj(m0):
        acc_ref[m0:m0 + OUT_SUBTILE, :] = _dot(m_ref[m0:m0 + OUT_SUBTILE, :], wb_ref[...])

    proj(0)
    for m0 in range(0, tm, OUT_SUBTILE):
        if m0 + OUT_SUBTILE < tm:
            proj(m0 + OUT_SUBTILE)
        for r0 in range(m0, m0 + OUT_SUBTILE, NORM_ROWS):
            sl = slice(r0, r0 + NORM_ROWS)
            if sample_t:
                s0 = r0 % n_seq
                gg = g * gate_ref[s0:s0 + NORM_ROWS, :]
            else:
                gg = gg_tile
            o_ref[sl, :] = x_ref[sl, :] + _unit_rms(acc_ref[sl, :]) * gg


def _outproj(m, x2, ada, p, tm, tiles_per_seq, ada_row_block, sample_t):
    rows = x2.shape[0]
    if sample_t:
        gate_spec = pl.BlockSpec((tm // sample_t, D_MODEL), lambda i: (0, 2))
    else:
        gate_spec = pl.BlockSpec((SUBLANES, D_MODEL), lambda i: (ada_row_block, 2))
    return pl.pallas_call(
        functools.partial(_outproj_kernel, tm=tm, tiles_per_seq=tiles_per_seq, sample_t=sample_t),
        grid=(rows // tm,),
        in_specs=[pl.BlockSpec((tm, D_MODEL), lambda i: (i, 0)),
                  pl.BlockSpec((tm, D_MODEL), lambda i: (i, 0)),
                  gate_spec,
                  pl.BlockSpec((1, D_MODEL), lambda i: (0, 0)),
                  pl.BlockSpec((D_MODEL, D_MODEL), lambda i: (0, 0), pipeline_mode=pl.Buffered(1))],
        out_specs=pl.BlockSpec((tm, D_MODEL), lambda i: (i, 0)),
        out_shape=jax.ShapeDtypeStruct((rows, D_MODEL), F32),
        scratch_shapes=[pltpu.VMEM((D_MODEL, D_MODEL), BF16), pltpu.VMEM((tm, D_MODEL), F32)],
        compiler_params=_cparams(1),
        name="outproj_sample" if sample_t else "outproj_prompt",
    )(m, x2, ada, p["g_post1"], p["w_out"])


FFN_ROWS = 64
FFN_SUBTILE = 256


def _ffn_conv_gate(eg, ev, wcg, wcv, bcg, bcv):
    g = bcg
    v = bcv
    for k in range(FFN_CONV):
        g = g + wcg[k:k + 1, :] * eg[k]
        v = v + wcv[k:k + 1, :] * ev[k]
    return (jax.nn.gelu(g, approximate=True) * v).astype(BF16)


def _ffn_prompt_kernel(x_ref, sh_ref, sc_ref, gate_ref, gpre_ref, gpost_ref,
                       wgt_ref, wval_ref, wcg_ref, wcv_ref, bcg_ref, bcv_ref, wdn_ref,
                       o_ref, nst_ref,
                       h_ref, eg_ref, ev_ref, f_ref, carry_ref, *, tm, tiles_per_seq, n_blocks):
    i = pl.program_id(0)
    c = pl.program_id(1)
    q = i // tiles_per_seq
    first = (i % tiles_per_seq) == 0
    nk = FFN_CONV - 1

    @pl.when(c == 0)
    def _():
        _build_h_prompt(x_ref, h_ref, gpre_ref[...], 1.0 + sc_ref[pl.ds(q, 1), :],
                        sh_ref[pl.ds(q, 1), :], tm)
        o_ref[...] = jnp.zeros((tm, D_MODEL), F32)

    @pl.when(first)
    def _():
        eg_ref[0:SUBLANES, :] = jnp.zeros((SUBLANES, CB), F32)
        ev_ref[0:SUBLANES, :] = jnp.zeros((SUBLANES, CB), F32)

    @pl.when(jnp.logical_not(first))
    def _():
        eg_ref[0:SUBLANES, :] = carry_ref[c, :, 0:CB]
        ev_ref[0:SUBLANES, :] = carry_ref[c, :, CB:2 * CB]

    wgt = wgt_ref[...].astype(BF16)
    wval = wval_ref[...].astype(BF16)
    wdn = wdn_ref[...].astype(BF16)
    wcg = wcg_ref[...]
    wcv = wcv_ref[...]
    bcg = bcg_ref[...]
    bcv = bcv_ref[...]
    def up_proj(m0):
        rows = slice(m0, m0 + FFN_SUBTILE)
        ext = slice(SUBLANES + m0, SUBLANES + m0 + FFN_SUBTILE)
        eg_ref[ext, :] = _dot(h_ref[rows, :], wgt)
        ev_ref[ext, :] = _dot(h_ref[rows, :], wval)

    up_proj(0)
    for m0 in range(0, tm, FFN_SUBTILE):
        rows = slice(m0, m0 + FFN_SUBTILE)
        if m0 + FFN_SUBTILE < tm:
            up_proj(m0 + FFN_SUBTILE)
        for r0 in range(m0, m0 + FFN_SUBTILE, FFN_ROWS):
            eg = [eg_ref[SUBLANES + r0 - nk + k:SUBLANES + r0 - nk + k + FFN_ROWS, :]
                  for k in range(FFN_CONV)]
            ev = [ev_ref[SUBLANES + r0 - nk + k:SUBLANES + r0 - nk + k + FFN_ROWS, :]
                  for k in range(FFN_CONV)]
            f_ref[r0:r0 + FFN_ROWS, :] = _ffn_conv_gate(eg, ev, wcg, wcv, bcg, bcv)
        o_ref[rows, :] += _dot(f_ref[rows, :], wdn)

    carry_ref[c, :, 0:CB] = eg_ref[tm:tm + SUBLANES, :]
    carry_ref[c, :, CB:2 * CB] = ev_ref[tm:tm + SUBLANES, :]
    nst_ref[0, :, 0, :] = eg_ref[pl.ds(SUBLANES + tm - nk, nk), :]
    nst_ref[0, :, 1, :] = ev_ref[pl.ds(SUBLANES + tm - nk, nk), :]

    @pl.when(c == n_blocks - 1)
    def _():
        _residual_norm_prompt(x_ref, o_ref, o_ref, gpost_ref[...], gate_ref[pl.ds(q, 1), :], tm)


def _ffn_sample_kernel(x_ref, sh_ref, sc_ref, gate_ref, gpre_ref, gpost_ref,
                       wgt_ref, wval_ref, wcg_ref, wcv_ref, bcg_ref, bcv_ref, wdn_ref,
                       sg_ref, sv_ref,
                       o_ref, ng_ref, nv_ref,
                       h_ref, eg_ref, ev_ref, f_ref, acc_ref, *, n_seq, n_t, n_blocks):
    c = pl.program_id(1)
    nk = FFN_CONV - 1

    @pl.when(c == 0)
    def _():
        _build_h_sample(x_ref, h_ref, gpre_ref[...], sc_ref, sh_ref, n_seq, n_t)
        acc_ref[...] = jnp.zeros((n_seq * n_t, D_MODEL), F32)

    wgt = wgt_ref[...].astype(BF16)
    wval = wval_ref[...].astype(BF16)
    wdn = wdn_ref[...].astype(BF16)
    wcg = wcg_ref[...]
    wcv = wcv_ref[...]
    bcg = bcg_ref[...]
    bcv = bcv_ref[...]

    def view(state_ref, e_ref, t, s0, n):
        src, tt = (state_ref, nk + t) if t < 0 else (e_ref, t)
        return src[tt * n_seq + s0:tt * n_seq + s0 + n, :]

    t_sub = max(FFN_SUBTILE // n_seq, 1)

    def up_proj(t0):
        rows = slice(t0 * n_seq, (t0 + t_sub) * n_seq)
        eg_ref[rows, :] = _dot(h_ref[rows, :], wgt)
        ev_ref[rows, :] = _dot(h_ref[rows, :], wval)

    up_proj(0)
    for t0 in range(0, n_t, t_sub):
        rows = slice(t0 * n_seq, (t0 + t_sub) * n_seq)
        if t0 + t_sub < n_t:
            up_proj(t0 + t_sub)
        for t in range(t0, t0 + t_sub):
            for s0 in range(0, n_seq, FFN_ROWS):
                eg = [view(sg_ref, eg_ref, t - nk + k, s0, FFN_ROWS) for k in range(FFN_CONV)]
                ev = [view(sv_ref, ev_ref, t - nk + k, s0, FFN_ROWS) for k in range(FFN_CONV)]
                f_ref[t * n_seq + s0:t * n_seq + s0 + FFN_ROWS, :] = _ffn_conv_gate(
                    eg, ev, wcg, wcv, bcg, bcv)
        acc_ref[rows, :] += _dot(f_ref[rows, :], wdn)

    for k in range(nk):
        ng_ref[k * n_seq:(k + 1) * n_seq, :] = view(sg_ref, eg_ref, n_t - nk + k, 0, n_seq)
        nv_ref[k * n_seq:(k + 1) * n_seq, :] = view(sv_ref, ev_ref, n_t - nk + k, 0, n_seq)

    @pl.when(c == n_blocks - 1)
    def _():
        g = gpost_ref[...]
        for s0 in range(0, n_seq, NORM_ROWS):
            gg = g * gate_ref[s0:s0 + NORM_ROWS, :]
            for t in range(n_t):
                sl = slice(t * n_seq + s0, t * n_seq + s0 + NORM_ROWS)
                o_ref[sl, :] = x_ref[sl, :] + _unit_rms(acc_ref[sl, :]) * gg


def _ffn_weight_specs(n_blocks):
    return [
        pl.BlockSpec((1, D_MODEL), lambda i, c: (0, 0)),
        pl.BlockSpec((1, D_MODEL), lambda i, c: (0, 0)),
        pl.BlockSpec((D_MODEL, CB), lambda i, c: (0, c)),
        pl.BlockSpec((D_MODEL, CB), lambda i, c: (0, n_blocks + c)),
        pl.BlockSpec((FFN_CONV, CB), lambda i, c: (0, c)),
        pl.BlockSpec((FFN_CONV, CB), lambda i, c: (0, n_blocks + c)),
        pl.BlockSpec((1, CB), lambda i, c: (0, c)),
        pl.BlockSpec((1, CB), lambda i, c: (0, n_blocks + c)),
        pl.BlockSpec((CB, D_MODEL), lambda i, c: (c, 0)),
    ]


def _ffn_prompt(x1, ada, p, n_seq, tm, tiles_per_seq, ada_row_block):
    rows = x1.shape[0]
    d_ff = p["w_ffn_down"].shape[0]
    n_blocks = d_ff // CB
    ada_spec = lambda k: pl.BlockSpec((SUBLANES, D_MODEL), lambda i, c: (ada_row_block, k))
    in_specs = [pl.BlockSpec((tm, D_MODEL), lambda i, c: (i, 0), pipeline_mode=pl.Buffered(1)),
                ada_spec(3), ada_spec(4), ada_spec(5)] + _ffn_weight_specs(n_blocks)
    out_specs = [
        pl.BlockSpec((tm, D_MODEL), lambda i, c: (i, 0)),
        pl.BlockSpec((1, FFN_CONV - 1, 2, CB), lambda i, c: (i, 0, 0, c)),
    ]
    out_shape = [jax.ShapeDtypeStruct((rows, D_MODEL), F32),
                 jax.ShapeDtypeStruct((rows // tm, FFN_CONV - 1, 2, d_ff), F32)]
    scratch = [
        pltpu.VMEM((tm, D_MODEL), BF16),
        pltpu.VMEM((SUBLANES + tm, CB), F32),
        pltpu.VMEM((SUBLANES + tm, CB), F32),
        pltpu.VMEM((tm, CB), BF16),
        pltpu.VMEM((n_blocks, SUBLANES, 2 * CB), F32),
    ]
    w = p["w_ffn_up"], p["w_ffn_conv"], p["b_ffn_conv"]
    return pl.pallas_call(
        functools.partial(_ffn_prompt_kernel, tm=tm, tiles_per_seq=tiles_per_seq,
                          n_blocks=n_blocks),
        grid=(rows // tm, n_blocks),
        in_specs=in_specs, out_specs=out_specs, out_shape=out_shape,
        scratch_shapes=scratch, compiler_params=_cparams(2), name="ffn_prompt",
    )(x1, ada, ada, ada, p["g_pre2"], p["g_post2"], w[0], w[0], w[1], w[1], w[2], w[2],
      p["w_ffn_down"])


def _ffn_sample(x1, ada, p, sffn2, n_seq, n_t):
    rows = n_seq * n_t
    d_ff = p["w_ffn_down"].shape[0]
    n_blocks = d_ff // CB
    nk = FFN_CONV - 1
    ada_spec = lambda k: pl.BlockSpec((n_seq, D_MODEL), lambda i, c: (0, k))
    in_specs = [pl.BlockSpec((rows, D_MODEL), lambda i, c: (0, 0), pipeline_mode=pl.Buffered(1)),
                ada_spec(3), ada_spec(4), ada_spec(5)] + _ffn_weight_specs(n_blocks) + [
        pl.BlockSpec((n_seq * nk, CB), lambda i, c: (0, c)),
        pl.BlockSpec((n_seq * nk, CB), lambda i, c: (0, n_blocks + c)),
    ]
    out_specs = [
        pl.BlockSpec((rows, D_MODEL), lambda i, c: (0, 0)),
        pl.BlockSpec((n_seq * nk, CB), lambda i, c: (0, c)),
        pl.BlockSpec((n_seq * nk, CB), lambda i, c: (0, c)),
    ]
    out_shape = [jax.ShapeDtypeStruct((rows, D_MODEL), F32),
                 jax.ShapeDtypeStruct((n_seq * nk, d_ff), F32),
                 jax.ShapeDtypeStruct((n_seq * nk, d_ff), F32)]
    scratch = [
        pltpu.VMEM((rows, D_MODEL), BF16),
        pltpu.VMEM((rows, CB), F32),
        pltpu.VMEM((rows, CB), F32),
        pltpu.VMEM((rows, CB), BF16),
        pltpu.VMEM((rows, D_MODEL), F32),
    ]
    w = p["w_ffn_up"], p["w_ffn_conv"], p["b_ffn_conv"]
    return pl.pallas_call(
        functools.partial(_ffn_sample_kernel, n_seq=n_seq, n_t=n_t, n_blocks=n_blocks),
        grid=(1, n_blocks),
        in_specs=in_specs, out_specs=out_specs, out_shape=out_shape,
        scratch_shapes=scratch, compiler_params=_cparams(2), name="ffn_sample",
    )(x1, ada, ada, ada, p["g_pre2"], p["g_post2"], w[0], w[0], w[1], w[1], w[2], w[2],
      p["w_ffn_down"], sffn2, sffn2)


TOKEN_TILE = 1024
OUTPROJ_TILE = 512


def kernel(x_prompt, x_sample, c_prompt, c_sample, state_pool, state_lru_conv, state_lru_h, state_ffn_conv, w_ada, b_ada, g_pre1, g_post1, g_pre2, g_post2, w_in, w_pool_grp, pool_scale, w_lru_conv, b_lru_conv, w_rg, b_rg, w_ig, b_ig, lru_lambda, w_pool_up, w_lru_up, w_out, w_ffn_up, w_ffn_conv, b_ffn_conv, w_ffn_down):
    batch, seq, d = x_prompt.shape
    dec_batch, dec_seq, _ = x_sample.shape
    depth = w_ada.shape[0]
    assert d == D_MODEL and dec_batch % SUBLANES == 0 and seq % TOKEN_TILE == 0
    assert w_in.shape[2] == N_MIX_BLOCKS * CB + 2 * D_MODEL

    pad = (-batch) % SUBLANES
    c_all = jnp.concatenate([c_sample, c_prompt, jnp.zeros((pad, d), c_prompt.dtype)], axis=0)
    prompt_row_block = dec_batch // SUBLANES

    vec_names = ("g_pre1", "g_post1", "g_pre2", "g_post2", "pool_scale", "b_lru_conv", "b_rg",
                 "b_ig", "lru_lambda", "b_ffn_conv")
    weights = dict(w_ada=w_ada, b_ada=b_ada, g_pre1=g_pre1, g_post1=g_post1, g_pre2=g_pre2,
                   g_post2=g_post2, w_in=w_in, w_pool_grp=w_pool_grp, pool_scale=pool_scale,
                   w_lru_conv=w_lru_conv, b_lru_conv=b_lru_conv, w_rg=w_rg, b_rg=b_rg, w_ig=w_ig,
                   b_ig=b_ig, lru_lambda=lru_lambda, w_pool_up=w_pool_up, w_lru_up=w_lru_up,
                   w_out=w_out, w_ffn_up=w_ffn_up, w_ffn_conv=w_ffn_conv, b_ffn_conv=b_ffn_conv,
                   w_ffn_down=w_ffn_down)

    def time_major(a):
        return jnp.swapaxes(a, 0, 1).reshape(-1, a.shape[-1])

    def seq_major(a2, n_rows):
        return jnp.swapaxes(a2.reshape(n_rows, dec_batch, -1), 0, 1)

    xp = x_prompt.reshape(batch * seq, d)
    xs = time_major(x_sample)
    tps = seq // TOKEN_TILE
    outs_p = ([], [], [], [])
    outs_s = ([], [], [], [])
    for l in range(depth):
        p = {k: v[l] for k, v in weights.items()}
        for k in vec_names + ("b_ada",):
            p[k] = p[k].reshape(1, -1)
        ada = _ada(c_all, p["w_ada"], p["b_ada"])

        y, npool, nconv, nh, h = _mix_prompt(xp, ada, p, batch, seq, prompt_row_block)
        m = _merge(h, y, p, TOKEN_TILE, "merge_prompt")
        x1 = _outproj(m, xp, ada, p, OUTPROJ_TILE, seq // OUTPROJ_TILE, prompt_row_block, 0)
        xp, nffn = _ffn_prompt(x1, ada, p, batch, TOKEN_TILE, tps, prompt_row_block)
        outs_p[0].append(npool)
        outs_p[1].append(nconv)
        outs_p[2].append(nh.reshape(batch, LRU_WIDTH))
        outs_p[3].append(nffn[tps - 1::tps].reshape(batch, FFN_CONV - 1, -1))

        rows_s = dec_batch * dec_seq
        y, npool, nconv, nh, h = _mix_sample(
            xs, ada, p, time_major(state_pool[l]), time_major(state_lru_conv[l]),
            state_lru_h[l], dec_batch, dec_seq, PAST_LEN)
        m = _merge(h, y, p, rows_s, "merge_sample")
        x1 = _outproj(m, xs, ada, p, rows_s, 1, 0, dec_seq)
        xs, ng, nv = _ffn_sample(x1, ada, p, time_major(state_ffn_conv[l]), dec_batch, dec_seq)
        outs_s[0].append(seq_major(npool, POOL_BUF))
        outs_s[1].append(seq_major(nconv, LRU_CONV - 1))
        outs_s[2].append(nh)
        outs_s[3].append(jnp.concatenate([seq_major(ng, FFN_CONV - 1),
                                          seq_major(nv, FFN_CONV - 1)], axis=-1))

    return (xp.reshape(batch, seq, d), seq_major(xs, dec_seq),
            jnp.stack(outs_p[0]), jnp.stack(outs_p[1]), jnp.stack(outs_p[2]), jnp.stack(outs_p[3]),
            jnp.stack(outs_s[0]), jnp.stack(outs_s[1]), jnp.stack(outs_s[2]), jnp.stack(outs_s[3]))
```

```python
import functools

import jax
import jax.numpy as jnp
from jax import lax
from jax.experimental import pallas as pl
from jax.experimental.pallas import tpu as pltpu

F32 = jnp.float32
BF16 = jnp.bfloat16

D_MODEL = 2048
POOL_WINDOWS = (2, 4, 8, 16)
POOL_GROUPS = len(POOL_WINDOWS)
POOL_BUF = max(POOL_WINDOWS) - 1
LRU_CONV = 4
LRU_C = 8.0
PAST_LEN = 16384
FFN_CONV = 3
N_ADA = 6
EPS = 1e-6

CB = 256
POOL_WIDTH = POOL_GROUPS * CB
LRU_BLOCKS = 8
LRU_WIDTH = LRU_BLOCKS * CB
N_MIX_BLOCKS = POOL_GROUPS + LRU_BLOCKS
HALO = 16
SUBLANES = 8
VMEM_LIMIT = 60 * 1024 * 1024


def _cparams(n_axes):
    return pltpu.CompilerParams(
        dimension_semantics=("arbitrary",) * n_axes, vmem_limit_bytes=VMEM_LIMIT)


def _dot(a, b):
    return jnp.dot(a, b, preferred_element_type=F32)


def _softplus(z):
    return jnp.maximum(z, 0.0) + jnp.log1p(jnp.exp(-jnp.abs(z)))


def _lru_coeffs(xc, r_pre, i_pre, neg_c_sp):
    r = jax.nn.sigmoid(r_pre)
    i = jax.nn.sigmoid(i_pre)
    log_a = r * neg_c_sp
    a = jnp.exp(log_a)
    b = jnp.sqrt(-jnp.tanh(log_a) * (a * a + 1.0)) * (i * xc)
    return a, b


def _ada_kernel(c_ref, w_ref, b_ref, o_ref):
    c = c_ref[...]
    s = (c * jax.nn.sigmoid(c)).astype(BF16)
    o_ref[...] = _dot(s, w_ref[...].astype(BF16)) + b_ref[...]


def _ada(c_all, w_ada, b_ada):
    rows = c_all.shape[0]
    tn = 1024
    return pl.pallas_call(
        _ada_kernel,
        grid=(N_ADA * D_MODEL // tn,),
        in_specs=[pl.BlockSpec((rows, D_MODEL), lambda n: (0, 0)),
                  pl.BlockSpec((D_MODEL, tn), lambda n: (0, n)),
                  pl.BlockSpec((1, tn), lambda n: (0, n))],
        out_specs=pl.BlockSpec((rows, tn), lambda n: (0, n)),
        out_shape=jax.ShapeDtypeStruct((rows, N_ADA * D_MODEL), F32),
        compiler_params=_cparams(1),
        name="ada",
    )(c_all, w_ada, b_ada)


NORM_ROWS = 32
NORM_UNROLL = 4
OUT_SUBTILE = 256


def _unit_rms(x):
    return x * lax.rsqrt(jnp.mean(x * x, axis=-1, keepdims=True) + EPS)


def _build_h_prompt(x_ref, h_ref, g, scale1p, shift, n_rows):
    gs = g * scale1p

    def body(i, carry):
        r0 = pl.multiple_of(i * NORM_ROWS, NORM_ROWS)
        x = x_ref[pl.ds(r0, NORM_ROWS), :]
        h_ref[pl.ds(r0, NORM_ROWS), :] = (_unit_rms(x) * gs + shift).astype(BF16)
        return carry
    lax.fori_loop(0, n_rows // NORM_ROWS, body, 0, unroll=NORM_UNROLL)


def _residual_norm_prompt(x_ref, acc_ref, o_ref, g, gate, n_rows):
    gg = g * gate

    def body(i, carry):
        sl = pl.ds(pl.multiple_of(i * NORM_ROWS, NORM_ROWS), NORM_ROWS)
        o_ref[sl, :] = x_ref[sl, :] + _unit_rms(acc_ref[sl, :]) * gg
        return carry
    lax.fori_loop(0, n_rows // NORM_ROWS, body, 0, unroll=NORM_UNROLL)


def _build_h_sample(x_ref, h_ref, g, sc_ref, sh_ref, n_seq, n_t):
    for s0 in range(0, n_seq, NORM_ROWS):
        gs = g * (1.0 + sc_ref[s0:s0 + NORM_ROWS, :])
        shift = sh_ref[s0:s0 + NORM_ROWS, :]
        for t in range(n_t):
            sl = slice(t * n_seq + s0, t * n_seq + s0 + NORM_ROWS)
            h_ref[sl, :] = (_unit_rms(x_ref[sl, :]) * gs + shift).astype(BF16)


MIX_ROWS = 128


def _mix_prompt_kernel(x_ref, sh_ref, sc_ref, g_ref, win_ref, wgrp_ref, pscale_ref,
                       wconv_ref, bconv_ref, wrg_ref, brg_ref, wig_ref, big_ref, lam_ref,
                       y_ref, npool_ref, nconv_ref, nh_ref, h_ref,
                       ext_ref, xb_ref, r_ref, i_ref, a_ref, u_ref, *, seq):
    q = pl.program_id(0)
    j = pl.program_id(1)

    @pl.when(j == 0)
    def _():
        _build_h_prompt(x_ref, h_ref, g_ref[...], 1.0 + sc_ref[pl.ds(q, 1), :],
                        sh_ref[pl.ds(q, 1), :], seq)
        ext_ref[0:HALO, :] = jnp.zeros((HALO, CB), F32)

    ext_ref[HALO:HALO + seq, :] = _dot(h_ref[...], win_ref[...].astype(BF16))

    def pool_branch(w):
        npool_ref[0] = ext_ref[pl.ds(HALO + seq - POOL_BUF, POOL_BUF), :]
        for r0 in range(0, seq, MIX_ROWS):
            u = ext_ref[HALO + r0:HALO + r0 + MIX_ROWS, :]
            s = u
            for k in range(1, w):
                s = s + ext_ref[HALO + r0 - k:HALO + r0 - k + MIX_ROWS, :]
            if r0 < w:
                pos = r0 + lax.broadcasted_iota(jnp.int32, (MIX_ROWS, 1), 0)
                cnt = jnp.minimum(w, pos + 1).astype(F32)
            else:
                cnt = float(w)
            xb_ref[r0:r0 + MIX_ROWS, :] = (s / cnt - u).astype(BF16)
        y = _dot(xb_ref[...], wgrp_ref[0].astype(BF16)) * pscale_ref[...]
        y_ref[...] = y.astype(BF16)

    for g, w in enumerate(POOL_WINDOWS):
        pl.when(j == g)(functools.partial(pool_branch, w))

    @pl.when(j >= POOL_GROUPS)
    def _():
        nconv_ref[0] = ext_ref[pl.ds(HALO + seq - (LRU_CONV - 1), LRU_CONV - 1), :]
        wc = wconv_ref[...]
        bc = bconv_ref[...]
        for r0 in range(0, seq, MIX_ROWS):
            xc = bc
            for k in range(LRU_CONV):
                o = HALO + r0 - (LRU_CONV - 1) + k
                xc = xc + wc[k:k + 1, :] * ext_ref[o:o + MIX_ROWS, :]
            a_ref[r0:r0 + MIX_ROWS, :] = xc
            xb_ref[r0:r0 + MIX_ROWS, :] = xc.astype(BF16)
        r_ref[...] = _dot(xb_ref[...], wrg_ref[0].astype(BF16))
        i_ref[...] = _dot(xb_ref[...], wig_ref[0].astype(BF16))
        neg_c_sp = (-LRU_C) * _softplus(-lam_ref[...])
        brg = brg_ref[...]
        big = big_ref[...]
        nb = MIX_ROWS // SUBLANES
        row = lax.broadcasted_iota(jnp.int32, (nb, SUBLANES, CB), 1)
        for r0 in range(0, seq, MIX_ROWS):
            sl = slice(r0, r0 + MIX_ROWS)
            a, b = _lru_coeffs(a_ref[sl, :], r_ref[sl, :] + brg, i_ref[sl, :] + big, neg_c_sp)
            a = a.reshape(nb, SUBLANES, CB)
            b = b.reshape(nb, SUBLANES, CB)
            for k in (1, 2, 4):
                a_sh = jnp.where(row >= k, pltpu.roll(a, k, 1), 1.0)
                b_sh = jnp.where(row >= k, pltpu.roll(b, k, 1), 0.0)
                b = b + a * b_sh
                a = a * a_sh
            a_ref[sl, :] = a.reshape(MIX_ROWS, CB)
            u_ref[sl, :] = b.reshape(MIX_ROWS, CB)

        def carry_body(blk, h_prev):
            r0 = pl.multiple_of(blk * SUBLANES, SUBLANES)
            h = a_ref[pl.ds(r0, SUBLANES), :] * h_prev + u_ref[pl.ds(r0, SUBLANES), :]
            u_ref[pl.ds(r0, SUBLANES), :] = h
            return h[SUBLANES - 1:SUBLANES, :]

        h_last = lax.fori_loop(0, seq // SUBLANES, carry_body, jnp.zeros((1, CB), F32), unroll=8)
        nh_ref[0] = h_last
        y_ref[...] = u_ref[...].astype(BF16)


def _mix_sample_kernel(x_ref, sh_ref, sc_ref, g_ref, win_ref, wgrp_ref, pscale_ref,
                       wconv_ref, bconv_ref, wrg_ref, brg_ref, wig_ref, big_ref, lam_ref,
                       spool_ref, sconv_ref, sh0_ref,
                       y_ref, npool_ref, nconv_ref, nh_ref, h_ref,
                       u_ref, d_ref, *, n_seq, n_t, start):
    j = pl.program_id(1)

    @pl.when(j == 0)
    def _():
        _build_h_sample(x_ref, h_ref, g_ref[...], sc_ref, sh_ref, n_seq, n_t)

    u_ref[...] = _dot(h_ref[...], win_ref[...].astype(BF16))

    def u_slab(t):
        return u_ref[t * n_seq:(t + 1) * n_seq, :]

    def pool_branch(w):
        e = [spool_ref[k * n_seq:(k + 1) * n_seq, :] for k in range(POOL_BUF)]
        e += [u_slab(t) for t in range(n_t)]
        for k in range(POOL_BUF):
            npool_ref[k * n_seq:(k + 1) * n_seq, :] = e[n_t + k]
        for t in range(n_t):
            s = e[POOL_BUF + t]
            for k in range(1, w):
                s = s + e[POOL_BUF + t - k]
            cnt = float(min(w, start + t + 1))
            d_ref[t * n_seq:(t + 1) * n_seq, :] = (s / cnt - e[POOL_BUF + t]).astype(BF16)
        y = _dot(d_ref[...], wgrp_ref[0].astype(BF16)) * pscale_ref[...]
        y_ref[...] = y.astype(BF16)

    for g, w in enumerate(POOL_WINDOWS):
        pl.when(j == g)(functools.partial(pool_branch, w))

    @pl.when(j >= POOL_GROUPS)
    def _():
        nk = LRU_CONV - 1
        e = [sconv_ref[k * n_seq:(k + 1) * n_seq, :] for k in range(nk)]
        e += [u_slab(t) for t in range(n_t)]
        for k in range(nk):
            nconv_ref[k * n_seq:(k + 1) * n_seq, :] = e[n_t + k]
        wc = wconv_ref[...]
        bc = bconv_ref[...]
        wrg = wrg_ref[0].astype(BF16)
        wig = wig_ref[0].astype(BF16)
        neg_c_sp = (-LRU_C) * _softplus(-lam_ref[...])
        h = sh0_ref[...]
        for t in range(n_t):
            xc = bc
            for k in range(LRU_CONV):
                xc = xc + wc[k:k + 1, :] * e[t + k]
            xb = xc.astype(BF16)
            a, b = _lru_coeffs(xc, _dot(xb, wrg) + brg_ref[...], _dot(xb, wig) + big_ref[...],
                               neg_c_sp)
            h = a * h + b
            y_ref[t * n_seq:(t + 1) * n_seq, :] = h.astype(BF16)
        nh_ref[...] = h


def _mix_weight_specs():
    pj = lambda j: jnp.minimum(j, POOL_GROUPS - 1)
    lj = lambda j: jnp.maximum(j - POOL_GROUPS, 0)
    return [
        pl.BlockSpec((1, D_MODEL), lambda q, j: (0, 0)),
        pl.BlockSpec((D_MODEL, CB), lambda q, j: (0, j)),
        pl.BlockSpec((1, CB, CB), lambda q, j: (pj(j), 0, 0)),
        pl.BlockSpec((1, CB), lambda q, j: (0, pj(j))),
        pl.BlockSpec((LRU_CONV, CB), lambda q, j: (0, lj(j))),
        pl.BlockSpec((1, CB), lambda q, j: (0, lj(j))),
        pl.BlockSpec((1, CB, CB), lambda q, j: (lj(j), 0, 0)),
        pl.BlockSpec((1, CB), lambda q, j: (0, lj(j))),
        pl.BlockSpec((1, CB, CB), lambda q, j: (lj(j), 0, 0)),
        pl.BlockSpec((1, CB), lambda q, j: (0, lj(j))),
        pl.BlockSpec((1, CB), lambda q, j: (0, lj(j))),
    ], pj, lj


def _mix_prompt(x2, ada, p, n_seq, seq, ada_row_block):
    wspecs, pj, lj = _mix_weight_specs()
    in_specs = [
        pl.BlockSpec((seq, D_MODEL), lambda q, j: (q, 0), pipeline_mode=pl.Buffered(1)),
        pl.BlockSpec((SUBLANES, D_MODEL), lambda q, j: (ada_row_block, 0)),
        pl.BlockSpec((SUBLANES, D_MODEL), lambda q, j: (ada_row_block, 1)),
    ] + wspecs
    out_specs = [
        pl.BlockSpec((seq, CB), lambda q, j: (q, j)),
        pl.BlockSpec((1, POOL_BUF, CB), lambda q, j: (q, 0, pj(j))),
        pl.BlockSpec((1, LRU_CONV - 1, CB), lambda q, j: (q, 0, lj(j))),
        pl.BlockSpec((1, 1, CB), lambda q, j: (q, 0, lj(j))),
        pl.BlockSpec((seq, D_MODEL), lambda q, j: (q, 0)),
    ]
    out_shape = [
        jax.ShapeDtypeStruct((n_seq * seq, N_MIX_BLOCKS * CB), BF16),
        jax.ShapeDtypeStruct((n_seq, POOL_BUF, POOL_WIDTH), F32),
        jax.ShapeDtypeStruct((n_seq, LRU_CONV - 1, LRU_WIDTH), F32),
        jax.ShapeDtypeStruct((n_seq, 1, LRU_WIDTH), F32),
        jax.ShapeDtypeStruct((n_seq * seq, D_MODEL), BF16),
    ]
    scratch = [
        pltpu.VMEM((HALO + seq, CB), F32),
        pltpu.VMEM((seq, CB), BF16),
        pltpu.VMEM((seq, CB), F32),
        pltpu.VMEM((seq, CB), F32),
        pltpu.VMEM((seq, CB), F32),
        pltpu.VMEM((seq, CB), F32),
    ]
    return pl.pallas_call(
        functools.partial(_mix_prompt_kernel, seq=seq),
        grid=(n_seq, N_MIX_BLOCKS),
        in_specs=in_specs, out_specs=out_specs, out_shape=out_shape,
        scratch_shapes=scratch, compiler_params=_cparams(2), name="mix_prompt",
    )(x2, ada, ada, p["g_pre1"], p["w_in"], p["w_pool_grp"], p["pool_scale"],
      p["w_lru_conv"], p["b_lru_conv"], p["w_rg"], p["b_rg"], p["w_ig"], p["b_ig"],
      p["lru_lambda"])


def _mix_sample(x2, ada, p, spool2, sconv2, sh0, n_seq, n_t, start):
    wspecs, pj, lj = _mix_weight_specs()
    rows = n_seq * n_t
    in_specs = [
        pl.BlockSpec((rows, D_MODEL), lambda q, j: (0, 0), pipeline_mode=pl.Buffered(1)),
        pl.BlockSpec((n_seq, D_MODEL), lambda q, j: (0, 0)),
        pl.BlockSpec((n_seq, D_MODEL), lambda q, j: (0, 1)),
    ] + wspecs + [
        pl.BlockSpec((n_seq * POOL_BUF, CB), lambda q, j: (0, pj(j))),
        pl.BlockSpec((n_seq * (LRU_CONV - 1), CB), lambda q, j: (0, lj(j))),
        pl.BlockSpec((n_seq, CB), lambda q, j: (0, lj(j))),
    ]
    out_specs = [
        pl.BlockSpec((rows, CB), lambda q, j: (0, j)),
        pl.BlockSpec((n_seq * POOL_BUF, CB), lambda q, j: (0, pj(j))),
        pl.BlockSpec((n_seq * (LRU_CONV - 1), CB), lambda q, j: (0, lj(j))),
        pl.BlockSpec((n_seq, CB), lambda q, j: (0, lj(j))),
        pl.BlockSpec((rows, D_MODEL), lambda q, j: (0, 0)),
    ]
    out_shape = [
        jax.ShapeDtypeStruct((rows, N_MIX_BLOCKS * CB), BF16),
        jax.ShapeDtypeStruct((n_seq * POOL_BUF, POOL_WIDTH), F32),
        jax.ShapeDtypeStruct((n_seq * (LRU_CONV - 1), LRU_WIDTH), F32),
        jax.ShapeDtypeStruct((n_seq, LRU_WIDTH), F32),
        jax.ShapeDtypeStruct((rows, D_MODEL), BF16),
    ]
    scratch = [
        pltpu.VMEM((rows, CB), F32),
        pltpu.VMEM((rows, CB), BF16),
    ]
    return pl.pallas_call(
        functools.partial(_mix_sample_kernel, n_seq=n_seq, n_t=n_t, start=start),
        grid=(1, N_MIX_BLOCKS),
        in_specs=in_specs, out_specs=out_specs, out_shape=out_shape,
        scratch_shapes=scratch, compiler_params=_cparams(2), name="mix_sample",
    )(x2, ada, ada, p["g_pre1"], p["w_in"], p["w_pool_grp"], p["pool_scale"],
      p["w_lru_conv"], p["b_lru_conv"], p["w_rg"], p["b_rg"], p["w_ig"], p["b_ig"],
      p["lru_lambda"], spool2, sconv2, sh0)


MERGE_ROWS = 64
MERGE_SUBTILE = 256


def _merge_kernel(h_ref, y_ref, wgp_ref, wgl_ref, wpu_ref, wlu_ref,
                  m_ref, gp_ref, gl_ref, pu_ref, lu_ref, *, tm):
    wgp = wgp_ref[...].astype(BF16)
    wgl = wgl_ref[...].astype(BF16)
    wpu = wpu_ref[...].astype(BF16)
    wlu = wlu_ref[...].astype(BF16)

    def proj(m0):
        rows = slice(m0, m0 + MERGE_SUBTILE)
        gp_ref[rows, :] = _dot(h_ref[rows, :], wgp)
        gl_ref[rows, :] = _dot(h_ref[rows, :], wgl)
        pu_ref[rows, :] = _dot(y_ref[rows, 0:POOL_WIDTH], wpu)
        lu_ref[rows, :] = _dot(y_ref[rows, POOL_WIDTH:POOL_WIDTH + LRU_WIDTH], wlu)

    proj(0)
    for m0 in range(0, tm, MERGE_SUBTILE):
        if m0 + MERGE_SUBTILE < tm:
            proj(m0 + MERGE_SUBTILE)
        for r0 in range(m0, m0 + MERGE_SUBTILE, MERGE_ROWS):
            sl = slice(r0, r0 + MERGE_ROWS)
            m = (jax.nn.sigmoid(gp_ref[sl, :]) * pu_ref[sl, :]
                 + jax.nn.sigmoid(gl_ref[sl, :]) * lu_ref[sl, :])
            m_ref[sl, :] = m.astype(BF16)


def _merge(h, y, p, tm, name):
    rows = h.shape[0]
    n_gate0 = N_MIX_BLOCKS
    n_gate1 = N_MIX_BLOCKS + D_MODEL // CB
    in_specs = [
        pl.BlockSpec((tm, D_MODEL), lambda i, c: (i, 0), pipeline_mode=pl.Buffered(1)),
        pl.BlockSpec((tm, N_MIX_BLOCKS * CB), lambda i, c: (i, 0), pipeline_mode=pl.Buffered(1)),
        pl.BlockSpec((D_MODEL, CB), lambda i, c: (0, n_gate0 + c)),
        pl.BlockSpec((D_MODEL, CB), lambda i, c: (0, n_gate1 + c)),
        pl.BlockSpec((POOL_WIDTH, CB), lambda i, c: (0, c)),
        pl.BlockSpec((LRU_WIDTH, CB), lambda i, c: (0, c)),
    ]
    return pl.pallas_call(
        functools.partial(_merge_kernel, tm=tm),
        grid=(rows // tm, D_MODEL // CB),
        in_specs=in_specs,
        out_specs=pl.BlockSpec((tm, CB), lambda i, c: (i, c)),
        out_shape=jax.ShapeDtypeStruct((rows, D_MODEL), BF16),
        scratch_shapes=[pltpu.VMEM((tm, CB), F32)] * 4, compiler_params=_cparams(2),
        name=name,
    )(h, y, p["w_in"], p["w_in"], p["w_pool_up"], p["w_lru_up"])


def _cast_weight(w_ref, wb_ref):
    rows = w_ref.shape[0]
    for r0 in range(0, rows, 256):
        wb_ref[r0:r0 + 256, :] = w_ref[r0:r0 + 256, :].astype(BF16)


def _outproj_kernel(m_ref, x_ref, gate_ref, g_ref, w_ref, o_ref, wb_ref, acc_ref,
                    *, tm, tiles_per_seq, sample_t):
    i = pl.program_id(0)

    @pl.when(i == 0)
    def _():
        _cast_weight(w_ref, wb_ref)

    g = g_ref[...]
    if sample_t:
        n_seq = tm // sample_t
    else:
        gg_tile = g * gate_ref[pl.ds(i // tiles_per_seq, 1), :]

    def proj(m0):
        acc_ref[m0:m0 + OUT_SUBTILE, :] = _dot(m_ref[m0:m0 + OUT_SUBTILE, :], wb_ref[...])

    proj(0)
    for m0 in range(0, tm, OUT_SUBTILE):
        if m0 + OUT_SUBTILE < tm:
            proj(m0 + OUT_SUBTILE)
        for r0 in range(m0, m0 + OUT_SUBTILE, NORM_ROWS):
            sl = slice(r0, r0 + NORM_ROWS)
            if sample_t:
                s0 = r0 % n_seq
                gg = g * gate_ref[s0:s0 + NORM_ROWS, :]
            else:
                gg = gg_tile
            o_ref[sl, :] = x_ref[sl, :] + _unit_rms(acc_ref[sl, :]) * gg


def _outproj(m, x2, ada, p, tm, tiles_per_seq, ada_row_block, sample_t):
    rows = x2.shape[0]
    if sample_t:
        gate_spec = pl.BlockSpec((tm // sample_t, D_MODEL), lambda i: (0, 2))
    else:
        gate_spec = pl.BlockSpec((SUBLANES, D_MODEL), lambda i: (ada_row_block, 2))
    return pl.pallas_call(
        functools.partial(_outproj_kernel, tm=tm, tiles_per_seq=tiles_per_seq, sample_t=sample_t),
        grid=(rows // tm,),
        in_specs=[pl.BlockSpec((tm, D_MODEL), lambda i: (i, 0)),
                  pl.BlockSpec((tm, D_MODEL), lambda i: (i, 0)),
                  gate_spec,
                  pl.BlockSpec((1, D_MODEL), lambda i: (0, 0)),
                  pl.BlockSpec((D_MODEL, D_MODEL), lambda i: (0, 0), pipeline_mode=pl.Buffered(1))],
        out_specs=pl.BlockSpec((tm, D_MODEL), lambda i: (i, 0)),
        out_shape=jax.ShapeDtypeStruct((rows, D_MODEL), F32),
        scratch_shapes=[pltpu.VMEM((D_MODEL, D_MODEL), BF16), pltpu.VMEM((tm, D_MODEL), F32)],
        compiler_params=_cparams(1),
        name="outproj_sample" if sample_t else "outproj_prompt",
    )(m, x2, ada, p["g_post1"], p["w_out"])


FFN_ROWS = 32
FFN_SUBTILE = 256
FFN_CB = 512


def _ffn_conv_gate(eg, ev, wcg, wcv, bcg, bcv):
    g = bcg
    v = bcv
    for k in range(FFN_CONV):
        g = g + wcg[k:k + 1, :] * eg[k]
        v = v + wcv[k:k + 1, :] * ev[k]
    return (jax.nn.gelu(g, approximate=True) * v).astype(BF16)


def _ffn_prompt_kernel(x_ref, sh_ref, sc_ref, gate_ref, gpre_ref, gpost_ref,
                       wup_ref, wcg_ref, wcv_ref, bcg_ref, bcv_ref, wdn_ref,
                       o_ref, nst_ref,
                       h_ref, ext_ref, f_ref, carry_ref, *, tm, tiles_per_seq, n_blocks, cb):
    i = pl.program_id(0)
    c = pl.program_id(1)
    q = i // tiles_per_seq
    first = (i % tiles_per_seq) == 0
    nk = FFN_CONV - 1

    @pl.when(c == 0)
    def _():
        _build_h_prompt(x_ref, h_ref, gpre_ref[...], 1.0 + sc_ref[pl.ds(q, 1), :],
                        sh_ref[pl.ds(q, 1), :], tm)
        o_ref[...] = jnp.zeros((tm, D_MODEL), F32)

    @pl.when(first)
    def _():
        ext_ref[0:SUBLANES, :] = jnp.zeros((SUBLANES, 2 * cb), F32)

    @pl.when(jnp.logical_not(first))
    def _():
        ext_ref[0:SUBLANES, :] = carry_ref[c]

    wcg = wcg_ref[...]
    wcv = wcv_ref[...]
    bcg = bcg_ref[...]
    bcv = bcv_ref[...]

    def up_proj(m0):
        ext_ref[SUBLANES + m0:SUBLANES + m0 + FFN_SUBTILE, :] = _dot(
            h_ref[m0:m0 + FFN_SUBTILE, :], wup_ref[0])

    up_proj(0)
    for m0 in range(0, tm, FFN_SUBTILE):
        rows = slice(m0, m0 + FFN_SUBTILE)
        if m0 + FFN_SUBTILE < tm:
            up_proj(m0 + FFN_SUBTILE)
        for r0 in range(m0, m0 + FFN_SUBTILE, FFN_ROWS):
            shifted = [slice(SUBLANES + r0 - nk + k, SUBLANES + r0 - nk + k + FFN_ROWS)
                       for k in range(FFN_CONV)]
            eg = [ext_ref[s, 0:cb] for s in shifted]
            ev = [ext_ref[s, cb:2 * cb] for s in shifted]
            f_ref[r0:r0 + FFN_ROWS, :] = _ffn_conv_gate(eg, ev, wcg, wcv, bcg, bcv)
        o_ref[rows, :] += _dot(f_ref[rows, :], wdn_ref[...])

    carry_ref[c] = ext_ref[tm:tm + SUBLANES, :]
    nst_ref[0, :, 0, :] = ext_ref[pl.ds(SUBLANES + tm - nk, nk), 0:cb]
    nst_ref[0, :, 1, :] = ext_ref[pl.ds(SUBLANES + tm - nk, nk), cb:2 * cb]

    @pl.when(c == n_blocks - 1)
    def _():
        _residual_norm_prompt(x_ref, o_ref, o_ref, gpost_ref[...], gate_ref[pl.ds(q, 1), :], tm)


def _ffn_sample_kernel(x_ref, sh_ref, sc_ref, gate_ref, gpre_ref, gpost_ref,
                       wup_ref, wcg_ref, wcv_ref, bcg_ref, bcv_ref, wdn_ref,
                       sg_ref, sv_ref,
                       o_ref, ng_ref, nv_ref,
                       h_ref, ext_ref, f_ref, acc_ref, *, n_seq, n_t, n_blocks, cb):
    c = pl.program_id(1)
    nk = FFN_CONV - 1

    @pl.when(c == 0)
    def _():
        _build_h_sample(x_ref, h_ref, gpre_ref[...], sc_ref, sh_ref, n_seq, n_t)
        acc_ref[...] = jnp.zeros((n_seq * n_t, D_MODEL), F32)

    wcg = wcg_ref[...]
    wcv = wcv_ref[...]
    bcg = bcg_ref[...]
    bcv = bcv_ref[...]

    def view(state_ref, half, t, s0, n):
        if t < 0:
            return state_ref[(nk + t) * n_seq + s0:(nk + t) * n_seq + s0 + n, :]
        return ext_ref[t * n_seq + s0:t * n_seq + s0 + n, half * cb:(half + 1) * cb]

    t_sub = max(FFN_SUBTILE // n_seq, 1)

    def up_proj(t0):
        rows = slice(t0 * n_seq, (t0 + t_sub) * n_seq)
        ext_ref[rows, :] = _dot(h_ref[rows, :], wup_ref[0])

    up_proj(0)
    for t0 in range(0, n_t, t_sub):
        rows = slice(t0 * n_seq, (t0 + t_sub) * n_seq)
        if t0 + t_sub < n_t:
            up_proj(t0 + t_sub)
        for t in range(t0, t0 + t_sub):
            for s0 in range(0, n_seq, FFN_ROWS):
                eg = [view(sg_ref, 0, t - nk + k, s0, FFN_ROWS) for k in range(FFN_CONV)]
                ev = [view(sv_ref, 1, t - nk + k, s0, FFN_ROWS) for k in range(FFN_CONV)]
                f_ref[t * n_seq + s0:t * n_seq + s0 + FFN_ROWS, :] = _ffn_conv_gate(
                    eg, ev, wcg, wcv, bcg, bcv)
        acc_ref[rows, :] += _dot(f_ref[rows, :], wdn_ref[...])

    for k in range(nk):
        ng_ref[k * n_seq:(k + 1) * n_seq, :] = view(sg_ref, 0, n_t - nk + k, 0, n_seq)
        nv_ref[k * n_seq:(k + 1) * n_seq, :] = view(sv_ref, 1, n_t - nk + k, 0, n_seq)

    @pl.when(c == n_blocks - 1)
    def _():
        g = gpost_ref[...]
        for s0 in range(0, n_seq, NORM_ROWS):
            gg = g * gate_ref[s0:s0 + NORM_ROWS, :]
            for t in range(n_t):
                sl = slice(t * n_seq + s0, t * n_seq + s0 + NORM_ROWS)
                o_ref[sl, :] = x_ref[sl, :] + _unit_rms(acc_ref[sl, :]) * gg


CAST_ROWS = 256


def _cast_up_kernel(wg_ref, wv_ref, o_ref):
    cb = wg_ref.shape[1]
    for r0 in range(0, wg_ref.shape[0], CAST_ROWS):
        rows = slice(r0, r0 + CAST_ROWS)
        o_ref[0, rows, 0:cb] = wg_ref[rows, :].astype(BF16)
        o_ref[0, rows, cb:2 * cb] = wv_ref[rows, :].astype(BF16)


def _cast_kernel(w_ref, o_ref):
    for r0 in range(0, w_ref.shape[0], CAST_ROWS):
        o_ref[r0:r0 + CAST_ROWS, :] = w_ref[r0:r0 + CAST_ROWS, :].astype(BF16)


def _ffn_cast_weights(w_up, w_down, cb):
    d_ff = w_down.shape[0]
    n_blocks = d_ff // cb
    up = pl.pallas_call(
        _cast_up_kernel,
        grid=(n_blocks,),
        in_specs=[pl.BlockSpec((D_MODEL, cb), lambda c: (0, c)),
                  pl.BlockSpec((D_MODEL, cb), lambda c: (0, n_blocks + c))],
        out_specs=pl.BlockSpec((1, D_MODEL, 2 * cb), lambda c: (c, 0, 0)),
        out_shape=jax.ShapeDtypeStruct((n_blocks, D_MODEL, 2 * cb), BF16),
        compiler_params=_cparams(1), name="cast_ffn_up",
    )(w_up, w_up)
    down = pl.pallas_call(
        _cast_kernel,
        grid=(n_blocks,),
        in_specs=[pl.BlockSpec((cb, D_MODEL), lambda c: (c, 0))],
        out_specs=pl.BlockSpec((cb, D_MODEL), lambda c: (c, 0)),
        out_shape=jax.ShapeDtypeStruct((d_ff, D_MODEL), BF16),
        compiler_params=_cparams(1), name="cast_ffn_down",
    )(w_down)
    return up, down


def _ffn_weight_specs(n_blocks, cb):
    return [
        pl.BlockSpec((1, D_MODEL), lambda i, c: (0, 0)),
        pl.BlockSpec((1, D_MODEL), lambda i, c: (0, 0)),
        pl.BlockSpec((1, D_MODEL, 2 * cb), lambda i, c: (c, 0, 0)),
        pl.BlockSpec((FFN_CONV, cb), lambda i, c: (0, c)),
        pl.BlockSpec((FFN_CONV, cb), lambda i, c: (0, n_blocks + c)),
        pl.BlockSpec((1, cb), lambda i, c: (0, c)),
        pl.BlockSpec((1, cb), lambda i, c: (0, n_blocks + c)),
        pl.BlockSpec((cb, D_MODEL), lambda i, c: (c, 0)),
    ]


def _ffn_prompt(x1, ada, p, wup_b, wdn_b, n_seq, tm, tiles_per_seq, ada_row_block):
    rows = x1.shape[0]
    n_blocks, _, cb2 = wup_b.shape
    cb = cb2 // 2
    ada_spec = lambda k: pl.BlockSpec((SUBLANES, D_MODEL), lambda i, c: (ada_row_block, k))
    in_specs = [pl.BlockSpec((tm, D_MODEL), lambda i, c: (i, 0), pipeline_mode=pl.Buffered(1)),
                ada_spec(3), ada_spec(4), ada_spec(5)] + _ffn_weight_specs(n_blocks, cb)
    out_specs = [
        pl.BlockSpec((tm, D_MODEL), lambda i, c: (i, 0)),
        pl.BlockSpec((1, FFN_CONV - 1, 2, cb), lambda i, c: (i, 0, 0, c)),
    ]
    out_shape = [jax.ShapeDtypeStruct((rows, D_MODEL), F32),
                 jax.ShapeDtypeStruct((rows // tm, FFN_CONV - 1, 2, n_blocks * cb), F32)]
    scratch = [
        pltpu.VMEM((tm, D_MODEL), BF16),
        pltpu.VMEM((SUBLANES + tm, 2 * cb), F32),
        pltpu.VMEM((tm, cb), BF16),
        pltpu.VMEM((n_blocks, SUBLANES, 2 * cb), F32),
    ]
    return pl.pallas_call(
        functools.partial(_ffn_prompt_kernel, tm=tm, tiles_per_seq=tiles_per_seq,
                          n_blocks=n_blocks, cb=cb),
        grid=(rows // tm, n_blocks),
        in_specs=in_specs, out_specs=out_specs, out_shape=out_shape,
        scratch_shapes=scratch, compiler_params=_cparams(2), name="ffn_prompt",
    )(x1, ada, ada, ada, p["g_pre2"], p["g_post2"], wup_b, p["w_ffn_conv"], p["w_ffn_conv"],
      p["b_ffn_conv"], p["b_ffn_conv"], wdn_b)


def _ffn_sample(x1, ada, p, wup_b, wdn_b, sffn2, n_seq, n_t):
    rows = n_seq * n_t
    n_blocks, _, cb2 = wup_b.shape
    cb = cb2 // 2
    nk = FFN_CONV - 1
    ada_spec = lambda k: pl.BlockSpec((n_seq, D_MODEL), lambda i, c: (0, k))
    in_specs = [pl.BlockSpec((rows, D_MODEL), lambda i, c: (0, 0), pipeline_mode=pl.Buffered(1)),
                ada_spec(3), ada_spec(4), ada_spec(5)] + _ffn_weight_specs(n_blocks, cb) + [
        pl.BlockSpec((n_seq * nk, cb), lambda i, c: (0, c)),
        pl.BlockSpec((n_seq * nk, cb), lambda i, c: (0, n_blocks + c)),
    ]
    out_specs = [
        pl.BlockSpec((rows, D_MODEL), lambda i, c: (0, 0)),
        pl.BlockSpec((n_seq * nk, cb), lambda i, c: (0, c)),
        pl.BlockSpec((n_seq * nk, cb), lambda i, c: (0, c)),
    ]
    out_shape = [jax.ShapeDtypeStruct((rows, D_MODEL), F32),
                 jax.ShapeDtypeStruct((n_seq * nk, n_blocks * cb), F32),
                 jax.ShapeDtypeStruct((n_seq * nk, n_blocks * cb), F32)]
    scratch = [
        pltpu.VMEM((rows, D_MODEL), BF16),
        pltpu.VMEM((rows, 2 * cb), F32),
        pltpu.VMEM((rows, cb), BF16),
        pltpu.VMEM((rows, D_MODEL), F32),
    ]
    return pl.pallas_call(
        functools.partial(_ffn_sample_kernel, n_seq=n_seq, n_t=n_t, n_blocks=n_blocks, cb=cb),
        grid=(1, n_blocks),
        in_specs=in_specs, out_specs=out_specs, out_shape=out_shape,
        scratch_shapes=scratch, compiler_params=_cparams(2), name="ffn_sample",
    )(x1, ada, ada, ada, p["g_pre2"], p["g_post2"], wup_b, p["w_ffn_conv"], p["w_ffn_conv"],
      p["b_ffn_conv"], p["b_ffn_conv"], wdn_b, sffn2, sffn2)


TOKEN_TILE = 1024
OUTPROJ_TILE = 512


def kernel(x_prompt, x_sample, c_prompt, c_sample, state_pool, state_lru_conv, state_lru_h, state_ffn_conv, w_ada, b_ada, g_pre1, g_post1, g_pre2, g_post2, w_in, w_pool_grp, pool_scale, w_lru_conv, b_lru_conv, w_rg, b_rg, w_ig, b_ig, lru_lambda, w_pool_up, w_lru_up, w_out, w_ffn_up, w_ffn_conv, b_ffn_conv, w_ffn_down):
    batch, seq, d = x_prompt.shape
    dec_batch, dec_seq, _ = x_sample.shape
    depth = w_ada.shape[0]
    assert d == D_MODEL and dec_batch % SUBLANES == 0 and seq % TOKEN_TILE == 0
    assert w_in.shape[2] == N_MIX_BLOCKS * CB + 2 * D_MODEL

    pad = (-batch) % SUBLANES
    c_all = jnp.concatenate([c_sample, c_prompt, jnp.zeros((pad, d), c_prompt.dtype)], axis=0)
    prompt_row_block = dec_batch // SUBLANES

    vec_names = ("g_pre1", "g_post1", "g_pre2", "g_post2", "pool_scale", "b_lru_conv", "b_rg",
                 "b_ig", "lru_lambda", "b_ffn_conv")
    weights = dict(w_ada=w_ada, b_ada=b_ada, g_pre1=g_pre1, g_post1=g_post1, g_pre2=g_pre2,
                   g_post2=g_post2, w_in=w_in, w_pool_grp=w_pool_grp, pool_scale=pool_scale,
                   w_lru_conv=w_lru_conv, b_lru_conv=b_lru_conv, w_rg=w_rg, b_rg=b_rg, w_ig=w_ig,
                   b_ig=b_ig, lru_lambda=lru_lambda, w_pool_up=w_pool_up, w_lru_up=w_lru_up,
                   w_out=w_out, w_ffn_up=w_ffn_up, w_ffn_conv=w_ffn_conv, b_ffn_conv=b_ffn_conv,
                   w_ffn_down=w_ffn_down)

    def time_major(a):
        return jnp.swapaxes(a, 0, 1).reshape(-1, a.shape[-1])

    def seq_major(a2, n_rows):
        return jnp.swapaxes(a2.reshape(n_rows, dec_batch, -1), 0, 1)

    xp = x_prompt.reshape(batch * seq, d)
    xs = time_major(x_sample)
    tps = seq // TOKEN_TILE
    outs_p = ([], [], [], [])
    outs_s = ([], [], [], [])
    for l in range(depth):
        p = {k: v[l] for k, v in weights.items()}
        for k in vec_names + ("b_ada",):
            p[k] = p[k].reshape(1, -1)
        ada = _ada(c_all, p["w_ada"], p["b_ada"])
        wup_b, wdn_b = _ffn_cast_weights(p["w_ffn_up"], p["w_ffn_down"], FFN_CB)

        y, npool, nconv, nh, h = _mix_prompt(xp, ada, p, batch, seq, prompt_row_block)
        m = _merge(h, y, p, TOKEN_TILE, "merge_prompt")
        x1 = _outproj(m, xp, ada, p, OUTPROJ_TILE, seq // OUTPROJ_TILE, prompt_row_block, 0)
        xp, nffn = _ffn_prompt(x1, ada, p, wup_b, wdn_b, batch, TOKEN_TILE, tps, prompt_row_block)
        outs_p[0].append(npool)
        outs_p[1].append(nconv)
        outs_p[2].append(nh.reshape(batch, LRU_WIDTH))
        outs_p[3].append(nffn[tps - 1::tps].reshape(batch, FFN_CONV - 1, -1))

        rows_s = dec_batch * dec_seq
        y, npool, nconv, nh, h = _mix_sample(
            xs, ada, p, time_major(state_pool[l]), time_major(state_lru_conv[l]),
            state_lru_h[l], dec_batch, dec_seq, PAST_LEN)
        m = _merge(h, y, p, rows_s, "merge_sample")
        x1 = _outproj(m, xs, ada, p, rows_s, 1, 0, dec_seq)
        xs, ng, nv = _ffn_sample(x1, ada, p, wup_b, wdn_b, time_major(state_ffn_conv[l]),
                                 dec_batch, dec_seq)
        outs_s[0].append(seq_major(npool, POOL_BUF))
        outs_s[1].append(seq_major(nconv, LRU_CONV - 1))
        outs_s[2].append(nh)
        outs_s[3].append(jnp.concatenate([seq_major(ng, FFN_CONV - 1),
                                          seq_major(nv, FFN_CONV - 1)], axis=-1))

    return (xp.reshape(batch, seq, d), seq_major(xs, dec_seq),
            jnp.stack(outs_p[0]), jnp.stack(outs_p[1]), jnp.stack(outs_p[2]), jnp.stack(outs_p[3]),
            jnp.stack(outs_s[0]), jnp.stack(outs_s[1]), jnp.stack(outs_s[2]), jnp.stack(outs_s[3]))
```

```python
import functools

import jax
import jax.numpy as jnp
from jax import lax
from jax.experimental import pallas as pl
from jax.experimental.pallas import tpu as pltpu

F32 = jnp.float32
BF16 = jnp.bfloat16

D_MODEL = 2048
POOL_WINDOWS = (2, 4, 8, 16)
POOL_GROUPS = len(POOL_WINDOWS)
POOL_BUF = max(POOL_WINDOWS) - 1
LRU_CONV = 4
LRU_C = 8.0
PAST_LEN = 16384
FFN_CONV = 3
N_ADA = 6
EPS = 1e-6

CB = 256
POOL_WIDTH = POOL_GROUPS * CB
LRU_BLOCKS = 8
LRU_WIDTH = LRU_BLOCKS * CB
N_MIX_BLOCKS = POOL_GROUPS + LRU_BLOCKS
HALO = 16
SUBLANES = 8
VMEM_LIMIT = 60 * 1024 * 1024


def _cparams(n_axes):
    return pltpu.CompilerParams(
        dimension_semantics=("arbitrary",) * n_axes, vmem_limit_bytes=VMEM_LIMIT)


def _dot(a, b):
    return jnp.dot(a, b, preferred_element_type=F32)


def _softplus(z):
    return jnp.maximum(z, 0.0) + jnp.log1p(jnp.exp(-jnp.abs(z)))


def _lru_coeffs(xc, r_pre, i_pre, neg_c_sp):
    r = jax.nn.sigmoid(r_pre)
    i = jax.nn.sigmoid(i_pre)
    log_a = r * neg_c_sp
    a = jnp.exp(log_a)
    b = jnp.sqrt(-jnp.tanh(log_a) * (a * a + 1.0)) * (i * xc)
    return a, b


def _ada_kernel(c_ref, w_ref, b_ref, o_ref):
    c = c_ref[...]
    s = (c * jax.nn.sigmoid(c)).astype(BF16)
    o_ref[...] = _dot(s, w_ref[...].astype(BF16)) + b_ref[...]


def _ada(c_all, w_ada, b_ada):
    rows = c_all.shape[0]
    tn = 1024
    return pl.pallas_call(
        _ada_kernel,
        grid=(N_ADA * D_MODEL // tn,),
        in_specs=[pl.BlockSpec((rows, D_MODEL), lambda n: (0, 0)),
                  pl.BlockSpec((D_MODEL, tn), lambda n: (0, n)),
                  pl.BlockSpec((1, tn), lambda n: (0, n))],
        out_specs=pl.BlockSpec((rows, tn), lambda n: (0, n)),
        out_shape=jax.ShapeDtypeStruct((rows, N_ADA * D_MODEL), F32),
        compiler_params=_cparams(1),
        name="ada",
    )(c_all, w_ada, b_ada)


NORM_ROWS = 32
NORM_UNROLL = 4
OUT_SUBTILE = 256


def _unit_rms(x):
    return x * lax.rsqrt(jnp.mean(x * x, axis=-1, keepdims=True) + EPS)


def _build_h_prompt(x_ref, h_ref, g, scale1p, shift, n_rows):
    gs = g * scale1p

    def body(i, carry):
        r0 = pl.multiple_of(i * NORM_ROWS, NORM_ROWS)
        x = x_ref[pl.ds(r0, NORM_ROWS), :]
        h_ref[pl.ds(r0, NORM_ROWS), :] = (_unit_rms(x) * gs + shift).astype(BF16)
        return carry
    lax.fori_loop(0, n_rows // NORM_ROWS, body, 0, unroll=NORM_UNROLL)


def _residual_norm_prompt(x_ref, acc_ref, o_ref, g, gate, n_rows):
    gg = g * gate

    def body(i, carry):
        sl = pl.ds(pl.multiple_of(i * NORM_ROWS, NORM_ROWS), NORM_ROWS)
        o_ref[sl, :] = x_ref[sl, :] + _unit_rms(acc_ref[sl, :]) * gg
        return carry
    lax.fori_loop(0, n_rows // NORM_ROWS, body, 0, unroll=NORM_UNROLL)


def _build_h_sample(x_ref, h_ref, g, sc_ref, sh_ref, n_seq, n_t):
    for s0 in range(0, n_seq, NORM_ROWS):
        gs = g * (1.0 + sc_ref[s0:s0 + NORM_ROWS, :])
        shift = sh_ref[s0:s0 + NORM_ROWS, :]
        for t in range(n_t):
            sl = slice(t * n_seq + s0, t * n_seq + s0 + NORM_ROWS)
            h_ref[sl, :] = (_unit_rms(x_ref[sl, :]) * gs + shift).astype(BF16)


MIX_SUBTILE = 512
POOL_ROWS = 128
LRU_ROWS = 64


def _mix_prompt_kernel(x_ref, sh_ref, sc_ref, g_ref, win_ref, wgrp_ref, pscale_ref,
                       wconv_ref, bconv_ref, wrg_ref, brg_ref, wig_ref, big_ref, lam_ref,
                       y_ref, npool_ref, nconv_ref, nh_ref, h_ref,
                       ext_ref, xb_ref, r_ref, i_ref, a_ref, u_ref, *, seq):
    q = pl.program_id(0)
    j = pl.program_id(1)

    @pl.when(j == 0)
    def _():
        _build_h_prompt(x_ref, h_ref, g_ref[...], 1.0 + sc_ref[pl.ds(q, 1), :],
                        sh_ref[pl.ds(q, 1), :], seq)
        for ref in (ext_ref, r_ref, i_ref, a_ref):
            ref[0:HALO, :] = jnp.zeros((HALO, CB), F32)

    win = win_ref[...].astype(BF16)

    def rows_of(r0, n, shift=0):
        return slice(HALO + r0 - shift, HALO + r0 - shift + n)

    def up_proj(m0):
        ext_ref[rows_of(m0, MIX_SUBTILE), :] = _dot(h_ref[m0:m0 + MIX_SUBTILE, :], win)

    def pool_branch(w):
        wg = wgrp_ref[0].astype(BF16)
        ps = pscale_ref[...]
        partial = {2: r_ref, 4: i_ref, 8: a_ref}
        up_proj(0)
        for m0 in range(0, seq, MIX_SUBTILE):
            if m0 + MIX_SUBTILE < seq:
                up_proj(m0 + MIX_SUBTILE)
            for r0 in range(m0, m0 + MIX_SUBTILE, POOL_ROWS):
                u = ext_ref[rows_of(r0, POOL_ROWS), :]
                s, src, width = u, ext_ref, 1
                while width < w:
                    s = s + src[rows_of(r0, POOL_ROWS, width), :]
                    width *= 2
                    if width < w:
                        src = partial[width]
                        src[rows_of(r0, POOL_ROWS), :] = s
                if r0 < w:
                    pos = r0 + lax.broadcasted_iota(jnp.int32, (POOL_ROWS, 1), 0)
                    cnt = jnp.minimum(w, pos + 1).astype(F32)
                else:
                    cnt = float(w)
                xb_ref[r0:r0 + POOL_ROWS, :] = (s / cnt - u).astype(BF16)
            sub = slice(m0, m0 + MIX_SUBTILE)
            y_ref[sub, :] = (_dot(xb_ref[sub, :], wg) * ps).astype(BF16)
        npool_ref[0] = ext_ref[pl.ds(HALO + seq - POOL_BUF, POOL_BUF), :]

    for g, w in enumerate(POOL_WINDOWS):
        pl.when(j == g)(functools.partial(pool_branch, w))

    @pl.when(j >= POOL_GROUPS)
    def _():
        wc = wconv_ref[...]
        bc = bconv_ref[...]
        wrg = wrg_ref[0].astype(BF16)
        wig = wig_ref[0].astype(BF16)
        neg_c_sp = (-LRU_C) * _softplus(-lam_ref[...])
        brg = brg_ref[...]
        big = big_ref[...]
        nb = LRU_ROWS // SUBLANES
        row = lax.broadcasted_iota(jnp.int32, (nb, SUBLANES, CB), 1)
        h_carry = jnp.zeros((1, CB), F32)
        up_proj(0)
        for m0 in range(0, seq, MIX_SUBTILE):
            sub = slice(m0, m0 + MIX_SUBTILE)
            if m0 + MIX_SUBTILE < seq:
                up_proj(m0 + MIX_SUBTILE)
            for r0 in range(m0, m0 + MIX_SUBTILE, LRU_ROWS):
                xc = wc[LRU_CONV - 1:, :] * ext_ref[rows_of(r0, LRU_ROWS), :] + bc
                for k in range(LRU_CONV - 1):
                    xc = xc + wc[k:k + 1, :] * ext_ref[rows_of(r0, LRU_ROWS, LRU_CONV - 1 - k), :]
                a_ref[rows_of(r0, LRU_ROWS), :] = xc
                xb_ref[r0:r0 + LRU_ROWS, :] = xc.astype(BF16)
            r_ref[rows_of(m0, MIX_SUBTILE), :] = _dot(xb_ref[sub, :], wrg)
            i_ref[rows_of(m0, MIX_SUBTILE), :] = _dot(xb_ref[sub, :], wig)
            for r0 in range(m0, m0 + MIX_SUBTILE, LRU_ROWS):
                sl = rows_of(r0, LRU_ROWS)
                a, b = _lru_coeffs(a_ref[sl, :], r_ref[sl, :] + brg, i_ref[sl, :] + big, neg_c_sp)
                a = a.reshape(nb, SUBLANES, CB)
                b = b.reshape(nb, SUBLANES, CB)
                for k in (1, 2, 4):
                    a_sh = jnp.where(row >= k, pltpu.roll(a, k, 1), 1.0)
                    b_sh = jnp.where(row >= k, pltpu.roll(b, k, 1), 0.0)
                    b = b + a * b_sh
                    a = a * a_sh
                a_ref[sl, :] = a.reshape(LRU_ROWS, CB)
                u_ref[sl, :] = b.reshape(LRU_ROWS, CB)
            for r0 in range(m0, m0 + MIX_SUBTILE, SUBLANES):
                sl = rows_of(r0, SUBLANES)
                h8 = a_ref[sl, :] * h_carry + u_ref[sl, :]
                u_ref[sl, :] = h8
                h_carry = h8[SUBLANES - 1:SUBLANES, :]
            y_ref[sub, :] = u_ref[rows_of(m0, MIX_SUBTILE), :].astype(BF16)
        nh_ref[0] = h_carry
        nconv_ref[0] = ext_ref[pl.ds(HALO + seq - (LRU_CONV - 1), LRU_CONV - 1), :]


def _mix_sample_kernel(x_ref, sh_ref, sc_ref, g_ref, win_ref, wgrp_ref, pscale_ref,
                       wconv_ref, bconv_ref, wrg_ref, brg_ref, wig_ref, big_ref, lam_ref,
                       spool_ref, sconv_ref, sh0_ref,
                       y_ref, npool_ref, nconv_ref, nh_ref, h_ref,
                       u_ref, d_ref, *, n_seq, n_t, start):
    j = pl.program_id(1)

    @pl.when(j == 0)
    def _():
        _build_h_sample(x_ref, h_ref, g_ref[...], sc_ref, sh_ref, n_seq, n_t)

    u_ref[...] = _dot(h_ref[...], win_ref[...].astype(BF16))

    def u_slab(t):
        return u_ref[t * n_seq:(t + 1) * n_seq, :]

    def pool_branch(w):
        e = [spool_ref[k * n_seq:(k + 1) * n_seq, :] for k in range(POOL_BUF)]
        e += [u_slab(t) for t in range(n_t)]
        for k in range(POOL_BUF):
            npool_ref[k * n_seq:(k + 1) * n_seq, :] = e[n_t + k]
        for t in range(n_t):
            s = e[POOL_BUF + t]
            for k in range(1, w):
                s = s + e[POOL_BUF + t - k]
            cnt = float(min(w, start + t + 1))
            d_ref[t * n_seq:(t + 1) * n_seq, :] = (s / cnt - e[POOL_BUF + t]).astype(BF16)
        y = _dot(d_ref[...], wgrp_ref[0].astype(BF16)) * pscale_ref[...]
        y_ref[...] = y.astype(BF16)

    for g, w in enumerate(POOL_WINDOWS):
        pl.when(j == g)(functools.partial(pool_branch, w))

    @pl.when(j >= POOL_GROUPS)
    def _():
        nk = LRU_CONV - 1
        e = [sconv_ref[k * n_seq:(k + 1) * n_seq, :] for k in range(nk)]
        e += [u_slab(t) for t in range(n_t)]
        for k in range(nk):
            nconv_ref[k * n_seq:(k + 1) * n_seq, :] = e[n_t + k]
        wc = wconv_ref[...]
        bc = bconv_ref[...]
        wrg = wrg_ref[0].astype(BF16)
        wig = wig_ref[0].astype(BF16)
        neg_c_sp = (-LRU_C) * _softplus(-lam_ref[...])
        h = sh0_ref[...]
        for t in range(n_t):
            xc = bc
            for k in range(LRU_CONV):
                xc = xc + wc[k:k + 1, :] * e[t + k]
            xb = xc.astype(BF16)
            a, b = _lru_coeffs(xc, _dot(xb, wrg) + brg_ref[...], _dot(xb, wig) + big_ref[...],
                               neg_c_sp)
            h = a * h + b
            y_ref[t * n_seq:(t + 1) * n_seq, :] = h.astype(BF16)
        nh_ref[...] = h


def _mix_weight_specs():
    pj = lambda j: jnp.minimum(j, POOL_GROUPS - 1)
    lj = lambda j: jnp.maximum(j - POOL_GROUPS, 0)
    return [
        pl.BlockSpec((1, D_MODEL), lambda q, j: (0, 0)),
        pl.BlockSpec((D_MODEL, CB), lambda q, j: (0, j)),
        pl.BlockSpec((1, CB, CB), lambda q, j: (pj(j), 0, 0)),
        pl.BlockSpec((1, CB), lambda q, j: (0, pj(j))),
        pl.BlockSpec((LRU_CONV, CB), lambda q, j: (0, lj(j))),
        pl.BlockSpec((1, CB), lambda q, j: (0, lj(j))),
        pl.BlockSpec((1, CB, CB), lambda q, j: (lj(j), 0, 0)),
        pl.BlockSpec((1, CB), lambda q, j: (0, lj(j))),
        pl.BlockSpec((1, CB, CB), lambda q, j: (lj(j), 0, 0)),
        pl.BlockSpec((1, CB), lambda q, j: (0, lj(j))),
        pl.BlockSpec((1, CB), lambda q, j: (0, lj(j))),
    ], pj, lj


def _mix_prompt(x2, ada, p, n_seq, seq, ada_row_block):
    wspecs, pj, lj = _mix_weight_specs()
    in_specs = [
        pl.BlockSpec((seq, D_MODEL), lambda q, j: (q, 0), pipeline_mode=pl.Buffered(1)),
        pl.BlockSpec((SUBLANES, D_MODEL), lambda q, j: (ada_row_block, 0)),
        pl.BlockSpec((SUBLANES, D_MODEL), lambda q, j: (ada_row_block, 1)),
    ] + wspecs
    out_specs = [
        pl.BlockSpec((seq, CB), lambda q, j: (q, j)),
        pl.BlockSpec((1, POOL_BUF, CB), lambda q, j: (q, 0, pj(j))),
        pl.BlockSpec((1, LRU_CONV - 1, CB), lambda q, j: (q, 0, lj(j))),
        pl.BlockSpec((1, 1, CB), lambda q, j: (q, 0, lj(j))),
        pl.BlockSpec((seq, D_MODEL), lambda q, j: (q, 0)),
    ]
    out_shape = [
        jax.ShapeDtypeStruct((n_seq * seq, N_MIX_BLOCKS * CB), BF16),
        jax.ShapeDtypeStruct((n_seq, POOL_BUF, POOL_WIDTH), F32),
        jax.ShapeDtypeStruct((n_seq, LRU_CONV - 1, LRU_WIDTH), F32),
        jax.ShapeDtypeStruct((n_seq, 1, LRU_WIDTH), F32),
        jax.ShapeDtypeStruct((n_seq * seq, D_MODEL), BF16),
    ]
    scratch = [
        pltpu.VMEM((HALO + seq, CB), F32),
        pltpu.VMEM((seq, CB), BF16),
        pltpu.VMEM((HALO + seq, CB), F32),
        pltpu.VMEM((HALO + seq, CB), F32),
        pltpu.VMEM((HALO + seq, CB), F32),
        pltpu.VMEM((HALO + seq, CB), F32),
    ]
    return pl.pallas_call(
        functools.partial(_mix_prompt_kernel, seq=seq),
        grid=(n_seq, N_MIX_BLOCKS),
        in_specs=in_specs, out_specs=out_specs, out_shape=out_shape,
        scratch_shapes=scratch, compiler_params=_cparams(2), name="mix_prompt",
    )(x2, ada, ada, p["g_pre1"], p["w_in"], p["w_pool_grp"], p["pool_scale"],
      p["w_lru_conv"], p["b_lru_conv"], p["w_rg"], p["b_rg"], p["w_ig"], p["b_ig"],
      p["lru_lambda"])


def _mix_sample(x2, ada, p, spool2, sconv2, sh0, n_seq, n_t, start):
    wspecs, pj, lj = _mix_weight_specs()
    rows = n_seq * n_t
    in_specs = [
        pl.BlockSpec((rows, D_MODEL), lambda q, j: (0, 0), pipeline_mode=pl.Buffered(1)),
        pl.BlockSpec((n_seq, D_MODEL), lambda q, j: (0, 0)),
        pl.BlockSpec((n_seq, D_MODEL), lambda q, j: (0, 1)),
    ] + wspecs + [
        pl.BlockSpec((n_seq * POOL_BUF, CB), lambda q, j: (0, pj(j))),
        pl.BlockSpec((n_seq * (LRU_CONV - 1), CB), lambda q, j: (0, lj(j))),
        pl.BlockSpec((n_seq, CB), lambda q, j: (0, lj(j))),
    ]
    out_specs = [
        pl.BlockSpec((rows, CB), lambda q, j: (0, j)),
        pl.BlockSpec((n_seq * POOL_BUF, CB), lambda q, j: (0, pj(j))),
        pl.BlockSpec((n_seq * (LRU_CONV - 1), CB), lambda q, j: (0, lj(j))),
        pl.BlockSpec((n_seq, CB), lambda q, j: (0, lj(j))),
        pl.BlockSpec((rows, D_MODEL), lambda q, j: (0, 0)),
    ]
    out_shape = [
        jax.ShapeDtypeStruct((rows, N_MIX_BLOCKS * CB), BF16),
        jax.ShapeDtypeStruct((n_seq * POOL_BUF, POOL_WIDTH), F32),
        jax.ShapeDtypeStruct((n_seq * (LRU_CONV - 1), LRU_WIDTH), F32),
        jax.ShapeDtypeStruct((n_seq, LRU_WIDTH), F32),
        jax.ShapeDtypeStruct((rows, D_MODEL), BF16),
    ]
    scratch = [
        pltpu.VMEM((rows, CB), F32),
        pltpu.VMEM((rows, CB), BF16),
    ]
    return pl.pallas_call(
        functools.partial(_mix_sample_kernel, n_seq=n_seq, n_t=n_t, start=start),
        grid=(1, N_MIX_BLOCKS),
        in_specs=in_specs, out_specs=out_specs, out_shape=out_shape,
        scratch_shapes=scratch, compiler_params=_cparams(2), name="mix_sample",
    )(x2, ada, ada, p["g_pre1"], p["w_in"], p["w_pool_grp"], p["pool_scale"],
      p["w_lru_conv"], p["b_lru_conv"], p["w_rg"], p["b_rg"], p["w_ig"], p["b_ig"],
      p["lru_lambda"], spool2, sconv2, sh0)


MERGE_ROWS = 64
MERGE_SUBTILE = 256


def _merge_kernel(h_ref, y_ref, wgp_ref, wgl_ref, wpu_ref, wlu_ref,
                  m_ref, gp_ref, gl_ref, pu_ref, lu_ref, *, tm):
    wgp = wgp_ref[...].astype(BF16)
    wgl = wgl_ref[...].astype(BF16)
    wpu = wpu_ref[...].astype(BF16)
    wlu = wlu_ref[...].astype(BF16)

    def proj(m0):
        rows = slice(m0, m0 + MERGE_SUBTILE)
        gp_ref[rows, :] = _dot(h_ref[rows, :], wgp)
        gl_ref[rows, :] = _dot(h_ref[rows, :], wgl)
        pu_ref[rows, :] = _dot(y_ref[rows, 0:POOL_WIDTH], wpu)
        lu_ref[rows, :] = _dot(y_ref[rows, POOL_WIDTH:POOL_WIDTH + LRU_WIDTH], wlu)

    proj(0)
    for m0 in range(0, tm, MERGE_SUBTILE):
        if m0 + MERGE_SUBTILE < tm:
            proj(m0 + MERGE_SUBTILE)
        for r0 in range(m0, m0 + MERGE_SUBTILE, MERGE_ROWS):
            sl = slice(r0, r0 + MERGE_ROWS)
            m = (jax.nn.sigmoid(gp_ref[sl, :]) * pu_ref[sl, :]
                 + jax.nn.sigmoid(gl_ref[sl, :]) * lu_ref[sl, :])
            m_ref[sl, :] = m.astype(BF16)


def _merge(h, y, p, tm, name):
    rows = h.shape[0]
    n_gate0 = N_MIX_BLOCKS
    n_gate1 = N_MIX_BLOCKS + D_MODEL // CB
    in_specs = [
        pl.BlockSpec((tm, D_MODEL), lambda i, c: (i, 0), pipeline_mode=pl.Buffered(1)),
        pl.BlockSpec((tm, N_MIX_BLOCKS * CB), lambda i, c: (i, 0), pipeline_mode=pl.Buffered(1)),
        pl.BlockSpec((D_MODEL, CB), lambda i, c: (0, n_gate0 + c)),
        pl.BlockSpec((D_MODEL, CB), lambda i, c: (0, n_gate1 + c)),
        pl.BlockSpec((POOL_WIDTH, CB), lambda i, c: (0, c)),
        pl.BlockSpec((LRU_WIDTH, CB), lambda i, c: (0, c)),
    ]
    return pl.pallas_call(
        functools.partial(_merge_kernel, tm=tm),
        grid=(rows // tm, D_MODEL // CB),
        in_specs=in_specs,
        out_specs=pl.BlockSpec((tm, CB), lambda i, c: (i, c)),
        out_shape=jax.ShapeDtypeStruct((rows, D_MODEL), BF16),
        scratch_shapes=[pltpu.VMEM((tm, CB), F32)] * 4, compiler_params=_cparams(2),
        name=name,
    )(h, y, p["w_in"], p["w_in"], p["w_pool_up"], p["w_lru_up"])


def _cast_weight(w_ref, wb_ref):
    rows = w_ref.shape[0]
    for r0 in range(0, rows, 256):
        wb_ref[r0:r0 + 256, :] = w_ref[r0:r0 + 256, :].astype(BF16)


def _outproj_kernel(m_ref, x_ref, gate_ref, g_ref, w_ref, o_ref, wb_ref, acc_ref,
                    *, tm, tiles_per_seq, sample_t):
    i = pl.program_id(0)

    @pl.when(i == 0)
    def _():
        _cast_weight(w_ref, wb_ref)

    g = g_ref[...]
    if sample_t:
        n_seq = tm // sample_t
    else:
        gg_tile = g * gate_ref[pl.ds(i // tiles_per_seq, 1), :]

    def proj(m0):
        acc_ref[m0:m0 + OUT_SUBTILE, :] = _dot(m_ref[m0:m0 + OUT_SUBTILE, :], wb_ref[...])

    proj(0)
    for m0 in range(0, tm, OUT_SUBTILE):
        if m0 + OUT_SUBTILE < tm:
            proj(m0 + OUT_SUBTILE)
        for r0 in range(m0, m0 + OUT_SUBTILE, NORM_ROWS):
            sl = slice(r0, r0 + NORM_ROWS)
            if sample_t:
                s0 = r0 % n_seq
                gg = g * gate_ref[s0:s0 + NORM_ROWS, :]
            else:
                gg = gg_tile
            o_ref[sl, :] = x_ref[sl, :] + _unit_rms(acc_ref[sl, :]) * gg


def _outproj(m, x2, ada, p, tm, tiles_per_seq, ada_row_block, sample_t):
    rows = x2.shape[0]
    if sample_t:
        gate_spec = pl.BlockSpec((tm // sample_t, D_MODEL), lambda i: (0, 2))
    else:
        gate_spec = pl.BlockSpec((SUBLANES, D_MODEL), lambda i: (ada_row_block, 2))
    return pl.pallas_call(
        functools.partial(_outproj_kernel, tm=tm, tiles_per_seq=tiles_per_seq, sample_t=sample_t),
        grid=(rows // tm,),
        in_specs=[pl.BlockSpec((tm, D_MODEL), lambda i: (i, 0)),
                  pl.BlockSpec((tm, D_MODEL), lambda i: (i, 0)),
                  gate_spec,
                  pl.BlockSpec((1, D_MODEL), lambda i: (0, 0)),
                  pl.BlockSpec((D_MODEL, D_MODEL), lambda i: (0, 0), pipeline_mode=pl.Buffered(1))],
        out_specs=pl.BlockSpec((tm, D_MODEL), lambda i: (i, 0)),
        out_shape=jax.ShapeDtypeStruct((rows, D_MODEL), F32),
        scratch_shapes=[pltpu.VMEM((D_MODEL, D_MODEL), BF16), pltpu.VMEM((tm, D_MODEL), F32)],
        compiler_params=_cparams(1),
        name="outproj_sample" if sample_t else "outproj_prompt",
    )(m, x2, ada, p["g_post1"], p["w_out"])


GELU_C0 = 0.7978845608028654
GELU_C1 = GELU_C0 * 0.044715
FFN_ROWS = 32
FFN_SUBTILE = 256
FFN_CB = 512


def _ffn_conv_gate(eg, ev, wcg, wcv, bcg, bcv):
    last = FFN_CONV - 1
    g = wcg[last:, :] * eg[last] + bcg
    v = wcv[last:, :] * ev[last] + bcv
    for k in range(last):
        g = g + wcg[k:k + 1, :] * eg[k]
        v = v + wcv[k:k + 1, :] * ev[k]
    t = jnp.tanh(g * (GELU_C0 + GELU_C1 * (g * g)))
    return ((g * v) * (0.5 + 0.5 * t)).astype(BF16)


def _ffn_prompt_kernel(x_ref, sh_ref, sc_ref, gate_ref, gpre_ref, gpost_ref,
                       wup_ref, wcg_ref, wcv_ref, bcg_ref, bcv_ref, wdn_ref,
                       o_ref, nst_ref,
                       h_ref, ext_ref, f_ref, carry_ref, *, tm, tiles_per_seq, n_blocks, cb):
    i = pl.program_id(0)
    c = pl.program_id(1)
    q = i // tiles_per_seq
    first = (i % tiles_per_seq) == 0
    nk = FFN_CONV - 1

    @pl.when(c == 0)
    def _():
        _build_h_prompt(x_ref, h_ref, gpre_ref[...], 1.0 + sc_ref[pl.ds(q, 1), :],
                        sh_ref[pl.ds(q, 1), :], tm)
        o_ref[...] = jnp.zeros((tm, D_MODEL), F32)

    @pl.when(first)
    def _():
        ext_ref[0:SUBLANES, :] = jnp.zeros((SUBLANES, 2 * cb), F32)

    @pl.when(jnp.logical_not(first))
    def _():
        ext_ref[0:SUBLANES, :] = carry_ref[c]

    wcg = wcg_ref[...]
    wcv = wcv_ref[...]
    bcg = bcg_ref[...]
    bcv = bcv_ref[...]

    def up_proj(m0):
        ext_ref[SUBLANES + m0:SUBLANES + m0 + FFN_SUBTILE, :] = _dot(
            h_ref[m0:m0 + FFN_SUBTILE, :], wup_ref[0])

    up_proj(0)
    for m0 in range(0, tm, FFN_SUBTILE):
        rows = slice(m0, m0 + FFN_SUBTILE)
        if m0 + FFN_SUBTILE < tm:
            up_proj(m0 + FFN_SUBTILE)
        for r0 in range(m0, m0 + FFN_SUBTILE, FFN_ROWS):
            shifted = [slice(SUBLANES + r0 - nk + k, SUBLANES + r0 - nk + k + FFN_ROWS)
                       for k in range(FFN_CONV)]
            eg = [ext_ref[s, 0:cb] for s in shifted]
            ev = [ext_ref[s, cb:2 * cb] for s in shifted]
            f_ref[r0:r0 + FFN_ROWS, :] = _ffn_conv_gate(eg, ev, wcg, wcv, bcg, bcv)
        o_ref[rows, :] += _dot(f_ref[rows, :], wdn_ref[...])

    carry_ref[c] = ext_ref[tm:tm + SUBLANES, :]
    nst_ref[0, :, 0, :] = ext_ref[pl.ds(SUBLANES + tm - nk, nk), 0:cb]
    nst_ref[0, :, 1, :] = ext_ref[pl.ds(SUBLANES + tm - nk, nk), cb:2 * cb]

    @pl.when(c == n_blocks - 1)
    def _():
        _residual_norm_prompt(x_ref, o_ref, o_ref, gpost_ref[...], gate_ref[pl.ds(q, 1), :], tm)


def _ffn_sample_kernel(x_ref, sh_ref, sc_ref, gate_ref, gpre_ref, gpost_ref,
                       wup_ref, wcg_ref, wcv_ref, bcg_ref, bcv_ref, wdn_ref,
                       sg_ref, sv_ref,
                       o_ref, ng_ref, nv_ref,
                       h_ref, ext_ref, f_ref, acc_ref, *, n_seq, n_t, n_blocks, cb):
    c = pl.program_id(1)
    nk = FFN_CONV - 1

    @pl.when(c == 0)
    def _():
        _build_h_sample(x_ref, h_ref, gpre_ref[...], sc_ref, sh_ref, n_seq, n_t)
        acc_ref[...] = jnp.zeros((n_seq * n_t, D_MODEL), F32)

    wcg = wcg_ref[...]
    wcv = wcv_ref[...]
    bcg = bcg_ref[...]
    bcv = bcv_ref[...]

    def view(state_ref, half, t, s0, n):
        if t < 0:
            return state_ref[(nk + t) * n_seq + s0:(nk + t) * n_seq + s0 + n, :]
        return ext_ref[t * n_seq + s0:t * n_seq + s0 + n, half * cb:(half + 1) * cb]

    t_sub = max(FFN_SUBTILE // n_seq, 1)

    def up_proj(t0):
        rows = slice(t0 * n_seq, (t0 + t_sub) * n_seq)
        ext_ref[rows, :] = _dot(h_ref[rows, :], wup_ref[0])

    up_proj(0)
    for t0 in range(0, n_t, t_sub):
        rows = slice(t0 * n_seq, (t0 + t_sub) * n_seq)
        if t0 + t_sub < n_t:
            up_proj(t0 + t_sub)
        for t in range(t0, t0 + t_sub):
            for s0 in range(0, n_seq, FFN_ROWS):
                eg = [view(sg_ref, 0, t - nk + k, s0, FFN_ROWS) for k in range(FFN_CONV)]
                ev = [view(sv_ref, 1, t - nk + k, s0, FFN_ROWS) for k in range(FFN_CONV)]
                f_ref[t * n_seq + s0:t * n_seq + s0 + FFN_ROWS, :] = _ffn_conv_gate(
                    eg, ev, wcg, wcv, bcg, bcv)
        acc_ref[rows, :] += _dot(f_ref[rows, :], wdn_ref[...])

    for k in range(nk):
        ng_ref[k * n_seq:(k + 1) * n_seq, :] = view(sg_ref, 0, n_t - nk + k, 0, n_seq)
        nv_ref[k * n_seq:(k + 1) * n_seq, :] = view(sv_ref, 1, n_t - nk + k, 0, n_seq)

    @pl.when(c == n_blocks - 1)
    def _():
        g = gpost_ref[...]
        for s0 in range(0, n_seq, NORM_ROWS):
            gg = g * gate_ref[s0:s0 + NORM_ROWS, :]
            for t in range(n_t):
                sl = slice(t * n_seq + s0, t * n_seq + s0 + NORM_ROWS)
                o_ref[sl, :] = x_ref[sl, :] + _unit_rms(acc_ref[sl, :]) * gg


CAST_ROWS = 256


def _cast_up_kernel(wg_ref, wv_ref, o_ref):
    cb = wg_ref.shape[1]
    for r0 in range(0, wg_ref.shape[0], CAST_ROWS):
        rows = slice(r0, r0 + CAST_ROWS)
        o_ref[0, rows, 0:cb] = wg_ref[rows, :].astype(BF16)
        o_ref[0, rows, cb:2 * cb] = wv_ref[rows, :].astype(BF16)


def _cast_kernel(w_ref, o_ref):
    for r0 in range(0, w_ref.shape[0], CAST_ROWS):
        o_ref[r0:r0 + CAST_ROWS, :] = w_ref[r0:r0 + CAST_ROWS, :].astype(BF16)


def _ffn_cast_weights(w_up, w_down, cb):
    d_ff = w_down.shape[0]
    n_blocks = d_ff // cb
    up = pl.pallas_call(
        _cast_up_kernel,
        grid=(n_blocks,),
        in_specs=[pl.BlockSpec((D_MODEL, cb), lambda c: (0, c)),
                  pl.BlockSpec((D_MODEL, cb), lambda c: (0, n_blocks + c))],
        out_specs=pl.BlockSpec((1, D_MODEL, 2 * cb), lambda c: (c, 0, 0)),
        out_shape=jax.ShapeDtypeStruct((n_blocks, D_MODEL, 2 * cb), BF16),
        compiler_params=_cparams(1), name="cast_ffn_up",
    )(w_up, w_up)
    down = pl.pallas_call(
        _cast_kernel,
        grid=(n_blocks,),
        in_specs=[pl.BlockSpec((cb, D_MODEL), lambda c: (c, 0))],
        out_specs=pl.BlockSpec((cb, D_MODEL), lambda c: (c, 0)),
        out_shape=jax.ShapeDtypeStruct((d_ff, D_MODEL), BF16),
        compiler_params=_cparams(1), name="cast_ffn_down",
    )(w_down)
    return up, down


def _ffn_weight_specs(n_blocks, cb):
    return [
        pl.BlockSpec((1, D_MODEL), lambda i, c: (0, 0)),
        pl.BlockSpec((1, D_MODEL), lambda i, c: (0, 0)),
        pl.BlockSpec((1, D_MODEL, 2 * cb), lambda i, c: (c, 0, 0)),
        pl.BlockSpec((FFN_CONV, cb), lambda i, c: (0, c)),
        pl.BlockSpec((FFN_CONV, cb), lambda i, c: (0, n_blocks + c)),
        pl.BlockSpec((1, cb), lambda i, c: (0, c)),
        pl.BlockSpec((1, cb), lambda i, c: (0, n_blocks + c)),
        pl.BlockSpec((cb, D_MODEL), lambda i, c: (c, 0)),
    ]


def _ffn_prompt(x1, ada, p, wup_b, wdn_b, n_seq, tm, tiles_per_seq, ada_row_block):
    rows = x1.shape[0]
    n_blocks, _, cb2 = wup_b.shape
    cb = cb2 // 2
    ada_spec = lambda k: pl.BlockSpec((SUBLANES, D_MODEL), lambda i, c: (ada_row_block, k))
    in_specs = [pl.BlockSpec((tm, D_MODEL), lambda i, c: (i, 0), pipeline_mode=pl.Buffered(1)),
                ada_spec(3), ada_spec(4), ada_spec(5)] + _ffn_weight_specs(n_blocks, cb)
    out_specs = [
        pl.BlockSpec((tm, D_MODEL), lambda i, c: (i, 0)),
        pl.BlockSpec((1, FFN_CONV - 1, 2, cb), lambda i, c: (i, 0, 0, c)),
    ]
    out_shape = [jax.ShapeDtypeStruct((rows, D_MODEL), F32),
                 jax.ShapeDtypeStruct((rows // tm, FFN_CONV - 1, 2, n_blocks * cb), F32)]
    scratch = [
        pltpu.VMEM((tm, D_MODEL), BF16),
        pltpu.VMEM((SUBLANES + tm, 2 * cb), F32),
        pltpu.VMEM((tm, cb), BF16),
        pltpu.VMEM((n_blocks, SUBLANES, 2 * cb), F32),
    ]
    return pl.pallas_call(
        functools.partial(_ffn_prompt_kernel, tm=tm, tiles_per_seq=tiles_per_seq,
                          n_blocks=n_blocks, cb=cb),
        grid=(rows // tm, n_blocks),
        in_specs=in_specs, out_specs=out_specs, out_shape=out_shape,
        scratch_shapes=scratch, compiler_params=_cparams(2), name="ffn_prompt",
    )(x1, ada, ada, ada, p["g_pre2"], p["g_post2"], wup_b, p["w_ffn_conv"], p["w_ffn_conv"],
      p["b_ffn_conv"], p["b_ffn_conv"], wdn_b)


def _ffn_sample(x1, ada, p, wup_b, wdn_b, sffn2, n_seq, n_t):
    rows = n_seq * n_t
    n_blocks, _, cb2 = wup_b.shape
    cb = cb2 // 2
    nk = FFN_CONV - 1
    ada_spec = lambda k: pl.BlockSpec((n_seq, D_MODEL), lambda i, c: (0, k))
    in_specs = [pl.BlockSpec((rows, D_MODEL), lambda i, c: (0, 0), pipeline_mode=pl.Buffered(1)),
                ada_spec(3), ada_spec(4), ada_spec(5)] + _ffn_weight_specs(n_blocks, cb) + [
        pl.BlockSpec((n_seq * nk, cb), lambda i, c: (0, c)),
        pl.BlockSpec((n_seq * nk, cb), lambda i, c: (0, n_blocks + c)),
    ]
    out_specs = [
        pl.BlockSpec((rows, D_MODEL), lambda i, c: (0, 0)),
        pl.BlockSpec((n_seq * nk, cb), lambda i, c: (0, c)),
        pl.BlockSpec((n_seq * nk, cb), lambda i, c: (0, c)),
    ]
    out_shape = [jax.ShapeDtypeStruct((rows, D_MODEL), F32),
                 jax.ShapeDtypeStruct((n_seq * nk, n_blocks * cb), F32),
                 jax.ShapeDtypeStruct((n_seq * nk, n_blocks * cb), F32)]
    scratch = [
        pltpu.VMEM((rows, D_MODEL), BF16),
        pltpu.VMEM((rows, 2 * cb), F32),
        pltpu.VMEM((rows, cb), BF16),
        pltpu.VMEM((rows, D_MODEL), F32),
    ]
    return pl.pallas_call(
        functools.partial(_ffn_sample_kernel, n_seq=n_seq, n_t=n_t, n_blocks=n_blocks, cb=cb),
        grid=(1, n_blocks),
        in_specs=in_specs, out_specs=out_specs, out_shape=out_shape,
        scratch_shapes=scratch, compiler_params=_cparams(2), name="ffn_sample",
    )(x1, ada, ada, ada, p["g_pre2"], p["g_post2"], wup_b, p["w_ffn_conv"], p["w_ffn_conv"],
      p["b_ffn_conv"], p["b_ffn_conv"], wdn_b, sffn2, sffn2)


TOKEN_TILE = 1024
OUTPROJ_TILE = 512


def kernel(x_prompt, x_sample, c_prompt, c_sample, state_pool, state_lru_conv, state_lru_h, state_ffn_conv, w_ada, b_ada, g_pre1, g_post1, g_pre2, g_post2, w_in, w_pool_grp, pool_scale, w_lru_conv, b_lru_conv, w_rg, b_rg, w_ig, b_ig, lru_lambda, w_pool_up, w_lru_up, w_out, w_ffn_up, w_ffn_conv, b_ffn_conv, w_ffn_down):
    batch, seq, d = x_prompt.shape
    dec_batch, dec_seq, _ = x_sample.shape
    depth = w_ada.shape[0]
    assert d == D_MODEL and dec_batch % SUBLANES == 0 and seq % TOKEN_TILE == 0
    assert w_in.shape[2] == N_MIX_BLOCKS * CB + 2 * D_MODEL

    pad = (-batch) % SUBLANES
    c_all = jnp.concatenate([c_sample, c_prompt, jnp.zeros((pad, d), c_prompt.dtype)], axis=0)
    prompt_row_block = dec_batch // SUBLANES

    vec_names = ("g_pre1", "g_post1", "g_pre2", "g_post2", "pool_scale", "b_lru_conv", "b_rg",
                 "b_ig", "lru_lambda", "b_ffn_conv")
    weights = dict(w_ada=w_ada, b_ada=b_ada, g_pre1=g_pre1, g_post1=g_post1, g_pre2=g_pre2,
                   g_post2=g_post2, w_in=w_in, w_pool_grp=w_pool_grp, pool_scale=pool_scale,
                   w_lru_conv=w_lru_conv, b_lru_conv=b_lru_conv, w_rg=w_rg, b_rg=b_rg, w_ig=w_ig,
                   b_ig=b_ig, lru_lambda=lru_lambda, w_pool_up=w_pool_up, w_lru_up=w_lru_up,
                   w_out=w_out, w_ffn_up=w_ffn_up, w_ffn_conv=w_ffn_conv, b_ffn_conv=b_ffn_conv,
                   w_ffn_down=w_ffn_down)

    def time_major(a):
        return jnp.swapaxes(a, 0, 1).reshape(-1, a.shape[-1])

    def seq_major(a2, n_rows):
        return jnp.swapaxes(a2.reshape(n_rows, dec_batch, -1), 0, 1)

    xp = x_prompt.reshape(batch * seq, d)
    xs = time_major(x_sample)
    tps = seq // TOKEN_TILE
    outs_p = ([], [], [], [])
    outs_s = ([], [], [], [])
    for l in range(depth):
        p = {k: v[l] for k, v in weights.items()}
        for k in vec_names + ("b_ada",):
            p[k] = p[k].reshape(1, -1)
        ada = _ada(c_all, p["w_ada"], p["b_ada"])
        wup_b, wdn_b = _ffn_cast_weights(p["w_ffn_up"], p["w_ffn_down"], FFN_CB)

        y, npool, nconv, nh, h = _mix_prompt(xp, ada, p, batch, seq, prompt_row_block)
        m = _merge(h, y, p, TOKEN_TILE, "merge_prompt")
        x1 = _outproj(m, xp, ada, p, OUTPROJ_TILE, seq // OUTPROJ_TILE, prompt_row_block, 0)
        xp, nffn = _ffn_prompt(x1, ada, p, wup_b, wdn_b, batch, TOKEN_TILE, tps, prompt_row_block)
        outs_p[0].append(npool)
        outs_p[1].append(nconv)
        outs_p[2].append(nh.reshape(batch, LRU_WIDTH))
        outs_p[3].append(nffn[tps - 1::tps].reshape(batch, FFN_CONV - 1, -1))

        rows_s = dec_batch * dec_seq
        y, npool, nconv, nh, h = _mix_sample(
            xs, ada, p, time_major(state_pool[l]), time_major(state_lru_conv[l]),
            state_lru_h[l], dec_batch, dec_seq, PAST_LEN)
        m = _merge(h, y, p, rows_s, "merge_sample")
        x1 = _outproj(m, xs, ada, p, rows_s, 1, 0, dec_seq)
        xs, ng, nv = _ffn_sample(x1, ada, p, wup_b, wdn_b, time_major(state_ffn_conv[l]),
                                 dec_batch, dec_seq)
        outs_s[0].append(seq_major(npool, POOL_BUF))
        outs_s[1].append(seq_major(nconv, LRU_CONV - 1))
        outs_s[2].append(nh)
        outs_s[3].append(jnp.concatenate([seq_major(ng, FFN_CONV - 1),
                                          seq_major(nv, FFN_CONV - 1)], axis=-1))

    return (xp.reshape(batch, seq, d), seq_major(xs, dec_seq),
            jnp.stack(outs_p[0]), jnp.stack(outs_p[1]), jnp.stack(outs_p[2]), jnp.stack(outs_p[3]),
            jnp.stack(outs_s[0]), jnp.stack(outs_s[1]), jnp.stack(outs_s[2]), jnp.stack(outs_s[3]))
```

```python
import functools

import jax
import jax.numpy as jnp
from jax import lax
from jax.experimental import pallas as pl
from jax.experimental.pallas import tpu as pltpu

F32 = jnp.float32
BF16 = jnp.bfloat16

D_MODEL = 2048
POOL_WINDOWS = (2, 4, 8, 16)
POOL_GROUPS = len(POOL_WINDOWS)
POOL_BUF = max(POOL_WINDOWS) - 1
LRU_CONV = 4
LRU_C = 8.0
PAST_LEN = 16384
FFN_CONV = 3
N_ADA = 6
EPS = 1e-6

CB = 256
POOL_WIDTH = POOL_GROUPS * CB
LRU_BLOCKS = 8
LRU_WIDTH = LRU_BLOCKS * CB
N_MIX_BLOCKS = POOL_GROUPS + LRU_BLOCKS
HALO = 16
SUBLANES = 8
VMEM_LIMIT = 60 * 1024 * 1024


def _cparams(n_axes):
    return pltpu.CompilerParams(
        dimension_semantics=("arbitrary",) * n_axes, vmem_limit_bytes=VMEM_LIMIT)


def _dot(a, b):
    return jnp.dot(a, b, preferred_element_type=F32)


def _softplus(z):
    return jnp.maximum(z, 0.0) + jnp.log1p(jnp.exp(-jnp.abs(z)))


def _lru_coeffs(xc, r_pre, i_pre, neg_c_sp):
    r = jax.nn.sigmoid(r_pre)
    i = jax.nn.sigmoid(i_pre)
    log_a = r * neg_c_sp
    a = jnp.exp(log_a)
    b = jnp.sqrt(-jnp.tanh(log_a) * (a * a + 1.0)) * (i * xc)
    return a, b


def _ada_kernel(c_ref, w_ref, b_ref, o_ref):
    c = c_ref[...]
    s = (c * jax.nn.sigmoid(c)).astype(BF16)
    o_ref[...] = _dot(s, w_ref[...].astype(BF16)) + b_ref[...]


def _ada(c_all, w_ada, b_ada):
    rows = c_all.shape[0]
    tn = 1024
    return pl.pallas_call(
        _ada_kernel,
        grid=(N_ADA * D_MODEL // tn,),
        in_specs=[pl.BlockSpec((rows, D_MODEL), lambda n: (0, 0)),
                  pl.BlockSpec((D_MODEL, tn), lambda n: (0, n)),
                  pl.BlockSpec((1, tn), lambda n: (0, n))],
        out_specs=pl.BlockSpec((rows, tn), lambda n: (0, n)),
        out_shape=jax.ShapeDtypeStruct((rows, N_ADA * D_MODEL), F32),
        compiler_params=_cparams(1),
        name="ada",
    )(c_all, w_ada, b_ada)


NORM_ROWS = 32
NORM_UNROLL = 4
OUT_SUBTILE = 256


def _unit_rms(x):
    return x * lax.rsqrt(jnp.mean(x * x, axis=-1, keepdims=True) + EPS)


def _build_h_prompt(x_ref, h_ref, g, scale1p, shift, n_rows):
    gs = g * scale1p

    def body(i, carry):
        r0 = pl.multiple_of(i * NORM_ROWS, NORM_ROWS)
        x = x_ref[pl.ds(r0, NORM_ROWS), :]
        h_ref[pl.ds(r0, NORM_ROWS), :] = (_unit_rms(x) * gs + shift).astype(BF16)
        return carry
    lax.fori_loop(0, n_rows // NORM_ROWS, body, 0, unroll=NORM_UNROLL)


def _residual_norm_prompt(x_ref, acc_ref, o_ref, g, gate, n_rows):
    gg = g * gate

    def body(i, carry):
        sl = pl.ds(pl.multiple_of(i * NORM_ROWS, NORM_ROWS), NORM_ROWS)
        o_ref[sl, :] = x_ref[sl, :] + _unit_rms(acc_ref[sl, :]) * gg
        return carry
    lax.fori_loop(0, n_rows // NORM_ROWS, body, 0, unroll=NORM_UNROLL)


def _build_h_sample(x_ref, h_ref, g, sc_ref, sh_ref, n_seq, n_t):
    for s0 in range(0, n_seq, NORM_ROWS):
        gs = g * (1.0 + sc_ref[s0:s0 + NORM_ROWS, :])
        shift = sh_ref[s0:s0 + NORM_ROWS, :]
        for t in range(n_t):
            sl = slice(t * n_seq + s0, t * n_seq + s0 + NORM_ROWS)
            h_ref[sl, :] = (_unit_rms(x_ref[sl, :]) * gs + shift).astype(BF16)


MIX_SUBTILE = 512
POOL_ROWS = 128
LRU_ROWS = 64


def _mix_prompt_kernel(x_ref, sh_ref, sc_ref, g_ref, win_ref, wgrp_ref, pscale_ref,
                       wconv_ref, bconv_ref, wrg_ref, brg_ref, wig_ref, big_ref, lam_ref,
                       y_ref, npool_ref, nconv_ref, nh_ref, h_ref,
                       ext_ref, xb_ref, r_ref, i_ref, a_ref, u_ref, *, seq):
    q = pl.program_id(0)
    j = pl.program_id(1)

    @pl.when(j == 0)
    def _():
        _build_h_prompt(x_ref, h_ref, g_ref[...], 1.0 + sc_ref[pl.ds(q, 1), :],
                        sh_ref[pl.ds(q, 1), :], seq)
        for ref in (ext_ref, r_ref, i_ref, a_ref):
            ref[0:HALO, :] = jnp.zeros((HALO, CB), F32)

    win = win_ref[...].astype(BF16)

    def rows_of(r0, n, shift=0):
        return slice(HALO + r0 - shift, HALO + r0 - shift + n)

    def up_proj(m0):
        ext_ref[rows_of(m0, MIX_SUBTILE), :] = _dot(h_ref[m0:m0 + MIX_SUBTILE, :], win)

    def pool_branch(w):
        wg = wgrp_ref[0].astype(BF16)
        ps = pscale_ref[...]
        partial = {2: r_ref, 4: i_ref, 8: a_ref}
        up_proj(0)
        for m0 in range(0, seq, MIX_SUBTILE):
            if m0 + MIX_SUBTILE < seq:
                up_proj(m0 + MIX_SUBTILE)
            for r0 in range(m0, m0 + MIX_SUBTILE, POOL_ROWS):
                u = ext_ref[rows_of(r0, POOL_ROWS), :]
                s, src, width = u, ext_ref, 1
                while width < w:
                    s = s + src[rows_of(r0, POOL_ROWS, width), :]
                    width *= 2
                    if width < w:
                        src = partial[width]
                        src[rows_of(r0, POOL_ROWS), :] = s
                if r0 < w:
                    pos = r0 + lax.broadcasted_iota(jnp.int32, (POOL_ROWS, 1), 0)
                    cnt = jnp.minimum(w, pos + 1).astype(F32)
                else:
                    cnt = float(w)
                xb_ref[r0:r0 + POOL_ROWS, :] = (s / cnt - u).astype(BF16)
            sub = slice(m0, m0 + MIX_SUBTILE)
            y_ref[sub, :] = (_dot(xb_ref[sub, :], wg) * ps).astype(BF16)
        npool_ref[0] = ext_ref[pl.ds(HALO + seq - POOL_BUF, POOL_BUF), :]

    for g, w in enumerate(POOL_WINDOWS):
        pl.when(j == g)(functools.partial(pool_branch, w))

    @pl.when(j >= POOL_GROUPS)
    def _():
        wc = wconv_ref[...]
        bc = bconv_ref[...]
        wrg = wrg_ref[0].astype(BF16)
        wig = wig_ref[0].astype(BF16)
        neg_c_sp = (-LRU_C) * _softplus(-lam_ref[...])
        brg = brg_ref[...]
        big = big_ref[...]
        nb = LRU_ROWS // SUBLANES
        row = lax.broadcasted_iota(jnp.int32, (nb, SUBLANES, CB), 1)
        h_carry = jnp.zeros((1, CB), F32)
        up_proj(0)
        for m0 in range(0, seq, MIX_SUBTILE):
            sub = slice(m0, m0 + MIX_SUBTILE)
            if m0 + MIX_SUBTILE < seq:
                up_proj(m0 + MIX_SUBTILE)
            for r0 in range(m0, m0 + MIX_SUBTILE, LRU_ROWS):
                xc = wc[LRU_CONV - 1:, :] * ext_ref[rows_of(r0, LRU_ROWS), :] + bc
                for k in range(LRU_CONV - 1):
                    xc = xc + wc[k:k + 1, :] * ext_ref[rows_of(r0, LRU_ROWS, LRU_CONV - 1 - k), :]
                a_ref[rows_of(r0, LRU_ROWS), :] = xc
                xb_ref[r0:r0 + LRU_ROWS, :] = xc.astype(BF16)
            r_ref[rows_of(m0, MIX_SUBTILE), :] = _dot(xb_ref[sub, :], wrg)
            i_ref[rows_of(m0, MIX_SUBTILE), :] = _dot(xb_ref[sub, :], wig)
            for r0 in range(m0, m0 + MIX_SUBTILE, LRU_ROWS):
                sl = rows_of(r0, LRU_ROWS)
                a, b = _lru_coeffs(a_ref[sl, :], r_ref[sl, :] + brg, i_ref[sl, :] + big, neg_c_sp)
                a = a.reshape(nb, SUBLANES, CB)
                b = b.reshape(nb, SUBLANES, CB)
                for k in (1, 2, 4):
                    a_sh = jnp.where(row >= k, pltpu.roll(a, k, 1), 1.0)
                    b_sh = jnp.where(row >= k, pltpu.roll(b, k, 1), 0.0)
                    b = b + a * b_sh
                    a = a * a_sh
                a_ref[sl, :] = a.reshape(LRU_ROWS, CB)
                u_ref[sl, :] = b.reshape(LRU_ROWS, CB)
            for r0 in range(m0, m0 + MIX_SUBTILE, SUBLANES):
                sl = rows_of(r0, SUBLANES)
                h8 = a_ref[sl, :] * h_carry + u_ref[sl, :]
                u_ref[sl, :] = h8
                h_carry = h8[SUBLANES - 1:SUBLANES, :]
            y_ref[sub, :] = u_ref[rows_of(m0, MIX_SUBTILE), :].astype(BF16)
        nh_ref[0] = h_carry
        nconv_ref[0] = ext_ref[pl.ds(HALO + seq - (LRU_CONV - 1), LRU_CONV - 1), :]


def _mix_sample_kernel(x_ref, sh_ref, sc_ref, g_ref, win_ref, wgrp_ref, pscale_ref,
                       wconv_ref, bconv_ref, wrg_ref, brg_ref, wig_ref, big_ref, lam_ref,
                       spool_ref, sconv_ref, sh0_ref,
                       y_ref, npool_ref, nconv_ref, nh_ref, h_ref,
                       u_ref, d_ref, *, n_seq, n_t, start):
    j = pl.program_id(1)

    @pl.when(j == 0)
    def _():
        _build_h_sample(x_ref, h_ref, g_ref[...], sc_ref, sh_ref, n_seq, n_t)

    u_ref[...] = _dot(h_ref[...], win_ref[...].astype(BF16))

    def u_slab(t):
        return u_ref[t * n_seq:(t + 1) * n_seq, :]

    def pool_branch(w):
        e = [spool_ref[k * n_seq:(k + 1) * n_seq, :] for k in range(POOL_BUF)]
        e += [u_slab(t) for t in range(n_t)]
        for k in range(POOL_BUF):
            npool_ref[k * n_seq:(k + 1) * n_seq, :] = e[n_t + k]
        for t in range(n_t):
            s = e[POOL_BUF + t]
            for k in range(1, w):
                s = s + e[POOL_BUF + t - k]
            cnt = float(min(w, start + t + 1))
            d_ref[t * n_seq:(t + 1) * n_seq, :] = (s / cnt - e[POOL_BUF + t]).astype(BF16)
        y = _dot(d_ref[...], wgrp_ref[0].astype(BF16)) * pscale_ref[...]
        y_ref[...] = y.astype(BF16)

    for g, w in enumerate(POOL_WINDOWS):
        pl.when(j == g)(functools.partial(pool_branch, w))

    @pl.when(j >= POOL_GROUPS)
    def _():
        nk = LRU_CONV - 1
        e = [sconv_ref[k * n_seq:(k + 1) * n_seq, :] for k in range(nk)]
        e += [u_slab(t) for t in range(n_t)]
        for k in range(nk):
            nconv_ref[k * n_seq:(k + 1) * n_seq, :] = e[n_t + k]
        wc = wconv_ref[...]
        bc = bconv_ref[...]
        wrg = wrg_ref[0].astype(BF16)
        wig = wig_ref[0].astype(BF16)
        neg_c_sp = (-LRU_C) * _softplus(-lam_ref[...])
        h = sh0_ref[...]
        for t in range(n_t):
            xc = bc
            for k in range(LRU_CONV):
                xc = xc + wc[k:k + 1, :] * e[t + k]
            xb = xc.astype(BF16)
            a, b = _lru_coeffs(xc, _dot(xb, wrg) + brg_ref[...], _dot(xb, wig) + big_ref[...],
                               neg_c_sp)
            h = a * h + b
            y_ref[t * n_seq:(t + 1) * n_seq, :] = h.astype(BF16)
        nh_ref[...] = h


def _mix_weight_specs():
    pj = lambda j: jnp.minimum(j, POOL_GROUPS - 1)
    lj = lambda j: jnp.maximum(j - POOL_GROUPS, 0)
    return [
        pl.BlockSpec((1, D_MODEL), lambda q, j: (0, 0)),
        pl.BlockSpec((D_MODEL, CB), lambda q, j: (0, j)),
        pl.BlockSpec((1, CB, CB), lambda q, j: (pj(j), 0, 0)),
        pl.BlockSpec((1, CB), lambda q, j: (0, pj(j))),
        pl.BlockSpec((LRU_CONV, CB), lambda q, j: (0, lj(j))),
        pl.BlockSpec((1, CB), lambda q, j: (0, lj(j))),
        pl.BlockSpec((1, CB, CB), lambda q, j: (lj(j), 0, 0)),
        pl.BlockSpec((1, CB), lambda q, j: (0, lj(j))),
        pl.BlockSpec((1, CB, CB), lambda q, j: (lj(j), 0, 0)),
        pl.BlockSpec((1, CB), lambda q, j: (0, lj(j))),
        pl.BlockSpec((1, CB), lambda q, j: (0, lj(j))),
    ], pj, lj


def _mix_prompt(x2, ada, p, n_seq, seq, ada_row_block):
    wspecs, pj, lj = _mix_weight_specs()
    in_specs = [
        pl.BlockSpec((seq, D_MODEL), lambda q, j: (q, 0), pipeline_mode=pl.Buffered(1)),
        pl.BlockSpec((SUBLANES, D_MODEL), lambda q, j: (ada_row_block, 0)),
        pl.BlockSpec((SUBLANES, D_MODEL), lambda q, j: (ada_row_block, 1)),
    ] + wspecs
    out_specs = [
        pl.BlockSpec((seq, CB), lambda q, j: (q, j)),
        pl.BlockSpec((1, POOL_BUF, CB), lambda q, j: (q, 0, pj(j))),
        pl.BlockSpec((1, LRU_CONV - 1, CB), lambda q, j: (q, 0, lj(j))),
        pl.BlockSpec((1, 1, CB), lambda q, j: (q, 0, lj(j))),
        pl.BlockSpec((seq, D_MODEL), lambda q, j: (q, 0)),
    ]
    out_shape = [
        jax.ShapeDtypeStruct((n_seq * seq, N_MIX_BLOCKS * CB), BF16),
        jax.ShapeDtypeStruct((n_seq, POOL_BUF, POOL_WIDTH), F32),
        jax.ShapeDtypeStruct((n_seq, LRU_CONV - 1, LRU_WIDTH), F32),
        jax.ShapeDtypeStruct((n_seq, 1, LRU_WIDTH), F32),
        jax.ShapeDtypeStruct((n_seq * seq, D_MODEL), BF16),
    ]
    scratch = [
        pltpu.VMEM((HALO + seq, CB), F32),
        pltpu.VMEM((seq, CB), BF16),
        pltpu.VMEM((HALO + seq, CB), F32),
        pltpu.VMEM((HALO + seq, CB), F32),
        pltpu.VMEM((HALO + seq, CB), F32),
        pltpu.VMEM((HALO + seq, CB), F32),
    ]
    return pl.pallas_call(
        functools.partial(_mix_prompt_kernel, seq=seq),
        grid=(n_seq, N_MIX_BLOCKS),
        in_specs=in_specs, out_specs=out_specs, out_shape=out_shape,
        scratch_shapes=scratch, compiler_params=_cparams(2), name="mix_prompt",
    )(x2, ada, ada, p["g_pre1"], p["w_in"], p["w_pool_grp"], p["pool_scale"],
      p["w_lru_conv"], p["b_lru_conv"], p["w_rg"], p["b_rg"], p["w_ig"], p["b_ig"],
      p["lru_lambda"])


def _mix_sample(x2, ada, p, spool2, sconv2, sh0, n_seq, n_t, start):
    wspecs, pj, lj = _mix_weight_specs()
    rows = n_seq * n_t
    in_specs = [
        pl.BlockSpec((rows, D_MODEL), lambda q, j: (0, 0), pipeline_mode=pl.Buffered(1)),
        pl.BlockSpec((n_seq, D_MODEL), lambda q, j: (0, 0)),
        pl.BlockSpec((n_seq, D_MODEL), lambda q, j: (0, 1)),
    ] + wspecs + [
        pl.BlockSpec((n_seq * POOL_BUF, CB), lambda q, j: (0, pj(j))),
        pl.BlockSpec((n_seq * (LRU_CONV - 1), CB), lambda q, j: (0, lj(j))),
        pl.BlockSpec((n_seq, CB), lambda q, j: (0, lj(j))),
    ]
    out_specs = [
        pl.BlockSpec((rows, CB), lambda q, j: (0, j)),
        pl.BlockSpec((n_seq * POOL_BUF, CB), lambda q, j: (0, pj(j))),
        pl.BlockSpec((n_seq * (LRU_CONV - 1), CB), lambda q, j: (0, lj(j))),
        pl.BlockSpec((n_seq, CB), lambda q, j: (0, lj(j))),
        pl.BlockSpec((rows, D_MODEL), lambda q, j: (0, 0)),
    ]
    out_shape = [
        jax.ShapeDtypeStruct((rows, N_MIX_BLOCKS * CB), BF16),
        jax.ShapeDtypeStruct((n_seq * POOL_BUF, POOL_WIDTH), F32),
        jax.ShapeDtypeStruct((n_seq * (LRU_CONV - 1), LRU_WIDTH), F32),
        jax.ShapeDtypeStruct((n_seq, LRU_WIDTH), F32),
        jax.ShapeDtypeStruct((rows, D_MODEL), BF16),
    ]
    scratch = [
        pltpu.VMEM((rows, CB), F32),
        pltpu.VMEM((rows, CB), BF16),
    ]
    return pl.pallas_call(
        functools.partial(_mix_sample_kernel, n_seq=n_seq, n_t=n_t, start=start),
        grid=(1, N_MIX_BLOCKS),
        in_specs=in_specs, out_specs=out_specs, out_shape=out_shape,
        scratch_shapes=scratch, compiler_params=_cparams(2), name="mix_sample",
    )(x2, ada, ada, p["g_pre1"], p["w_in"], p["w_pool_grp"], p["pool_scale"],
      p["w_lru_conv"], p["b_lru_conv"], p["w_rg"], p["b_rg"], p["w_ig"], p["b_ig"],
      p["lru_lambda"], spool2, sconv2, sh0)


MERGE_ROWS = 64
MERGE_SUBTILE = 256
CAST_ROWS = 32
BF16_ROWS = 16


def _cast_ffn_slab(wup_ref, wdn_ref, wupb_ref, wdnb_ref):
    n_blocks, _, cb2 = wupb_ref.shape
    cb = cb2 // 2
    for blk in range(n_blocks):
        wupb_ref[blk, :, 0:cb] = wup_ref[:, blk * cb:(blk + 1) * cb].astype(BF16)
        wupb_ref[blk, :, cb:cb2] = wup_ref[:, (n_blocks + blk) * cb:(n_blocks + blk + 1) * cb].astype(BF16)
    for r0 in range(0, wdn_ref.shape[0], CAST_ROWS):
        wdnb_ref[r0:r0 + CAST_ROWS, :] = wdn_ref[r0:r0 + CAST_ROWS, :].astype(BF16)


def _merge_kernel(h_ref, y_ref, wgp_ref, wgl_ref, wpu_ref, wlu_ref, *rest, tm, cast_ffn):
    if cast_ffn:
        wup_ref, wdn_ref, m_ref, wupb_ref, wdnb_ref, wg_ref, g_ref, pu_ref, lu_ref = rest
    else:
        m_ref, wg_ref, g_ref, pu_ref, lu_ref = rest
    wg_ref[:, 0:CB] = wgp_ref[...].astype(BF16)
    wg_ref[:, CB:2 * CB] = wgl_ref[...].astype(BF16)
    wpu = wpu_ref[...].astype(BF16)
    wlu = wlu_ref[...].astype(BF16)

    def proj(m0):
        rows = slice(m0, m0 + MERGE_SUBTILE)
        g_ref[rows, :] = _dot(h_ref[rows, :], wg_ref[...])
        pu_ref[rows, :] = _dot(y_ref[rows, 0:POOL_WIDTH], wpu)
        lu_ref[rows, :] = _dot(y_ref[rows, POOL_WIDTH:POOL_WIDTH + LRU_WIDTH], wlu)

    proj(0)
    for m0 in range(0, tm, MERGE_SUBTILE):
        if m0 + MERGE_SUBTILE < tm:
            proj(m0 + MERGE_SUBTILE)
        for r0 in range(m0, m0 + MERGE_SUBTILE, MERGE_ROWS):
            sl = slice(r0, r0 + MERGE_ROWS)
            m = (jax.nn.sigmoid(g_ref[sl, 0:CB]) * pu_ref[sl, :]
                 + jax.nn.sigmoid(g_ref[sl, CB:2 * CB]) * lu_ref[sl, :])
            m_ref[sl, :] = m.astype(BF16)
    if cast_ffn:
        _cast_ffn_slab(wup_ref, wdn_ref, wupb_ref, wdnb_ref)


def _merge(h, y, p, tm, name, cast_ffn_cb=0):
    rows = h.shape[0]
    n_col = D_MODEL // CB
    n_gate0 = N_MIX_BLOCKS
    n_gate1 = N_MIX_BLOCKS + n_col
    in_specs = [
        pl.BlockSpec((tm, D_MODEL), lambda i, c: (i, 0), pipeline_mode=pl.Buffered(1)),
        pl.BlockSpec((tm, N_MIX_BLOCKS * CB), lambda i, c: (i, 0), pipeline_mode=pl.Buffered(1)),
        pl.BlockSpec((D_MODEL, CB), lambda i, c: (0, n_gate0 + c)),
        pl.BlockSpec((D_MODEL, CB), lambda i, c: (0, n_gate1 + c)),
        pl.BlockSpec((POOL_WIDTH, CB), lambda i, c: (0, c)),
        pl.BlockSpec((LRU_WIDTH, CB), lambda i, c: (0, c)),
    ]
    out_specs = [pl.BlockSpec((tm, CB), lambda i, c: (i, c))]
    out_shape = [jax.ShapeDtypeStruct((rows, D_MODEL), BF16)]
    args = [h, y, p["w_in"], p["w_in"], p["w_pool_up"], p["w_lru_up"]]
    if cast_ffn_cb:
        n_steps = (rows // tm) * n_col
        d_ff = p["w_ffn_down"].shape[0]
        n_blocks = d_ff // cast_ffn_cb
        up_rows, dn_rows = D_MODEL // n_steps, d_ff // n_steps
        assert up_rows * n_steps == D_MODEL and dn_rows * n_steps == d_ff
        assert up_rows % BF16_ROWS == 0 and dn_rows % BF16_ROWS == 0
        step = lambda i, c: i * n_col + c
        in_specs += [pl.BlockSpec((up_rows, 2 * d_ff), lambda i, c: (step(i, c), 0)),
                     pl.BlockSpec((dn_rows, D_MODEL), lambda i, c: (step(i, c), 0))]
        out_specs += [pl.BlockSpec((n_blocks, up_rows, 2 * cast_ffn_cb),
                                   lambda i, c: (0, step(i, c), 0)),
                      pl.BlockSpec((dn_rows, D_MODEL), lambda i, c: (step(i, c), 0))]
        out_shape += [jax.ShapeDtypeStruct((n_blocks, D_MODEL, 2 * cast_ffn_cb), BF16),
                      jax.ShapeDtypeStruct((d_ff, D_MODEL), BF16)]
        args += [p["w_ffn_up"], p["w_ffn_down"]]
    return pl.pallas_call(
        functools.partial(_merge_kernel, tm=tm, cast_ffn=bool(cast_ffn_cb)),
        grid=(rows // tm, n_col),
        in_specs=in_specs, out_specs=out_specs, out_shape=out_shape,
        scratch_shapes=[pltpu.VMEM((D_MODEL, 2 * CB), BF16), pltpu.VMEM((tm, 2 * CB), F32),
                        pltpu.VMEM((tm, CB), F32), pltpu.VMEM((tm, CB), F32)],
        compiler_params=_cparams(2), name=name,
    )(*args)


def _cast_weight(w_ref, wb_ref):
    rows = w_ref.shape[0]
    for r0 in range(0, rows, 256):
        wb_ref[r0:r0 + 256, :] = w_ref[r0:r0 + 256, :].astype(BF16)


def _outproj_kernel(m_ref, x_ref, gate_ref, g_ref, w_ref, o_ref, wb_ref, acc_ref,
                    *, tm, tiles_per_seq, sample_t):
    i = pl.program_id(0)

    @pl.when(i == 0)
    def _():
        _cast_weight(w_ref, wb_ref)

    g = g_ref[...]
    if sample_t:
        n_seq = tm // sample_t
    else:
        gg_tile = g * gate_ref[pl.ds(i // tiles_per_seq, 1), :]

    def proj(m0):
        acc_ref[m0:m0 + OUT_SUBTILE, :] = _dot(m_ref[m0:m0 + OUT_SUBTILE, :], wb_ref[...])

    proj(0)
    for m0 in range(0, tm, OUT_SUBTILE):
        if m0 + OUT_SUBTILE < tm:
            proj(m0 + OUT_SUBTILE)
        for r0 in range(m0, m0 + OUT_SUBTILE, NORM_ROWS):
            sl = slice(r0, r0 + NORM_ROWS)
            if sample_t:
                s0 = r0 % n_seq
                gg = g * gate_ref[s0:s0 + NORM_ROWS, :]
            else:
                gg = gg_tile
            o_ref[sl, :] = x_ref[sl, :] + _unit_rms(acc_ref[sl, :]) * gg


def _outproj(m, x2, ada, p, tm, tiles_per_seq, ada_row_block, sample_t):
    rows = x2.shape[0]
    if sample_t:
        gate_spec = pl.BlockSpec((tm // sample_t, D_MODEL), lambda i: (0, 2))
    else:
        gate_spec = pl.BlockSpec((SUBLANES, D_MODEL), lambda i: (ada_row_block, 2))
    return pl.pallas_call(
        functools.partial(_outproj_kernel, tm=tm, tiles_per_seq=tiles_per_seq, sample_t=sample_t),
        grid=(rows // tm,),
        in_specs=[pl.BlockSpec((tm, D_MODEL), lambda i: (i, 0)),
                  pl.BlockSpec((tm, D_MODEL), lambda i: (i, 0)),
                  gate_spec,
                  pl.BlockSpec((1, D_MODEL), lambda i: (0, 0)),
                  pl.BlockSpec((D_MODEL, D_MODEL), lambda i: (0, 0), pipeline_mode=pl.Buffered(1))],
        out_specs=pl.BlockSpec((tm, D_MODEL), lambda i: (i, 0)),
        out_shape=jax.ShapeDtypeStruct((rows, D_MODEL), F32),
        scratch_shapes=[pltpu.VMEM((D_MODEL, D_MODEL), BF16), pltpu.VMEM((tm, D_MODEL), F32)],
        compiler_params=_cparams(1),
        name="outproj_sample" if sample_t else "outproj_prompt",
    )(m, x2, ada, p["g_post1"], p["w_out"])


GELU_C0 = 0.7978845608028654
GELU_C1 = GELU_C0 * 0.044715
FFN_ROWS = 32
FFN_SUBTILE = 256
FFN_CB = 512


def _ffn_conv_gate(eg, ev, wcg, wcv, bcg, bcv):
    last = FFN_CONV - 1
    g = wcg[last:, :] * eg[last] + bcg
    v = wcv[last:, :] * ev[last] + bcv
    for k in range(last):
        g = g + wcg[k:k + 1, :] * eg[k]
        v = v + wcv[k:k + 1, :] * ev[k]
    t = jnp.tanh(g * (GELU_C0 + GELU_C1 * (g * g)))
    return ((g * v) * (0.5 + 0.5 * t)).astype(BF16)


def _ffn_prompt_kernel(x_ref, sh_ref, sc_ref, gate_ref, gpre_ref, gpost_ref,
                       wup_ref, wcg_ref, wcv_ref, bcg_ref, bcv_ref, wdn_ref,
                       o_ref, nst_ref,
                       h_ref, ext_ref, f_ref, carry_ref, *, tm, tiles_per_seq, n_blocks, cb):
    i = pl.program_id(0)
    c = pl.program_id(1)
    q = i // tiles_per_seq
    first = (i % tiles_per_seq) == 0
    nk = FFN_CONV - 1

    @pl.when(c == 0)
    def _():
        _build_h_prompt(x_ref, h_ref, gpre_ref[...], 1.0 + sc_ref[pl.ds(q, 1), :],
                        sh_ref[pl.ds(q, 1), :], tm)
        o_ref[...] = jnp.zeros((tm, D_MODEL), F32)

    @pl.when(first)
    def _():
        ext_ref[0:SUBLANES, :] = jnp.zeros((SUBLANES, 2 * cb), F32)

    @pl.when(jnp.logical_not(first))
    def _():
        ext_ref[0:SUBLANES, :] = carry_ref[c]

    wcg = wcg_ref[...]
    wcv = wcv_ref[...]
    bcg = bcg_ref[...]
    bcv = bcv_ref[...]

    def up_proj(m0):
        ext_ref[SUBLANES + m0:SUBLANES + m0 + FFN_SUBTILE, :] = _dot(
            h_ref[m0:m0 + FFN_SUBTILE, :], wup_ref[0])

    up_proj(0)
    for m0 in range(0, tm, FFN_SUBTILE):
        rows = slice(m0, m0 + FFN_SUBTILE)
        if m0 + FFN_SUBTILE < tm:
            up_proj(m0 + FFN_SUBTILE)
        for r0 in range(m0, m0 + FFN_SUBTILE, FFN_ROWS):
            shifted = [slice(SUBLANES + r0 - nk + k, SUBLANES + r0 - nk + k + FFN_ROWS)
                       for k in range(FFN_CONV)]
            eg = [ext_ref[s, 0:cb] for s in shifted]
            ev = [ext_ref[s, cb:2 * cb] for s in shifted]
            f_ref[r0:r0 + FFN_ROWS, :] = _ffn_conv_gate(eg, ev, wcg, wcv, bcg, bcv)
        o_ref[rows, :] += _dot(f_ref[rows, :], wdn_ref[...])

    carry_ref[c] = ext_ref[tm:tm + SUBLANES, :]
    nst_ref[0, :, 0, :] = ext_ref[pl.ds(SUBLANES + tm - nk, nk), 0:cb]
    nst_ref[0, :, 1, :] = ext_ref[pl.ds(SUBLANES + tm - nk, nk), cb:2 * cb]

    @pl.when(c == n_blocks - 1)
    def _():
        _residual_norm_prompt(x_ref, o_ref, o_ref, gpost_ref[...], gate_ref[pl.ds(q, 1), :], tm)


def _ffn_sample_kernel(x_ref, sh_ref, sc_ref, gate_ref, gpre_ref, gpost_ref,
                       wup_ref, wcg_ref, wcv_ref, bcg_ref, bcv_ref, wdn_ref,
                       sg_ref, sv_ref,
                       o_ref, ng_ref, nv_ref,
                       h_ref, ext_ref, f_ref, acc_ref, *, n_seq, n_t, n_blocks, cb):
    c = pl.program_id(1)
    nk = FFN_CONV - 1

    @pl.when(c == 0)
    def _():
        _build_h_sample(x_ref, h_ref, gpre_ref[...], sc_ref, sh_ref, n_seq, n_t)
        acc_ref[...] = jnp.zeros((n_seq * n_t, D_MODEL), F32)

    wcg = wcg_ref[...]
    wcv = wcv_ref[...]
    bcg = bcg_ref[...]
    bcv = bcv_ref[...]

    def view(state_ref, half, t, s0, n):
        if t < 0:
            return state_ref[(nk + t) * n_seq + s0:(nk + t) * n_seq + s0 + n, :]
        return ext_ref[t * n_seq + s0:t * n_seq + s0 + n, half * cb:(half + 1) * cb]

    t_sub = max(FFN_SUBTILE // n_seq, 1)

    def up_proj(t0):
        rows = slice(t0 * n_seq, (t0 + t_sub) * n_seq)
        ext_ref[rows, :] = _dot(h_ref[rows, :], wup_ref[0])

    up_proj(0)
    for t0 in range(0, n_t, t_sub):
        rows = slice(t0 * n_seq, (t0 + t_sub) * n_seq)
        if t0 + t_sub < n_t:
            up_proj(t0 + t_sub)
        for t in range(t0, t0 + t_sub):
            for s0 in range(0, n_seq, FFN_ROWS):
                eg = [view(sg_ref, 0, t - nk + k, s0, FFN_ROWS) for k in range(FFN_CONV)]
                ev = [view(sv_ref, 1, t - nk + k, s0, FFN_ROWS) for k in range(FFN_CONV)]
                f_ref[t * n_seq + s0:t * n_seq + s0 + FFN_ROWS, :] = _ffn_conv_gate(
                    eg, ev, wcg, wcv, bcg, bcv)
        acc_ref[rows, :] += _dot(f_ref[rows, :], wdn_ref[...])

    for k in range(nk):
        ng_ref[k * n_seq:(k + 1) * n_seq, :] = view(sg_ref, 0, n_t - nk + k, 0, n_seq)
        nv_ref[k * n_seq:(k + 1) * n_seq, :] = view(sv_ref, 1, n_t - nk + k, 0, n_seq)

    @pl.when(c == n_blocks - 1)
    def _():
        g = gpost_ref[...]
        for s0 in range(0, n_seq, NORM_ROWS):
            gg = g * gate_ref[s0:s0 + NORM_ROWS, :]
            for t in range(n_t):
                sl = slice(t * n_seq + s0, t * n_seq + s0 + NORM_ROWS)
                o_ref[sl, :] = x_ref[sl, :] + _unit_rms(acc_ref[sl, :]) * gg


def _ffn_weight_specs(n_blocks, cb):
    return [
        pl.BlockSpec((1, D_MODEL), lambda i, c: (0, 0)),
        pl.BlockSpec((1, D_MODEL), lambda i, c: (0, 0)),
        pl.BlockSpec((1, D_MODEL, 2 * cb), lambda i, c: (c, 0, 0)),
        pl.BlockSpec((FFN_CONV, cb), lambda i, c: (0, c)),
        pl.BlockSpec((FFN_CONV, cb), lambda i, c: (0, n_blocks + c)),
        pl.BlockSpec((1, cb), lambda i, c: (0, c)),
        pl.BlockSpec((1, cb), lambda i, c: (0, n_blocks + c)),
        pl.BlockSpec((cb, D_MODEL), lambda i, c: (c, 0)),
    ]


def _ffn_prompt(x1, ada, p, wup_b, wdn_b, n_seq, tm, tiles_per_seq, ada_row_block):
    rows = x1.shape[0]
    n_blocks, _, cb2 = wup_b.shape
    cb = cb2 // 2
    ada_spec = lambda k: pl.BlockSpec((SUBLANES, D_MODEL), lambda i, c: (ada_row_block, k))
    in_specs = [pl.BlockSpec((tm, D_MODEL), lambda i, c: (i, 0), pipeline_mode=pl.Buffered(1)),
                ada_spec(3), ada_spec(4), ada_spec(5)] + _ffn_weight_specs(n_blocks, cb)
    out_specs = [
        pl.BlockSpec((tm, D_MODEL), lambda i, c: (i, 0)),
        pl.BlockSpec((1, FFN_CONV - 1, 2, cb), lambda i, c: (i, 0, 0, c)),
    ]
    out_shape = [jax.ShapeDtypeStruct((rows, D_MODEL), F32),
                 jax.ShapeDtypeStruct((rows // tm, FFN_CONV - 1, 2, n_blocks * cb), F32)]
    scratch = [
        pltpu.VMEM((tm, D_MODEL), BF16),
        pltpu.VMEM((SUBLANES + tm, 2 * cb), F32),
        pltpu.VMEM((tm, cb), BF16),
        pltpu.VMEM((n_blocks, SUBLANES, 2 * cb), F32),
    ]
    return pl.pallas_call(
        functools.partial(_ffn_prompt_kernel, tm=tm, tiles_per_seq=tiles_per_seq,
                          n_blocks=n_blocks, cb=cb),
        grid=(rows // tm, n_blocks),
        in_specs=in_specs, out_specs=out_specs, out_shape=out_shape,
        scratch_shapes=scratch, compiler_params=_cparams(2), name="ffn_prompt",
    )(x1, ada, ada, ada, p["g_pre2"], p["g_post2"], wup_b, p["w_ffn_conv"], p["w_ffn_conv"],
      p["b_ffn_conv"], p["b_ffn_conv"], wdn_b)


def _ffn_sample(x1, ada, p, wup_b, wdn_b, sffn2, n_seq, n_t):
    rows = n_seq * n_t
    n_blocks, _, cb2 = wup_b.shape
    cb = cb2 // 2
    nk = FFN_CONV - 1
    ada_spec = lambda k: pl.BlockSpec((n_seq, D_MODEL), lambda i, c: (0, k))
    in_specs = [pl.BlockSpec((rows, D_MODEL), lambda i, c: (0, 0), pipeline_mode=pl.Buffered(1)),
                ada_spec(3), ada_spec(4), ada_spec(5)] + _ffn_weight_specs(n_blocks, cb) + [
        pl.BlockSpec((n_seq * nk, cb), lambda i, c: (0, c)),
        pl.BlockSpec((n_seq * nk, cb), lambda i, c: (0, n_blocks + c)),
    ]
    out_specs = [
        pl.BlockSpec((rows, D_MODEL), lambda i, c: (0, 0)),
        pl.BlockSpec((n_seq * nk, cb), lambda i, c: (0, c)),
        pl.BlockSpec((n_seq * nk, cb), lambda i, c: (0, c)),
    ]
    out_shape = [jax.ShapeDtypeStruct((rows, D_MODEL), F32),
                 jax.ShapeDtypeStruct((n_seq * nk, n_blocks * cb), F32),
                 jax.ShapeDtypeStruct((n_seq * nk, n_blocks * cb), F32)]
    scratch = [
        pltpu.VMEM((rows, D_MODEL), BF16),
        pltpu.VMEM((rows, 2 * cb), F32),
        pltpu.VMEM((rows, cb), BF16),
        pltpu.VMEM((rows, D_MODEL), F32),
    ]
    return pl.pallas_call(
        functools.partial(_ffn_sample_kernel, n_seq=n_seq, n_t=n_t, n_blocks=n_blocks, cb=cb),
        grid=(1, n_blocks),
        in_specs=in_specs, out_specs=out_specs, out_shape=out_shape,
        scratch_shapes=scratch, compiler_params=_cparams(2), name="ffn_sample",
    )(x1, ada, ada, ada, p["g_pre2"], p["g_post2"], wup_b, p["w_ffn_conv"], p["w_ffn_conv"],
      p["b_ffn_conv"], p["b_ffn_conv"], wdn_b, sffn2, sffn2)


TOKEN_TILE = 1024
OUTPROJ_TILE = 512


def kernel(x_prompt, x_sample, c_prompt, c_sample, state_pool, state_lru_conv, state_lru_h, state_ffn_conv, w_ada, b_ada, g_pre1, g_post1, g_pre2, g_post2, w_in, w_pool_grp, pool_scale, w_lru_conv, b_lru_conv, w_rg, b_rg, w_ig, b_ig, lru_lambda, w_pool_up, w_lru_up, w_out, w_ffn_up, w_ffn_conv, b_ffn_conv, w_ffn_down):
    batch, seq, d = x_prompt.shape
    dec_batch, dec_seq, _ = x_sample.shape
    depth = w_ada.shape[0]
    assert d == D_MODEL and dec_batch % SUBLANES == 0 and seq % TOKEN_TILE == 0
    assert w_in.shape[2] == N_MIX_BLOCKS * CB + 2 * D_MODEL

    pad = (-batch) % SUBLANES
    c_all = jnp.concatenate([c_sample, c_prompt, jnp.zeros((pad, d), c_prompt.dtype)], axis=0)
    prompt_row_block = dec_batch // SUBLANES

    vec_names = ("g_pre1", "g_post1", "g_pre2", "g_post2", "pool_scale", "b_lru_conv", "b_rg",
                 "b_ig", "lru_lambda", "b_ffn_conv")
    weights = dict(w_ada=w_ada, b_ada=b_ada, g_pre1=g_pre1, g_post1=g_post1, g_pre2=g_pre2,
                   g_post2=g_post2, w_in=w_in, w_pool_grp=w_pool_grp, pool_scale=pool_scale,
                   w_lru_conv=w_lru_conv, b_lru_conv=b_lru_conv, w_rg=w_rg, b_rg=b_rg, w_ig=w_ig,
                   b_ig=b_ig, lru_lambda=lru_lambda, w_pool_up=w_pool_up, w_lru_up=w_lru_up,
                   w_out=w_out, w_ffn_up=w_ffn_up, w_ffn_conv=w_ffn_conv, b_ffn_conv=b_ffn_conv,
                   w_ffn_down=w_ffn_down)

    def time_major(a):
        return jnp.swapaxes(a, 0, 1).reshape(-1, a.shape[-1])

    def seq_major(a2, n_rows):
        return jnp.swapaxes(a2.reshape(n_rows, dec_batch, -1), 0, 1)

    xp = x_prompt.reshape(batch * seq, d)
    xs = time_major(x_sample)
    tps = seq // TOKEN_TILE
    outs_p = ([], [], [], [])
    outs_s = ([], [], [], [])
    for l in range(depth):
        p = {k: v[l] for k, v in weights.items()}
        for k in vec_names + ("b_ada",):
            p[k] = p[k].reshape(1, -1)
        ada = _ada(c_all, p["w_ada"], p["b_ada"])

        y, npool, nconv, nh, h = _mix_prompt(xp, ada, p, batch, seq, prompt_row_block)
        m, wup_b, wdn_b = _merge(h, y, p, TOKEN_TILE, "merge_prompt", cast_ffn_cb=FFN_CB)
        x1 = _outproj(m, xp, ada, p, OUTPROJ_TILE, seq // OUTPROJ_TILE, prompt_row_block, 0)
        xp, nffn = _ffn_prompt(x1, ada, p, wup_b, wdn_b, batch, TOKEN_TILE, tps, prompt_row_block)
        outs_p[0].append(npool)
        outs_p[1].append(nconv)
        outs_p[2].append(nh.reshape(batch, LRU_WIDTH))
        outs_p[3].append(nffn[tps - 1::tps].reshape(batch, FFN_CONV - 1, -1))

        rows_s = dec_batch * dec_seq
        y, npool, nconv, nh, h = _mix_sample(
            xs, ada, p, time_major(state_pool[l]), time_major(state_lru_conv[l]),
            state_lru_h[l], dec_batch, dec_seq, PAST_LEN)
        m, = _merge(h, y, p, rows_s, "merge_sample")
        x1 = _outproj(m, xs, ada, p, rows_s, 1, 0, dec_seq)
        xs, ng, nv = _ffn_sample(x1, ada, p, wup_b, wdn_b, time_major(state_ffn_conv[l]),
                                 dec_batch, dec_seq)
        outs_s[0].append(seq_major(npool, POOL_BUF))
        outs_s[1].append(seq_major(nconv, LRU_CONV - 1))
        outs_s[2].append(nh)
        outs_s[3].append(jnp.concatenate([seq_major(ng, FFN_CONV - 1),
                                          seq_major(nv, FFN_CONV - 1)], axis=-1))

    return (xp.reshape(batch, seq, d), seq_major(xs, dec_seq),
            jnp.stack(outs_p[0]), jnp.stack(outs_p[1]), jnp.stack(outs_p[2]), jnp.stack(outs_p[3]),
            jnp.stack(outs_s[0]), jnp.stack(outs_s[1]), jnp.stack(outs_s[2]), jnp.stack(outs_s[3]))
```

```python
import functools

import jax
import jax.numpy as jnp
from jax import lax
from jax.experimental import pallas as pl
from jax.experimental.pallas import tpu as pltpu

F32 = jnp.float32
BF16 = jnp.bfloat16

D_MODEL = 2048
POOL_WINDOWS = (2, 4, 8, 16)
POOL_GROUPS = len(POOL_WINDOWS)
POOL_BUF = max(POOL_WINDOWS) - 1
LRU_CONV = 4
LRU_C = 8.0
PAST_LEN = 16384
FFN_CONV = 3
N_ADA = 6
EPS = 1e-6

CB = 256
POOL_WIDTH = POOL_GROUPS * CB
LRU_BLOCKS = 8
LRU_WIDTH = LRU_BLOCKS * CB
N_MIX_BLOCKS = POOL_GROUPS + LRU_BLOCKS
HALO = 16
SUBLANES = 8
VMEM_LIMIT = 60 * 1024 * 1024


def _cparams(n_axes):
    return pltpu.CompilerParams(
        dimension_semantics=("arbitrary",) * n_axes, vmem_limit_bytes=VMEM_LIMIT)


def _dot(a, b):
    return jnp.dot(a, b, preferred_element_type=F32)


def _softplus(z):
    return jnp.maximum(z, 0.0) + jnp.log1p(jnp.exp(-jnp.abs(z)))


def _lru_coeffs(xc, r_pre, i_pre, neg_c_sp):
    r = jax.nn.sigmoid(r_pre)
    i = jax.nn.sigmoid(i_pre)
    log_a = r * neg_c_sp
    a = jnp.exp(log_a)
    m2 = -jnp.tanh(log_a) * (a * a + 1.0)
    root = jnp.where(m2 > 0.0, m2 * lax.rsqrt(m2), 0.0)
    return a, root * (i * xc)


ADA_K = 256
ADA_N = 1024


def _ada_kernel(c_ref, w_ref, b_ref, o_ref):
    rows, n_out = o_ref.shape

    @pl.when(pl.program_id(0) == 0)
    def _():
        o_ref[...] = jnp.broadcast_to(b_ref[...], (rows, n_out))

    c = c_ref[...]
    s = (c * jax.nn.sigmoid(c)).astype(BF16)
    for n0 in range(0, n_out, ADA_N):
        o_ref[:, n0:n0 + ADA_N] += _dot(s, w_ref[:, n0:n0 + ADA_N].astype(BF16))


def _ada(c_all, w_ada, b_ada):
    rows = c_all.shape[0]
    n_out = N_ADA * D_MODEL
    return pl.pallas_call(
        _ada_kernel,
        grid=(D_MODEL // ADA_K,),
        in_specs=[pl.BlockSpec((rows, ADA_K), lambda k: (0, k)),
                  pl.BlockSpec((ADA_K, n_out), lambda k: (k, 0)),
                  pl.BlockSpec((1, n_out), lambda k: (0, 0))],
        out_specs=pl.BlockSpec((rows, n_out), lambda k: (0, 0)),
        out_shape=jax.ShapeDtypeStruct((rows, n_out), F32),
        compiler_params=_cparams(1),
        name="ada",
    )(c_all, w_ada, b_ada)


NORM_ROWS = 32
NORM_UNROLL = 8
OUT_SUBTILE = 256


def _unit_rms(x):
    return x * lax.rsqrt(jnp.mean(x * x, axis=-1, keepdims=True) + EPS)


def _build_h_prompt(x_ref, h_ref, g, scale1p, shift, n_rows):
    gs = g * scale1p

    def body(i, carry):
        r0 = pl.multiple_of(i * NORM_ROWS, NORM_ROWS)
        x = x_ref[pl.ds(r0, NORM_ROWS), :]
        h_ref[pl.ds(r0, NORM_ROWS), :] = (_unit_rms(x) * gs + shift).astype(BF16)
        return carry
    lax.fori_loop(0, n_rows // NORM_ROWS, body, 0, unroll=NORM_UNROLL)


def _residual_norm_prompt(x_ref, acc_ref, o_ref, g, gate, n_rows):
    gg = g * gate
    for r0 in range(0, n_rows, NORM_ROWS):
        sl = slice(r0, r0 + NORM_ROWS)
        o_ref[sl, :] = x_ref[sl, :] + _unit_rms(acc_ref[sl, :]) * gg


def _build_h_sample(x_ref, h_ref, g, sc_ref, sh_ref, n_seq, n_t):
    for s0 in range(0, n_seq, NORM_ROWS):
        gs = g * (1.0 + sc_ref[s0:s0 + NORM_ROWS, :])
        shift = sh_ref[s0:s0 + NORM_ROWS, :]
        for t in range(n_t):
            sl = slice(t * n_seq + s0, t * n_seq + s0 + NORM_ROWS)
            h_ref[sl, :] = (_unit_rms(x_ref[sl, :]) * gs + shift).astype(BF16)


MIX_SUBTILE = 512
POOL_ROWS = 128
LRU_ROWS = 64


def _mix_prompt_kernel(x_ref, sh_ref, sc_ref, g_ref, win_ref, wgrp_ref, pscale_ref,
                       wconv_ref, bconv_ref, wrg_ref, brg_ref, wig_ref, big_ref, lam_ref,
                       y_ref, npool_ref, nconv_ref, nh_ref, h_ref,
                       ext_ref, xb_ref, r_ref, i_ref, a_ref, u_ref, *, seq):
    q = pl.program_id(0)
    j = pl.program_id(1)

    @pl.when(j == 0)
    def _():
        _build_h_prompt(x_ref, h_ref, g_ref[...], 1.0 + sc_ref[pl.ds(q, 1), :],
                        sh_ref[pl.ds(q, 1), :], seq)
        for ref in (ext_ref, r_ref, i_ref, a_ref):
            ref[0:HALO, :] = jnp.zeros((HALO, CB), F32)

    win = win_ref[...].astype(BF16)

    def rows_of(r0, n, shift=0):
        return slice(HALO + r0 - shift, HALO + r0 - shift + n)

    def up_proj(m0):
        ext_ref[rows_of(m0, MIX_SUBTILE), :] = _dot(h_ref[m0:m0 + MIX_SUBTILE, :], win)

    def pool_branch(w):
        wg = wgrp_ref[0].astype(BF16)
        ps = pscale_ref[...]
        partial = {2: r_ref, 4: i_ref, 8: a_ref}
        up_proj(0)
        for m0 in range(0, seq, MIX_SUBTILE):
            if m0 + MIX_SUBTILE < seq:
                up_proj(m0 + MIX_SUBTILE)
            for r0 in range(m0, m0 + MIX_SUBTILE, POOL_ROWS):
                u = ext_ref[rows_of(r0, POOL_ROWS), :]
                s, src, width = u, ext_ref, 1
                while width < w:
                    s = s + src[rows_of(r0, POOL_ROWS, width), :]
                    width *= 2
                    if width < w:
                        src = partial[width]
                        src[rows_of(r0, POOL_ROWS), :] = s
                if r0 < w:
                    pos = r0 + lax.broadcasted_iota(jnp.int32, (POOL_ROWS, 1), 0)
                    cnt = jnp.minimum(w, pos + 1).astype(F32)
                else:
                    cnt = float(w)
                xb_ref[r0:r0 + POOL_ROWS, :] = (s / cnt - u).astype(BF16)
            sub = slice(m0, m0 + MIX_SUBTILE)
            y_ref[sub, :] = (_dot(xb_ref[sub, :], wg) * ps).astype(BF16)
        npool_ref[0] = ext_ref[pl.ds(HALO + seq - POOL_BUF, POOL_BUF), :]

    for g, w in enumerate(POOL_WINDOWS):
        pl.when(j == g)(functools.partial(pool_branch, w))

    @pl.when(j >= POOL_GROUPS)
    def _():
        wc = wconv_ref[...]
        bc = bconv_ref[...]
        wrg = wrg_ref[0].astype(BF16)
        wig = wig_ref[0].astype(BF16)
        neg_c_sp = (-LRU_C) * _softplus(-lam_ref[...])
        brg = brg_ref[...]
        big = big_ref[...]
        nb = LRU_ROWS // SUBLANES
        row = lax.broadcasted_iota(jnp.int32, (nb, SUBLANES, CB), 1)
        h_carry = jnp.zeros((1, CB), F32)
        up_proj(0)
        for m0 in range(0, seq, MIX_SUBTILE):
            sub = slice(m0, m0 + MIX_SUBTILE)
            if m0 + MIX_SUBTILE < seq:
                up_proj(m0 + MIX_SUBTILE)
            for r0 in range(m0, m0 + MIX_SUBTILE, LRU_ROWS):
                xc = wc[LRU_CONV - 1:, :] * ext_ref[rows_of(r0, LRU_ROWS), :] + bc
                for k in range(LRU_CONV - 1):
                    xc = xc + wc[k:k + 1, :] * ext_ref[rows_of(r0, LRU_ROWS, LRU_CONV - 1 - k), :]
                a_ref[rows_of(r0, LRU_ROWS), :] = xc
                xb_ref[r0:r0 + LRU_ROWS, :] = xc.astype(BF16)
            r_ref[rows_of(m0, MIX_SUBTILE), :] = _dot(xb_ref[sub, :], wrg)
            i_ref[rows_of(m0, MIX_SUBTILE), :] = _dot(xb_ref[sub, :], wig)
            for r0 in range(m0, m0 + MIX_SUBTILE, LRU_ROWS):
                sl = rows_of(r0, LRU_ROWS)
                a, b = _lru_coeffs(a_ref[sl, :], r_ref[sl, :] + brg, i_ref[sl, :] + big, neg_c_sp)
                a = a.reshape(nb, SUBLANES, CB)
                b = b.reshape(nb, SUBLANES, CB)
                for k in (1, 2, 4):
                    a_sh = jnp.where(row >= k, pltpu.roll(a, k, 1), 1.0)
                    b_sh = jnp.where(row >= k, pltpu.roll(b, k, 1), 0.0)
                    b = b + a * b_sh
                    a = a * a_sh
                a_ref[sl, :] = a.reshape(LRU_ROWS, CB)
                u_ref[sl, :] = b.reshape(LRU_ROWS, CB)
            for r0 in range(m0, m0 + MIX_SUBTILE, SUBLANES):
                sl = rows_of(r0, SUBLANES)
                h8 = a_ref[sl, :] * h_carry + u_ref[sl, :]
                u_ref[sl, :] = h8
                h_carry = h8[SUBLANES - 1:SUBLANES, :]
            y_ref[sub, :] = u_ref[rows_of(m0, MIX_SUBTILE), :].astype(BF16)
        nh_ref[0] = h_carry
        nconv_ref[0] = ext_ref[pl.ds(HALO + seq - (LRU_CONV - 1), LRU_CONV - 1), :]


def _mix_sample_kernel(x_ref, sh_ref, sc_ref, g_ref, win_ref, wgrp_ref, pscale_ref,
                       wconv_ref, bconv_ref, wrg_ref, brg_ref, wig_ref, big_ref, lam_ref,
                       spool_ref, sconv_ref, sh0_ref,
                       y_ref, npool_ref, nconv_ref, nh_ref, h_ref,
                       u_ref, d_ref, *, n_seq, n_t, start):
    j = pl.program_id(1)

    @pl.when(j == 0)
    def _():
        _build_h_sample(x_ref, h_ref, g_ref[...], sc_ref, sh_ref, n_seq, n_t)

    u_ref[...] = _dot(h_ref[...], win_ref[...].astype(BF16))

    def u_slab(t):
        return u_ref[t * n_seq:(t + 1) * n_seq, :]

    def pool_branch(w):
        e = [spool_ref[k * n_seq:(k + 1) * n_seq, :] for k in range(POOL_BUF)]
        e += [u_slab(t) for t in range(n_t)]
        for k in range(POOL_BUF):
            npool_ref[k * n_seq:(k + 1) * n_seq, :] = e[n_t + k]
        for t in range(n_t):
            s = e[POOL_BUF + t]
            for k in range(1, w):
                s = s + e[POOL_BUF + t - k]
            cnt = float(min(w, start + t + 1))
            d_ref[t * n_seq:(t + 1) * n_seq, :] = (s / cnt - e[POOL_BUF + t]).astype(BF16)
        y = _dot(d_ref[...], wgrp_ref[0].astype(BF16)) * pscale_ref[...]
        y_ref[...] = y.astype(BF16)

    for g, w in enumerate(POOL_WINDOWS):
        pl.when(j == g)(functools.partial(pool_branch, w))

    @pl.when(j >= POOL_GROUPS)
    def _():
        nk = LRU_CONV - 1
        e = [sconv_ref[k * n_seq:(k + 1) * n_seq, :] for k in range(nk)]
        e += [u_slab(t) for t in range(n_t)]
        for k in range(nk):
            nconv_ref[k * n_seq:(k + 1) * n_seq, :] = e[n_t + k]
        wc = wconv_ref[...]
        bc = bconv_ref[...]
        wrg = wrg_ref[0].astype(BF16)
        wig = wig_ref[0].astype(BF16)
        neg_c_sp = (-LRU_C) * _softplus(-lam_ref[...])
        h = sh0_ref[...]
        for t in range(n_t):
            xc = bc
            for k in range(LRU_CONV):
                xc = xc + wc[k:k + 1, :] * e[t + k]
            xb = xc.astype(BF16)
            a, b = _lru_coeffs(xc, _dot(xb, wrg) + brg_ref[...], _dot(xb, wig) + big_ref[...],
                               neg_c_sp)
            h = a * h + b
            y_ref[t * n_seq:(t + 1) * n_seq, :] = h.astype(BF16)
        nh_ref[...] = h


def _mix_weight_specs():
    pj = lambda j: jnp.minimum(j, POOL_GROUPS - 1)
    lj = lambda j: jnp.maximum(j - POOL_GROUPS, 0)
    return [
        pl.BlockSpec((1, D_MODEL), lambda q, j: (0, 0)),
        pl.BlockSpec((D_MODEL, CB), lambda q, j: (0, j)),
        pl.BlockSpec((1, CB, CB), lambda q, j: (pj(j), 0, 0)),
        pl.BlockSpec((1, CB), lambda q, j: (0, pj(j))),
        pl.BlockSpec((LRU_CONV, CB), lambda q, j: (0, lj(j))),
        pl.BlockSpec((1, CB), lambda q, j: (0, lj(j))),
        pl.BlockSpec((1, CB, CB), lambda q, j: (lj(j), 0, 0)),
        pl.BlockSpec((1, CB), lambda q, j: (0, lj(j))),
        pl.BlockSpec((1, CB, CB), lambda q, j: (lj(j), 0, 0)),
        pl.BlockSpec((1, CB), lambda q, j: (0, lj(j))),
        pl.BlockSpec((1, CB), lambda q, j: (0, lj(j))),
    ], pj, lj


def _mix_prompt(x2, ada, p, n_seq, seq, ada_row_block):
    wspecs, pj, lj = _mix_weight_specs()
    in_specs = [
        pl.BlockSpec((seq, D_MODEL), lambda q, j: (q, 0), pipeline_mode=pl.Buffered(1)),
        pl.BlockSpec((SUBLANES, D_MODEL), lambda q, j: (ada_row_block, 0)),
        pl.BlockSpec((SUBLANES, D_MODEL), lambda q, j: (ada_row_block, 1)),
    ] + wspecs
    out_specs = [
        pl.BlockSpec((seq, CB), lambda q, j: (q, j)),
        pl.BlockSpec((1, POOL_BUF, CB), lambda q, j: (q, 0, pj(j))),
        pl.BlockSpec((1, LRU_CONV - 1, CB), lambda q, j: (q, 0, lj(j))),
        pl.BlockSpec((1, 1, CB), lambda q, j: (q, 0, lj(j))),
        pl.BlockSpec((seq, D_MODEL), lambda q, j: (q, 0)),
    ]
    out_shape = [
        jax.ShapeDtypeStruct((n_seq * seq, N_MIX_BLOCKS * CB), BF16),
        jax.ShapeDtypeStruct((n_seq, POOL_BUF, POOL_WIDTH), F32),
        jax.ShapeDtypeStruct((n_seq, LRU_CONV - 1, LRU_WIDTH), F32),
        jax.ShapeDtypeStruct((n_seq, 1, LRU_WIDTH), F32),
        jax.ShapeDtypeStruct((n_seq * seq, D_MODEL), BF16),
    ]
    scratch = [
        pltpu.VMEM((HALO + seq, CB), F32),
        pltpu.VMEM((seq, CB), BF16),
        pltpu.VMEM((HALO + seq, CB), F32),
        pltpu.VMEM((HALO + seq, CB), F32),
        pltpu.VMEM((HALO + seq, CB), F32),
        pltpu.VMEM((HALO + seq, CB), F32),
    ]
    return pl.pallas_call(
        functools.partial(_mix_prompt_kernel, seq=seq),
        grid=(n_seq, N_MIX_BLOCKS),
        in_specs=in_specs, out_specs=out_specs, out_shape=out_shape,
        scratch_shapes=scratch, compiler_params=_cparams(2), name="mix_prompt",
    )(x2, ada, ada, p["g_pre1"], p["w_in"], p["w_pool_grp"], p["pool_scale"],
      p["w_lru_conv"], p["b_lru_conv"], p["w_rg"], p["b_rg"], p["w_ig"], p["b_ig"],
      p["lru_lambda"])


def _mix_sample(x2, ada, p, spool2, sconv2, sh0, n_seq, n_t, start):
    wspecs, pj, lj = _mix_weight_specs()
    rows = n_seq * n_t
    in_specs = [
        pl.BlockSpec((rows, D_MODEL), lambda q, j: (0, 0), pipeline_mode=pl.Buffered(1)),
        pl.BlockSpec((n_seq, D_MODEL), lambda q, j: (0, 0)),
        pl.BlockSpec((n_seq, D_MODEL), lambda q, j: (0, 1)),
    ] + wspecs + [
        pl.BlockSpec((n_seq * POOL_BUF, CB), lambda q, j: (0, pj(j))),
        pl.BlockSpec((n_seq * (LRU_CONV - 1), CB), lambda q, j: (0, lj(j))),
        pl.BlockSpec((n_seq, CB), lambda q, j: (0, lj(j))),
    ]
    out_specs = [
        pl.BlockSpec((rows, CB), lambda q, j: (0, j)),
        pl.BlockSpec((n_seq * POOL_BUF, CB), lambda q, j: (0, pj(j))),
        pl.BlockSpec((n_seq * (LRU_CONV - 1), CB), lambda q, j: (0, lj(j))),
        pl.BlockSpec((n_seq, CB), lambda q, j: (0, lj(j))),
        pl.BlockSpec((rows, D_MODEL), lambda q, j: (0, 0)),
    ]
    out_shape = [
        jax.ShapeDtypeStruct((rows, N_MIX_BLOCKS * CB), BF16),
        jax.ShapeDtypeStruct((n_seq * POOL_BUF, POOL_WIDTH), F32),
        jax.ShapeDtypeStruct((n_seq * (LRU_CONV - 1), LRU_WIDTH), F32),
        jax.ShapeDtypeStruct((n_seq, LRU_WIDTH), F32),
        jax.ShapeDtypeStruct((rows, D_MODEL), BF16),
    ]
    scratch = [
        pltpu.VMEM((rows, CB), F32),
        pltpu.VMEM((rows, CB), BF16),
    ]
    return pl.pallas_call(
        functools.partial(_mix_sample_kernel, n_seq=n_seq, n_t=n_t, start=start),
        grid=(1, N_MIX_BLOCKS),
        in_specs=in_specs, out_specs=out_specs, out_shape=out_shape,
        scratch_shapes=scratch, compiler_params=_cparams(2), name="mix_sample",
    )(x2, ada, ada, p["g_pre1"], p["w_in"], p["w_pool_grp"], p["pool_scale"],
      p["w_lru_conv"], p["b_lru_conv"], p["w_rg"], p["b_rg"], p["w_ig"], p["b_ig"],
      p["lru_lambda"], spool2, sconv2, sh0)


MERGE_ROWS = 64
MERGE_SUBTILE = 256
CAST_ROWS = 32
BF16_ROWS = 16


def _cast_ffn_slab(wup_ref, wdn_ref, wupb_ref, wdnb_ref):
    n_blocks, _, cb2 = wupb_ref.shape
    cb = cb2 // 2
    for blk in range(n_blocks):
        wupb_ref[blk, :, 0:cb] = wup_ref[:, blk * cb:(blk + 1) * cb].astype(BF16)
        wupb_ref[blk, :, cb:cb2] = wup_ref[:, (n_blocks + blk) * cb:(n_blocks + blk + 1) * cb].astype(BF16)
    for r0 in range(0, wdn_ref.shape[0], CAST_ROWS):
        wdnb_ref[r0:r0 + CAST_ROWS, :] = wdn_ref[r0:r0 + CAST_ROWS, :].astype(BF16)


def _merge_kernel(h_ref, y_ref, wgp_ref, wgl_ref, wpu_ref, wlu_ref, *rest, tm, cast_ffn):
    if cast_ffn:
        wup_ref, wdn_ref, m_ref, wupb_ref, wdnb_ref, wg_ref, g_ref, pu_ref, lu_ref = rest
    else:
        m_ref, wg_ref, g_ref, pu_ref, lu_ref = rest
    wg_ref[:, 0:CB] = wgp_ref[...].astype(BF16)
    wg_ref[:, CB:2 * CB] = wgl_ref[...].astype(BF16)
    wpu = wpu_ref[...].astype(BF16)
    wlu = wlu_ref[...].astype(BF16)

    def proj(m0):
        rows = slice(m0, m0 + MERGE_SUBTILE)
        g_ref[rows, :] = _dot(h_ref[rows, :], wg_ref[...])
        pu_ref[rows, :] = _dot(y_ref[rows, 0:POOL_WIDTH], wpu)
        lu_ref[rows, :] = _dot(y_ref[rows, POOL_WIDTH:POOL_WIDTH + LRU_WIDTH], wlu)

    proj(0)
    for m0 in range(0, tm, MERGE_SUBTILE):
        if m0 + MERGE_SUBTILE < tm:
            proj(m0 + MERGE_SUBTILE)
        for r0 in range(m0, m0 + MERGE_SUBTILE, MERGE_ROWS):
            sl = slice(r0, r0 + MERGE_ROWS)
            m = (jax.nn.sigmoid(g_ref[sl, 0:CB]) * pu_ref[sl, :]
                 + jax.nn.sigmoid(g_ref[sl, CB:2 * CB]) * lu_ref[sl, :])
            m_ref[sl, :] = m.astype(BF16)
    if cast_ffn:
        _cast_ffn_slab(wup_ref, wdn_ref, wupb_ref, wdnb_ref)


def _merge(h, y, p, tm, name, cast_ffn_cb=0):
    rows = h.shape[0]
    n_col = D_MODEL // CB
    n_gate0 = N_MIX_BLOCKS
    n_gate1 = N_MIX_BLOCKS + n_col
    in_specs = [
        pl.BlockSpec((tm, D_MODEL), lambda i, c: (i, 0), pipeline_mode=pl.Buffered(1)),
        pl.BlockSpec((tm, N_MIX_BLOCKS * CB), lambda i, c: (i, 0), pipeline_mode=pl.Buffered(1)),
        pl.BlockSpec((D_MODEL, CB), lambda i, c: (0, n_gate0 + c)),
        pl.BlockSpec((D_MODEL, CB), lambda i, c: (0, n_gate1 + c)),
        pl.BlockSpec((POOL_WIDTH, CB), lambda i, c: (0, c)),
        pl.BlockSpec((LRU_WIDTH, CB), lambda i, c: (0, c)),
    ]
    out_specs = [pl.BlockSpec((tm, CB), lambda i, c: (i, c))]
    out_shape = [jax.ShapeDtypeStruct((rows, D_MODEL), BF16)]
    args = [h, y, p["w_in"], p["w_in"], p["w_pool_up"], p["w_lru_up"]]
    if cast_ffn_cb:
        n_steps = (rows // tm) * n_col
        d_ff = p["w_ffn_down"].shape[0]
        n_blocks = d_ff // cast_ffn_cb
        up_rows, dn_rows = D_MODEL // n_steps, d_ff // n_steps
        assert up_rows * n_steps == D_MODEL and dn_rows * n_steps == d_ff
        assert up_rows % BF16_ROWS == 0 and dn_rows % BF16_ROWS == 0
        step = lambda i, c: i * n_col + c
        in_specs += [pl.BlockSpec((up_rows, 2 * d_ff), lambda i, c: (step(i, c), 0)),
                     pl.BlockSpec((dn_rows, D_MODEL), lambda i, c: (step(i, c), 0))]
        out_specs += [pl.BlockSpec((n_blocks, up_rows, 2 * cast_ffn_cb),
                                   lambda i, c: (0, step(i, c), 0)),
                      pl.BlockSpec((dn_rows, D_MODEL), lambda i, c: (step(i, c), 0))]
        out_shape += [jax.ShapeDtypeStruct((n_blocks, D_MODEL, 2 * cast_ffn_cb), BF16),
                      jax.ShapeDtypeStruct((d_ff, D_MODEL), BF16)]
        args += [p["w_ffn_up"], p["w_ffn_down"]]
    return pl.pallas_call(
        functools.partial(_merge_kernel, tm=tm, cast_ffn=bool(cast_ffn_cb)),
        grid=(rows // tm, n_col),
        in_specs=in_specs, out_specs=out_specs, out_shape=out_shape,
        scratch_shapes=[pltpu.VMEM((D_MODEL, 2 * CB), BF16), pltpu.VMEM((tm, 2 * CB), F32),
                        pltpu.VMEM((tm, CB), F32), pltpu.VMEM((tm, CB), F32)],
        compiler_params=_cparams(2), name=name,
    )(*args)


def _cast_weight(w_ref, wb_ref):
    rows = w_ref.shape[0]
    for r0 in range(0, rows, 256):
        wb_ref[r0:r0 + 256, :] = w_ref[r0:r0 + 256, :].astype(BF16)


def _outproj_kernel(m_ref, x_ref, gate_ref, g_ref, w_ref, o_ref, wb_ref, acc_ref,
                    *, tm, tiles_per_seq, sample_t):
    i = pl.program_id(0)

    @pl.when(i == 0)
    def _():
        _cast_weight(w_ref, wb_ref)

    g = g_ref[...]
    if sample_t:
        n_seq = tm // sample_t
    else:
        gg_tile = g * gate_ref[pl.ds(i // tiles_per_seq, 1), :]

    def proj(m0):
        acc_ref[m0:m0 + OUT_SUBTILE, :] = _dot(m_ref[m0:m0 + OUT_SUBTILE, :], wb_ref[...])

    proj(0)
    for m0 in range(0, tm, OUT_SUBTILE):
        if m0 + OUT_SUBTILE < tm:
            proj(m0 + OUT_SUBTILE)
        for r0 in range(m0, m0 + OUT_SUBTILE, NORM_ROWS):
            sl = slice(r0, r0 + NORM_ROWS)
            if sample_t:
                s0 = r0 % n_seq
                gg = g * gate_ref[s0:s0 + NORM_ROWS, :]
            else:
                gg = gg_tile
            o_ref[sl, :] = x_ref[sl, :] + _unit_rms(acc_ref[sl, :]) * gg


def _outproj(m, x2, ada, p, tm, tiles_per_seq, ada_row_block, sample_t):
    rows = x2.shape[0]
    if sample_t:
        gate_spec = pl.BlockSpec((tm // sample_t, D_MODEL), lambda i: (0, 2))
    else:
        gate_spec = pl.BlockSpec((SUBLANES, D_MODEL), lambda i: (ada_row_block, 2))
    return pl.pallas_call(
        functools.partial(_outproj_kernel, tm=tm, tiles_per_seq=tiles_per_seq, sample_t=sample_t),
        grid=(rows // tm,),
        in_specs=[pl.BlockSpec((tm, D_MODEL), lambda i: (i, 0)),
                  pl.BlockSpec((tm, D_MODEL), lambda i: (i, 0)),
                  gate_spec,
                  pl.BlockSpec((1, D_MODEL), lambda i: (0, 0)),
                  pl.BlockSpec((D_MODEL, D_MODEL), lambda i: (0, 0), pipeline_mode=pl.Buffered(1))],
        out_specs=pl.BlockSpec((tm, D_MODEL), lambda i: (i, 0)),
        out_shape=jax.ShapeDtypeStruct((rows, D_MODEL), F32),
        scratch_shapes=[pltpu.VMEM((D_MODEL, D_MODEL), BF16), pltpu.VMEM((tm, D_MODEL), F32)],
        compiler_params=_cparams(1),
        name="outproj_sample" if sample_t else "outproj_prompt",
    )(m, x2, ada, p["g_post1"], p["w_out"])


GELU_C0 = 0.7978845608028654
GELU_C1 = GELU_C0 * 0.044715
FFN_ROWS = 32
FFN_SUBTILE = 256
FFN_CB = 512


def _ffn_conv_gate(eg, ev, wcg, wcv, bcg, bcv):
    last = FFN_CONV - 1
    g = wcg[last:, :] * eg[last] + bcg
    v = wcv[last:, :] * ev[last] + bcv
    for k in range(last):
        g = g + wcg[k:k + 1, :] * eg[k]
        v = v + wcv[k:k + 1, :] * ev[k]
    t = jnp.tanh(g * (GELU_C0 + GELU_C1 * (g * g)))
    return ((g * v) * (0.5 + 0.5 * t)).astype(BF16)


def _ffn_prompt_kernel(x_ref, sh_ref, sc_ref, gate_ref, gpre_ref, gpost_ref,
                       wup_ref, wcg_ref, wcv_ref, bcg_ref, bcv_ref, wdn_ref,
                       o_ref, nst_ref,
                       h_ref, ext_ref, f_ref, carry_ref, *, tm, tiles_per_seq, n_blocks, cb):
    i = pl.program_id(0)
    c = pl.program_id(1)
    q = i // tiles_per_seq
    first = (i % tiles_per_seq) == 0
    nk = FFN_CONV - 1

    @pl.when(c == 0)
    def _():
        _build_h_prompt(x_ref, h_ref, gpre_ref[...], 1.0 + sc_ref[pl.ds(q, 1), :],
                        sh_ref[pl.ds(q, 1), :], tm)
        o_ref[...] = jnp.zeros((tm, D_MODEL), F32)

    @pl.when(first)
    def _():
        ext_ref[0:SUBLANES, :] = jnp.zeros((SUBLANES, 2 * cb), F32)

    @pl.when(jnp.logical_not(first))
    def _():
        ext_ref[0:SUBLANES, :] = carry_ref[c]

    wcg = wcg_ref[...]
    wcv = wcv_ref[...]
    bcg = bcg_ref[...]
    bcv = bcv_ref[...]

    def up_proj(m0):
        ext_ref[SUBLANES + m0:SUBLANES + m0 + FFN_SUBTILE, :] = _dot(
            h_ref[m0:m0 + FFN_SUBTILE, :], wup_ref[0])

    up_proj(0)
    for m0 in range(0, tm, FFN_SUBTILE):
        rows = slice(m0, m0 + FFN_SUBTILE)
        if m0 + FFN_SUBTILE < tm:
            up_proj(m0 + FFN_SUBTILE)
        for r0 in range(m0, m0 + FFN_SUBTILE, FFN_ROWS):
            shifted = [slice(SUBLANES + r0 - nk + k, SUBLANES + r0 - nk + k + FFN_ROWS)
                       for k in range(FFN_CONV)]
            eg = [ext_ref[s, 0:cb] for s in shifted]
            ev = [ext_ref[s, cb:2 * cb] for s in shifted]
            f_ref[r0:r0 + FFN_ROWS, :] = _ffn_conv_gate(eg, ev, wcg, wcv, bcg, bcv)
        o_ref[rows, :] += _dot(f_ref[rows, :], wdn_ref[...])

    carry_ref[c] = ext_ref[tm:tm + SUBLANES, :]
    nst_ref[0, :, 0, :] = ext_ref[pl.ds(SUBLANES + tm - nk, nk), 0:cb]
    nst_ref[0, :, 1, :] = ext_ref[pl.ds(SUBLANES + tm - nk, nk), cb:2 * cb]

    @pl.when(c == n_blocks - 1)
    def _():
        _residual_norm_prompt(x_ref, o_ref, o_ref, gpost_ref[...], gate_ref[pl.ds(q, 1), :], tm)


def _ffn_sample_kernel(x_ref, sh_ref, sc_ref, gate_ref, gpre_ref, gpost_ref,
                       wup_ref, wcg_ref, wcv_ref, bcg_ref, bcv_ref, wdn_ref,
                       sg_ref, sv_ref,
                       o_ref, ng_ref, nv_ref,
                       h_ref, ext_ref, f_ref, acc_ref, *, n_seq, n_t, n_blocks, cb):
    c = pl.program_id(1)
    nk = FFN_CONV - 1

    @pl.when(c == 0)
    def _():
        _build_h_sample(x_ref, h_ref, gpre_ref[...], sc_ref, sh_ref, n_seq, n_t)
        acc_ref[...] = jnp.zeros((n_seq * n_t, D_MODEL), F32)

    wcg = wcg_ref[...]
    wcv = wcv_ref[...]
    bcg = bcg_ref[...]
    bcv = bcv_ref[...]

    def view(state_ref, half, t, s0, n):
        if t < 0:
            return state_ref[(nk + t) * n_seq + s0:(nk + t) * n_seq + s0 + n, :]
        return ext_ref[t * n_seq + s0:t * n_seq + s0 + n, half * cb:(half + 1) * cb]

    t_sub = max(FFN_SUBTILE // n_seq, 1)

    def up_proj(t0):
        rows = slice(t0 * n_seq, (t0 + t_sub) * n_seq)
        ext_ref[rows, :] = _dot(h_ref[rows, :], wup_ref[0])

    up_proj(0)
    for t0 in range(0, n_t, t_sub):
        rows = slice(t0 * n_seq, (t0 + t_sub) * n_seq)
        if t0 + t_sub < n_t:
            up_proj(t0 + t_sub)
        for t in range(t0, t0 + t_sub):
            for s0 in range(0, n_seq, FFN_ROWS):
                eg = [view(sg_ref, 0, t - nk + k, s0, FFN_ROWS) for k in range(FFN_CONV)]
                ev = [view(sv_ref, 1, t - nk + k, s0, FFN_ROWS) for k in range(FFN_CONV)]
                f_ref[t * n_seq + s0:t * n_seq + s0 + FFN_ROWS, :] = _ffn_conv_gate(
                    eg, ev, wcg, wcv, bcg, bcv)
        acc_ref[rows, :] += _dot(f_ref[rows, :], wdn_ref[...])

    for k in range(nk):
        ng_ref[k * n_seq:(k + 1) * n_seq, :] = view(sg_ref, 0, n_t - nk + k, 0, n_seq)
        nv_ref[k * n_seq:(k + 1) * n_seq, :] = view(sv_ref, 1, n_t - nk + k, 0, n_seq)

    @pl.when(c == n_blocks - 1)
    def _():
        g = gpost_ref[...]
        for s0 in range(0, n_seq, NORM_ROWS):
            gg = g * gate_ref[s0:s0 + NORM_ROWS, :]
            for t in range(n_t):
                sl = slice(t * n_seq + s0, t * n_seq + s0 + NORM_ROWS)
                o_ref[sl, :] = x_ref[sl, :] + _unit_rms(acc_ref[sl, :]) * gg


def _ffn_weight_specs(n_blocks, cb):
    return [
        pl.BlockSpec((1, D_MODEL), lambda i, c: (0, 0)),
        pl.BlockSpec((1, D_MODEL), lambda i, c: (0, 0)),
        pl.BlockSpec((1, D_MODEL, 2 * cb), lambda i, c: (c, 0, 0)),
        pl.BlockSpec((FFN_CONV, cb), lambda i, c: (0, c)),
        pl.BlockSpec((FFN_CONV, cb), lambda i, c: (0, n_blocks + c)),
        pl.BlockSpec((1, cb), lambda i, c: (0, c)),
        pl.BlockSpec((1, cb), lambda i, c: (0, n_blocks + c)),
        pl.BlockSpec((cb, D_MODEL), lambda i, c: (c, 0)),
    ]


def _ffn_prompt(x1, ada, p, wup_b, wdn_b, n_seq, tm, tiles_per_seq, ada_row_block):
    rows = x1.shape[0]
    n_blocks, _, cb2 = wup_b.shape
    cb = cb2 // 2
    ada_spec = lambda k: pl.BlockSpec((SUBLANES, D_MODEL), lambda i, c: (ada_row_block, k))
    in_specs = [pl.BlockSpec((tm, D_MODEL), lambda i, c: (i, 0), pipeline_mode=pl.Buffered(1)),
                ada_spec(3), ada_spec(4), ada_spec(5)] + _ffn_weight_specs(n_blocks, cb)
    out_specs = [
        pl.BlockSpec((tm, D_MODEL), lambda i, c: (i, 0)),
        pl.BlockSpec((1, FFN_CONV - 1, 2, cb), lambda i, c: (i, 0, 0, c)),
    ]
    out_shape = [jax.ShapeDtypeStruct((rows, D_MODEL), F32),
                 jax.ShapeDtypeStruct((rows // tm, FFN_CONV - 1, 2, n_blocks * cb), F32)]
    scratch = [
        pltpu.VMEM((tm, D_MODEL), BF16),
        pltpu.VMEM((SUBLANES + tm, 2 * cb), F32),
        pltpu.VMEM((tm, cb), BF16),
        pltpu.VMEM((n_blocks, SUBLANES, 2 * cb), F32),
    ]
    return pl.pallas_call(
        functools.partial(_ffn_prompt_kernel, tm=tm, tiles_per_seq=tiles_per_seq,
                          n_blocks=n_blocks, cb=cb),
        grid=(rows // tm, n_blocks),
        in_specs=in_specs, out_specs=out_specs, out_shape=out_shape,
        scratch_shapes=scratch, compiler_params=_cparams(2), name="ffn_prompt",
    )(x1, ada, ada, ada, p["g_pre2"], p["g_post2"], wup_b, p["w_ffn_conv"], p["w_ffn_conv"],
      p["b_ffn_conv"], p["b_ffn_conv"], wdn_b)


def _ffn_sample(x1, ada, p, wup_b, wdn_b, sffn2, n_seq, n_t):
    rows = n_seq * n_t
    n_blocks, _, cb2 = wup_b.shape
    cb = cb2 // 2
    nk = FFN_CONV - 1
    ada_spec = lambda k: pl.BlockSpec((n_seq, D_MODEL), lambda i, c: (0, k))
    in_specs = [pl.BlockSpec((rows, D_MODEL), lambda i, c: (0, 0), pipeline_mode=pl.Buffered(1)),
                ada_spec(3), ada_spec(4), ada_spec(5)] + _ffn_weight_specs(n_blocks, cb) + [
        pl.BlockSpec((n_seq * nk, cb), lambda i, c: (0, c)),
        pl.BlockSpec((n_seq * nk, cb), lambda i, c: (0, n_blocks + c)),
    ]
    out_specs = [
        pl.BlockSpec((rows, D_MODEL), lambda i, c: (0, 0)),
        pl.BlockSpec((n_seq * nk, cb), lambda i, c: (0, c)),
        pl.BlockSpec((n_seq * nk, cb), lambda i, c: (0, c)),
    ]
    out_shape = [jax.ShapeDtypeStruct((rows, D_MODEL), F32),
                 jax.ShapeDtypeStruct((n_seq * nk, n_blocks * cb), F32),
                 jax.ShapeDtypeStruct((n_seq * nk, n_blocks * cb), F32)]
    scratch = [
        pltpu.VMEM((rows, D_MODEL), BF16),
        pltpu.VMEM((rows, 2 * cb), F32),
        pltpu.VMEM((rows, cb), BF16),
        pltpu.VMEM((rows, D_MODEL), F32),
    ]
    return pl.pallas_call(
        functools.partial(_ffn_sample_kernel, n_seq=n_seq, n_t=n_t, n_blocks=n_blocks, cb=cb),
        grid=(1, n_blocks),
        in_specs=in_specs, out_specs=out_specs, out_shape=out_shape,
        scratch_shapes=scratch, compiler_params=_cparams(2), name="ffn_sample",
    )(x1, ada, ada, ada, p["g_pre2"], p["g_post2"], wup_b, p["w_ffn_conv"], p["w_ffn_conv"],
      p["b_ffn_conv"], p["b_ffn_conv"], wdn_b, sffn2, sffn2)


TOKEN_TILE = 1024
OUTPROJ_TILE = 512


def kernel(x_prompt, x_sample, c_prompt, c_sample, state_pool, state_lru_conv, state_lru_h, state_ffn_conv, w_ada, b_ada, g_pre1, g_post1, g_pre2, g_post2, w_in, w_pool_grp, pool_scale, w_lru_conv, b_lru_conv, w_rg, b_rg, w_ig, b_ig, lru_lambda, w_pool_up, w_lru_up, w_out, w_ffn_up, w_ffn_conv, b_ffn_conv, w_ffn_down):
    batch, seq, d = x_prompt.shape
    dec_batch, dec_seq, _ = x_sample.shape
    depth = w_ada.shape[0]
    assert d == D_MODEL and dec_batch % SUBLANES == 0 and seq % TOKEN_TILE == 0
    assert w_in.shape[2] == N_MIX_BLOCKS * CB + 2 * D_MODEL

    pad = (-batch) % SUBLANES
    c_all = jnp.concatenate([c_sample, c_prompt, jnp.zeros((pad, d), c_prompt.dtype)], axis=0)
    prompt_row_block = dec_batch // SUBLANES

    vec_names = ("g_pre1", "g_post1", "g_pre2", "g_post2", "pool_scale", "b_lru_conv", "b_rg",
                 "b_ig", "lru_lambda", "b_ffn_conv")
    weights = dict(w_ada=w_ada, b_ada=b_ada, g_pre1=g_pre1, g_post1=g_post1, g_pre2=g_pre2,
                   g_post2=g_post2, w_in=w_in, w_pool_grp=w_pool_grp, pool_scale=pool_scale,
                   w_lru_conv=w_lru_conv, b_lru_conv=b_lru_conv, w_rg=w_rg, b_rg=b_rg, w_ig=w_ig,
                   b_ig=b_ig, lru_lambda=lru_lambda, w_pool_up=w_pool_up, w_lru_up=w_lru_up,
                   w_out=w_out, w_ffn_up=w_ffn_up, w_ffn_conv=w_ffn_conv, b_ffn_conv=b_ffn_conv,
                   w_ffn_down=w_ffn_down)

    def time_major(a):
        return jnp.swapaxes(a, 0, 1).reshape(-1, a.shape[-1])

    def seq_major(a2, n_rows):
        return jnp.swapaxes(a2.reshape(n_rows, dec_batch, -1), 0, 1)

    xp = x_prompt.reshape(batch * seq, d)
    xs = time_major(x_sample)
    tps = seq // TOKEN_TILE
    outs_p = ([], [], [], [])
    outs_s = ([], [], [], [])
    for l in range(depth):
        p = {k: v[l] for k, v in weights.items()}
        for k in vec_names + ("b_ada",):
            p[k] = p[k].reshape(1, -1)
        ada = _ada(c_all, p["w_ada"], p["b_ada"])

        y, npool, nconv, nh, h = _mix_prompt(xp, ada, p, batch, seq, prompt_row_block)
        m, wup_b, wdn_b = _merge(h, y, p, TOKEN_TILE, "merge_prompt", cast_ffn_cb=FFN_CB)
        x1 = _outproj(m, xp, ada, p, OUTPROJ_TILE, seq // OUTPROJ_TILE, prompt_row_block, 0)
        xp, nffn = _ffn_prompt(x1, ada, p, wup_b, wdn_b, batch, TOKEN_TILE, tps, prompt_row_block)
        outs_p[0].append(npool)
        outs_p[1].append(nconv)
        outs_p[2].append(nh.reshape(batch, LRU_WIDTH))
        outs_p[3].append(nffn[tps - 1::tps].reshape(batch, FFN_CONV - 1, -1))

        rows_s = dec_batch * dec_seq
        y, npool, nconv, nh, h = _mix_sample(
            xs, ada, p, time_major(state_pool[l]), time_major(state_lru_conv[l]),
            state_lru_h[l], dec_batch, dec_seq, PAST_LEN)
        m, = _merge(h, y, p, rows_s, "merge_sample")
        x1 = _outproj(m, xs, ada, p, rows_s, 1, 0, dec_seq)
        xs, ng, nv = _ffn_sample(x1, ada, p, wup_b, wdn_b, time_major(state_ffn_conv[l]),
                                 dec_batch, dec_seq)
        outs_s[0].append(seq_major(npool, POOL_BUF))
        outs_s[1].append(seq_major(nconv, LRU_CONV - 1))
        outs_s[2].append(nh)
        outs_s[3].append(jnp.concatenate([seq_major(ng, FFN_CONV - 1),
                                          seq_major(nv, FFN_CONV - 1)], axis=-1))

    return (xp.reshape(batch, seq, d), seq_major(xs, dec_seq),
            jnp.stack(outs_p[0]), jnp.stack(outs_p[1]), jnp.stack(outs_p[2]), jnp.stack(outs_p[3]),
            jnp.stack(outs_s[0]), jnp.stack(outs_s[1]), jnp.stack(outs_s[2]), jnp.stack(outs_s[3]))
```

```python
import functools

import jax
import jax.numpy as jnp
from jax import lax
from jax.experimental import pallas as pl
from jax.experimental.pallas import tpu as pltpu

F32 = jnp.float32
BF16 = jnp.bfloat16

D_MODEL = 2048
POOL_WINDOWS = (2, 4, 8, 16)
POOL_GROUPS = len(POOL_WINDOWS)
POOL_BUF = max(POOL_WINDOWS) - 1
LRU_CONV = 4
LRU_C = 8.0
PAST_LEN = 16384
FFN_CONV = 3
N_ADA = 6
EPS = 1e-6

CB = 256
POOL_WIDTH = POOL_GROUPS * CB
LRU_BLOCKS = 8
LRU_WIDTH = LRU_BLOCKS * CB
N_MIX_BLOCKS = POOL_GROUPS + LRU_BLOCKS
HALO = 16
SUBLANES = 8
VMEM_LIMIT = 60 * 1024 * 1024


def _cparams(n_axes):
    return pltpu.CompilerParams(
        dimension_semantics=("arbitrary",) * n_axes, vmem_limit_bytes=VMEM_LIMIT)


def _dot(a, b):
    return jnp.dot(a, b, preferred_element_type=F32)


def _softplus(z):
    return jnp.maximum(z, 0.0) + jnp.log1p(jnp.exp(-jnp.abs(z)))


def _lru_coeffs(xc, r_pre, i_pre, neg_c_sp):
    r = jax.nn.sigmoid(r_pre)
    i = jax.nn.sigmoid(i_pre)
    log_a = r * neg_c_sp
    a = jnp.exp(log_a)
    m2 = -jnp.tanh(log_a) * (a * a + 1.0)
    root = jnp.where(m2 > 0.0, m2 * lax.rsqrt(m2), 0.0)
    return a, root * (i * xc)


ADA_K = 256
ADA_N = 1024


def _ada_kernel(c_ref, w_ref, b_ref, o_ref):
    rows, n_out = o_ref.shape

    @pl.when(pl.program_id(0) == 0)
    def _():
        o_ref[...] = jnp.broadcast_to(b_ref[...], (rows, n_out))

    c = c_ref[...]
    s = (c * jax.nn.sigmoid(c)).astype(BF16)
    for n0 in range(0, n_out, ADA_N):
        o_ref[:, n0:n0 + ADA_N] += _dot(s, w_ref[:, n0:n0 + ADA_N].astype(BF16))


def _ada(c_all, w_ada, b_ada):
    rows = c_all.shape[0]
    n_out = N_ADA * D_MODEL
    return pl.pallas_call(
        _ada_kernel,
        grid=(D_MODEL // ADA_K,),
        in_specs=[pl.BlockSpec((rows, ADA_K), lambda k: (0, k)),
                  pl.BlockSpec((ADA_K, n_out), lambda k: (k, 0)),
                  pl.BlockSpec((1, n_out), lambda k: (0, 0))],
        out_specs=pl.BlockSpec((rows, n_out), lambda k: (0, 0)),
        out_shape=jax.ShapeDtypeStruct((rows, n_out), F32),
        compiler_params=_cparams(1),
        name="ada",
    )(c_all, w_ada, b_ada)


NORM_ROWS = 32
NORM_UNROLL = 8
OUT_SUBTILE = 256


def _unit_rms(x):
    return x * lax.rsqrt(jnp.mean(x * x, axis=-1, keepdims=True) + EPS)


def _build_h_prompt(x_ref, h_ref, g, scale1p, shift, n_rows):
    gs = g * scale1p

    def body(i, carry):
        r0 = pl.multiple_of(i * NORM_ROWS, NORM_ROWS)
        x = x_ref[pl.ds(r0, NORM_ROWS), :]
        h_ref[pl.ds(r0, NORM_ROWS), :] = (_unit_rms(x) * gs + shift).astype(BF16)
        return carry
    lax.fori_loop(0, n_rows // NORM_ROWS, body, 0, unroll=NORM_UNROLL)


def _residual_norm_prompt(x_ref, acc_ref, o_ref, g, gate, n_rows):
    gg = g * gate
    for r0 in range(0, n_rows, NORM_ROWS):
        sl = slice(r0, r0 + NORM_ROWS)
        o_ref[sl, :] = x_ref[sl, :] + _unit_rms(acc_ref[sl, :]) * gg


def _build_h_sample(x_ref, h_ref, g, sc_ref, sh_ref, n_seq, n_t):
    for s0 in range(0, n_seq, NORM_ROWS):
        gs = g * (1.0 + sc_ref[s0:s0 + NORM_ROWS, :])
        shift = sh_ref[s0:s0 + NORM_ROWS, :]
        for t in range(n_t):
            sl = slice(t * n_seq + s0, t * n_seq + s0 + NORM_ROWS)
            h_ref[sl, :] = (_unit_rms(x_ref[sl, :]) * gs + shift).astype(BF16)


MIX_SUBTILE = 512
POOL_ROWS = 128
LRU_ROWS = 64


def _mix_prompt_kernel(x_ref, sh_ref, sc_ref, g_ref, win_ref, wgrp_ref, pscale_ref,
                       wconv_ref, bconv_ref, wrg_ref, brg_ref, wig_ref, big_ref, lam_ref, wdn_ref,
                       y_ref, npool_ref, nconv_ref, nh_ref, h_ref, wdnb_ref,
                       ext_ref, xb_ref, r_ref, i_ref, a_ref, u_ref, *, seq):
    q = pl.program_id(0)
    j = pl.program_id(1)
    _cast_slab(wdn_ref, wdnb_ref)

    @pl.when(j == 0)
    def _():
        _build_h_prompt(x_ref, h_ref, g_ref[...], 1.0 + sc_ref[pl.ds(q, 1), :],
                        sh_ref[pl.ds(q, 1), :], seq)
        for ref in (ext_ref, r_ref, i_ref, a_ref):
            ref[0:HALO, :] = jnp.zeros((HALO, CB), F32)

    win = win_ref[...].astype(BF16)

    def rows_of(r0, n, shift=0):
        return slice(HALO + r0 - shift, HALO + r0 - shift + n)

    def up_proj(m0):
        ext_ref[rows_of(m0, MIX_SUBTILE), :] = _dot(h_ref[m0:m0 + MIX_SUBTILE, :], win)

    def pool_branch(w):
        wg = wgrp_ref[0].astype(BF16)
        ps = pscale_ref[...]
        partial = {2: r_ref, 4: i_ref, 8: a_ref}
        up_proj(0)
        for m0 in range(0, seq, MIX_SUBTILE):
            if m0 + MIX_SUBTILE < seq:
                up_proj(m0 + MIX_SUBTILE)
            for r0 in range(m0, m0 + MIX_SUBTILE, POOL_ROWS):
                u = ext_ref[rows_of(r0, POOL_ROWS), :]
                s, src, width = u, ext_ref, 1
                while width < w:
                    s = s + src[rows_of(r0, POOL_ROWS, width), :]
                    width *= 2
                    if width < w:
                        src = partial[width]
                        src[rows_of(r0, POOL_ROWS), :] = s
                if r0 < w:
                    pos = r0 + lax.broadcasted_iota(jnp.int32, (POOL_ROWS, 1), 0)
                    cnt = jnp.minimum(w, pos + 1).astype(F32)
                else:
                    cnt = float(w)
                xb_ref[r0:r0 + POOL_ROWS, :] = (s / cnt - u).astype(BF16)
            sub = slice(m0, m0 + MIX_SUBTILE)
            y_ref[sub, :] = (_dot(xb_ref[sub, :], wg) * ps).astype(BF16)
        npool_ref[0] = ext_ref[pl.ds(HALO + seq - POOL_BUF, POOL_BUF), :]

    for g, w in enumerate(POOL_WINDOWS):
        pl.when(j == g)(functools.partial(pool_branch, w))

    @pl.when(j >= POOL_GROUPS)
    def _():
        wc = wconv_ref[...]
        bc = bconv_ref[...]
        wrg = wrg_ref[0].astype(BF16)
        wig = wig_ref[0].astype(BF16)
        neg_c_sp = (-LRU_C) * _softplus(-lam_ref[...])
        brg = brg_ref[...]
        big = big_ref[...]
        nb = LRU_ROWS // SUBLANES
        row = lax.broadcasted_iota(jnp.int32, (nb, SUBLANES, CB), 1)
        h_carry = jnp.zeros((1, CB), F32)
        up_proj(0)
        for m0 in range(0, seq, MIX_SUBTILE):
            sub = slice(m0, m0 + MIX_SUBTILE)
            if m0 + MIX_SUBTILE < seq:
                up_proj(m0 + MIX_SUBTILE)
            for r0 in range(m0, m0 + MIX_SUBTILE, LRU_ROWS):
                xc = wc[LRU_CONV - 1:, :] * ext_ref[rows_of(r0, LRU_ROWS), :] + bc
                for k in range(LRU_CONV - 1):
                    xc = xc + wc[k:k + 1, :] * ext_ref[rows_of(r0, LRU_ROWS, LRU_CONV - 1 - k), :]
                a_ref[rows_of(r0, LRU_ROWS), :] = xc
                xb_ref[r0:r0 + LRU_ROWS, :] = xc.astype(BF16)
            r_ref[rows_of(m0, MIX_SUBTILE), :] = _dot(xb_ref[sub, :], wrg)
            i_ref[rows_of(m0, MIX_SUBTILE), :] = _dot(xb_ref[sub, :], wig)
            for r0 in range(m0, m0 + MIX_SUBTILE, LRU_ROWS):
                sl = rows_of(r0, LRU_ROWS)
                a, b = _lru_coeffs(a_ref[sl, :], r_ref[sl, :] + brg, i_ref[sl, :] + big, neg_c_sp)
                a = a.reshape(nb, SUBLANES, CB)
                b = b.reshape(nb, SUBLANES, CB)
                for k in (1, 2, 4):
                    a_sh = jnp.where(row >= k, pltpu.roll(a, k, 1), 1.0)
                    b_sh = jnp.where(row >= k, pltpu.roll(b, k, 1), 0.0)
                    b = b + a * b_sh
                    a = a * a_sh
                a_ref[sl, :] = a.reshape(LRU_ROWS, CB)
                u_ref[sl, :] = b.reshape(LRU_ROWS, CB)
            for r0 in range(m0, m0 + MIX_SUBTILE, SUBLANES):
                sl = rows_of(r0, SUBLANES)
                h8 = a_ref[sl, :] * h_carry + u_ref[sl, :]
                u_ref[sl, :] = h8
                h_carry = h8[SUBLANES - 1:SUBLANES, :]
            y_ref[sub, :] = u_ref[rows_of(m0, MIX_SUBTILE), :].astype(BF16)
        nh_ref[0] = h_carry
        nconv_ref[0] = ext_ref[pl.ds(HALO + seq - (LRU_CONV - 1), LRU_CONV - 1), :]


def _mix_sample_kernel(x_ref, sh_ref, sc_ref, g_ref, win_ref, wgrp_ref, pscale_ref,
                       wconv_ref, bconv_ref, wrg_ref, brg_ref, wig_ref, big_ref, lam_ref,
                       spool_ref, sconv_ref, sh0_ref,
                       y_ref, npool_ref, nconv_ref, nh_ref, h_ref,
                       u_ref, d_ref, *, n_seq, n_t, start):
    j = pl.program_id(1)

    @pl.when(j == 0)
    def _():
        _build_h_sample(x_ref, h_ref, g_ref[...], sc_ref, sh_ref, n_seq, n_t)

    u_ref[...] = _dot(h_ref[...], win_ref[...].astype(BF16))

    def u_slab(t):
        return u_ref[t * n_seq:(t + 1) * n_seq, :]

    def pool_branch(w):
        e = [spool_ref[k * n_seq:(k + 1) * n_seq, :] for k in range(POOL_BUF)]
        e += [u_slab(t) for t in range(n_t)]
        for k in range(POOL_BUF):
            npool_ref[k * n_seq:(k + 1) * n_seq, :] = e[n_t + k]
        for t in range(n_t):
            s = e[POOL_BUF + t]
            for k in range(1, w):
                s = s + e[POOL_BUF + t - k]
            cnt = float(min(w, start + t + 1))
            d_ref[t * n_seq:(t + 1) * n_seq, :] = (s / cnt - e[POOL_BUF + t]).astype(BF16)
        y = _dot(d_ref[...], wgrp_ref[0].astype(BF16)) * pscale_ref[...]
        y_ref[...] = y.astype(BF16)

    for g, w in enumerate(POOL_WINDOWS):
        pl.when(j == g)(functools.partial(pool_branch, w))

    @pl.when(j >= POOL_GROUPS)
    def _():
        nk = LRU_CONV - 1
        e = [sconv_ref[k * n_seq:(k + 1) * n_seq, :] for k in range(nk)]
        e += [u_slab(t) for t in range(n_t)]
        for k in range(nk):
            nconv_ref[k * n_seq:(k + 1) * n_seq, :] = e[n_t + k]
        wc = wconv_ref[...]
        bc = bconv_ref[...]
        wrg = wrg_ref[0].astype(BF16)
        wig = wig_ref[0].astype(BF16)
        neg_c_sp = (-LRU_C) * _softplus(-lam_ref[...])
        h = sh0_ref[...]
        for t in range(n_t):
            xc = bc
            for k in range(LRU_CONV):
                xc = xc + wc[k:k + 1, :] * e[t + k]
            xb = xc.astype(BF16)
            a, b = _lru_coeffs(xc, _dot(xb, wrg) + brg_ref[...], _dot(xb, wig) + big_ref[...],
                               neg_c_sp)
            h = a * h + b
            y_ref[t * n_seq:(t + 1) * n_seq, :] = h.astype(BF16)
        nh_ref[...] = h


def _mix_weight_specs():
    pj = lambda j: jnp.minimum(j, POOL_GROUPS - 1)
    lj = lambda j: jnp.maximum(j - POOL_GROUPS, 0)
    return [
        pl.BlockSpec((1, D_MODEL), lambda q, j: (0, 0)),
        pl.BlockSpec((D_MODEL, CB), lambda q, j: (0, j)),
        pl.BlockSpec((1, CB, CB), lambda q, j: (pj(j), 0, 0)),
        pl.BlockSpec((1, CB), lambda q, j: (0, pj(j))),
        pl.BlockSpec((LRU_CONV, CB), lambda q, j: (0, lj(j))),
        pl.BlockSpec((1, CB), lambda q, j: (0, lj(j))),
        pl.BlockSpec((1, CB, CB), lambda q, j: (lj(j), 0, 0)),
        pl.BlockSpec((1, CB), lambda q, j: (0, lj(j))),
        pl.BlockSpec((1, CB, CB), lambda q, j: (lj(j), 0, 0)),
        pl.BlockSpec((1, CB), lambda q, j: (0, lj(j))),
        pl.BlockSpec((1, CB), lambda q, j: (0, lj(j))),
    ], pj, lj


def _mix_prompt(x2, ada, p, n_seq, seq, ada_row_block):
    wspecs, pj, lj = _mix_weight_specs()
    in_specs = [
        pl.BlockSpec((seq, D_MODEL), lambda q, j: (q, 0), pipeline_mode=pl.Buffered(1)),
        pl.BlockSpec((SUBLANES, D_MODEL), lambda q, j: (ada_row_block, 0)),
        pl.BlockSpec((SUBLANES, D_MODEL), lambda q, j: (ada_row_block, 1)),
    ] + wspecs
    d_ff = p["w_ffn_down"].shape[0]
    dn_rows = d_ff // (n_seq * N_MIX_BLOCKS)
    assert dn_rows * n_seq * N_MIX_BLOCKS == d_ff and dn_rows % BF16_ROWS == 0
    slab_spec = pl.BlockSpec((dn_rows, D_MODEL), lambda q, j: (q * N_MIX_BLOCKS + j, 0))
    in_specs.append(slab_spec)
    out_specs = [
        pl.BlockSpec((seq, CB), lambda q, j: (q, j)),
        pl.BlockSpec((1, POOL_BUF, CB), lambda q, j: (q, 0, pj(j))),
        pl.BlockSpec((1, LRU_CONV - 1, CB), lambda q, j: (q, 0, lj(j))),
        pl.BlockSpec((1, 1, CB), lambda q, j: (q, 0, lj(j))),
        pl.BlockSpec((seq, D_MODEL), lambda q, j: (q, 0)),
        slab_spec,
    ]
    out_shape = [
        jax.ShapeDtypeStruct((n_seq * seq, N_MIX_BLOCKS * CB), BF16),
        jax.ShapeDtypeStruct((n_seq, POOL_BUF, POOL_WIDTH), F32),
        jax.ShapeDtypeStruct((n_seq, LRU_CONV - 1, LRU_WIDTH), F32),
        jax.ShapeDtypeStruct((n_seq, 1, LRU_WIDTH), F32),
        jax.ShapeDtypeStruct((n_seq * seq, D_MODEL), BF16),
        jax.ShapeDtypeStruct((d_ff, D_MODEL), BF16),
    ]
    scratch = [
        pltpu.VMEM((HALO + seq, CB), F32),
        pltpu.VMEM((seq, CB), BF16),
        pltpu.VMEM((HALO + seq, CB), F32),
        pltpu.VMEM((HALO + seq, CB), F32),
        pltpu.VMEM((HALO + seq, CB), F32),
        pltpu.VMEM((HALO + seq, CB), F32),
    ]
    return pl.pallas_call(
        functools.partial(_mix_prompt_kernel, seq=seq),
        grid=(n_seq, N_MIX_BLOCKS),
        in_specs=in_specs, out_specs=out_specs, out_shape=out_shape,
        scratch_shapes=scratch, compiler_params=_cparams(2), name="mix_prompt",
    )(x2, ada, ada, p["g_pre1"], p["w_in"], p["w_pool_grp"], p["pool_scale"],
      p["w_lru_conv"], p["b_lru_conv"], p["w_rg"], p["b_rg"], p["w_ig"], p["b_ig"],
      p["lru_lambda"], p["w_ffn_down"])


def _mix_sample(x2, ada, p, spool2, sconv2, sh0, n_seq, n_t, start):
    wspecs, pj, lj = _mix_weight_specs()
    rows = n_seq * n_t
    in_specs = [
        pl.BlockSpec((rows, D_MODEL), lambda q, j: (0, 0), pipeline_mode=pl.Buffered(1)),
        pl.BlockSpec((n_seq, D_MODEL), lambda q, j: (0, 0)),
        pl.BlockSpec((n_seq, D_MODEL), lambda q, j: (0, 1)),
    ] + wspecs + [
        pl.BlockSpec((n_seq * POOL_BUF, CB), lambda q, j: (0, pj(j))),
        pl.BlockSpec((n_seq * (LRU_CONV - 1), CB), lambda q, j: (0, lj(j))),
        pl.BlockSpec((n_seq, CB), lambda q, j: (0, lj(j))),
    ]
    out_specs = [
        pl.BlockSpec((rows, CB), lambda q, j: (0, j)),
        pl.BlockSpec((n_seq * POOL_BUF, CB), lambda q, j: (0, pj(j))),
        pl.BlockSpec((n_seq * (LRU_CONV - 1), CB), lambda q, j: (0, lj(j))),
        pl.BlockSpec((n_seq, CB), lambda q, j: (0, lj(j))),
        pl.BlockSpec((rows, D_MODEL), lambda q, j: (0, 0)),
    ]
    out_shape = [
        jax.ShapeDtypeStruct((rows, N_MIX_BLOCKS * CB), BF16),
        jax.ShapeDtypeStruct((n_seq * POOL_BUF, POOL_WIDTH), F32),
        jax.ShapeDtypeStruct((n_seq * (LRU_CONV - 1), LRU_WIDTH), F32),
        jax.ShapeDtypeStruct((n_seq, LRU_WIDTH), F32),
        jax.ShapeDtypeStruct((rows, D_MODEL), BF16),
    ]
    scratch = [
        pltpu.VMEM((rows, CB), F32),
        pltpu.VMEM((rows, CB), BF16),
    ]
    return pl.pallas_call(
        functools.partial(_mix_sample_kernel, n_seq=n_seq, n_t=n_t, start=start),
        grid=(1, N_MIX_BLOCKS),
        in_specs=in_specs, out_specs=out_specs, out_shape=out_shape,
        scratch_shapes=scratch, compiler_params=_cparams(2), name="mix_sample",
    )(x2, ada, ada, p["g_pre1"], p["w_in"], p["w_pool_grp"], p["pool_scale"],
      p["w_lru_conv"], p["b_lru_conv"], p["w_rg"], p["b_rg"], p["w_ig"], p["b_ig"],
      p["lru_lambda"], spool2, sconv2, sh0)


MERGE_ROWS = 64
MERGE_SUBTILE = 256
CAST_ROWS = 32
BF16_ROWS = 16


def _cast_ffn_up_slab(wup_ref, wupb_ref):
    n_blocks, _, cb2 = wupb_ref.shape
    cb = cb2 // 2
    for blk in range(n_blocks):
        wupb_ref[blk, :, 0:cb] = wup_ref[:, blk * cb:(blk + 1) * cb].astype(BF16)
        wupb_ref[blk, :, cb:cb2] = wup_ref[:, (n_blocks + blk) * cb:(n_blocks + blk + 1) * cb].astype(BF16)


def _cast_slab(w_ref, wb_ref):
    for r0 in range(0, w_ref.shape[0], CAST_ROWS):
        wb_ref[r0:r0 + CAST_ROWS, :] = w_ref[r0:r0 + CAST_ROWS, :].astype(BF16)


def _merge_kernel(h_ref, y_ref, wgp_ref, wgl_ref, wpu_ref, wlu_ref, *rest, tm, cast_ffn):
    if cast_ffn:
        wup_ref, m_ref, wupb_ref, wg_ref, g_ref, pu_ref, lu_ref = rest
    else:
        m_ref, wg_ref, g_ref, pu_ref, lu_ref = rest
    wg_ref[:, 0:CB] = wgp_ref[...].astype(BF16)
    wg_ref[:, CB:2 * CB] = wgl_ref[...].astype(BF16)
    wpu = wpu_ref[...].astype(BF16)
    wlu = wlu_ref[...].astype(BF16)

    def slot(m0):
        s0 = (m0 // MERGE_SUBTILE) % 2 * MERGE_SUBTILE
        return slice(s0, s0 + MERGE_SUBTILE)

    def proj(m0):
        rows = slice(m0, m0 + MERGE_SUBTILE)
        g_ref[slot(m0), :] = _dot(h_ref[rows, :], wg_ref[...])
        pu_ref[slot(m0), :] = _dot(y_ref[rows, 0:POOL_WIDTH], wpu)
        lu_ref[slot(m0), :] = _dot(y_ref[rows, POOL_WIDTH:POOL_WIDTH + LRU_WIDTH], wlu)

    proj(0)
    for m0 in range(0, tm, MERGE_SUBTILE):
        if m0 + MERGE_SUBTILE < tm:
            proj(m0 + MERGE_SUBTILE)
        s0 = slot(m0).start
        for r0 in range(0, MERGE_SUBTILE, MERGE_ROWS):
            sl = slice(s0 + r0, s0 + r0 + MERGE_ROWS)
            m = (jax.nn.sigmoid(g_ref[sl, 0:CB]) * pu_ref[sl, :]
                 + jax.nn.sigmoid(g_ref[sl, CB:2 * CB]) * lu_ref[sl, :])
            m_ref[m0 + r0:m0 + r0 + MERGE_ROWS, :] = m.astype(BF16)
    if cast_ffn:
        _cast_ffn_up_slab(wup_ref, wupb_ref)


def _merge(h, y, p, tm, name, cast_ffn_cb=0):
    rows = h.shape[0]
    n_col = D_MODEL // CB
    n_gate0 = N_MIX_BLOCKS
    n_gate1 = N_MIX_BLOCKS + n_col
    in_specs = [
        pl.BlockSpec((tm, D_MODEL), lambda i, c: (i, 0), pipeline_mode=pl.Buffered(1)),
        pl.BlockSpec((tm, N_MIX_BLOCKS * CB), lambda i, c: (i, 0), pipeline_mode=pl.Buffered(1)),
        pl.BlockSpec((D_MODEL, CB), lambda i, c: (0, n_gate0 + c)),
        pl.BlockSpec((D_MODEL, CB), lambda i, c: (0, n_gate1 + c)),
        pl.BlockSpec((POOL_WIDTH, CB), lambda i, c: (0, c)),
        pl.BlockSpec((LRU_WIDTH, CB), lambda i, c: (0, c)),
    ]
    out_specs = [pl.BlockSpec((tm, CB), lambda i, c: (i, c))]
    out_shape = [jax.ShapeDtypeStruct((rows, D_MODEL), BF16)]
    args = [h, y, p["w_in"], p["w_in"], p["w_pool_up"], p["w_lru_up"]]
    if cast_ffn_cb:
        n_steps = (rows // tm) * n_col
        d_ff = p["w_ffn_down"].shape[0]
        n_blocks = d_ff // cast_ffn_cb
        up_rows = D_MODEL // n_steps
        assert up_rows * n_steps == D_MODEL and up_rows % BF16_ROWS == 0
        step = lambda i, c: i * n_col + c
        in_specs += [pl.BlockSpec((up_rows, 2 * d_ff), lambda i, c: (step(i, c), 0))]
        out_specs += [pl.BlockSpec((n_blocks, up_rows, 2 * cast_ffn_cb),
                                   lambda i, c: (0, step(i, c), 0))]
        out_shape += [jax.ShapeDtypeStruct((n_blocks, D_MODEL, 2 * cast_ffn_cb), BF16)]
        args += [p["w_ffn_up"]]
    ring = 2 * MERGE_SUBTILE
    return pl.pallas_call(
        functools.partial(_merge_kernel, tm=tm, cast_ffn=bool(cast_ffn_cb)),
        grid=(rows // tm, n_col),
        in_specs=in_specs, out_specs=out_specs, out_shape=out_shape,
        scratch_shapes=[pltpu.VMEM((D_MODEL, 2 * CB), BF16), pltpu.VMEM((ring, 2 * CB), F32),
                        pltpu.VMEM((ring, CB), F32), pltpu.VMEM((ring, CB), F32)],
        compiler_params=_cparams(2), name=name,
    )(*args)


def _cast_weight(w_ref, wb_ref):
    rows = w_ref.shape[0]
    for r0 in range(0, rows, 256):
        wb_ref[r0:r0 + 256, :] = w_ref[r0:r0 + 256, :].astype(BF16)


def _outproj_kernel(m_ref, x_ref, gate_ref, g_ref, w_ref, o_ref, wb_ref, acc_ref,
                    *, tm, tiles_per_seq, sample_t):
    i = pl.program_id(0)

    @pl.when(i == 0)
    def _():
        _cast_weight(w_ref, wb_ref)

    g = g_ref[...]
    if sample_t:
        n_seq = tm // sample_t
    else:
        gg_tile = g * gate_ref[pl.ds(i // tiles_per_seq, 1), :]

    def proj(m0):
        acc_ref[m0:m0 + OUT_SUBTILE, :] = _dot(m_ref[m0:m0 + OUT_SUBTILE, :], wb_ref[...])

    proj(0)
    for m0 in range(0, tm, OUT_SUBTILE):
        if m0 + OUT_SUBTILE < tm:
            proj(m0 + OUT_SUBTILE)
        for r0 in range(m0, m0 + OUT_SUBTILE, NORM_ROWS):
            sl = slice(r0, r0 + NORM_ROWS)
            if sample_t:
                s0 = r0 % n_seq
                gg = g * gate_ref[s0:s0 + NORM_ROWS, :]
            else:
                gg = gg_tile
            o_ref[sl, :] = x_ref[sl, :] + _unit_rms(acc_ref[sl, :]) * gg


def _outproj(m, x2, ada, p, tm, tiles_per_seq, ada_row_block, sample_t):
    rows = x2.shape[0]
    if sample_t:
        gate_spec = pl.BlockSpec((tm // sample_t, D_MODEL), lambda i: (0, 2))
    else:
        gate_spec = pl.BlockSpec((SUBLANES, D_MODEL), lambda i: (ada_row_block, 2))
    return pl.pallas_call(
        functools.partial(_outproj_kernel, tm=tm, tiles_per_seq=tiles_per_seq, sample_t=sample_t),
        grid=(rows // tm,),
        in_specs=[pl.BlockSpec((tm, D_MODEL), lambda i: (i, 0)),
                  pl.BlockSpec((tm, D_MODEL), lambda i: (i, 0)),
                  gate_spec,
                  pl.BlockSpec((1, D_MODEL), lambda i: (0, 0)),
                  pl.BlockSpec((D_MODEL, D_MODEL), lambda i: (0, 0), pipeline_mode=pl.Buffered(1))],
        out_specs=pl.BlockSpec((tm, D_MODEL), lambda i: (i, 0)),
        out_shape=jax.ShapeDtypeStruct((rows, D_MODEL), F32),
        scratch_shapes=[pltpu.VMEM((D_MODEL, D_MODEL), BF16), pltpu.VMEM((tm, D_MODEL), F32)],
        compiler_params=_cparams(1),
        name="outproj_sample" if sample_t else "outproj_prompt",
    )(m, x2, ada, p["g_post1"], p["w_out"])


GELU_C0 = 0.7978845608028654
GELU_C1 = GELU_C0 * 0.044715
FFN_ROWS = 32
FFN_SUBTILE = 256
FFN_AHEAD = 2
FFN_CB = 512


def _ffn_conv_gate(eg, ev, wcg, wcv, bcg, bcv):
    last = FFN_CONV - 1
    g = wcg[last:, :] * eg[last] + bcg
    v = wcv[last:, :] * ev[last] + bcv
    for k in range(last):
        g = g + wcg[k:k + 1, :] * eg[k]
        v = v + wcv[k:k + 1, :] * ev[k]
    t = jnp.tanh(g * (GELU_C0 + GELU_C1 * (g * g)))
    return ((g * v) * (0.5 + 0.5 * t)).astype(BF16)


def _ffn_prompt_kernel(x_ref, sh_ref, sc_ref, gate_ref, gpre_ref, gpost_ref,
                       wup_ref, wcg_ref, wcv_ref, bcg_ref, bcv_ref, wdn_ref,
                       o_ref, nst_ref,
                       h_ref, ext_ref, f_ref, carry_ref, *, tm, tiles_per_seq, n_blocks, cb):
    i = pl.program_id(0)
    c = pl.program_id(1)
    q = i // tiles_per_seq
    first = (i % tiles_per_seq) == 0
    nk = FFN_CONV - 1

    @pl.when(c == 0)
    def _():
        _build_h_prompt(x_ref, h_ref, gpre_ref[...], 1.0 + sc_ref[pl.ds(q, 1), :],
                        sh_ref[pl.ds(q, 1), :], tm)
        o_ref[...] = jnp.zeros((tm, D_MODEL), F32)

    @pl.when(first)
    def _():
        ext_ref[0:SUBLANES, :] = jnp.zeros((SUBLANES, 2 * cb), F32)

    @pl.when(jnp.logical_not(first))
    def _():
        ext_ref[0:SUBLANES, :] = carry_ref[c]

    wcg = wcg_ref[...]
    wcv = wcv_ref[...]
    bcg = bcg_ref[...]
    bcv = bcv_ref[...]

    def up_proj(m0):
        ext_ref[SUBLANES + m0:SUBLANES + m0 + FFN_SUBTILE, :] = _dot(
            h_ref[m0:m0 + FFN_SUBTILE, :], wup_ref[0])

    for m0 in range(0, min(FFN_AHEAD * FFN_SUBTILE, tm), FFN_SUBTILE):
        up_proj(m0)
    for m0 in range(0, tm, FFN_SUBTILE):
        rows = slice(m0, m0 + FFN_SUBTILE)
        for r0 in range(m0, m0 + FFN_SUBTILE, FFN_ROWS):
            shifted = [slice(SUBLANES + r0 - nk + k, SUBLANES + r0 - nk + k + FFN_ROWS)
                       for k in range(FFN_CONV)]
            eg = [ext_ref[s, 0:cb] for s in shifted]
            ev = [ext_ref[s, cb:2 * cb] for s in shifted]
            f_ref[r0:r0 + FFN_ROWS, :] = _ffn_conv_gate(eg, ev, wcg, wcv, bcg, bcv)
        if m0 + FFN_AHEAD * FFN_SUBTILE < tm:
            up_proj(m0 + FFN_AHEAD * FFN_SUBTILE)
        o_ref[rows, :] += _dot(f_ref[rows, :], wdn_ref[...])

    carry_ref[c] = ext_ref[tm:tm + SUBLANES, :]
    nst_ref[0, :, 0, :] = ext_ref[pl.ds(SUBLANES + tm - nk, nk), 0:cb]
    nst_ref[0, :, 1, :] = ext_ref[pl.ds(SUBLANES + tm - nk, nk), cb:2 * cb]

    @pl.when(c == n_blocks - 1)
    def _():
        _residual_norm_prompt(x_ref, o_ref, o_ref, gpost_ref[...], gate_ref[pl.ds(q, 1), :], tm)


def _ffn_sample_kernel(x_ref, sh_ref, sc_ref, gate_ref, gpre_ref, gpost_ref,
                       wup_ref, wcg_ref, wcv_ref, bcg_ref, bcv_ref, wdn_ref,
                       sg_ref, sv_ref,
                       o_ref, ng_ref, nv_ref,
                       h_ref, ext_ref, f_ref, acc_ref, *, n_seq, n_t, n_blocks, cb):
    c = pl.program_id(1)
    nk = FFN_CONV - 1

    @pl.when(c == 0)
    def _():
        _build_h_sample(x_ref, h_ref, gpre_ref[...], sc_ref, sh_ref, n_seq, n_t)
        acc_ref[...] = jnp.zeros((n_seq * n_t, D_MODEL), F32)

    wcg = wcg_ref[...]
    wcv = wcv_ref[...]
    bcg = bcg_ref[...]
    bcv = bcv_ref[...]

    def view(state_ref, half, t, s0, n):
        if t < 0:
            return state_ref[(nk + t) * n_seq + s0:(nk + t) * n_seq + s0 + n, :]
        return ext_ref[t * n_seq + s0:t * n_seq + s0 + n, half * cb:(half + 1) * cb]

    t_sub = max(FFN_SUBTILE // n_seq, 1)

    def up_proj(t0):
        rows = slice(t0 * n_seq, (t0 + t_sub) * n_seq)
        ext_ref[rows, :] = _dot(h_ref[rows, :], wup_ref[0])

    for t0 in range(0, min(FFN_AHEAD * t_sub, n_t), t_sub):
        up_proj(t0)
    for t0 in range(0, n_t, t_sub):
        rows = slice(t0 * n_seq, (t0 + t_sub) * n_seq)
        for t in range(t0, t0 + t_sub):
            for s0 in range(0, n_seq, FFN_ROWS):
                eg = [view(sg_ref, 0, t - nk + k, s0, FFN_ROWS) for k in range(FFN_CONV)]
                ev = [view(sv_ref, 1, t - nk + k, s0, FFN_ROWS) for k in range(FFN_CONV)]
                f_ref[t * n_seq + s0:t * n_seq + s0 + FFN_ROWS, :] = _ffn_conv_gate(
                    eg, ev, wcg, wcv, bcg, bcv)
        if t0 + FFN_AHEAD * t_sub < n_t:
            up_proj(t0 + FFN_AHEAD * t_sub)
        acc_ref[rows, :] += _dot(f_ref[rows, :], wdn_ref[...])

    for k in range(nk):
        ng_ref[k * n_seq:(k + 1) * n_seq, :] = view(sg_ref, 0, n_t - nk + k, 0, n_seq)
        nv_ref[k * n_seq:(k + 1) * n_seq, :] = view(sv_ref, 1, n_t - nk + k, 0, n_seq)

    @pl.when(c == n_blocks - 1)
    def _():
        g = gpost_ref[...]
        for s0 in range(0, n_seq, NORM_ROWS):
            gg = g * gate_ref[s0:s0 + NORM_ROWS, :]
            for t in range(n_t):
                sl = slice(t * n_seq + s0, t * n_seq + s0 + NORM_ROWS)
                o_ref[sl, :] = x_ref[sl, :] + _unit_rms(acc_ref[sl, :]) * gg


def _ffn_weight_specs(n_blocks, cb):
    return [
        pl.BlockSpec((1, D_MODEL), lambda i, c: (0, 0)),
        pl.BlockSpec((1, D_MODEL), lambda i, c: (0, 0)),
        pl.BlockSpec((1, D_MODEL, 2 * cb), lambda i, c: (c, 0, 0)),
        pl.BlockSpec((FFN_CONV, cb), lambda i, c: (0, c)),
        pl.BlockSpec((FFN_CONV, cb), lambda i, c: (0, n_blocks + c)),
        pl.BlockSpec((1, cb), lambda i, c: (0, c)),
        pl.BlockSpec((1, cb), lambda i, c: (0, n_blocks + c)),
        pl.BlockSpec((cb, D_MODEL), lambda i, c: (c, 0)),
    ]


def _ffn_prompt(x1, ada, p, wup_b, wdn_b, n_seq, tm, tiles_per_seq, ada_row_block):
    rows = x1.shape[0]
    n_blocks, _, cb2 = wup_b.shape
    cb = cb2 // 2
    ada_spec = lambda k: pl.BlockSpec((SUBLANES, D_MODEL), lambda i, c: (ada_row_block, k))
    in_specs = [pl.BlockSpec((tm, D_MODEL), lambda i, c: (i, 0), pipeline_mode=pl.Buffered(1)),
                ada_spec(3), ada_spec(4), ada_spec(5)] + _ffn_weight_specs(n_blocks, cb)
    out_specs = [
        pl.BlockSpec((tm, D_MODEL), lambda i, c: (i, 0)),
        pl.BlockSpec((1, FFN_CONV - 1, 2, cb), lambda i, c: (i, 0, 0, c)),
    ]
    out_shape = [jax.ShapeDtypeStruct((rows, D_MODEL), F32),
                 jax.ShapeDtypeStruct((rows // tm, FFN_CONV - 1, 2, n_blocks * cb), F32)]
    scratch = [
        pltpu.VMEM((tm, D_MODEL), BF16),
        pltpu.VMEM((SUBLANES + tm, 2 * cb), F32),
        pltpu.VMEM((tm, cb), BF16),
        pltpu.VMEM((n_blocks, SUBLANES, 2 * cb), F32),
    ]
    return pl.pallas_call(
        functools.partial(_ffn_prompt_kernel, tm=tm, tiles_per_seq=tiles_per_seq,
                          n_blocks=n_blocks, cb=cb),
        grid=(rows // tm, n_blocks),
        in_specs=in_specs, out_specs=out_specs, out_shape=out_shape,
        scratch_shapes=scratch, compiler_params=_cparams(2), name="ffn_prompt",
    )(x1, ada, ada, ada, p["g_pre2"], p["g_post2"], wup_b, p["w_ffn_conv"], p["w_ffn_conv"],
      p["b_ffn_conv"], p["b_ffn_conv"], wdn_b)


def _ffn_sample(x1, ada, p, wup_b, wdn_b, sffn2, n_seq, n_t):
    rows = n_seq * n_t
    n_blocks, _, cb2 = wup_b.shape
    cb = cb2 // 2
    nk = FFN_CONV - 1
    ada_spec = lambda k: pl.BlockSpec((n_seq, D_MODEL), lambda i, c: (0, k))
    in_specs = [pl.BlockSpec((rows, D_MODEL), lambda i, c: (0, 0), pipeline_mode=pl.Buffered(1)),
                ada_spec(3), ada_spec(4), ada_spec(5)] + _ffn_weight_specs(n_blocks, cb) + [
        pl.BlockSpec((n_seq * nk, cb), lambda i, c: (0, c)),
        pl.BlockSpec((n_seq * nk, cb), lambda i, c: (0, n_blocks + c)),
    ]
    out_specs = [
        pl.BlockSpec((rows, D_MODEL), lambda i, c: (0, 0)),
        pl.BlockSpec((n_seq * nk, cb), lambda i, c: (0, c)),
        pl.BlockSpec((n_seq * nk, cb), lambda i, c: (0, c)),
    ]
    out_shape = [jax.ShapeDtypeStruct((rows, D_MODEL), F32),
                 jax.ShapeDtypeStruct((n_seq * nk, n_blocks * cb), F32),
                 jax.ShapeDtypeStruct((n_seq * nk, n_blocks * cb), F32)]
    scratch = [
        pltpu.VMEM((rows, D_MODEL), BF16),
        pltpu.VMEM((rows, 2 * cb), F32),
        pltpu.VMEM((rows, cb), BF16),
        pltpu.VMEM((rows, D_MODEL), F32),
    ]
    return pl.pallas_call(
        functools.partial(_ffn_sample_kernel, n_seq=n_seq, n_t=n_t, n_blocks=n_blocks, cb=cb),
        grid=(1, n_blocks),
        in_specs=in_specs, out_specs=out_specs, out_shape=out_shape,
        scratch_shapes=scratch, compiler_params=_cparams(2), name="ffn_sample",
    )(x1, ada, ada, ada, p["g_pre2"], p["g_post2"], wup_b, p["w_ffn_conv"], p["w_ffn_conv"],
      p["b_ffn_conv"], p["b_ffn_conv"], wdn_b, sffn2, sffn2)


TOKEN_TILE = 1024
MERGE_TILE = 2048
OUTPROJ_TILE = 512


def kernel(x_prompt, x_sample, c_prompt, c_sample, state_pool, state_lru_conv, state_lru_h, state_ffn_conv, w_ada, b_ada, g_pre1, g_post1, g_pre2, g_post2, w_in, w_pool_grp, pool_scale, w_lru_conv, b_lru_conv, w_rg, b_rg, w_ig, b_ig, lru_lambda, w_pool_up, w_lru_up, w_out, w_ffn_up, w_ffn_conv, b_ffn_conv, w_ffn_down):
    batch, seq, d = x_prompt.shape
    dec_batch, dec_seq, _ = x_sample.shape
    depth = w_ada.shape[0]
    assert d == D_MODEL and dec_batch % SUBLANES == 0 and seq % TOKEN_TILE == 0
    assert w_in.shape[2] == N_MIX_BLOCKS * CB + 2 * D_MODEL

    pad = (-batch) % SUBLANES
    c_all = jnp.concatenate([c_sample, c_prompt, jnp.zeros((pad, d), c_prompt.dtype)], axis=0)
    prompt_row_block = dec_batch // SUBLANES

    vec_names = ("g_pre1", "g_post1", "g_pre2", "g_post2", "pool_scale", "b_lru_conv", "b_rg",
                 "b_ig", "lru_lambda", "b_ffn_conv")
    weights = dict(w_ada=w_ada, b_ada=b_ada, g_pre1=g_pre1, g_post1=g_post1, g_pre2=g_pre2,
                   g_post2=g_post2, w_in=w_in, w_pool_grp=w_pool_grp, pool_scale=pool_scale,
                   w_lru_conv=w_lru_conv, b_lru_conv=b_lru_conv, w_rg=w_rg, b_rg=b_rg, w_ig=w_ig,
                   b_ig=b_ig, lru_lambda=lru_lambda, w_pool_up=w_pool_up, w_lru_up=w_lru_up,
                   w_out=w_out, w_ffn_up=w_ffn_up, w_ffn_conv=w_ffn_conv, b_ffn_conv=b_ffn_conv,
                   w_ffn_down=w_ffn_down)

    def time_major(a):
        return jnp.swapaxes(a, 0, 1).reshape(-1, a.shape[-1])

    def seq_major(a2, n_rows):
        return jnp.swapaxes(a2.reshape(n_rows, dec_batch, -1), 0, 1)

    xp = x_prompt.reshape(batch * seq, d)
    xs = time_major(x_sample)
    tps = seq // TOKEN_TILE
    outs_p = ([], [], [], [])
    outs_s = ([], [], [], [])
    for l in range(depth):
        p = {k: v[l] for k, v in weights.items()}
        for k in vec_names + ("b_ada",):
            p[k] = p[k].reshape(1, -1)
        ada = _ada(c_all, p["w_ada"], p["b_ada"])

        y, npool, nconv, nh, h, wdn_b = _mix_prompt(xp, ada, p, batch, seq, prompt_row_block)
        m, wup_b = _merge(h, y, p, MERGE_TILE, "merge_prompt", cast_ffn_cb=FFN_CB)
        x1 = _outproj(m, xp, ada, p, OUTPROJ_TILE, seq // OUTPROJ_TILE, prompt_row_block, 0)
        xp, nffn = _ffn_prompt(x1, ada, p, wup_b, wdn_b, batch, TOKEN_TILE, tps, prompt_row_block)
        outs_p[0].append(npool)
        outs_p[1].append(nconv)
        outs_p[2].append(nh.reshape(batch, LRU_WIDTH))
        outs_p[3].append(nffn[tps - 1::tps].reshape(batch, FFN_CONV - 1, -1))

        rows_s = dec_batch * dec_seq
        y, npool, nconv, nh, h = _mix_sample(
            xs, ada, p, time_major(state_pool[l]), time_major(state_lru_conv[l]),
            state_lru_h[l], dec_batch, dec_seq, PAST_LEN)
        m, = _merge(h, y, p, rows_s, "merge_sample")
        x1 = _outproj(m, xs, ada, p, rows_s, 1, 0, dec_seq)
        xs, ng, nv = _ffn_sample(x1, ada, p, wup_b, wdn_b, time_major(state_ffn_conv[l]),
                                 dec_batch, dec_seq)
        outs_s[0].append(seq_major(npool, POOL_BUF))
        outs_s[1].append(seq_major(nconv, LRU_CONV - 1))
        outs_s[2].append(nh)
        outs_s[3].append(jnp.concatenate([seq_major(ng, FFN_CONV - 1),
                                          seq_major(nv, FFN_CONV - 1)], axis=-1))

    return (xp.reshape(batch, seq, d), seq_major(xs, dec_seq),
            jnp.stack(outs_p[0]), jnp.stack(outs_p[1]), jnp.stack(outs_p[2]), jnp.stack(outs_p[3]),
            jnp.stack(outs_s[0]), jnp.stack(outs_s[1]), jnp.stack(outs_s[2]), jnp.stack(outs_s[3]))
```

```python
import functools

import jax
import jax.numpy as jnp
from jax import lax
from jax.experimental import pallas as pl
from jax.experimental.pallas import tpu as pltpu

F32 = jnp.float32
BF16 = jnp.bfloat16

D_MODEL = 2048
POOL_WINDOWS = (2, 4, 8, 16)
POOL_GROUPS = len(POOL_WINDOWS)
POOL_BUF = max(POOL_WINDOWS) - 1
LRU_CONV = 4
LRU_C = 8.0
PAST_LEN = 16384
FFN_CONV = 3
N_ADA = 6
EPS = 1e-6

CB = 256
POOL_WIDTH = POOL_GROUPS * CB
LRU_BLOCKS = 8
LRU_WIDTH = LRU_BLOCKS * CB
N_MIX_BLOCKS = POOL_GROUPS + LRU_BLOCKS
HALO = 16
SUBLANES = 8
VMEM_LIMIT = 60 * 1024 * 1024


def _cparams(n_axes):
    return pltpu.CompilerParams(
        dimension_semantics=("arbitrary",) * n_axes, vmem_limit_bytes=VMEM_LIMIT)


def _dot(a, b):
    return jnp.dot(a, b, preferred_element_type=F32)


def _softplus(z):
    return jnp.maximum(z, 0.0) + jnp.log1p(jnp.exp(-jnp.abs(z)))


def _lru_coeffs(xc, r_pre, i_pre, neg_c_sp):
    r = jax.nn.sigmoid(r_pre)
    i = jax.nn.sigmoid(i_pre)
    log_a = r * neg_c_sp
    a = jnp.exp(log_a)
    m2 = -jnp.tanh(log_a) * (a * a + 1.0)
    root = jnp.where(m2 > 0.0, m2 * lax.rsqrt(m2), 0.0)
    return a, root * (i * xc)


ADA_K = 256
ADA_N = 1024


def _ada_kernel(c_ref, w_ref, b_ref, o_ref):
    rows, n_out = o_ref.shape

    @pl.when(pl.program_id(0) == 0)
    def _():
        o_ref[...] = jnp.broadcast_to(b_ref[...], (rows, n_out))

    c = c_ref[...]
    s = (c * jax.nn.sigmoid(c)).astype(BF16)
    for n0 in range(0, n_out, ADA_N):
        o_ref[:, n0:n0 + ADA_N] += _dot(s, w_ref[:, n0:n0 + ADA_N].astype(BF16))


def _ada(c_all, w_ada, b_ada):
    rows = c_all.shape[0]
    n_out = N_ADA * D_MODEL
    return pl.pallas_call(
        _ada_kernel,
        grid=(D_MODEL // ADA_K,),
        in_specs=[pl.BlockSpec((rows, ADA_K), lambda k: (0, k)),
                  pl.BlockSpec((ADA_K, n_out), lambda k: (k, 0)),
                  pl.BlockSpec((1, n_out), lambda k: (0, 0))],
        out_specs=pl.BlockSpec((rows, n_out), lambda k: (0, 0)),
        out_shape=jax.ShapeDtypeStruct((rows, n_out), F32),
        compiler_params=_cparams(1),
        name="ada",
    )(c_all, w_ada, b_ada)


NORM_ROWS = 32
NORM_UNROLL = 8
OUT_SUBTILE = 256


def _unit_rms(x):
    return x * lax.rsqrt(jnp.mean(x * x, axis=-1, keepdims=True) + EPS)


def _build_h_prompt(x_ref, h_ref, g, scale1p, shift, n_rows):
    gs = g * scale1p

    def body(i, carry):
        r0 = pl.multiple_of(i * NORM_ROWS, NORM_ROWS)
        x = x_ref[pl.ds(r0, NORM_ROWS), :]
        h_ref[pl.ds(r0, NORM_ROWS), :] = (_unit_rms(x) * gs + shift).astype(BF16)
        return carry
    lax.fori_loop(0, n_rows // NORM_ROWS, body, 0, unroll=NORM_UNROLL)


def _residual_norm_prompt(x_ref, acc_ref, o_ref, g, gate, n_rows):
    gg = g * gate
    for r0 in range(0, n_rows, NORM_ROWS):
        sl = slice(r0, r0 + NORM_ROWS)
        o_ref[sl, :] = x_ref[sl, :] + _unit_rms(acc_ref[sl, :]) * gg


def _build_h_sample(x_ref, h_ref, g, sc_ref, sh_ref, n_seq, n_t):
    for s0 in range(0, n_seq, NORM_ROWS):
        gs = g * (1.0 + sc_ref[s0:s0 + NORM_ROWS, :])
        shift = sh_ref[s0:s0 + NORM_ROWS, :]
        for t in range(n_t):
            sl = slice(t * n_seq + s0, t * n_seq + s0 + NORM_ROWS)
            h_ref[sl, :] = (_unit_rms(x_ref[sl, :]) * gs + shift).astype(BF16)


MIX_SUBTILE = 512
POOL_ROWS = 128
LRU_ROWS = 64


def _mix_prompt_kernel(x_ref, sh_ref, sc_ref, g_ref, win_ref, wgrp_ref, pscale_ref,
                       wconv_ref, bconv_ref, wrg_ref, brg_ref, wig_ref, big_ref, lam_ref, wdn_ref,
                       y_ref, npool_ref, nconv_ref, nh_ref, h_ref, wdnb_ref,
                       ext_ref, xb_ref, r_ref, i_ref, a_ref, u_ref, *, seq):
    q = pl.program_id(0)
    j = pl.program_id(1)
    _cast_slab(wdn_ref, wdnb_ref)

    @pl.when(j == 0)
    def _():
        _build_h_prompt(x_ref, h_ref, g_ref[...], 1.0 + sc_ref[pl.ds(q, 1), :],
                        sh_ref[pl.ds(q, 1), :], seq)
        for ref in (ext_ref, r_ref, i_ref, a_ref):
            ref[0:HALO, :] = jnp.zeros((HALO, CB), F32)

    win = win_ref[...].astype(BF16)

    def rows_of(r0, n, shift=0):
        return slice(HALO + r0 - shift, HALO + r0 - shift + n)

    def up_proj(m0):
        ext_ref[rows_of(m0, MIX_SUBTILE), :] = _dot(h_ref[m0:m0 + MIX_SUBTILE, :], win)

    def pool_branch(w):
        wg = wgrp_ref[0].astype(BF16)
        ps = pscale_ref[...]
        partial = {2: r_ref, 4: i_ref, 8: a_ref}
        up_proj(0)
        for m0 in range(0, seq, MIX_SUBTILE):
            if m0 + MIX_SUBTILE < seq:
                up_proj(m0 + MIX_SUBTILE)
            for r0 in range(m0, m0 + MIX_SUBTILE, POOL_ROWS):
                u = ext_ref[rows_of(r0, POOL_ROWS), :]
                s, src, width = u, ext_ref, 1
                while width < w:
                    s = s + src[rows_of(r0, POOL_ROWS, width), :]
                    width *= 2
                    if width < w:
                        src = partial[width]
                        src[rows_of(r0, POOL_ROWS), :] = s
                if r0 < w:
                    pos = r0 + lax.broadcasted_iota(jnp.int32, (POOL_ROWS, 1), 0)
                    cnt = jnp.minimum(w, pos + 1).astype(F32)
                else:
                    cnt = float(w)
                xb_ref[r0:r0 + POOL_ROWS, :] = (s / cnt - u).astype(BF16)
            sub = slice(m0, m0 + MIX_SUBTILE)
            y_ref[sub, :] = (_dot(xb_ref[sub, :], wg) * ps).astype(BF16)
        npool_ref[0] = ext_ref[pl.ds(HALO + seq - POOL_BUF, POOL_BUF), :]

    for g, w in enumerate(POOL_WINDOWS):
        pl.when(j == g)(functools.partial(pool_branch, w))

    @pl.when(j >= POOL_GROUPS)
    def _():
        wc = wconv_ref[...]
        bc = bconv_ref[...]
        wrg = wrg_ref[0].astype(BF16)
        wig = wig_ref[0].astype(BF16)
        neg_c_sp = (-LRU_C) * _softplus(-lam_ref[...])
        brg = brg_ref[...]
        big = big_ref[...]
        nb = LRU_ROWS // SUBLANES
        row = lax.broadcasted_iota(jnp.int32, (nb, SUBLANES, CB), 1)
        h_carry = jnp.zeros((1, CB), F32)
        up_proj(0)
        for m0 in range(0, seq, MIX_SUBTILE):
            sub = slice(m0, m0 + MIX_SUBTILE)
            if m0 + MIX_SUBTILE < seq:
                up_proj(m0 + MIX_SUBTILE)
            for r0 in range(m0, m0 + MIX_SUBTILE, LRU_ROWS):
                xc = wc[LRU_CONV - 1:, :] * ext_ref[rows_of(r0, LRU_ROWS), :] + bc
                for k in range(LRU_CONV - 1):
                    xc = xc + wc[k:k + 1, :] * ext_ref[rows_of(r0, LRU_ROWS, LRU_CONV - 1 - k), :]
                a_ref[rows_of(r0, LRU_ROWS), :] = xc
                xb_ref[r0:r0 + LRU_ROWS, :] = xc.astype(BF16)
            r_ref[rows_of(m0, MIX_SUBTILE), :] = _dot(xb_ref[sub, :], wrg)
            i_ref[rows_of(m0, MIX_SUBTILE), :] = _dot(xb_ref[sub, :], wig)
            for r0 in range(m0, m0 + MIX_SUBTILE, LRU_ROWS):
                sl = rows_of(r0, LRU_ROWS)
                a, b = _lru_coeffs(a_ref[sl, :], r_ref[sl, :] + brg, i_ref[sl, :] + big, neg_c_sp)
                a = a.reshape(nb, SUBLANES, CB)
                b = b.reshape(nb, SUBLANES, CB)
                for k in (1, 2, 4):
                    a_sh = jnp.where(row >= k, pltpu.roll(a, k, 1), 1.0)
                    b_sh = jnp.where(row >= k, pltpu.roll(b, k, 1), 0.0)
                    b = b + a * b_sh
                    a = a * a_sh
                a_ref[sl, :] = a.reshape(LRU_ROWS, CB)
                u_ref[sl, :] = b.reshape(LRU_ROWS, CB)
            for r0 in range(m0, m0 + MIX_SUBTILE, SUBLANES):
                sl = rows_of(r0, SUBLANES)
                h8 = a_ref[sl, :] * h_carry + u_ref[sl, :]
                u_ref[sl, :] = h8
                h_carry = h8[SUBLANES - 1:SUBLANES, :]
            y_ref[sub, :] = u_ref[rows_of(m0, MIX_SUBTILE), :].astype(BF16)
        nh_ref[0] = h_carry
        nconv_ref[0] = ext_ref[pl.ds(HALO + seq - (LRU_CONV - 1), LRU_CONV - 1), :]


def _mix_sample_kernel(x_ref, sh_ref, sc_ref, g_ref, win_ref, wgrp_ref, pscale_ref,
                       wconv_ref, bconv_ref, wrg_ref, brg_ref, wig_ref, big_ref, lam_ref,
                       spool_ref, sconv_ref, sh0_ref,
                       y_ref, npool_ref, nconv_ref, nh_ref, h_ref,
                       u_ref, d_ref, *, n_seq, n_t, start):
    j = pl.program_id(1)

    @pl.when(j == 0)
    def _():
        _build_h_sample(x_ref, h_ref, g_ref[...], sc_ref, sh_ref, n_seq, n_t)

    u_ref[...] = _dot(h_ref[...], win_ref[...].astype(BF16))

    def u_slab(t):
        return u_ref[t * n_seq:(t + 1) * n_seq, :]

    def pool_branch(w):
        e = [spool_ref[k * n_seq:(k + 1) * n_seq, :] for k in range(POOL_BUF)]
        e += [u_slab(t) for t in range(n_t)]
        for k in range(POOL_BUF):
            npool_ref[k * n_seq:(k + 1) * n_seq, :] = e[n_t + k]
        for t in range(n_t):
            s = e[POOL_BUF + t]
            for k in range(1, w):
                s = s + e[POOL_BUF + t - k]
            cnt = float(min(w, start + t + 1))
            d_ref[t * n_seq:(t + 1) * n_seq, :] = (s / cnt - e[POOL_BUF + t]).astype(BF16)
        y = _dot(d_ref[...], wgrp_ref[0].astype(BF16)) * pscale_ref[...]
        y_ref[...] = y.astype(BF16)

    for g, w in enumerate(POOL_WINDOWS):
        pl.when(j == g)(functools.partial(pool_branch, w))

    @pl.when(j >= POOL_GROUPS)
    def _():
        nk = LRU_CONV - 1
        e = [sconv_ref[k * n_seq:(k + 1) * n_seq, :] for k in range(nk)]
        e += [u_slab(t) for t in range(n_t)]
        for k in range(nk):
            nconv_ref[k * n_seq:(k + 1) * n_seq, :] = e[n_t + k]
        wc = wconv_ref[...]
        bc = bconv_ref[...]
        wrg = wrg_ref[0].astype(BF16)
        wig = wig_ref[0].astype(BF16)
        neg_c_sp = (-LRU_C) * _softplus(-lam_ref[...])
        h = sh0_ref[...]
        for t in range(n_t):
            xc = bc
            for k in range(LRU_CONV):
                xc = xc + wc[k:k + 1, :] * e[t + k]
            xb = xc.astype(BF16)
            a, b = _lru_coeffs(xc, _dot(xb, wrg) + brg_ref[...], _dot(xb, wig) + big_ref[...],
                               neg_c_sp)
            h = a * h + b
            y_ref[t * n_seq:(t + 1) * n_seq, :] = h.astype(BF16)
        nh_ref[...] = h


def _mix_weight_specs():
    pj = lambda j: jnp.minimum(j, POOL_GROUPS - 1)
    lj = lambda j: jnp.maximum(j - POOL_GROUPS, 0)
    return [
        pl.BlockSpec((1, D_MODEL), lambda q, j: (0, 0)),
        pl.BlockSpec((D_MODEL, CB), lambda q, j: (0, j)),
        pl.BlockSpec((1, CB, CB), lambda q, j: (pj(j), 0, 0)),
        pl.BlockSpec((1, CB), lambda q, j: (0, pj(j))),
        pl.BlockSpec((LRU_CONV, CB), lambda q, j: (0, lj(j))),
        pl.BlockSpec((1, CB), lambda q, j: (0, lj(j))),
        pl.BlockSpec((1, CB, CB), lambda q, j: (lj(j), 0, 0)),
        pl.BlockSpec((1, CB), lambda q, j: (0, lj(j))),
        pl.BlockSpec((1, CB, CB), lambda q, j: (lj(j), 0, 0)),
        pl.BlockSpec((1, CB), lambda q, j: (0, lj(j))),
        pl.BlockSpec((1, CB), lambda q, j: (0, lj(j))),
    ], pj, lj


def _mix_prompt(x2, ada, p, n_seq, seq, ada_row_block):
    wspecs, pj, lj = _mix_weight_specs()
    in_specs = [
        pl.BlockSpec((seq, D_MODEL), lambda q, j: (q, 0), pipeline_mode=pl.Buffered(1)),
        pl.BlockSpec((SUBLANES, D_MODEL), lambda q, j: (ada_row_block, 0)),
        pl.BlockSpec((SUBLANES, D_MODEL), lambda q, j: (ada_row_block, 1)),
    ] + wspecs
    d_ff = p["w_ffn_down"].shape[0]
    dn_rows = d_ff // (n_seq * N_MIX_BLOCKS)
    assert dn_rows * n_seq * N_MIX_BLOCKS == d_ff and dn_rows % BF16_ROWS == 0
    slab_spec = pl.BlockSpec((dn_rows, D_MODEL), lambda q, j: (q * N_MIX_BLOCKS + j, 0))
    in_specs.append(slab_spec)
    out_specs = [
        pl.BlockSpec((seq, CB), lambda q, j: (q, j)),
        pl.BlockSpec((1, POOL_BUF, CB), lambda q, j: (q, 0, pj(j))),
        pl.BlockSpec((1, LRU_CONV - 1, CB), lambda q, j: (q, 0, lj(j))),
        pl.BlockSpec((1, 1, CB), lambda q, j: (q, 0, lj(j))),
        pl.BlockSpec((seq, D_MODEL), lambda q, j: (q, 0)),
        slab_spec,
    ]
    out_shape = [
        jax.ShapeDtypeStruct((n_seq * seq, N_MIX_BLOCKS * CB), BF16),
        jax.ShapeDtypeStruct((n_seq, POOL_BUF, POOL_WIDTH), F32),
        jax.ShapeDtypeStruct((n_seq, LRU_CONV - 1, LRU_WIDTH), F32),
        jax.ShapeDtypeStruct((n_seq, 1, LRU_WIDTH), F32),
        jax.ShapeDtypeStruct((n_seq * seq, D_MODEL), BF16),
        jax.ShapeDtypeStruct((d_ff, D_MODEL), BF16),
    ]
    scratch = [
        pltpu.VMEM((HALO + seq, CB), F32),
        pltpu.VMEM((seq, CB), BF16),
        pltpu.VMEM((HALO + seq, CB), F32),
        pltpu.VMEM((HALO + seq, CB), F32),
        pltpu.VMEM((HALO + seq, CB), F32),
        pltpu.VMEM((HALO + seq, CB), F32),
    ]
    return pl.pallas_call(
        functools.partial(_mix_prompt_kernel, seq=seq),
        grid=(n_seq, N_MIX_BLOCKS),
        in_specs=in_specs, out_specs=out_specs, out_shape=out_shape,
        scratch_shapes=scratch, compiler_params=_cparams(2), name="mix_prompt",
    )(x2, ada, ada, p["g_pre1"], p["w_in"], p["w_pool_grp"], p["pool_scale"],
      p["w_lru_conv"], p["b_lru_conv"], p["w_rg"], p["b_rg"], p["w_ig"], p["b_ig"],
      p["lru_lambda"], p["w_ffn_down"])


def _mix_sample(x2, ada, p, spool2, sconv2, sh0, n_seq, n_t, start):
    wspecs, pj, lj = _mix_weight_specs()
    rows = n_seq * n_t
    in_specs = [
        pl.BlockSpec((rows, D_MODEL), lambda q, j: (0, 0), pipeline_mode=pl.Buffered(1)),
        pl.BlockSpec((n_seq, D_MODEL), lambda q, j: (0, 0)),
        pl.BlockSpec((n_seq, D_MODEL), lambda q, j: (0, 1)),
    ] + wspecs + [
        pl.BlockSpec((n_seq * POOL_BUF, CB), lambda q, j: (0, pj(j))),
        pl.BlockSpec((n_seq * (LRU_CONV - 1), CB), lambda q, j: (0, lj(j))),
        pl.BlockSpec((n_seq, CB), lambda q, j: (0, lj(j))),
    ]
    out_specs = [
        pl.BlockSpec((rows, CB), lambda q, j: (0, j)),
        pl.BlockSpec((n_seq * POOL_BUF, CB), lambda q, j: (0, pj(j))),
        pl.BlockSpec((n_seq * (LRU_CONV - 1), CB), lambda q, j: (0, lj(j))),
        pl.BlockSpec((n_seq, CB), lambda q, j: (0, lj(j))),
        pl.BlockSpec((rows, D_MODEL), lambda q, j: (0, 0)),
    ]
    out_shape = [
        jax.ShapeDtypeStruct((rows, N_MIX_BLOCKS * CB), BF16),
        jax.ShapeDtypeStruct((n_seq * POOL_BUF, POOL_WIDTH), F32),
        jax.ShapeDtypeStruct((n_seq * (LRU_CONV - 1), LRU_WIDTH), F32),
        jax.ShapeDtypeStruct((n_seq, LRU_WIDTH), F32),
        jax.ShapeDtypeStruct((rows, D_MODEL), BF16),
    ]
    scratch = [
        pltpu.VMEM((rows, CB), F32),
        pltpu.VMEM((rows, CB), BF16),
    ]
    return pl.pallas_call(
        functools.partial(_mix_sample_kernel, n_seq=n_seq, n_t=n_t, start=start),
        grid=(1, N_MIX_BLOCKS),
        in_specs=in_specs, out_specs=out_specs, out_shape=out_shape,
        scratch_shapes=scratch, compiler_params=_cparams(2), name="mix_sample",
    )(x2, ada, ada, p["g_pre1"], p["w_in"], p["w_pool_grp"], p["pool_scale"],
      p["w_lru_conv"], p["b_lru_conv"], p["w_rg"], p["b_rg"], p["w_ig"], p["b_ig"],
      p["lru_lambda"], spool2, sconv2, sh0)


MERGE_ROWS = 64
MERGE_SUBTILE = 256
CAST_ROWS = 32
BF16_ROWS = 16


def _cast_ffn_up_slab(wup_ref, wupb_ref):
    n_blocks, _, cb2 = wupb_ref.shape
    cb = cb2 // 2
    for blk in range(n_blocks):
        wupb_ref[blk, :, 0:cb] = wup_ref[:, blk * cb:(blk + 1) * cb].astype(BF16)
        wupb_ref[blk, :, cb:cb2] = wup_ref[:, (n_blocks + blk) * cb:(n_blocks + blk + 1) * cb].astype(BF16)


def _cast_slab(w_ref, wb_ref):
    for r0 in range(0, w_ref.shape[0], CAST_ROWS):
        wb_ref[r0:r0 + CAST_ROWS, :] = w_ref[r0:r0 + CAST_ROWS, :].astype(BF16)


def _merge_kernel(h_ref, y_ref, wgp_ref, wgl_ref, wpu_ref, wlu_ref, *rest, tm, cast_ffn):
    if cast_ffn:
        wup_ref, m_ref, wupb_ref, wg_ref, g_ref, pu_ref, lu_ref = rest
    else:
        m_ref, wg_ref, g_ref, pu_ref, lu_ref = rest
    wg_ref[:, 0:CB] = wgp_ref[...].astype(BF16)
    wg_ref[:, CB:2 * CB] = wgl_ref[...].astype(BF16)
    wpu = wpu_ref[...].astype(BF16)
    wlu = wlu_ref[...].astype(BF16)

    def slot(m0):
        s0 = (m0 // MERGE_SUBTILE) % 2 * MERGE_SUBTILE
        return slice(s0, s0 + MERGE_SUBTILE)

    def proj(m0):
        rows = slice(m0, m0 + MERGE_SUBTILE)
        g_ref[slot(m0), :] = _dot(h_ref[rows, :], wg_ref[...])
        pu_ref[slot(m0), :] = _dot(y_ref[rows, 0:POOL_WIDTH], wpu)
        lu_ref[slot(m0), :] = _dot(y_ref[rows, POOL_WIDTH:POOL_WIDTH + LRU_WIDTH], wlu)

    proj(0)
    for m0 in range(0, tm, MERGE_SUBTILE):
        if m0 + MERGE_SUBTILE < tm:
            proj(m0 + MERGE_SUBTILE)
        s0 = slot(m0).start
        for r0 in range(0, MERGE_SUBTILE, MERGE_ROWS):
            sl = slice(s0 + r0, s0 + r0 + MERGE_ROWS)
            m = (jax.nn.sigmoid(g_ref[sl, 0:CB]) * pu_ref[sl, :]
                 + jax.nn.sigmoid(g_ref[sl, CB:2 * CB]) * lu_ref[sl, :])
            m_ref[m0 + r0:m0 + r0 + MERGE_ROWS, :] = m.astype(BF16)
    if cast_ffn:
        _cast_ffn_up_slab(wup_ref, wupb_ref)


def _merge(h, y, p, tm, name, cast_ffn_cb=0):
    rows = h.shape[0]
    n_col = D_MODEL // CB
    n_gate0 = N_MIX_BLOCKS
    n_gate1 = N_MIX_BLOCKS + n_col
    in_specs = [
        pl.BlockSpec((tm, D_MODEL), lambda i, c: (i, 0), pipeline_mode=pl.Buffered(1)),
        pl.BlockSpec((tm, N_MIX_BLOCKS * CB), lambda i, c: (i, 0), pipeline_mode=pl.Buffered(1)),
        pl.BlockSpec((D_MODEL, CB), lambda i, c: (0, n_gate0 + c)),
        pl.BlockSpec((D_MODEL, CB), lambda i, c: (0, n_gate1 + c)),
        pl.BlockSpec((POOL_WIDTH, CB), lambda i, c: (0, c)),
        pl.BlockSpec((LRU_WIDTH, CB), lambda i, c: (0, c)),
    ]
    out_specs = [pl.BlockSpec((tm, CB), lambda i, c: (i, c))]
    out_shape = [jax.ShapeDtypeStruct((rows, D_MODEL), BF16)]
    args = [h, y, p["w_in"], p["w_in"], p["w_pool_up"], p["w_lru_up"]]
    if cast_ffn_cb:
        n_steps = (rows // tm) * n_col
        d_ff = p["w_ffn_down"].shape[0]
        n_blocks = d_ff // cast_ffn_cb
        up_rows = D_MODEL // n_steps
        assert up_rows * n_steps == D_MODEL and up_rows % BF16_ROWS == 0
        step = lambda i, c: i * n_col + c
        in_specs += [pl.BlockSpec((up_rows, 2 * d_ff), lambda i, c: (step(i, c), 0))]
        out_specs += [pl.BlockSpec((n_blocks, up_rows, 2 * cast_ffn_cb),
                                   lambda i, c: (0, step(i, c), 0))]
        out_shape += [jax.ShapeDtypeStruct((n_blocks, D_MODEL, 2 * cast_ffn_cb), BF16)]
        args += [p["w_ffn_up"]]
    ring = 2 * MERGE_SUBTILE
    return pl.pallas_call(
        functools.partial(_merge_kernel, tm=tm, cast_ffn=bool(cast_ffn_cb)),
        grid=(rows // tm, n_col),
        in_specs=in_specs, out_specs=out_specs, out_shape=out_shape,
        scratch_shapes=[pltpu.VMEM((D_MODEL, 2 * CB), BF16), pltpu.VMEM((ring, 2 * CB), F32),
                        pltpu.VMEM((ring, CB), F32), pltpu.VMEM((ring, CB), F32)],
        compiler_params=_cparams(2), name=name,
    )(*args)


def _cast_weight(w_ref, wb_ref):
    rows = w_ref.shape[0]
    for r0 in range(0, rows, 256):
        wb_ref[r0:r0 + 256, :] = w_ref[r0:r0 + 256, :].astype(BF16)


def _outproj_kernel(m_ref, x_ref, gate_ref, g_ref, w_ref, o_ref, wb_ref, acc_ref,
                    *, tm, tiles_per_seq, sample_t):
    i = pl.program_id(0)

    @pl.when(i == 0)
    def _():
        _cast_weight(w_ref, wb_ref)

    g = g_ref[...]
    if sample_t:
        n_seq = tm // sample_t
    else:
        gg_tile = g * gate_ref[pl.ds(i // tiles_per_seq, 1), :]

    def proj(m0):
        acc_ref[m0:m0 + OUT_SUBTILE, :] = _dot(m_ref[m0:m0 + OUT_SUBTILE, :], wb_ref[...])

    proj(0)
    for m0 in range(0, tm, OUT_SUBTILE):
        if m0 + OUT_SUBTILE < tm:
            proj(m0 + OUT_SUBTILE)
        for r0 in range(m0, m0 + OUT_SUBTILE, NORM_ROWS):
            sl = slice(r0, r0 + NORM_ROWS)
            if sample_t:
                s0 = r0 % n_seq
                gg = g * gate_ref[s0:s0 + NORM_ROWS, :]
            else:
                gg = gg_tile
            o_ref[sl, :] = x_ref[sl, :] + _unit_rms(acc_ref[sl, :]) * gg


def _outproj(m, x2, ada, p, tm, tiles_per_seq, ada_row_block, sample_t):
    rows = x2.shape[0]
    if sample_t:
        gate_spec = pl.BlockSpec((tm // sample_t, D_MODEL), lambda i: (0, 2))
    else:
        gate_spec = pl.BlockSpec((SUBLANES, D_MODEL), lambda i: (ada_row_block, 2))
    return pl.pallas_call(
        functools.partial(_outproj_kernel, tm=tm, tiles_per_seq=tiles_per_seq, sample_t=sample_t),
        grid=(rows // tm,),
        in_specs=[pl.BlockSpec((tm, D_MODEL), lambda i: (i, 0)),
                  pl.BlockSpec((tm, D_MODEL), lambda i: (i, 0)),
                  gate_spec,
                  pl.BlockSpec((1, D_MODEL), lambda i: (0, 0)),
                  pl.BlockSpec((D_MODEL, D_MODEL), lambda i: (0, 0), pipeline_mode=pl.Buffered(1))],
        out_specs=pl.BlockSpec((tm, D_MODEL), lambda i: (i, 0)),
        out_shape=jax.ShapeDtypeStruct((rows, D_MODEL), F32),
        scratch_shapes=[pltpu.VMEM((D_MODEL, D_MODEL), BF16), pltpu.VMEM((tm, D_MODEL), F32)],
        compiler_params=_cparams(1),
        name="outproj_sample" if sample_t else "outproj_prompt",
    )(m, x2, ada, p["g_post1"], p["w_out"])


GELU_C0 = 0.7978845608028654
GELU_C1 = GELU_C0 * 0.044715
FFN_ROWS = 32
FFN_SUBTILE = 256
FFN_AHEAD = 2
FFN_CB = 768


def _ffn_conv_gate(eg, ev, wcg, wcv, bcg, bcv):
    last = FFN_CONV - 1
    g = wcg[last:, :] * eg[last] + bcg
    v = wcv[last:, :] * ev[last] + bcv
    for k in range(last):
        g = g + wcg[k:k + 1, :] * eg[k]
        v = v + wcv[k:k + 1, :] * ev[k]
    t = jnp.tanh(g * (GELU_C0 + GELU_C1 * (g * g)))
    return ((g * v) * (0.5 + 0.5 * t)).astype(BF16)


def _ffn_prompt_kernel(x_ref, sh_ref, sc_ref, gate_ref, gpre_ref, gpost_ref,
                       wup_ref, wcg_ref, wcv_ref, bcg_ref, bcv_ref, wdn_ref,
                       o_ref, nst_ref,
                       h_ref, ext_ref, f_ref, carry_ref, *, tm, tiles_per_seq, n_blocks, cb):
    i = pl.program_id(0)
    c = pl.program_id(1)
    q = i // tiles_per_seq
    first = (i % tiles_per_seq) == 0
    nk = FFN_CONV - 1

    @pl.when(c == 0)
    def _():
        _build_h_prompt(x_ref, h_ref, gpre_ref[...], 1.0 + sc_ref[pl.ds(q, 1), :],
                        sh_ref[pl.ds(q, 1), :], tm)
        o_ref[...] = jnp.zeros((tm, D_MODEL), F32)

    @pl.when(first)
    def _():
        ext_ref[0:SUBLANES, :] = jnp.zeros((SUBLANES, 2 * cb), F32)

    @pl.when(jnp.logical_not(first))
    def _():
        ext_ref[0:SUBLANES, :] = carry_ref[c]

    wcg = wcg_ref[...]
    wcv = wcv_ref[...]
    bcg = bcg_ref[...]
    bcv = bcv_ref[...]

    def up_proj(m0):
        ext_ref[SUBLANES + m0:SUBLANES + m0 + FFN_SUBTILE, :] = _dot(
            h_ref[m0:m0 + FFN_SUBTILE, :], wup_ref[0])

    for m0 in range(0, min(FFN_AHEAD * FFN_SUBTILE, tm), FFN_SUBTILE):
        up_proj(m0)
    for m0 in range(0, tm, FFN_SUBTILE):
        rows = slice(m0, m0 + FFN_SUBTILE)
        for r0 in range(m0, m0 + FFN_SUBTILE, FFN_ROWS):
            shifted = [slice(SUBLANES + r0 - nk + k, SUBLANES + r0 - nk + k + FFN_ROWS)
                       for k in range(FFN_CONV)]
            eg = [ext_ref[s, 0:cb] for s in shifted]
            ev = [ext_ref[s, cb:2 * cb] for s in shifted]
            f_ref[r0:r0 + FFN_ROWS, :] = _ffn_conv_gate(eg, ev, wcg, wcv, bcg, bcv)
        if m0 + FFN_AHEAD * FFN_SUBTILE < tm:
            up_proj(m0 + FFN_AHEAD * FFN_SUBTILE)
        o_ref[rows, :] += _dot(f_ref[rows, :], wdn_ref[...])

    carry_ref[c] = ext_ref[tm:tm + SUBLANES, :]
    nst_ref[0, :, 0, :] = ext_ref[pl.ds(SUBLANES + tm - nk, nk), 0:cb]
    nst_ref[0, :, 1, :] = ext_ref[pl.ds(SUBLANES + tm - nk, nk), cb:2 * cb]

    @pl.when(c == n_blocks - 1)
    def _():
        _residual_norm_prompt(x_ref, o_ref, o_ref, gpost_ref[...], gate_ref[pl.ds(q, 1), :], tm)


def _ffn_sample_kernel(x_ref, sh_ref, sc_ref, gate_ref, gpre_ref, gpost_ref,
                       wup_ref, wcg_ref, wcv_ref, bcg_ref, bcv_ref, wdn_ref,
                       sg_ref, sv_ref,
                       o_ref, ng_ref, nv_ref,
                       h_ref, ext_ref, f_ref, *, n_seq, n_t, n_blocks, cb):
    c = pl.program_id(1)
    nk = FFN_CONV - 1

    @pl.when(c == 0)
    def _():
        _build_h_sample(x_ref, h_ref, gpre_ref[...], sc_ref, sh_ref, n_seq, n_t)
        o_ref[...] = jnp.zeros((n_seq * n_t, D_MODEL), F32)

    wcg = wcg_ref[...]
    wcv = wcv_ref[...]
    bcg = bcg_ref[...]
    bcv = bcv_ref[...]

    def view(state_ref, half, t, s0, n):
        if t < 0:
            return state_ref[(nk + t) * n_seq + s0:(nk + t) * n_seq + s0 + n, :]
        return ext_ref[t * n_seq + s0:t * n_seq + s0 + n, half * cb:(half + 1) * cb]

    t_sub = max(FFN_SUBTILE // n_seq, 1)

    def up_proj(t0):
        rows = slice(t0 * n_seq, (t0 + t_sub) * n_seq)
        ext_ref[rows, :] = _dot(h_ref[rows, :], wup_ref[0])

    for t0 in range(0, min(FFN_AHEAD * t_sub, n_t), t_sub):
        up_proj(t0)
    for t0 in range(0, n_t, t_sub):
        rows = slice(t0 * n_seq, (t0 + t_sub) * n_seq)
        for t in range(t0, t0 + t_sub):
            for s0 in range(0, n_seq, FFN_ROWS):
                eg = [view(sg_ref, 0, t - nk + k, s0, FFN_ROWS) for k in range(FFN_CONV)]
                ev = [view(sv_ref, 1, t - nk + k, s0, FFN_ROWS) for k in range(FFN_CONV)]
                f_ref[t * n_seq + s0:t * n_seq + s0 + FFN_ROWS, :] = _ffn_conv_gate(
                    eg, ev, wcg, wcv, bcg, bcv)
        if t0 + FFN_AHEAD * t_sub < n_t:
            up_proj(t0 + FFN_AHEAD * t_sub)
        o_ref[rows, :] += _dot(f_ref[rows, :], wdn_ref[...])

    for k in range(nk):
        ng_ref[k * n_seq:(k + 1) * n_seq, :] = view(sg_ref, 0, n_t - nk + k, 0, n_seq)
        nv_ref[k * n_seq:(k + 1) * n_seq, :] = view(sv_ref, 1, n_t - nk + k, 0, n_seq)

    @pl.when(c == n_blocks - 1)
    def _():
        g = gpost_ref[...]
        for s0 in range(0, n_seq, NORM_ROWS):
            gg = g * gate_ref[s0:s0 + NORM_ROWS, :]
            for t in range(n_t):
                sl = slice(t * n_seq + s0, t * n_seq + s0 + NORM_ROWS)
                o_ref[sl, :] = x_ref[sl, :] + _unit_rms(o_ref[sl, :]) * gg


def _ffn_weight_specs(n_blocks, cb):
    return [
        pl.BlockSpec((1, D_MODEL), lambda i, c: (0, 0)),
        pl.BlockSpec((1, D_MODEL), lambda i, c: (0, 0)),
        pl.BlockSpec((1, D_MODEL, 2 * cb), lambda i, c: (c, 0, 0)),
        pl.BlockSpec((FFN_CONV, cb), lambda i, c: (0, c)),
        pl.BlockSpec((FFN_CONV, cb), lambda i, c: (0, n_blocks + c)),
        pl.BlockSpec((1, cb), lambda i, c: (0, c)),
        pl.BlockSpec((1, cb), lambda i, c: (0, n_blocks + c)),
        pl.BlockSpec((cb, D_MODEL), lambda i, c: (c, 0)),
    ]


def _ffn_prompt(x1, ada, p, wup_b, wdn_b, n_seq, tm, tiles_per_seq, ada_row_block):
    rows = x1.shape[0]
    n_blocks, _, cb2 = wup_b.shape
    cb = cb2 // 2
    ada_spec = lambda k: pl.BlockSpec((SUBLANES, D_MODEL), lambda i, c: (ada_row_block, k))
    in_specs = [pl.BlockSpec((tm, D_MODEL), lambda i, c: (i, 0), pipeline_mode=pl.Buffered(1)),
                ada_spec(3), ada_spec(4), ada_spec(5)] + _ffn_weight_specs(n_blocks, cb)
    out_specs = [
        pl.BlockSpec((tm, D_MODEL), lambda i, c: (i, 0)),
        pl.BlockSpec((1, FFN_CONV - 1, 2, cb), lambda i, c: (i, 0, 0, c)),
    ]
    out_shape = [jax.ShapeDtypeStruct((rows, D_MODEL), F32),
                 jax.ShapeDtypeStruct((rows // tm, FFN_CONV - 1, 2, n_blocks * cb), F32)]
    scratch = [
        pltpu.VMEM((tm, D_MODEL), BF16),
        pltpu.VMEM((SUBLANES + tm, 2 * cb), F32),
        pltpu.VMEM((tm, cb), BF16),
        pltpu.VMEM((n_blocks, SUBLANES, 2 * cb), F32),
    ]
    return pl.pallas_call(
        functools.partial(_ffn_prompt_kernel, tm=tm, tiles_per_seq=tiles_per_seq,
                          n_blocks=n_blocks, cb=cb),
        grid=(rows // tm, n_blocks),
        in_specs=in_specs, out_specs=out_specs, out_shape=out_shape,
        scratch_shapes=scratch, compiler_params=_cparams(2), name="ffn_prompt",
    )(x1, ada, ada, ada, p["g_pre2"], p["g_post2"], wup_b, p["w_ffn_conv"], p["w_ffn_conv"],
      p["b_ffn_conv"], p["b_ffn_conv"], wdn_b)


def _ffn_sample(x1, ada, p, wup_b, wdn_b, sffn2, n_seq, n_t):
    rows = n_seq * n_t
    n_blocks, _, cb2 = wup_b.shape
    cb = cb2 // 2
    nk = FFN_CONV - 1
    ada_spec = lambda k: pl.BlockSpec((n_seq, D_MODEL), lambda i, c: (0, k))
    in_specs = [pl.BlockSpec((rows, D_MODEL), lambda i, c: (0, 0), pipeline_mode=pl.Buffered(1)),
                ada_spec(3), ada_spec(4), ada_spec(5)] + _ffn_weight_specs(n_blocks, cb) + [
        pl.BlockSpec((n_seq * nk, cb), lambda i, c: (0, c)),
        pl.BlockSpec((n_seq * nk, cb), lambda i, c: (0, n_blocks + c)),
    ]
    out_specs = [
        pl.BlockSpec((rows, D_MODEL), lambda i, c: (0, 0)),
        pl.BlockSpec((n_seq * nk, cb), lambda i, c: (0, c)),
        pl.BlockSpec((n_seq * nk, cb), lambda i, c: (0, c)),
    ]
    out_shape = [jax.ShapeDtypeStruct((rows, D_MODEL), F32),
                 jax.ShapeDtypeStruct((n_seq * nk, n_blocks * cb), F32),
                 jax.ShapeDtypeStruct((n_seq * nk, n_blocks * cb), F32)]
    scratch = [
        pltpu.VMEM((rows, D_MODEL), BF16),
        pltpu.VMEM((rows, 2 * cb), F32),
        pltpu.VMEM((rows, cb), BF16),
    ]
    return pl.pallas_call(
        functools.partial(_ffn_sample_kernel, n_seq=n_seq, n_t=n_t, n_blocks=n_blocks, cb=cb),
        grid=(1, n_blocks),
        in_specs=in_specs, out_specs=out_specs, out_shape=out_shape,
        scratch_shapes=scratch, compiler_params=_cparams(2), name="ffn_sample",
    )(x1, ada, ada, ada, p["g_pre2"], p["g_post2"], wup_b, p["w_ffn_conv"], p["w_ffn_conv"],
      p["b_ffn_conv"], p["b_ffn_conv"], wdn_b, sffn2, sffn2)


TOKEN_TILE = 1024
MERGE_TILE = 2048
OUTPROJ_TILE = 512


def kernel(x_prompt, x_sample, c_prompt, c_sample, state_pool, state_lru_conv, state_lru_h, state_ffn_conv, w_ada, b_ada, g_pre1, g_post1, g_pre2, g_post2, w_in, w_pool_grp, pool_scale, w_lru_conv, b_lru_conv, w_rg, b_rg, w_ig, b_ig, lru_lambda, w_pool_up, w_lru_up, w_out, w_ffn_up, w_ffn_conv, b_ffn_conv, w_ffn_down):
    batch, seq, d = x_prompt.shape
    dec_batch, dec_seq, _ = x_sample.shape
    depth = w_ada.shape[0]
    assert d == D_MODEL and dec_batch % SUBLANES == 0 and seq % TOKEN_TILE == 0
    assert w_in.shape[2] == N_MIX_BLOCKS * CB + 2 * D_MODEL

    pad = (-batch) % SUBLANES
    c_all = jnp.concatenate([c_sample, c_prompt, jnp.zeros((pad, d), c_prompt.dtype)], axis=0)
    prompt_row_block = dec_batch // SUBLANES

    vec_names = ("g_pre1", "g_post1", "g_pre2", "g_post2", "pool_scale", "b_lru_conv", "b_rg",
                 "b_ig", "lru_lambda", "b_ffn_conv")
    weights = dict(w_ada=w_ada, b_ada=b_ada, g_pre1=g_pre1, g_post1=g_post1, g_pre2=g_pre2,
                   g_post2=g_post2, w_in=w_in, w_pool_grp=w_pool_grp, pool_scale=pool_scale,
                   w_lru_conv=w_lru_conv, b_lru_conv=b_lru_conv, w_rg=w_rg, b_rg=b_rg, w_ig=w_ig,
                   b_ig=b_ig, lru_lambda=lru_lambda, w_pool_up=w_pool_up, w_lru_up=w_lru_up,
                   w_out=w_out, w_ffn_up=w_ffn_up, w_ffn_conv=w_ffn_conv, b_ffn_conv=b_ffn_conv,
                   w_ffn_down=w_ffn_down)

    def time_major(a):
        return jnp.swapaxes(a, 0, 1).reshape(-1, a.shape[-1])

    def seq_major(a2, n_rows):
        return jnp.swapaxes(a2.reshape(n_rows, dec_batch, -1), 0, 1)

    xp = x_prompt.reshape(batch * seq, d)
    xs = time_major(x_sample)
    tps = seq // TOKEN_TILE
    outs_p = ([], [], [], [])
    outs_s = ([], [], [], [])
    for l in range(depth):
        p = {k: v[l] for k, v in weights.items()}
        for k in vec_names + ("b_ada",):
            p[k] = p[k].reshape(1, -1)
        ada = _ada(c_all, p["w_ada"], p["b_ada"])

        y, npool, nconv, nh, h, wdn_b = _mix_prompt(xp, ada, p, batch, seq, prompt_row_block)
        m, wup_b = _merge(h, y, p, MERGE_TILE, "merge_prompt", cast_ffn_cb=FFN_CB)
        x1 = _outproj(m, xp, ada, p, OUTPROJ_TILE, seq // OUTPROJ_TILE, prompt_row_block, 0)
        xp, nffn = _ffn_prompt(x1, ada, p, wup_b, wdn_b, batch, TOKEN_TILE, tps, prompt_row_block)
        outs_p[0].append(npool)
        outs_p[1].append(nconv)
        outs_p[2].append(nh.reshape(batch, LRU_WIDTH))
        outs_p[3].append(nffn[tps - 1::tps].reshape(batch, FFN_CONV - 1, -1))

        rows_s = dec_batch * dec_seq
        y, npool, nconv, nh, h = _mix_sample(
            xs, ada, p, time_major(state_pool[l]), time_major(state_lru_conv[l]),
            state_lru_h[l], dec_batch, dec_seq, PAST_LEN)
        m, = _merge(h, y, p, rows_s, "merge_sample")
        x1 = _outproj(m, xs, ada, p, rows_s, 1, 0, dec_seq)
        xs, ng, nv = _ffn_sample(x1, ada, p, wup_b, wdn_b, time_major(state_ffn_conv[l]),
                                 dec_batch, dec_seq)
        outs_s[0].append(seq_major(npool, POOL_BUF))
        outs_s[1].append(seq_major(nconv, LRU_CONV - 1))
        outs_s[2].append(nh)
        outs_s[3].append(jnp.concatenate([seq_major(ng, FFN_CONV - 1),
                                          seq_major(nv, FFN_CONV - 1)], axis=-1))

    return (xp.reshape(batch, seq, d), seq_major(xs, dec_seq),
            jnp.stack(outs_p[0]), jnp.stack(outs_p[1]), jnp.stack(outs_p[2]), jnp.stack(outs_p[3]),
            jnp.stack(outs_s[0]), jnp.stack(outs_s[1]), jnp.stack(outs_s[2]), jnp.stack(outs_s[3]))
```

```python
import functools

import jax
import jax.numpy as jnp
from jax import lax
from jax.experimental import pallas as pl
from jax.experimental.pallas import tpu as pltpu

F32 = jnp.float32
BF16 = jnp.bfloat16

D_MODEL = 2048
POOL_WINDOWS = (2, 4, 8, 16)
POOL_GROUPS = len(POOL_WINDOWS)
POOL_BUF = max(POOL_WINDOWS) - 1
LRU_CONV = 4
LRU_C = 8.0
PAST_LEN = 16384
FFN_CONV = 3
N_ADA = 6
EPS = 1e-6

CB = 256
POOL_WIDTH = POOL_GROUPS * CB
LRU_BLOCKS = 8
LRU_WIDTH = LRU_BLOCKS * CB
N_MIX_BLOCKS = POOL_GROUPS + LRU_BLOCKS
HALO = 16
SUBLANES = 8
VMEM_LIMIT = 60 * 1024 * 1024


def _cparams(n_axes):
    return pltpu.CompilerParams(
        dimension_semantics=("arbitrary",) * n_axes, vmem_limit_bytes=VMEM_LIMIT)


def _dot(a, b):
    return jnp.dot(a, b, preferred_element_type=F32)


def _softplus(z):
    return jnp.maximum(z, 0.0) + jnp.log1p(jnp.exp(-jnp.abs(z)))


def _lru_coeffs(xc, r_pre, i_pre, neg_c_sp):
    r = jax.nn.sigmoid(r_pre)
    i = jax.nn.sigmoid(i_pre)
    log_a = r * neg_c_sp
    a = jnp.exp(log_a)
    m2 = -jnp.tanh(log_a) * (a * a + 1.0)
    root = jnp.where(m2 > 0.0, m2 * lax.rsqrt(m2), 0.0)
    return a, root * (i * xc)


ADA_K = 256
ADA_N = 1024


def _ada_kernel(c_ref, w_ref, b_ref, o_ref):
    rows, n_out = o_ref.shape

    @pl.when(pl.program_id(0) == 0)
    def _():
        o_ref[...] = jnp.broadcast_to(b_ref[...], (rows, n_out))

    c = c_ref[...]
    s = (c * jax.nn.sigmoid(c)).astype(BF16)
    for n0 in range(0, n_out, ADA_N):
        o_ref[:, n0:n0 + ADA_N] += _dot(s, w_ref[:, n0:n0 + ADA_N].astype(BF16))


def _ada(c_all, w_ada, b_ada):
    rows = c_all.shape[0]
    n_out = N_ADA * D_MODEL
    return pl.pallas_call(
        _ada_kernel,
        grid=(D_MODEL // ADA_K,),
        in_specs=[pl.BlockSpec((rows, ADA_K), lambda k: (0, k)),
                  pl.BlockSpec((ADA_K, n_out), lambda k: (k, 0)),
                  pl.BlockSpec((1, n_out), lambda k: (0, 0))],
        out_specs=pl.BlockSpec((rows, n_out), lambda k: (0, 0)),
        out_shape=jax.ShapeDtypeStruct((rows, n_out), F32),
        compiler_params=_cparams(1),
        name="ada",
    )(c_all, w_ada, b_ada)


NORM_ROWS = 32
NORM_UNROLL = 8
OUT_SUBTILE = 256


def _unit_rms(x):
    return x * lax.rsqrt(jnp.mean(x * x, axis=-1, keepdims=True) + EPS)


def _build_h_prompt(x_ref, h_ref, g, scale1p, shift, n_rows):
    gs = g * scale1p

    def body(i, carry):
        r0 = pl.multiple_of(i * NORM_ROWS, NORM_ROWS)
        x = x_ref[pl.ds(r0, NORM_ROWS), :]
        h_ref[pl.ds(r0, NORM_ROWS), :] = (_unit_rms(x) * gs + shift).astype(BF16)
        return carry
    lax.fori_loop(0, n_rows // NORM_ROWS, body, 0, unroll=NORM_UNROLL)


def _residual_norm_prompt(x_ref, acc_ref, o_ref, g, gate, n_rows):
    gg = g * gate
    for r0 in range(0, n_rows, NORM_ROWS):
        sl = slice(r0, r0 + NORM_ROWS)
        o_ref[sl, :] = x_ref[sl, :] + _unit_rms(acc_ref[sl, :]) * gg


def _build_h_sample(x_ref, h_ref, g, sc_ref, sh_ref, n_seq, n_t):
    for s0 in range(0, n_seq, NORM_ROWS):
        gs = g * (1.0 + sc_ref[s0:s0 + NORM_ROWS, :])
        shift = sh_ref[s0:s0 + NORM_ROWS, :]
        for t in range(n_t):
            sl = slice(t * n_seq + s0, t * n_seq + s0 + NORM_ROWS)
            h_ref[sl, :] = (_unit_rms(x_ref[sl, :]) * gs + shift).astype(BF16)


MIX_SUBTILE = 512
POOL_ROWS = 128
LRU_ROWS = 64


def _mix_prompt_kernel(x_hbm, sh_ref, sc_ref, g_ref, win_ref, wgrp_ref, pscale_ref,
                       wconv_ref, bconv_ref, wrg_ref, brg_ref, wig_ref, big_ref, lam_ref, wdn_ref,
                       y_ref, npool_ref, nconv_ref, nh_ref, h_ref, wdnb_ref,
                       x_ref, x_sem, ext_ref, xb_ref, r_ref, i_ref, a_ref, u_ref, *, seq, n_seq):
    q = pl.program_id(0)
    j = pl.program_id(1)
    _cast_slab(wdn_ref, wdnb_ref)

    def x_copy(s):
        rows = pl.ds(pl.multiple_of(s * seq, seq), seq)
        return pltpu.make_async_copy(x_hbm.at[rows, :], x_ref, x_sem)

    @pl.when(jnp.logical_and(q == 0, j == 0))
    def _():
        x_copy(0).start()

    @pl.when(jnp.logical_and(j == 1, q + 1 < n_seq))
    def _():
        x_copy(q + 1).start()

    @pl.when(j == 0)
    def _():
        x_copy(q).wait()
        _build_h_prompt(x_ref, h_ref, g_ref[...], 1.0 + sc_ref[pl.ds(q, 1), :],
                        sh_ref[pl.ds(q, 1), :], seq)
        for ref in (ext_ref, r_ref, i_ref, a_ref):
            ref[0:HALO, :] = jnp.zeros((HALO, CB), F32)

    win = win_ref[...].astype(BF16)

    def rows_of(r0, n, shift=0):
        return slice(HALO + r0 - shift, HALO + r0 - shift + n)

    def up_proj(m0):
        ext_ref[rows_of(m0, MIX_SUBTILE), :] = _dot(h_ref[m0:m0 + MIX_SUBTILE, :], win)

    def pool_branch(w):
        wg = wgrp_ref[0].astype(BF16)
        ps = pscale_ref[...]
        partial = {2: r_ref, 4: i_ref, 8: a_ref}
        up_proj(0)
        for m0 in range(0, seq, MIX_SUBTILE):
            if m0 + MIX_SUBTILE < seq:
                up_proj(m0 + MIX_SUBTILE)
            for r0 in range(m0, m0 + MIX_SUBTILE, POOL_ROWS):
                u = ext_ref[rows_of(r0, POOL_ROWS), :]
                s, src, width = u, ext_ref, 1
                while width < w:
                    s = s + src[rows_of(r0, POOL_ROWS, width), :]
                    width *= 2
                    if width < w:
                        src = partial[width]
                        src[rows_of(r0, POOL_ROWS), :] = s
                if r0 < w:
                    pos = r0 + lax.broadcasted_iota(jnp.int32, (POOL_ROWS, 1), 0)
                    cnt = jnp.minimum(w, pos + 1).astype(F32)
                else:
                    cnt = float(w)
                xb_ref[r0:r0 + POOL_ROWS, :] = (s / cnt - u).astype(BF16)
            sub = slice(m0, m0 + MIX_SUBTILE)
            y_ref[sub, :] = (_dot(xb_ref[sub, :], wg) * ps).astype(BF16)
        npool_ref[0] = ext_ref[pl.ds(HALO + seq - POOL_BUF, POOL_BUF), :]

    for g, w in enumerate(POOL_WINDOWS):
        pl.when(j == g)(functools.partial(pool_branch, w))

    @pl.when(j >= POOL_GROUPS)
    def _():
        wc = wconv_ref[...]
        bc = bconv_ref[...]
        wrg = wrg_ref[0].astype(BF16)
        wig = wig_ref[0].astype(BF16)
        neg_c_sp = (-LRU_C) * _softplus(-lam_ref[...])
        brg = brg_ref[...]
        big = big_ref[...]
        nb = LRU_ROWS // SUBLANES
        row = lax.broadcasted_iota(jnp.int32, (nb, SUBLANES, CB), 1)
        h_carry = jnp.zeros((1, CB), F32)
        up_proj(0)
        for m0 in range(0, seq, MIX_SUBTILE):
            sub = slice(m0, m0 + MIX_SUBTILE)
            if m0 + MIX_SUBTILE < seq:
                up_proj(m0 + MIX_SUBTILE)
            for r0 in range(m0, m0 + MIX_SUBTILE, LRU_ROWS):
                xc = wc[LRU_CONV - 1:, :] * ext_ref[rows_of(r0, LRU_ROWS), :] + bc
                for k in range(LRU_CONV - 1):
                    xc = xc + wc[k:k + 1, :] * ext_ref[rows_of(r0, LRU_ROWS, LRU_CONV - 1 - k), :]
                a_ref[rows_of(r0, LRU_ROWS), :] = xc
                xb_ref[r0:r0 + LRU_ROWS, :] = xc.astype(BF16)
            r_ref[rows_of(m0, MIX_SUBTILE), :] = _dot(xb_ref[sub, :], wrg)
            i_ref[rows_of(m0, MIX_SUBTILE), :] = _dot(xb_ref[sub, :], wig)
            for r0 in range(m0, m0 + MIX_SUBTILE, LRU_ROWS):
                sl = rows_of(r0, LRU_ROWS)
                a, b = _lru_coeffs(a_ref[sl, :], r_ref[sl, :] + brg, i_ref[sl, :] + big, neg_c_sp)
                a = a.reshape(nb, SUBLANES, CB)
                b = b.reshape(nb, SUBLANES, CB)
                for k in (1, 2, 4):
                    a_sh = jnp.where(row >= k, pltpu.roll(a, k, 1), 1.0)
                    b_sh = jnp.where(row >= k, pltpu.roll(b, k, 1), 0.0)
                    b = b + a * b_sh
                    a = a * a_sh
                a_ref[sl, :] = a.reshape(LRU_ROWS, CB)
                u_ref[sl, :] = b.reshape(LRU_ROWS, CB)
            for r0 in range(m0, m0 + MIX_SUBTILE, SUBLANES):
                sl = rows_of(r0, SUBLANES)
                h8 = a_ref[sl, :] * h_carry + u_ref[sl, :]
                u_ref[sl, :] = h8
                h_carry = h8[SUBLANES - 1:SUBLANES, :]
            y_ref[sub, :] = u_ref[rows_of(m0, MIX_SUBTILE), :].astype(BF16)
        nh_ref[0] = h_carry
        nconv_ref[0] = ext_ref[pl.ds(HALO + seq - (LRU_CONV - 1), LRU_CONV - 1), :]


def _mix_sample_kernel(x_ref, sh_ref, sc_ref, g_ref, win_ref, wgrp_ref, pscale_ref,
                       wconv_ref, bconv_ref, wrg_ref, brg_ref, wig_ref, big_ref, lam_ref,
                       spool_ref, sconv_ref, sh0_ref,
                       y_ref, npool_ref, nconv_ref, nh_ref, h_ref,
                       u_ref, d_ref, *, n_seq, n_t, start):
    j = pl.program_id(1)

    @pl.when(j == 0)
    def _():
        _build_h_sample(x_ref, h_ref, g_ref[...], sc_ref, sh_ref, n_seq, n_t)

    u_ref[...] = _dot(h_ref[...], win_ref[...].astype(BF16))

    def u_slab(t):
        return u_ref[t * n_seq:(t + 1) * n_seq, :]

    def pool_branch(w):
        e = [spool_ref[k * n_seq:(k + 1) * n_seq, :] for k in range(POOL_BUF)]
        e += [u_slab(t) for t in range(n_t)]
        for k in range(POOL_BUF):
            npool_ref[k * n_seq:(k + 1) * n_seq, :] = e[n_t + k]
        for t in range(n_t):
            s = e[POOL_BUF + t]
            for k in range(1, w):
                s = s + e[POOL_BUF + t - k]
            cnt = float(min(w, start + t + 1))
            d_ref[t * n_seq:(t + 1) * n_seq, :] = (s / cnt - e[POOL_BUF + t]).astype(BF16)
        y = _dot(d_ref[...], wgrp_ref[0].astype(BF16)) * pscale_ref[...]
        y_ref[...] = y.astype(BF16)

    for g, w in enumerate(POOL_WINDOWS):
        pl.when(j == g)(functools.partial(pool_branch, w))

    @pl.when(j >= POOL_GROUPS)
    def _():
        nk = LRU_CONV - 1
        e = [sconv_ref[k * n_seq:(k + 1) * n_seq, :] for k in range(nk)]
        e += [u_slab(t) for t in range(n_t)]
        for k in range(nk):
            nconv_ref[k * n_seq:(k + 1) * n_seq, :] = e[n_t + k]
        wc = wconv_ref[...]
        bc = bconv_ref[...]
        wrg = wrg_ref[0].astype(BF16)
        wig = wig_ref[0].astype(BF16)
        neg_c_sp = (-LRU_C) * _softplus(-lam_ref[...])
        h = sh0_ref[...]
        for t in range(n_t):
            xc = bc
            for k in range(LRU_CONV):
                xc = xc + wc[k:k + 1, :] * e[t + k]
            xb = xc.astype(BF16)
            a, b = _lru_coeffs(xc, _dot(xb, wrg) + brg_ref[...], _dot(xb, wig) + big_ref[...],
                               neg_c_sp)
            h = a * h + b
            y_ref[t * n_seq:(t + 1) * n_seq, :] = h.astype(BF16)
        nh_ref[...] = h


def _mix_weight_specs():
    pj = lambda j: jnp.minimum(j, POOL_GROUPS - 1)
    lj = lambda j: jnp.maximum(j - POOL_GROUPS, 0)
    return [
        pl.BlockSpec((1, D_MODEL), lambda q, j: (0, 0)),
        pl.BlockSpec((D_MODEL, CB), lambda q, j: (0, j)),
        pl.BlockSpec((1, CB, CB), lambda q, j: (pj(j), 0, 0)),
        pl.BlockSpec((1, CB), lambda q, j: (0, pj(j))),
        pl.BlockSpec((LRU_CONV, CB), lambda q, j: (0, lj(j))),
        pl.BlockSpec((1, CB), lambda q, j: (0, lj(j))),
        pl.BlockSpec((1, CB, CB), lambda q, j: (lj(j), 0, 0)),
        pl.BlockSpec((1, CB), lambda q, j: (0, lj(j))),
        pl.BlockSpec((1, CB, CB), lambda q, j: (lj(j), 0, 0)),
        pl.BlockSpec((1, CB), lambda q, j: (0, lj(j))),
        pl.BlockSpec((1, CB), lambda q, j: (0, lj(j))),
    ], pj, lj


def _mix_prompt(x2, ada, p, n_seq, seq, ada_row_block):
    wspecs, pj, lj = _mix_weight_specs()
    in_specs = [
        pl.BlockSpec(memory_space=pl.ANY),
        pl.BlockSpec((SUBLANES, D_MODEL), lambda q, j: (ada_row_block, 0)),
        pl.BlockSpec((SUBLANES, D_MODEL), lambda q, j: (ada_row_block, 1)),
    ] + wspecs
    d_ff = p["w_ffn_down"].shape[0]
    dn_rows = d_ff // (n_seq * N_MIX_BLOCKS)
    assert dn_rows * n_seq * N_MIX_BLOCKS == d_ff and dn_rows % BF16_ROWS == 0
    slab_spec = pl.BlockSpec((dn_rows, D_MODEL), lambda q, j: (q * N_MIX_BLOCKS + j, 0))
    in_specs.append(slab_spec)
    out_specs = [
        pl.BlockSpec((seq, CB), lambda q, j: (q, j)),
        pl.BlockSpec((1, POOL_BUF, CB), lambda q, j: (q, 0, pj(j))),
        pl.BlockSpec((1, LRU_CONV - 1, CB), lambda q, j: (q, 0, lj(j))),
        pl.BlockSpec((1, 1, CB), lambda q, j: (q, 0, lj(j))),
        pl.BlockSpec((seq, D_MODEL), lambda q, j: (q, 0)),
        slab_spec,
    ]
    out_shape = [
        jax.ShapeDtypeStruct((n_seq * seq, N_MIX_BLOCKS * CB), BF16),
        jax.ShapeDtypeStruct((n_seq, POOL_BUF, POOL_WIDTH), F32),
        jax.ShapeDtypeStruct((n_seq, LRU_CONV - 1, LRU_WIDTH), F32),
        jax.ShapeDtypeStruct((n_seq, 1, LRU_WIDTH), F32),
        jax.ShapeDtypeStruct((n_seq * seq, D_MODEL), BF16),
        jax.ShapeDtypeStruct((d_ff, D_MODEL), BF16),
    ]
    scratch = [
        pltpu.VMEM((seq, D_MODEL), F32),
        pltpu.SemaphoreType.DMA(()),
        pltpu.VMEM((HALO + seq, CB), F32),
        pltpu.VMEM((seq, CB), BF16),
        pltpu.VMEM((HALO + seq, CB), F32),
        pltpu.VMEM((HALO + seq, CB), F32),
        pltpu.VMEM((HALO + seq, CB), F32),
        pltpu.VMEM((HALO + seq, CB), F32),
    ]
    return pl.pallas_call(
        functools.partial(_mix_prompt_kernel, seq=seq, n_seq=n_seq),
        grid=(n_seq, N_MIX_BLOCKS),
        in_specs=in_specs, out_specs=out_specs, out_shape=out_shape,
        scratch_shapes=scratch, compiler_params=_cparams(2), name="mix_prompt",
    )(x2, ada, ada, p["g_pre1"], p["w_in"], p["w_pool_grp"], p["pool_scale"],
      p["w_lru_conv"], p["b_lru_conv"], p["w_rg"], p["b_rg"], p["w_ig"], p["b_ig"],
      p["lru_lambda"], p["w_ffn_down"])


def _mix_sample(x2, ada, p, spool2, sconv2, sh0, n_seq, n_t, start):
    wspecs, pj, lj = _mix_weight_specs()
    rows = n_seq * n_t
    in_specs = [
        pl.BlockSpec((rows, D_MODEL), lambda q, j: (0, 0), pipeline_mode=pl.Buffered(1)),
        pl.BlockSpec((n_seq, D_MODEL), lambda q, j: (0, 0)),
        pl.BlockSpec((n_seq, D_MODEL), lambda q, j: (0, 1)),
    ] + wspecs + [
        pl.BlockSpec((n_seq * POOL_BUF, CB), lambda q, j: (0, pj(j))),
        pl.BlockSpec((n_seq * (LRU_CONV - 1), CB), lambda q, j: (0, lj(j))),
        pl.BlockSpec((n_seq, CB), lambda q, j: (0, lj(j))),
    ]
    out_specs = [
        pl.BlockSpec((rows, CB), lambda q, j: (0, j)),
        pl.BlockSpec((n_seq * POOL_BUF, CB), lambda q, j: (0, pj(j))),
        pl.BlockSpec((n_seq * (LRU_CONV - 1), CB), lambda q, j: (0, lj(j))),
        pl.BlockSpec((n_seq, CB), lambda q, j: (0, lj(j))),
        pl.BlockSpec((rows, D_MODEL), lambda q, j: (0, 0)),
    ]
    out_shape = [
        jax.ShapeDtypeStruct((rows, N_MIX_BLOCKS * CB), BF16),
        jax.ShapeDtypeStruct((n_seq * POOL_BUF, POOL_WIDTH), F32),
        jax.ShapeDtypeStruct((n_seq * (LRU_CONV - 1), LRU_WIDTH), F32),
        jax.ShapeDtypeStruct((n_seq, LRU_WIDTH), F32),
        jax.ShapeDtypeStruct((rows, D_MODEL), BF16),
    ]
    scratch = [
        pltpu.VMEM((rows, CB), F32),
        pltpu.VMEM((rows, CB), BF16),
    ]
    return pl.pallas_call(
        functools.partial(_mix_sample_kernel, n_seq=n_seq, n_t=n_t, start=start),
        grid=(1, N_MIX_BLOCKS),
        in_specs=in_specs, out_specs=out_specs, out_shape=out_shape,
        scratch_shapes=scratch, compiler_params=_cparams(2), name="mix_sample",
    )(x2, ada, ada, p["g_pre1"], p["w_in"], p["w_pool_grp"], p["pool_scale"],
      p["w_lru_conv"], p["b_lru_conv"], p["w_rg"], p["b_rg"], p["w_ig"], p["b_ig"],
      p["lru_lambda"], spool2, sconv2, sh0)


MERGE_ROWS = 64
MERGE_SUBTILE = 256
CAST_ROWS = 32
BF16_ROWS = 16


def _cast_ffn_up_slab(wup_ref, wupb_ref):
    n_blocks, _, cb2 = wupb_ref.shape
    cb = cb2 // 2
    for blk in range(n_blocks):
        wupb_ref[blk, :, 0:cb] = wup_ref[:, blk * cb:(blk + 1) * cb].astype(BF16)
        wupb_ref[blk, :, cb:cb2] = wup_ref[:, (n_blocks + blk) * cb:(n_blocks + blk + 1) * cb].astype(BF16)


def _cast_slab(w_ref, wb_ref):
    for r0 in range(0, w_ref.shape[0], CAST_ROWS):
        wb_ref[r0:r0 + CAST_ROWS, :] = w_ref[r0:r0 + CAST_ROWS, :].astype(BF16)


def _merge_kernel(h_ref, y_ref, wgp_ref, wgl_ref, wpu_ref, wlu_ref, *rest, tm, cast_ffn):
    if cast_ffn:
        wup_ref, m_ref, wupb_ref, wg_ref, g_ref, pu_ref, lu_ref = rest
    else:
        m_ref, wg_ref, g_ref, pu_ref, lu_ref = rest
    wg_ref[:, 0:CB] = wgp_ref[...].astype(BF16)
    wg_ref[:, CB:2 * CB] = wgl_ref[...].astype(BF16)
    wpu = wpu_ref[...].astype(BF16)
    wlu = wlu_ref[...].astype(BF16)

    def slot(m0):
        s0 = (m0 // MERGE_SUBTILE) % 2 * MERGE_SUBTILE
        return slice(s0, s0 + MERGE_SUBTILE)

    def proj(m0):
        rows = slice(m0, m0 + MERGE_SUBTILE)
        g_ref[slot(m0), :] = _dot(h_ref[rows, :], wg_ref[...])
        pu_ref[slot(m0), :] = _dot(y_ref[rows, 0:POOL_WIDTH], wpu)
        lu_ref[slot(m0), :] = _dot(y_ref[rows, POOL_WIDTH:POOL_WIDTH + LRU_WIDTH], wlu)

    proj(0)
    for m0 in range(0, tm, MERGE_SUBTILE):
        if m0 + MERGE_SUBTILE < tm:
            proj(m0 + MERGE_SUBTILE)
        s0 = slot(m0).start
        for r0 in range(0, MERGE_SUBTILE, MERGE_ROWS):
            sl = slice(s0 + r0, s0 + r0 + MERGE_ROWS)
            m = (jax.nn.sigmoid(g_ref[sl, 0:CB]) * pu_ref[sl, :]
                 + jax.nn.sigmoid(g_ref[sl, CB:2 * CB]) * lu_ref[sl, :])
            m_ref[m0 + r0:m0 + r0 + MERGE_ROWS, :] = m.astype(BF16)
    if cast_ffn:
        _cast_ffn_up_slab(wup_ref, wupb_ref)


def _merge(h, y, p, tm, name, cast_ffn_cb=0):
    rows = h.shape[0]
    n_col = D_MODEL // CB
    n_gate0 = N_MIX_BLOCKS
    n_gate1 = N_MIX_BLOCKS + n_col
    in_specs = [
        pl.BlockSpec((tm, D_MODEL), lambda i, c: (i, 0), pipeline_mode=pl.Buffered(1)),
        pl.BlockSpec((tm, N_MIX_BLOCKS * CB), lambda i, c: (i, 0), pipeline_mode=pl.Buffered(1)),
        pl.BlockSpec((D_MODEL, CB), lambda i, c: (0, n_gate0 + c)),
        pl.BlockSpec((D_MODEL, CB), lambda i, c: (0, n_gate1 + c)),
        pl.BlockSpec((POOL_WIDTH, CB), lambda i, c: (0, c)),
        pl.BlockSpec((LRU_WIDTH, CB), lambda i, c: (0, c)),
    ]
    out_specs = [pl.BlockSpec((tm, CB), lambda i, c: (i, c))]
    out_shape = [jax.ShapeDtypeStruct((rows, D_MODEL), BF16)]
    args = [h, y, p["w_in"], p["w_in"], p["w_pool_up"], p["w_lru_up"]]
    if cast_ffn_cb:
        n_steps = (rows // tm) * n_col
        d_ff = p["w_ffn_down"].shape[0]
        n_blocks = d_ff // cast_ffn_cb
        up_rows = D_MODEL // n_steps
        assert up_rows * n_steps == D_MODEL and up_rows % BF16_ROWS == 0
        step = lambda i, c: i * n_col + c
        in_specs += [pl.BlockSpec((up_rows, 2 * d_ff), lambda i, c: (step(i, c), 0))]
        out_specs += [pl.BlockSpec((n_blocks, up_rows, 2 * cast_ffn_cb),
                                   lambda i, c: (0, step(i, c), 0))]
        out_shape += [jax.ShapeDtypeStruct((n_blocks, D_MODEL, 2 * cast_ffn_cb), BF16)]
        args += [p["w_ffn_up"]]
    ring = 2 * MERGE_SUBTILE
    return pl.pallas_call(
        functools.partial(_merge_kernel, tm=tm, cast_ffn=bool(cast_ffn_cb)),
        grid=(rows // tm, n_col),
        in_specs=in_specs, out_specs=out_specs, out_shape=out_shape,
        scratch_shapes=[pltpu.VMEM((D_MODEL, 2 * CB), BF16), pltpu.VMEM((ring, 2 * CB), F32),
                        pltpu.VMEM((ring, CB), F32), pltpu.VMEM((ring, CB), F32)],
        compiler_params=_cparams(2), name=name,
    )(*args)


def _cast_weight(w_ref, wb_ref):
    rows = w_ref.shape[0]
    for r0 in range(0, rows, 256):
        wb_ref[r0:r0 + 256, :] = w_ref[r0:r0 + 256, :].astype(BF16)


def _outproj_kernel(m_ref, x_ref, gate_ref, g_ref, w_ref, o_ref, wb_ref, acc_ref,
                    *, tm, tiles_per_seq, sample_t):
    i = pl.program_id(0)

    @pl.when(i == 0)
    def _():
        _cast_weight(w_ref, wb_ref)

    g = g_ref[...]
    if sample_t:
        n_seq = tm // sample_t
    else:
        gg_tile = g * gate_ref[pl.ds(i // tiles_per_seq, 1), :]

    def proj(m0):
        acc_ref[m0:m0 + OUT_SUBTILE, :] = _dot(m_ref[m0:m0 + OUT_SUBTILE, :], wb_ref[...])

    proj(0)
    for m0 in range(0, tm, OUT_SUBTILE):
        if m0 + OUT_SUBTILE < tm:
            proj(m0 + OUT_SUBTILE)
        for r0 in range(m0, m0 + OUT_SUBTILE, NORM_ROWS):
            sl = slice(r0, r0 + NORM_ROWS)
            if sample_t:
                s0 = r0 % n_seq
                gg = g * gate_ref[s0:s0 + NORM_ROWS, :]
            else:
                gg = gg_tile
            o_ref[sl, :] = x_ref[sl, :] + _unit_rms(acc_ref[sl, :]) * gg


def _outproj(m, x2, ada, p, tm, tiles_per_seq, ada_row_block, sample_t):
    rows = x2.shape[0]
    if sample_t:
        gate_spec = pl.BlockSpec((tm // sample_t, D_MODEL), lambda i: (0, 2))
    else:
        gate_spec = pl.BlockSpec((SUBLANES, D_MODEL), lambda i: (ada_row_block, 2))
    return pl.pallas_call(
        functools.partial(_outproj_kernel, tm=tm, tiles_per_seq=tiles_per_seq, sample_t=sample_t),
        grid=(rows // tm,),
        in_specs=[pl.BlockSpec((tm, D_MODEL), lambda i: (i, 0)),
                  pl.BlockSpec((tm, D_MODEL), lambda i: (i, 0)),
                  gate_spec,
                  pl.BlockSpec((1, D_MODEL), lambda i: (0, 0)),
                  pl.BlockSpec((D_MODEL, D_MODEL), lambda i: (0, 0), pipeline_mode=pl.Buffered(1))],
        out_specs=pl.BlockSpec((tm, D_MODEL), lambda i: (i, 0)),
        out_shape=jax.ShapeDtypeStruct((rows, D_MODEL), F32),
        scratch_shapes=[pltpu.VMEM((D_MODEL, D_MODEL), BF16), pltpu.VMEM((tm, D_MODEL), F32)],
        compiler_params=_cparams(1),
        name="outproj_sample" if sample_t else "outproj_prompt",
    )(m, x2, ada, p["g_post1"], p["w_out"])


GELU_C0 = 0.7978845608028654
GELU_C1 = GELU_C0 * 0.044715
FFN_ROWS = 32
FFN_SUBTILE = 256
FFN_AHEAD = 2
FFN_CB = 768


def _ffn_conv_gate(eg, ev, wcg, wcv, bcg, bcv):
    last = FFN_CONV - 1
    g = wcg[last:, :] * eg[last] + bcg
    v = wcv[last:, :] * ev[last] + bcv
    for k in range(last):
        g = g + wcg[k:k + 1, :] * eg[k]
        v = v + wcv[k:k + 1, :] * ev[k]
    t = jnp.tanh(g * (GELU_C0 + GELU_C1 * (g * g)))
    return ((g * v) * (0.5 + 0.5 * t)).astype(BF16)


def _ffn_prompt_kernel(x_ref, sh_ref, sc_ref, gate_ref, gpre_ref, gpost_ref,
                       wup_ref, wcg_ref, wcv_ref, bcg_ref, bcv_ref, wdn_ref,
                       o_ref, nst_ref,
                       h_ref, ext_ref, f_ref, carry_ref, *, tm, tiles_per_seq, n_blocks, cb):
    i = pl.program_id(0)
    c = pl.program_id(1)
    q = i // tiles_per_seq
    first = (i % tiles_per_seq) == 0
    nk = FFN_CONV - 1

    @pl.when(c == 0)
    def _():
        _build_h_prompt(x_ref, h_ref, gpre_ref[...], 1.0 + sc_ref[pl.ds(q, 1), :],
                        sh_ref[pl.ds(q, 1), :], tm)
        o_ref[...] = jnp.zeros((tm, D_MODEL), F32)

    @pl.when(first)
    def _():
        ext_ref[0:SUBLANES, :] = jnp.zeros((SUBLANES, 2 * cb), F32)

    @pl.when(jnp.logical_not(first))
    def _():
        ext_ref[0:SUBLANES, :] = carry_ref[c]

    wcg = wcg_ref[...]
    wcv = wcv_ref[...]
    bcg = bcg_ref[...]
    bcv = bcv_ref[...]

    def up_proj(m0):
        ext_ref[SUBLANES + m0:SUBLANES + m0 + FFN_SUBTILE, :] = _dot(
            h_ref[m0:m0 + FFN_SUBTILE, :], wup_ref[0])

    for m0 in range(0, min(FFN_AHEAD * FFN_SUBTILE, tm), FFN_SUBTILE):
        up_proj(m0)
    for m0 in range(0, tm, FFN_SUBTILE):
        rows = slice(m0, m0 + FFN_SUBTILE)
        for r0 in range(m0, m0 + FFN_SUBTILE, FFN_ROWS):
            shifted = [slice(SUBLANES + r0 - nk + k, SUBLANES + r0 - nk + k + FFN_ROWS)
                       for k in range(FFN_CONV)]
            eg = [ext_ref[s, 0:cb] for s in shifted]
            ev = [ext_ref[s, cb:2 * cb] for s in shifted]
            f_ref[r0:r0 + FFN_ROWS, :] = _ffn_conv_gate(eg, ev, wcg, wcv, bcg, bcv)
        if m0 + FFN_AHEAD * FFN_SUBTILE < tm:
            up_proj(m0 + FFN_AHEAD * FFN_SUBTILE)
        o_ref[rows, :] += _dot(f_ref[rows, :], wdn_ref[...])

    carry_ref[c] = ext_ref[tm:tm + SUBLANES, :]
    nst_ref[0, :, 0, :] = ext_ref[pl.ds(SUBLANES + tm - nk, nk), 0:cb]
    nst_ref[0, :, 1, :] = ext_ref[pl.ds(SUBLANES + tm - nk, nk), cb:2 * cb]

    @pl.when(c == n_blocks - 1)
    def _():
        _residual_norm_prompt(x_ref, o_ref, o_ref, gpost_ref[...], gate_ref[pl.ds(q, 1), :], tm)


def _ffn_sample_kernel(x_ref, sh_ref, sc_ref, gate_ref, gpre_ref, gpost_ref,
                       wup_ref, wcg_ref, wcv_ref, bcg_ref, bcv_ref, wdn_ref,
                       sg_ref, sv_ref,
                       o_ref, ng_ref, nv_ref,
                       h_ref, ext_ref, f_ref, *, n_seq, n_t, n_blocks, cb):
    c = pl.program_id(1)
    nk = FFN_CONV - 1

    @pl.when(c == 0)
    def _():
        _build_h_sample(x_ref, h_ref, gpre_ref[...], sc_ref, sh_ref, n_seq, n_t)
        o_ref[...] = jnp.zeros((n_seq * n_t, D_MODEL), F32)

    wcg = wcg_ref[...]
    wcv = wcv_ref[...]
    bcg = bcg_ref[...]
    bcv = bcv_ref[...]

    def view(state_ref, half, t, s0, n):
        if t < 0:
            return state_ref[(nk + t) * n_seq + s0:(nk + t) * n_seq + s0 + n, :]
        return ext_ref[t * n_seq + s0:t * n_seq + s0 + n, half * cb:(half + 1) * cb]

    t_sub = max(FFN_SUBTILE // n_seq, 1)

    def up_proj(t0):
        rows = slice(t0 * n_seq, (t0 + t_sub) * n_seq)
        ext_ref[rows, :] = _dot(h_ref[rows, :], wup_ref[0])

    for t0 in range(0, min(FFN_AHEAD * t_sub, n_t), t_sub):
        up_proj(t0)
    for t0 in range(0, n_t, t_sub):
        rows = slice(t0 * n_seq, (t0 + t_sub) * n_seq)
        for t in range(t0, t0 + t_sub):
            for s0 in range(0, n_seq, FFN_ROWS):
                eg = [view(sg_ref, 0, t - nk + k, s0, FFN_ROWS) for k in range(FFN_CONV)]
                ev = [view(sv_ref, 1, t - nk + k, s0, FFN_ROWS) for k in range(FFN_CONV)]
                f_ref[t * n_seq + s0:t * n_seq + s0 + FFN_ROWS, :] = _ffn_conv_gate(
                    eg, ev, wcg, wcv, bcg, bcv)
        if t0 + FFN_AHEAD * t_sub < n_t:
            up_proj(t0 + FFN_AHEAD * t_sub)
        o_ref[rows, :] += _dot(f_ref[rows, :], wdn_ref[...])

    for k in range(nk):
        ng_ref[k * n_seq:(k + 1) * n_seq, :] = view(sg_ref, 0, n_t - nk + k, 0, n_seq)
        nv_ref[k * n_seq:(k + 1) * n_seq, :] = view(sv_ref, 1, n_t - nk + k, 0, n_seq)

    @pl.when(c == n_blocks - 1)
    def _():
        g = gpost_ref[...]
        for s0 in range(0, n_seq, NORM_ROWS):
            gg = g * gate_ref[s0:s0 + NORM_ROWS, :]
            for t in range(n_t):
                sl = slice(t * n_seq + s0, t * n_seq + s0 + NORM_ROWS)
                o_ref[sl, :] = x_ref[sl, :] + _unit_rms(o_ref[sl, :]) * gg


def _ffn_weight_specs(n_blocks, cb):
    return [
        pl.BlockSpec((1, D_MODEL), lambda i, c: (0, 0)),
        pl.BlockSpec((1, D_MODEL), lambda i, c: (0, 0)),
        pl.BlockSpec((1, D_MODEL, 2 * cb), lambda i, c: (c, 0, 0)),
        pl.BlockSpec((FFN_CONV, cb), lambda i, c: (0, c)),
        pl.BlockSpec((FFN_CONV, cb), lambda i, c: (0, n_blocks + c)),
        pl.BlockSpec((1, cb), lambda i, c: (0, c)),
        pl.BlockSpec((1, cb), lambda i, c: (0, n_blocks + c)),
        pl.BlockSpec((cb, D_MODEL), lambda i, c: (c, 0)),
    ]


def _ffn_prompt(x1, ada, p, wup_b, wdn_b, n_seq, tm, tiles_per_seq, ada_row_block):
    rows = x1.shape[0]
    n_blocks, _, cb2 = wup_b.shape
    cb = cb2 // 2
    ada_spec = lambda k: pl.BlockSpec((SUBLANES, D_MODEL), lambda i, c: (ada_row_block, k))
    in_specs = [pl.BlockSpec((tm, D_MODEL), lambda i, c: (i, 0), pipeline_mode=pl.Buffered(1)),
                ada_spec(3), ada_spec(4), ada_spec(5)] + _ffn_weight_specs(n_blocks, cb)
    out_specs = [
        pl.BlockSpec((tm, D_MODEL), lambda i, c: (i, 0)),
        pl.BlockSpec((1, FFN_CONV - 1, 2, cb), lambda i, c: (i, 0, 0, c)),
    ]
    out_shape = [jax.ShapeDtypeStruct((rows, D_MODEL), F32),
                 jax.ShapeDtypeStruct((rows // tm, FFN_CONV - 1, 2, n_blocks * cb), F32)]
    scratch = [
        pltpu.VMEM((tm, D_MODEL), BF16),
        pltpu.VMEM((SUBLANES + tm, 2 * cb), F32),
        pltpu.VMEM((tm, cb), BF16),
        pltpu.VMEM((n_blocks, SUBLANES, 2 * cb), F32),
    ]
    return pl.pallas_call(
        functools.partial(_ffn_prompt_kernel, tm=tm, tiles_per_seq=tiles_per_seq,
                          n_blocks=n_blocks, cb=cb),
        grid=(rows // tm, n_blocks),
        in_specs=in_specs, out_specs=out_specs, out_shape=out_shape,
        scratch_shapes=scratch, compiler_params=_cparams(2), name="ffn_prompt",
    )(x1, ada, ada, ada, p["g_pre2"], p["g_post2"], wup_b, p["w_ffn_conv"], p["w_ffn_conv"],
      p["b_ffn_conv"], p["b_ffn_conv"], wdn_b)


def _ffn_sample(x1, ada, p, wup_b, wdn_b, sffn2, n_seq, n_t):
    rows = n_seq * n_t
    n_blocks, _, cb2 = wup_b.shape
    cb = cb2 // 2
    nk = FFN_CONV - 1
    ada_spec = lambda k: pl.BlockSpec((n_seq, D_MODEL), lambda i, c: (0, k))
    in_specs = [pl.BlockSpec((rows, D_MODEL), lambda i, c: (0, 0), pipeline_mode=pl.Buffered(1)),
                ada_spec(3), ada_spec(4), ada_spec(5)] + _ffn_weight_specs(n_blocks, cb) + [
        pl.BlockSpec((n_seq * nk, cb), lambda i, c: (0, c)),
        pl.BlockSpec((n_seq * nk, cb), lambda i, c: (0, n_blocks + c)),
    ]
    out_specs = [
        pl.BlockSpec((rows, D_MODEL), lambda i, c: (0, 0)),
        pl.BlockSpec((n_seq * nk, cb), lambda i, c: (0, c)),
        pl.BlockSpec((n_seq * nk, cb), lambda i, c: (0, c)),
    ]
    out_shape = [jax.ShapeDtypeStruct((rows, D_MODEL), F32),
                 jax.ShapeDtypeStruct((n_seq * nk, n_blocks * cb), F32),
                 jax.ShapeDtypeStruct((n_seq * nk, n_blocks * cb), F32)]
    scratch = [
        pltpu.VMEM((rows, D_MODEL), BF16),
        pltpu.VMEM((rows, 2 * cb), F32),
        pltpu.VMEM((rows, cb), BF16),
    ]
    return pl.pallas_call(
        functools.partial(_ffn_sample_kernel, n_seq=n_seq, n_t=n_t, n_blocks=n_blocks, cb=cb),
        grid=(1, n_blocks),
        in_specs=in_specs, out_specs=out_specs, out_shape=out_shape,
        scratch_shapes=scratch, compiler_params=_cparams(2), name="ffn_sample",
    )(x1, ada, ada, ada, p["g_pre2"], p["g_post2"], wup_b, p["w_ffn_conv"], p["w_ffn_conv"],
      p["b_ffn_conv"], p["b_ffn_conv"], wdn_b, sffn2, sffn2)


TOKEN_TILE = 1024
MERGE_TILE = 2048
OUTPROJ_TILE = 512


def kernel(x_prompt, x_sample, c_prompt, c_sample, state_pool, state_lru_conv, state_lru_h, state_ffn_conv, w_ada, b_ada, g_pre1, g_post1, g_pre2, g_post2, w_in, w_pool_grp, pool_scale, w_lru_conv, b_lru_conv, w_rg, b_rg, w_ig, b_ig, lru_lambda, w_pool_up, w_lru_up, w_out, w_ffn_up, w_ffn_conv, b_ffn_conv, w_ffn_down):
    batch, seq, d = x_prompt.shape
    dec_batch, dec_seq, _ = x_sample.shape
    depth = w_ada.shape[0]
    assert d == D_MODEL and dec_batch % SUBLANES == 0 and seq % TOKEN_TILE == 0
    assert w_in.shape[2] == N_MIX_BLOCKS * CB + 2 * D_MODEL

    pad = (-batch) % SUBLANES
    c_all = jnp.concatenate([c_sample, c_prompt, jnp.zeros((pad, d), c_prompt.dtype)], axis=0)
    prompt_row_block = dec_batch // SUBLANES

    vec_names = ("g_pre1", "g_post1", "g_pre2", "g_post2", "pool_scale", "b_lru_conv", "b_rg",
                 "b_ig", "lru_lambda", "b_ffn_conv")
    weights = dict(w_ada=w_ada, b_ada=b_ada, g_pre1=g_pre1, g_post1=g_post1, g_pre2=g_pre2,
                   g_post2=g_post2, w_in=w_in, w_pool_grp=w_pool_grp, pool_scale=pool_scale,
                   w_lru_conv=w_lru_conv, b_lru_conv=b_lru_conv, w_rg=w_rg, b_rg=b_rg, w_ig=w_ig,
                   b_ig=b_ig, lru_lambda=lru_lambda, w_pool_up=w_pool_up, w_lru_up=w_lru_up,
                   w_out=w_out, w_ffn_up=w_ffn_up, w_ffn_conv=w_ffn_conv, b_ffn_conv=b_ffn_conv,
                   w_ffn_down=w_ffn_down)

    def time_major(a):
        return jnp.swapaxes(a, 0, 1).reshape(-1, a.shape[-1])

    def seq_major(a2, n_rows):
        return jnp.swapaxes(a2.reshape(n_rows, dec_batch, -1), 0, 1)

    xp = x_prompt.reshape(batch * seq, d)
    xs = time_major(x_sample)
    tps = seq // TOKEN_TILE
    outs_p = ([], [], [], [])
    outs_s = ([], [], [], [])
    for l in range(depth):
        p = {k: v[l] for k, v in weights.items()}
        for k in vec_names + ("b_ada",):
            p[k] = p[k].reshape(1, -1)
        ada = _ada(c_all, p["w_ada"], p["b_ada"])

        y, npool, nconv, nh, h, wdn_b = _mix_prompt(xp, ada, p, batch, seq, prompt_row_block)
        m, wup_b = _merge(h, y, p, MERGE_TILE, "merge_prompt", cast_ffn_cb=FFN_CB)
        x1 = _outproj(m, xp, ada, p, OUTPROJ_TILE, seq // OUTPROJ_TILE, prompt_row_block, 0)
        xp, nffn = _ffn_prompt(x1, ada, p, wup_b, wdn_b, batch, TOKEN_TILE, tps, prompt_row_block)
        outs_p[0].append(npool)
        outs_p[1].append(nconv)
        outs_p[2].append(nh.reshape(batch, LRU_WIDTH))
        outs_p[3].append(nffn[tps - 1::tps].reshape(batch, FFN_CONV - 1, -1))

        rows_s = dec_batch * dec_seq
        y, npool, nconv, nh, h = _mix_sample(
            xs, ada, p, time_major(state_pool[l]), time_major(state_lru_conv[l]),
            state_lru_h[l], dec_batch, dec_seq, PAST_LEN)
        m, = _merge(h, y, p, rows_s, "merge_sample")
        x1 = _outproj(m, xs, ada, p, rows_s, 1, 0, dec_seq)
        xs, ng, nv = _ffn_sample(x1, ada, p, wup_b, wdn_b, time_major(state_ffn_conv[l]),
                                 dec_batch, dec_seq)
        outs_s[0].append(seq_major(npool, POOL_BUF))
        outs_s[1].append(seq_major(nconv, LRU_CONV - 1))
        outs_s[2].append(nh)
        outs_s[3].append(jnp.concatenate([seq_major(ng, FFN_CONV - 1),
                                          seq_major(nv, FFN_CONV - 1)], axis=-1))

    return (xp.reshape(batch, seq, d), seq_major(xs, dec_seq),
            jnp.stack(outs_p[0]), jnp.stack(outs_p[1]), jnp.stack(outs_p[2]), jnp.stack(outs_p[3]),
            jnp.stack(outs_s[0]), jnp.stack(outs_s[1]), jnp.stack(outs_s[2]), jnp.stack(outs_s[3]))
```

```python
import functools

import jax
import jax.numpy as jnp
from jax import lax
from jax.experimental import pallas as pl
from jax.experimental.pallas import tpu as pltpu

F32 = jnp.float32
BF16 = jnp.bfloat16

D_MODEL = 2048
POOL_WINDOWS = (2, 4, 8, 16)
POOL_GROUPS = len(POOL_WINDOWS)
POOL_BUF = max(POOL_WINDOWS) - 1
LRU_CONV = 4
LRU_C = 8.0
PAST_LEN = 16384
FFN_CONV = 3
N_ADA = 6
EPS = 1e-6

CB = 256
POOL_WIDTH = POOL_GROUPS * CB
LRU_BLOCKS = 8
LRU_WIDTH = LRU_BLOCKS * CB
N_MIX_BLOCKS = POOL_GROUPS + LRU_BLOCKS
HALO = 16
SUBLANES = 8
VMEM_LIMIT = 60 * 1024 * 1024


def _cparams(n_axes):
    return pltpu.CompilerParams(
        dimension_semantics=("arbitrary",) * n_axes, vmem_limit_bytes=VMEM_LIMIT)


def _dot(a, b):
    return jnp.dot(a, b, preferred_element_type=F32)


def _softplus(z):
    return jnp.maximum(z, 0.0) + jnp.log1p(jnp.exp(-jnp.abs(z)))


def _lru_coeffs(xc, r_pre, i_pre, neg_c_sp):
    r = jax.nn.sigmoid(r_pre)
    i = jax.nn.sigmoid(i_pre)
    log_a = r * neg_c_sp
    a = jnp.exp(log_a)
    m2 = -jnp.tanh(log_a) * (a * a + 1.0)
    root = jnp.where(m2 > 0.0, m2 * lax.rsqrt(m2), 0.0)
    return a, root * (i * xc)


ADA_K = 256
ADA_N = 1024


def _ada_kernel(c_ref, w_ref, b_ref, o_ref):
    rows, n_out = o_ref.shape

    @pl.when(pl.program_id(0) == 0)
    def _():
        o_ref[...] = jnp.broadcast_to(b_ref[...], (rows, n_out))

    c = c_ref[...]
    s = (c * jax.nn.sigmoid(c)).astype(BF16)
    for n0 in range(0, n_out, ADA_N):
        o_ref[:, n0:n0 + ADA_N] += _dot(s, w_ref[:, n0:n0 + ADA_N].astype(BF16))


def _ada(c_all, w_ada, b_ada):
    rows = c_all.shape[0]
    n_out = N_ADA * D_MODEL
    return pl.pallas_call(
        _ada_kernel,
        grid=(D_MODEL // ADA_K,),
        in_specs=[pl.BlockSpec((rows, ADA_K), lambda k: (0, k)),
                  pl.BlockSpec((ADA_K, n_out), lambda k: (k, 0)),
                  pl.BlockSpec((1, n_out), lambda k: (0, 0))],
        out_specs=pl.BlockSpec((rows, n_out), lambda k: (0, 0)),
        out_shape=jax.ShapeDtypeStruct((rows, n_out), F32),
        compiler_params=_cparams(1),
        name="ada",
    )(c_all, w_ada, b_ada)


NORM_ROWS = 32
NORM_UNROLL = 8
OUT_SUBTILE = 256


def _unit_rms(x):
    return x * lax.rsqrt(jnp.mean(x * x, axis=-1, keepdims=True) + EPS)


def _build_h_prompt(x_ref, h_ref, g, scale1p, shift, n_rows):
    gs = g * scale1p

    def body(i, carry):
        r0 = pl.multiple_of(i * NORM_ROWS, NORM_ROWS)
        x = x_ref[pl.ds(r0, NORM_ROWS), :]
        h_ref[pl.ds(r0, NORM_ROWS), :] = (_unit_rms(x) * gs + shift).astype(BF16)
        return carry
    lax.fori_loop(0, n_rows // NORM_ROWS, body, 0, unroll=NORM_UNROLL)


def _residual_norm_prompt(x_ref, acc_ref, o_ref, g, gate, n_rows):
    gg = g * gate
    for r0 in range(0, n_rows, NORM_ROWS):
        sl = slice(r0, r0 + NORM_ROWS)
        o_ref[sl, :] = x_ref[sl, :] + _unit_rms(acc_ref[sl, :]) * gg


def _build_h_sample(x_ref, h_ref, g, sc_ref, sh_ref, n_seq, n_t):
    for s0 in range(0, n_seq, NORM_ROWS):
        gs = g * (1.0 + sc_ref[s0:s0 + NORM_ROWS, :])
        shift = sh_ref[s0:s0 + NORM_ROWS, :]
        for t in range(n_t):
            sl = slice(t * n_seq + s0, t * n_seq + s0 + NORM_ROWS)
            h_ref[sl, :] = (_unit_rms(x_ref[sl, :]) * gs + shift).astype(BF16)


MIX_SUBTILE = 512
POOL_ROWS = 128
LRU_ROWS = 64


def _mix_prompt_kernel(x_hbm, sh_ref, sc_ref, g_ref, win_ref, wgrp_ref, pscale_ref,
                       wconv_ref, bconv_ref, wrg_ref, brg_ref, wig_ref, big_ref, lam_ref, wdn_ref,
                       y_ref, npool_ref, nconv_ref, nh_ref, h_ref, wdnb_ref,
                       x_ref, x_sem, ext_ref, xb_ref, r_ref, i_ref, a_ref, u_ref, *, seq, n_seq):
    q = pl.program_id(0)
    j = pl.program_id(1)
    _cast_slab(wdn_ref, wdnb_ref)

    def x_copy(s):
        rows = pl.ds(pl.multiple_of(s * seq, seq), seq)
        return pltpu.make_async_copy(x_hbm.at[rows, :], x_ref, x_sem)

    @pl.when(jnp.logical_and(q == 0, j == 0))
    def _():
        x_copy(0).start()

    @pl.when(jnp.logical_and(j == 1, q + 1 < n_seq))
    def _():
        x_copy(q + 1).start()

    @pl.when(j == 0)
    def _():
        x_copy(q).wait()
        _build_h_prompt(x_ref, h_ref, g_ref[...], 1.0 + sc_ref[pl.ds(q, 1), :],
                        sh_ref[pl.ds(q, 1), :], seq)
        for ref in (ext_ref, r_ref, i_ref, a_ref):
            ref[0:HALO, :] = jnp.zeros((HALO, CB), F32)

    win = win_ref[...].astype(BF16)

    def rows_of(r0, n, shift=0):
        return slice(HALO + r0 - shift, HALO + r0 - shift + n)

    def up_proj(m0):
        ext_ref[rows_of(m0, MIX_SUBTILE), :] = _dot(h_ref[m0:m0 + MIX_SUBTILE, :], win)

    def pool_branch(w):
        wg = wgrp_ref[0].astype(BF16)
        ps = pscale_ref[...]
        partial = {2: r_ref, 4: i_ref, 8: a_ref}
        up_proj(0)
        for m0 in range(0, seq, MIX_SUBTILE):
            if m0 + MIX_SUBTILE < seq:
                up_proj(m0 + MIX_SUBTILE)
            for r0 in range(m0, m0 + MIX_SUBTILE, POOL_ROWS):
                u = ext_ref[rows_of(r0, POOL_ROWS), :]
                s, src, width = u, ext_ref, 1
                while width < w:
                    s = s + src[rows_of(r0, POOL_ROWS, width), :]
                    width *= 2
                    if width < w:
                        src = partial[width]
                        src[rows_of(r0, POOL_ROWS), :] = s
                if r0 < w:
                    pos = r0 + lax.broadcasted_iota(jnp.int32, (POOL_ROWS, 1), 0)
                    cnt = jnp.minimum(w, pos + 1).astype(F32)
                else:
                    cnt = float(w)
                xb_ref[r0:r0 + POOL_ROWS, :] = (s / cnt - u).astype(BF16)
            sub = slice(m0, m0 + MIX_SUBTILE)
            y_ref[sub, :] = (_dot(xb_ref[sub, :], wg) * ps).astype(BF16)
        npool_ref[0] = ext_ref[pl.ds(HALO + seq - POOL_BUF, POOL_BUF), :]

    for g, w in enumerate(POOL_WINDOWS):
        pl.when(j == g)(functools.partial(pool_branch, w))

    @pl.when(j >= POOL_GROUPS)
    def _():
        wc = wconv_ref[...]
        bc = bconv_ref[...]
        wrg = wrg_ref[0].astype(BF16)
        wig = wig_ref[0].astype(BF16)
        neg_c_sp = (-LRU_C) * _softplus(-lam_ref[...])
        brg = brg_ref[...]
        big = big_ref[...]
        nb = LRU_ROWS // SUBLANES
        row = lax.broadcasted_iota(jnp.int32, (nb, SUBLANES, CB), 1)
        h_carry = jnp.zeros((1, CB), F32)
        up_proj(0)
        for m0 in range(0, seq, MIX_SUBTILE):
            sub = slice(m0, m0 + MIX_SUBTILE)
            if m0 + MIX_SUBTILE < seq:
                up_proj(m0 + MIX_SUBTILE)
            for r0 in range(m0, m0 + MIX_SUBTILE, LRU_ROWS):
                xc = wc[LRU_CONV - 1:, :] * ext_ref[rows_of(r0, LRU_ROWS), :] + bc
                for k in range(LRU_CONV - 1):
                    xc = xc + wc[k:k + 1, :] * ext_ref[rows_of(r0, LRU_ROWS, LRU_CONV - 1 - k), :]
                a_ref[rows_of(r0, LRU_ROWS), :] = xc
                xb_ref[r0:r0 + LRU_ROWS, :] = xc.astype(BF16)
            r_ref[rows_of(m0, MIX_SUBTILE), :] = _dot(xb_ref[sub, :], wrg)
            i_ref[rows_of(m0, MIX_SUBTILE), :] = _dot(xb_ref[sub, :], wig)
            for r0 in range(m0, m0 + MIX_SUBTILE, LRU_ROWS):
                sl = rows_of(r0, LRU_ROWS)
                a, b = _lru_coeffs(a_ref[sl, :], r_ref[sl, :] + brg, i_ref[sl, :] + big, neg_c_sp)
                a = a.reshape(nb, SUBLANES, CB)
                b = b.reshape(nb, SUBLANES, CB)
                for k in (1, 2, 4):
                    a_sh = jnp.where(row >= k, pltpu.roll(a, k, 1), 1.0)
                    b_sh = jnp.where(row >= k, pltpu.roll(b, k, 1), 0.0)
                    b = b + a * b_sh
                    a = a * a_sh
                a_ref[sl, :] = a.reshape(LRU_ROWS, CB)
                u_ref[sl, :] = b.reshape(LRU_ROWS, CB)
            for r0 in range(m0, m0 + MIX_SUBTILE, SUBLANES):
                sl = rows_of(r0, SUBLANES)
                h8 = a_ref[sl, :] * h_carry + u_ref[sl, :]
                u_ref[sl, :] = h8
                h_carry = h8[SUBLANES - 1:SUBLANES, :]
            y_ref[sub, :] = u_ref[rows_of(m0, MIX_SUBTILE), :].astype(BF16)
        nh_ref[0] = h_carry
        nconv_ref[0] = ext_ref[pl.ds(HALO + seq - (LRU_CONV - 1), LRU_CONV - 1), :]


def _mix_sample_kernel(x_ref, sh_ref, sc_ref, g_ref, win_ref, wgrp_ref, pscale_ref,
                       wconv_ref, bconv_ref, wrg_ref, brg_ref, wig_ref, big_ref, lam_ref,
                       spool_ref, sconv_ref, sh0_ref,
                       y_ref, npool_ref, nconv_ref, nh_ref, h_ref,
                       u_ref, d_ref, *, n_seq, n_t, start):
    j = pl.program_id(1)

    @pl.when(j == 0)
    def _():
        _build_h_sample(x_ref, h_ref, g_ref[...], sc_ref, sh_ref, n_seq, n_t)

    u_ref[...] = _dot(h_ref[...], win_ref[...].astype(BF16))

    def u_slab(t):
        return u_ref[t * n_seq:(t + 1) * n_seq, :]

    def pool_branch(w):
        e = [spool_ref[k * n_seq:(k + 1) * n_seq, :] for k in range(POOL_BUF)]
        e += [u_slab(t) for t in range(n_t)]
        for k in range(POOL_BUF):
            npool_ref[k * n_seq:(k + 1) * n_seq, :] = e[n_t + k]
        for t in range(n_t):
            s = e[POOL_BUF + t]
            for k in range(1, w):
                s = s + e[POOL_BUF + t - k]
            cnt = float(min(w, start + t + 1))
            d_ref[t * n_seq:(t + 1) * n_seq, :] = (s / cnt - e[POOL_BUF + t]).astype(BF16)
        y = _dot(d_ref[...], wgrp_ref[0].astype(BF16)) * pscale_ref[...]
        y_ref[...] = y.astype(BF16)

    for g, w in enumerate(POOL_WINDOWS):
        pl.when(j == g)(functools.partial(pool_branch, w))

    @pl.when(j >= POOL_GROUPS)
    def _():
        nk = LRU_CONV - 1
        e = [sconv_ref[k * n_seq:(k + 1) * n_seq, :] for k in range(nk)]
        e += [u_slab(t) for t in range(n_t)]
        for k in range(nk):
            nconv_ref[k * n_seq:(k + 1) * n_seq, :] = e[n_t + k]
        wc = wconv_ref[...]
        bc = bconv_ref[...]
        wrg = wrg_ref[0].astype(BF16)
        wig = wig_ref[0].astype(BF16)
        neg_c_sp = (-LRU_C) * _softplus(-lam_ref[...])
        h = sh0_ref[...]
        for t in range(n_t):
            xc = bc
            for k in range(LRU_CONV):
                xc = xc + wc[k:k + 1, :] * e[t + k]
            xb = xc.astype(BF16)
            a, b = _lru_coeffs(xc, _dot(xb, wrg) + brg_ref[...], _dot(xb, wig) + big_ref[...],
                               neg_c_sp)
            h = a * h + b
            y_ref[t * n_seq:(t + 1) * n_seq, :] = h.astype(BF16)
        nh_ref[...] = h


def _mix_weight_specs():
    pj = lambda j: jnp.minimum(j, POOL_GROUPS - 1)
    lj = lambda j: jnp.maximum(j - POOL_GROUPS, 0)
    return [
        pl.BlockSpec((1, D_MODEL), lambda q, j: (0, 0)),
        pl.BlockSpec((D_MODEL, CB), lambda q, j: (0, j)),
        pl.BlockSpec((1, CB, CB), lambda q, j: (pj(j), 0, 0)),
        pl.BlockSpec((1, CB), lambda q, j: (0, pj(j))),
        pl.BlockSpec((LRU_CONV, CB), lambda q, j: (0, lj(j))),
        pl.BlockSpec((1, CB), lambda q, j: (0, lj(j))),
        pl.BlockSpec((1, CB, CB), lambda q, j: (lj(j), 0, 0)),
        pl.BlockSpec((1, CB), lambda q, j: (0, lj(j))),
        pl.BlockSpec((1, CB, CB), lambda q, j: (lj(j), 0, 0)),
        pl.BlockSpec((1, CB), lambda q, j: (0, lj(j))),
        pl.BlockSpec((1, CB), lambda q, j: (0, lj(j))),
    ], pj, lj


def _mix_prompt(x2, ada, p, n_seq, seq, ada_row_block):
    wspecs, pj, lj = _mix_weight_specs()
    in_specs = [
        pl.BlockSpec(memory_space=pl.ANY),
        pl.BlockSpec((SUBLANES, D_MODEL), lambda q, j: (ada_row_block, 0)),
        pl.BlockSpec((SUBLANES, D_MODEL), lambda q, j: (ada_row_block, 1)),
    ] + wspecs
    d_ff = p["w_ffn_down"].shape[0]
    dn_rows = d_ff // (n_seq * N_MIX_BLOCKS)
    assert dn_rows * n_seq * N_MIX_BLOCKS == d_ff and dn_rows % BF16_ROWS == 0
    slab_spec = pl.BlockSpec((dn_rows, D_MODEL), lambda q, j: (q * N_MIX_BLOCKS + j, 0))
    in_specs.append(slab_spec)
    out_specs = [
        pl.BlockSpec((seq, CB), lambda q, j: (q, j)),
        pl.BlockSpec((1, POOL_BUF, CB), lambda q, j: (q, 0, pj(j))),
        pl.BlockSpec((1, LRU_CONV - 1, CB), lambda q, j: (q, 0, lj(j))),
        pl.BlockSpec((1, 1, CB), lambda q, j: (q, 0, lj(j))),
        pl.BlockSpec((seq, D_MODEL), lambda q, j: (q, 0)),
        slab_spec,
    ]
    out_shape = [
        jax.ShapeDtypeStruct((n_seq * seq, N_MIX_BLOCKS * CB), BF16),
        jax.ShapeDtypeStruct((n_seq, POOL_BUF, POOL_WIDTH), F32),
        jax.ShapeDtypeStruct((n_seq, LRU_CONV - 1, LRU_WIDTH), F32),
        jax.ShapeDtypeStruct((n_seq, 1, LRU_WIDTH), F32),
        jax.ShapeDtypeStruct((n_seq * seq, D_MODEL), BF16),
        jax.ShapeDtypeStruct((d_ff, D_MODEL), BF16),
    ]
    scratch = [
        pltpu.VMEM((seq, D_MODEL), F32),
        pltpu.SemaphoreType.DMA(()),
        pltpu.VMEM((HALO + seq, CB), F32),
        pltpu.VMEM((seq, CB), BF16),
        pltpu.VMEM((HALO + seq, CB), F32),
        pltpu.VMEM((HALO + seq, CB), F32),
        pltpu.VMEM((HALO + seq, CB), F32),
        pltpu.VMEM((HALO + seq, CB), F32),
    ]
    return pl.pallas_call(
        functools.partial(_mix_prompt_kernel, seq=seq, n_seq=n_seq),
        grid=(n_seq, N_MIX_BLOCKS),
        in_specs=in_specs, out_specs=out_specs, out_shape=out_shape,
        scratch_shapes=scratch, compiler_params=_cparams(2), name="mix_prompt",
    )(x2, ada, ada, p["g_pre1"], p["w_in"], p["w_pool_grp"], p["pool_scale"],
      p["w_lru_conv"], p["b_lru_conv"], p["w_rg"], p["b_rg"], p["w_ig"], p["b_ig"],
      p["lru_lambda"], p["w_ffn_down"])


def _mix_sample(x2, ada, p, spool2, sconv2, sh0, n_seq, n_t, start):
    wspecs, pj, lj = _mix_weight_specs()
    rows = n_seq * n_t
    in_specs = [
        pl.BlockSpec((rows, D_MODEL), lambda q, j: (0, 0), pipeline_mode=pl.Buffered(1)),
        pl.BlockSpec((n_seq, D_MODEL), lambda q, j: (0, 0)),
        pl.BlockSpec((n_seq, D_MODEL), lambda q, j: (0, 1)),
    ] + wspecs + [
        pl.BlockSpec((n_seq * POOL_BUF, CB), lambda q, j: (0, pj(j))),
        pl.BlockSpec((n_seq * (LRU_CONV - 1), CB), lambda q, j: (0, lj(j))),
        pl.BlockSpec((n_seq, CB), lambda q, j: (0, lj(j))),
    ]
    out_specs = [
        pl.BlockSpec((rows, CB), lambda q, j: (0, j)),
        pl.BlockSpec((n_seq * POOL_BUF, CB), lambda q, j: (0, pj(j))),
        pl.BlockSpec((n_seq * (LRU_CONV - 1), CB), lambda q, j: (0, lj(j))),
        pl.BlockSpec((n_seq, CB), lambda q, j: (0, lj(j))),
        pl.BlockSpec((rows, D_MODEL), lambda q, j: (0, 0)),
    ]
    out_shape = [
        jax.ShapeDtypeStruct((rows, N_MIX_BLOCKS * CB), BF16),
        jax.ShapeDtypeStruct((n_seq * POOL_BUF, POOL_WIDTH), F32),
        jax.ShapeDtypeStruct((n_seq * (LRU_CONV - 1), LRU_WIDTH), F32),
        jax.ShapeDtypeStruct((n_seq, LRU_WIDTH), F32),
        jax.ShapeDtypeStruct((rows, D_MODEL), BF16),
    ]
    scratch = [
        pltpu.VMEM((rows, CB), F32),
        pltpu.VMEM((rows, CB), BF16),
    ]
    return pl.pallas_call(
        functools.partial(_mix_sample_kernel, n_seq=n_seq, n_t=n_t, start=start),
        grid=(1, N_MIX_BLOCKS),
        in_specs=in_specs, out_specs=out_specs, out_shape=out_shape,
        scratch_shapes=scratch, compiler_params=_cparams(2), name="mix_sample",
    )(x2, ada, ada, p["g_pre1"], p["w_in"], p["w_pool_grp"], p["pool_scale"],
      p["w_lru_conv"], p["b_lru_conv"], p["w_rg"], p["b_rg"], p["w_ig"], p["b_ig"],
      p["lru_lambda"], spool2, sconv2, sh0)


MERGE_ROWS = 64
MERGE_SUBTILE = 256
CAST_ROWS = 32
BF16_ROWS = 16


def _cast_ffn_up_slab(wup_ref, wupb_ref):
    n_blocks, _, cb2 = wupb_ref.shape
    cb = cb2 // 2
    for blk in range(n_blocks):
        wupb_ref[blk, :, 0:cb] = wup_ref[:, blk * cb:(blk + 1) * cb].astype(BF16)
        wupb_ref[blk, :, cb:cb2] = wup_ref[:, (n_blocks + blk) * cb:(n_blocks + blk + 1) * cb].astype(BF16)


def _cast_slab(w_ref, wb_ref):
    for r0 in range(0, w_ref.shape[0], CAST_ROWS):
        wb_ref[r0:r0 + CAST_ROWS, :] = w_ref[r0:r0 + CAST_ROWS, :].astype(BF16)


def _merge_kernel(h_ref, y_ref, wgp_ref, wgl_ref, wpu_ref, wlu_ref, *rest, tm, cast_ffn):
    if cast_ffn:
        wup_ref, m_ref, wupb_ref, wg_ref, g_ref, pu_ref, lu_ref = rest
    else:
        m_ref, wg_ref, g_ref, pu_ref, lu_ref = rest
    wg_ref[:, 0:CB] = wgp_ref[...].astype(BF16)
    wg_ref[:, CB:2 * CB] = wgl_ref[...].astype(BF16)
    wpu = wpu_ref[...].astype(BF16)
    wlu = wlu_ref[...].astype(BF16)

    def slot(m0):
        s0 = (m0 // MERGE_SUBTILE) % 2 * MERGE_SUBTILE
        return slice(s0, s0 + MERGE_SUBTILE)

    def proj(m0):
        rows = slice(m0, m0 + MERGE_SUBTILE)
        g_ref[slot(m0), :] = _dot(h_ref[rows, :], wg_ref[...])
        pu_ref[slot(m0), :] = _dot(y_ref[rows, 0:POOL_WIDTH], wpu)
        lu_ref[slot(m0), :] = _dot(y_ref[rows, POOL_WIDTH:POOL_WIDTH + LRU_WIDTH], wlu)

    proj(0)
    for m0 in range(0, tm, MERGE_SUBTILE):
        if m0 + MERGE_SUBTILE < tm:
            proj(m0 + MERGE_SUBTILE)
        s0 = slot(m0).start
        for r0 in range(0, MERGE_SUBTILE, MERGE_ROWS):
            sl = slice(s0 + r0, s0 + r0 + MERGE_ROWS)
            m = (jax.nn.sigmoid(g_ref[sl, 0:CB]) * pu_ref[sl, :]
                 + jax.nn.sigmoid(g_ref[sl, CB:2 * CB]) * lu_ref[sl, :])
            m_ref[m0 + r0:m0 + r0 + MERGE_ROWS, :] = m.astype(BF16)
    if cast_ffn:
        _cast_ffn_up_slab(wup_ref, wupb_ref)


def _merge(h, y, p, tm, name, cast_ffn_cb=0):
    rows = h.shape[0]
    n_col = D_MODEL // CB
    n_gate0 = N_MIX_BLOCKS
    n_gate1 = N_MIX_BLOCKS + n_col
    in_specs = [
        pl.BlockSpec((tm, D_MODEL), lambda i, c: (i, 0), pipeline_mode=pl.Buffered(1)),
        pl.BlockSpec((tm, N_MIX_BLOCKS * CB), lambda i, c: (i, 0), pipeline_mode=pl.Buffered(1)),
        pl.BlockSpec((D_MODEL, CB), lambda i, c: (0, n_gate0 + c)),
        pl.BlockSpec((D_MODEL, CB), lambda i, c: (0, n_gate1 + c)),
        pl.BlockSpec((POOL_WIDTH, CB), lambda i, c: (0, c)),
        pl.BlockSpec((LRU_WIDTH, CB), lambda i, c: (0, c)),
    ]
    out_specs = [pl.BlockSpec((tm, CB), lambda i, c: (i, c))]
    out_shape = [jax.ShapeDtypeStruct((rows, D_MODEL), BF16)]
    args = [h, y, p["w_in"], p["w_in"], p["w_pool_up"], p["w_lru_up"]]
    if cast_ffn_cb:
        n_steps = (rows // tm) * n_col
        d_ff = p["w_ffn_down"].shape[0]
        n_blocks = d_ff // cast_ffn_cb
        up_rows = D_MODEL // n_steps
        assert up_rows * n_steps == D_MODEL and up_rows % BF16_ROWS == 0
        step = lambda i, c: i * n_col + c
        in_specs += [pl.BlockSpec((up_rows, 2 * d_ff), lambda i, c: (step(i, c), 0))]
        out_specs += [pl.BlockSpec((n_blocks, up_rows, 2 * cast_ffn_cb),
                                   lambda i, c: (0, step(i, c), 0))]
        out_shape += [jax.ShapeDtypeStruct((n_blocks, D_MODEL, 2 * cast_ffn_cb), BF16)]
        args += [p["w_ffn_up"]]
    ring = 2 * MERGE_SUBTILE
    return pl.pallas_call(
        functools.partial(_merge_kernel, tm=tm, cast_ffn=bool(cast_ffn_cb)),
        grid=(rows // tm, n_col),
        in_specs=in_specs, out_specs=out_specs, out_shape=out_shape,
        scratch_shapes=[pltpu.VMEM((D_MODEL, 2 * CB), BF16), pltpu.VMEM((ring, 2 * CB), F32),
                        pltpu.VMEM((ring, CB), F32), pltpu.VMEM((ring, CB), F32)],
        compiler_params=_cparams(2), name=name,
    )(*args)


def _cast_weight(w_ref, wb_ref):
    rows = w_ref.shape[0]
    for r0 in range(0, rows, 256):
        wb_ref[r0:r0 + 256, :] = w_ref[r0:r0 + 256, :].astype(BF16)


def _outproj_kernel(m_ref, x_ref, gate_ref, g_ref, w_ref, o_ref, wb_ref, acc_ref,
                    *, tm, tiles_per_seq, sample_t):
    i = pl.program_id(0)

    @pl.when(i == 0)
    def _():
        _cast_weight(w_ref, wb_ref)

    g = g_ref[...]
    if sample_t:
        n_seq = tm // sample_t
    else:
        gg_tile = g * gate_ref[pl.ds(i // tiles_per_seq, 1), :]

    def proj(m0):
        acc_ref[m0:m0 + OUT_SUBTILE, :] = _dot(m_ref[m0:m0 + OUT_SUBTILE, :], wb_ref[...])

    proj(0)
    for m0 in range(0, tm, OUT_SUBTILE):
        if m0 + OUT_SUBTILE < tm:
            proj(m0 + OUT_SUBTILE)
        for r0 in range(m0, m0 + OUT_SUBTILE, NORM_ROWS):
            sl = slice(r0, r0 + NORM_ROWS)
            if sample_t:
                s0 = r0 % n_seq
                gg = g * gate_ref[s0:s0 + NORM_ROWS, :]
            else:
                gg = gg_tile
            o_ref[sl, :] = x_ref[sl, :] + _unit_rms(acc_ref[sl, :]) * gg


def _outproj(m, x2, ada, p, tm, tiles_per_seq, ada_row_block, sample_t):
    rows = x2.shape[0]
    if sample_t:
        gate_spec = pl.BlockSpec((tm // sample_t, D_MODEL), lambda i: (0, 2))
    else:
        gate_spec = pl.BlockSpec((SUBLANES, D_MODEL), lambda i: (ada_row_block, 2))
    return pl.pallas_call(
        functools.partial(_outproj_kernel, tm=tm, tiles_per_seq=tiles_per_seq, sample_t=sample_t),
        grid=(rows // tm,),
        in_specs=[pl.BlockSpec((tm, D_MODEL), lambda i: (i, 0)),
                  pl.BlockSpec((tm, D_MODEL), lambda i: (i, 0)),
                  gate_spec,
                  pl.BlockSpec((1, D_MODEL), lambda i: (0, 0)),
                  pl.BlockSpec((D_MODEL, D_MODEL), lambda i: (0, 0), pipeline_mode=pl.Buffered(1))],
        out_specs=pl.BlockSpec((tm, D_MODEL), lambda i: (i, 0)),
        out_shape=jax.ShapeDtypeStruct((rows, D_MODEL), F32),
        scratch_shapes=[pltpu.VMEM((D_MODEL, D_MODEL), BF16), pltpu.VMEM((tm, D_MODEL), F32)],
        compiler_params=_cparams(1),
        name="outproj_sample" if sample_t else "outproj_prompt",
    )(m, x2, ada, p["g_post1"], p["w_out"])


GELU_C0 = 0.7978845608028654
GELU_C1 = GELU_C0 * 0.044715
FFN_ROWS = 32
FFN_SUBTILE = 256
FFN_AHEAD = 2
FFN_CB = 768


def _ffn_conv_gate(eg, ev, wcg, wcv, bcg, bcv):
    last = FFN_CONV - 1
    g = wcg[last:, :] * eg[last] + bcg
    v = wcv[last:, :] * ev[last] + bcv
    for k in range(last):
        g = g + wcg[k:k + 1, :] * eg[k]
        v = v + wcv[k:k + 1, :] * ev[k]
    t = jnp.tanh(g * (GELU_C0 + GELU_C1 * (g * g)))
    return ((g * v) * (0.5 + 0.5 * t)).astype(BF16)


def _ffn_prompt_kernel(x_ref, sh_ref, sc_ref, gate_ref, gpre_ref, gpost_ref,
                       wup_ref, wcg_ref, wcv_ref, bcg_ref, bcv_ref, wdn_ref,
                       o_ref, nst_ref,
                       h_ref, ext_ref, f_ref, carry_ref, *, tm, tiles_per_seq, n_blocks, cb):
    i = pl.program_id(0)
    c = pl.program_id(1)
    q = i // tiles_per_seq
    first = (i % tiles_per_seq) == 0
    nk = FFN_CONV - 1

    @pl.when(c == 0)
    def _():
        _build_h_prompt(x_ref, h_ref, gpre_ref[...], 1.0 + sc_ref[pl.ds(q, 1), :],
                        sh_ref[pl.ds(q, 1), :], tm)
        o_ref[...] = jnp.zeros((tm, D_MODEL), F32)

    @pl.when(first)
    def _():
        ext_ref[0:SUBLANES, :] = jnp.zeros((SUBLANES, 2 * cb), F32)

    @pl.when(jnp.logical_not(first))
    def _():
        ext_ref[0:SUBLANES, :] = carry_ref[c]

    wcg = wcg_ref[...]
    wcv = wcv_ref[...]
    bcg = bcg_ref[...]
    bcv = bcv_ref[...]

    def up_proj(m0):
        ext_ref[SUBLANES + m0:SUBLANES + m0 + FFN_SUBTILE, :] = _dot(
            h_ref[m0:m0 + FFN_SUBTILE, :], wup_ref[0])

    for m0 in range(0, min(FFN_AHEAD * FFN_SUBTILE, tm), FFN_SUBTILE):
        up_proj(m0)
    for m0 in range(0, tm, FFN_SUBTILE):
        rows = slice(m0, m0 + FFN_SUBTILE)
        for r0 in range(m0, m0 + FFN_SUBTILE, FFN_ROWS):
            shifted = [slice(SUBLANES + r0 - nk + k, SUBLANES + r0 - nk + k + FFN_ROWS)
                       for k in range(FFN_CONV)]
            eg = [ext_ref[s, 0:cb] for s in shifted]
            ev = [ext_ref[s, cb:2 * cb] for s in shifted]
            f_ref[r0:r0 + FFN_ROWS, :] = _ffn_conv_gate(eg, ev, wcg, wcv, bcg, bcv)
        if m0 + FFN_AHEAD * FFN_SUBTILE < tm:
            up_proj(m0 + FFN_AHEAD * FFN_SUBTILE)
        o_ref[rows, :] += _dot(f_ref[rows, :], wdn_ref[...])

    carry_ref[c] = ext_ref[tm:tm + SUBLANES, :]
    nst_ref[0, :, 0, :] = ext_ref[pl.ds(SUBLANES + tm - nk, nk), 0:cb]
    nst_ref[0, :, 1, :] = ext_ref[pl.ds(SUBLANES + tm - nk, nk), cb:2 * cb]

    @pl.when(c == n_blocks - 1)
    def _():
        _residual_norm_prompt(x_ref, o_ref, o_ref, gpost_ref[...], gate_ref[pl.ds(q, 1), :], tm)


def _ffn_sample_kernel(x_ref, sh_ref, sc_ref, gate_ref, gpre_ref, gpost_ref,
                       wup_ref, wcg_ref, wcv_ref, bcg_ref, bcv_ref, wdn_ref,
                       sg_ref, sv_ref,
                       o_ref, nst_ref,
                       h_ref, ext_ref, f_ref, *, n_seq, n_t, n_blocks, cb):
    c = pl.program_id(1)
    nk = FFN_CONV - 1

    @pl.when(c == 0)
    def _():
        _build_h_sample(x_ref, h_ref, gpre_ref[...], sc_ref, sh_ref, n_seq, n_t)
        o_ref[...] = jnp.zeros((n_seq * n_t, D_MODEL), F32)

    wcg = wcg_ref[...]
    wcv = wcv_ref[...]
    bcg = bcg_ref[...]
    bcv = bcv_ref[...]

    def view(state_ref, half, t, s0, n):
        if t < 0:
            return state_ref[(nk + t) * n_seq + s0:(nk + t) * n_seq + s0 + n, :]
        return ext_ref[t * n_seq + s0:t * n_seq + s0 + n, half * cb:(half + 1) * cb]

    t_sub = max(FFN_SUBTILE // n_seq, 1)

    def up_proj(t0):
        rows = slice(t0 * n_seq, (t0 + t_sub) * n_seq)
        ext_ref[rows, :] = _dot(h_ref[rows, :], wup_ref[0])

    for t0 in range(0, min(FFN_AHEAD * t_sub, n_t), t_sub):
        up_proj(t0)
    for t0 in range(0, n_t, t_sub):
        rows = slice(t0 * n_seq, (t0 + t_sub) * n_seq)
        for t in range(t0, t0 + t_sub):
            for s0 in range(0, n_seq, FFN_ROWS):
                eg = [view(sg_ref, 0, t - nk + k, s0, FFN_ROWS) for k in range(FFN_CONV)]
                ev = [view(sv_ref, 1, t - nk + k, s0, FFN_ROWS) for k in range(FFN_CONV)]
                f_ref[t * n_seq + s0:t * n_seq + s0 + FFN_ROWS, :] = _ffn_conv_gate(
                    eg, ev, wcg, wcv, bcg, bcv)
        if t0 + FFN_AHEAD * t_sub < n_t:
            up_proj(t0 + FFN_AHEAD * t_sub)
        o_ref[rows, :] += _dot(f_ref[rows, :], wdn_ref[...])

    for k in range(nk):
        nst_ref[:, k, 0, :] = view(sg_ref, 0, n_t - nk + k, 0, n_seq)
        nst_ref[:, k, 1, :] = view(sv_ref, 1, n_t - nk + k, 0, n_seq)

    @pl.when(c == n_blocks - 1)
    def _():
        g = gpost_ref[...]
        for s0 in range(0, n_seq, NORM_ROWS):
            gg = g * gate_ref[s0:s0 + NORM_ROWS, :]
            for t in range(n_t):
                sl = slice(t * n_seq + s0, t * n_seq + s0 + NORM_ROWS)
                o_ref[sl, :] = x_ref[sl, :] + _unit_rms(o_ref[sl, :]) * gg


def _ffn_weight_specs(n_blocks, cb):
    return [
        pl.BlockSpec((1, D_MODEL), lambda i, c: (0, 0)),
        pl.BlockSpec((1, D_MODEL), lambda i, c: (0, 0)),
        pl.BlockSpec((1, D_MODEL, 2 * cb), lambda i, c: (c, 0, 0)),
        pl.BlockSpec((FFN_CONV, cb), lambda i, c: (0, c)),
        pl.BlockSpec((FFN_CONV, cb), lambda i, c: (0, n_blocks + c)),
        pl.BlockSpec((1, cb), lambda i, c: (0, c)),
        pl.BlockSpec((1, cb), lambda i, c: (0, n_blocks + c)),
        pl.BlockSpec((cb, D_MODEL), lambda i, c: (c, 0)),
    ]


def _ffn_prompt(x1, ada, p, wup_b, wdn_b, n_seq, tm, tiles_per_seq, ada_row_block):
    rows = x1.shape[0]
    n_blocks, _, cb2 = wup_b.shape
    cb = cb2 // 2
    ada_spec = lambda k: pl.BlockSpec((SUBLANES, D_MODEL), lambda i, c: (ada_row_block, k))
    in_specs = [pl.BlockSpec((tm, D_MODEL), lambda i, c: (i, 0), pipeline_mode=pl.Buffered(1)),
                ada_spec(3), ada_spec(4), ada_spec(5)] + _ffn_weight_specs(n_blocks, cb)
    out_specs = [
        pl.BlockSpec((tm, D_MODEL), lambda i, c: (i, 0)),
        pl.BlockSpec((1, FFN_CONV - 1, 2, cb), lambda i, c: (i, 0, 0, c)),
    ]
    out_shape = [jax.ShapeDtypeStruct((rows, D_MODEL), F32),
                 jax.ShapeDtypeStruct((rows // tm, FFN_CONV - 1, 2, n_blocks * cb), F32)]
    scratch = [
        pltpu.VMEM((tm, D_MODEL), BF16),
        pltpu.VMEM((SUBLANES + tm, 2 * cb), F32),
        pltpu.VMEM((tm, cb), BF16),
        pltpu.VMEM((n_blocks, SUBLANES, 2 * cb), F32),
    ]
    return pl.pallas_call(
        functools.partial(_ffn_prompt_kernel, tm=tm, tiles_per_seq=tiles_per_seq,
                          n_blocks=n_blocks, cb=cb),
        grid=(rows // tm, n_blocks),
        in_specs=in_specs, out_specs=out_specs, out_shape=out_shape,
        scratch_shapes=scratch, compiler_params=_cparams(2), name="ffn_prompt",
    )(x1, ada, ada, ada, p["g_pre2"], p["g_post2"], wup_b, p["w_ffn_conv"], p["w_ffn_conv"],
      p["b_ffn_conv"], p["b_ffn_conv"], wdn_b)


def _ffn_sample(x1, ada, p, wup_b, wdn_b, sffn2, n_seq, n_t):
    rows = n_seq * n_t
    n_blocks, _, cb2 = wup_b.shape
    cb = cb2 // 2
    nk = FFN_CONV - 1
    ada_spec = lambda k: pl.BlockSpec((n_seq, D_MODEL), lambda i, c: (0, k))
    in_specs = [pl.BlockSpec((rows, D_MODEL), lambda i, c: (0, 0), pipeline_mode=pl.Buffered(1)),
                ada_spec(3), ada_spec(4), ada_spec(5)] + _ffn_weight_specs(n_blocks, cb) + [
        pl.BlockSpec((n_seq * nk, cb), lambda i, c: (0, c)),
        pl.BlockSpec((n_seq * nk, cb), lambda i, c: (0, n_blocks + c)),
    ]
    out_specs = [
        pl.BlockSpec((rows, D_MODEL), lambda i, c: (0, 0)),
        pl.BlockSpec((n_seq, nk, 2, cb), lambda i, c: (0, 0, 0, c)),
    ]
    out_shape = [jax.ShapeDtypeStruct((rows, D_MODEL), F32),
                 jax.ShapeDtypeStruct((n_seq, nk, 2, n_blocks * cb), F32)]
    scratch = [
        pltpu.VMEM((rows, D_MODEL), BF16),
        pltpu.VMEM((rows, 2 * cb), F32),
        pltpu.VMEM((rows, cb), BF16),
    ]
    return pl.pallas_call(
        functools.partial(_ffn_sample_kernel, n_seq=n_seq, n_t=n_t, n_blocks=n_blocks, cb=cb),
        grid=(1, n_blocks),
        in_specs=in_specs, out_specs=out_specs, out_shape=out_shape,
        scratch_shapes=scratch, compiler_params=_cparams(2), name="ffn_sample",
    )(x1, ada, ada, ada, p["g_pre2"], p["g_post2"], wup_b, p["w_ffn_conv"], p["w_ffn_conv"],
      p["b_ffn_conv"], p["b_ffn_conv"], wdn_b, sffn2, sffn2)


TOKEN_TILE = 1024
MERGE_TILE = 2048
OUTPROJ_TILE = 512


def kernel(x_prompt, x_sample, c_prompt, c_sample, state_pool, state_lru_conv, state_lru_h, state_ffn_conv, w_ada, b_ada, g_pre1, g_post1, g_pre2, g_post2, w_in, w_pool_grp, pool_scale, w_lru_conv, b_lru_conv, w_rg, b_rg, w_ig, b_ig, lru_lambda, w_pool_up, w_lru_up, w_out, w_ffn_up, w_ffn_conv, b_ffn_conv, w_ffn_down):
    batch, seq, d = x_prompt.shape
    dec_batch, dec_seq, _ = x_sample.shape
    depth = w_ada.shape[0]
    assert d == D_MODEL and dec_batch % SUBLANES == 0 and seq % TOKEN_TILE == 0
    assert w_in.shape[2] == N_MIX_BLOCKS * CB + 2 * D_MODEL

    pad = (-batch) % SUBLANES
    c_all = jnp.concatenate([c_sample, c_prompt, jnp.zeros((pad, d), c_prompt.dtype)], axis=0)
    prompt_row_block = dec_batch // SUBLANES

    vec_names = ("g_pre1", "g_post1", "g_pre2", "g_post2", "pool_scale", "b_lru_conv", "b_rg",
                 "b_ig", "lru_lambda", "b_ffn_conv")
    weights = dict(w_ada=w_ada, b_ada=b_ada, g_pre1=g_pre1, g_post1=g_post1, g_pre2=g_pre2,
                   g_post2=g_post2, w_in=w_in, w_pool_grp=w_pool_grp, pool_scale=pool_scale,
                   w_lru_conv=w_lru_conv, b_lru_conv=b_lru_conv, w_rg=w_rg, b_rg=b_rg, w_ig=w_ig,
                   b_ig=b_ig, lru_lambda=lru_lambda, w_pool_up=w_pool_up, w_lru_up=w_lru_up,
                   w_out=w_out, w_ffn_up=w_ffn_up, w_ffn_conv=w_ffn_conv, b_ffn_conv=b_ffn_conv,
                   w_ffn_down=w_ffn_down)

    def time_major(a):
        return jnp.swapaxes(a, 0, 1).reshape(-1, a.shape[-1])

    def seq_major(a2, n_rows):
        return jnp.swapaxes(a2.reshape(n_rows, dec_batch, -1), 0, 1)

    xp = x_prompt.reshape(batch * seq, d)
    xs = time_major(x_sample)
    tps = seq // TOKEN_TILE
    outs_p = ([], [], [], [])
    outs_s = ([], [], [], [])
    for l in range(depth):
        p = {k: v[l] for k, v in weights.items()}
        for k in vec_names + ("b_ada",):
            p[k] = p[k].reshape(1, -1)
        ada = _ada(c_all, p["w_ada"], p["b_ada"])

        y, npool, nconv, nh, h, wdn_b = _mix_prompt(xp, ada, p, batch, seq, prompt_row_block)
        m, wup_b = _merge(h, y, p, MERGE_TILE, "merge_prompt", cast_ffn_cb=FFN_CB)
        x1 = _outproj(m, xp, ada, p, OUTPROJ_TILE, seq // OUTPROJ_TILE, prompt_row_block, 0)
        xp, nffn = _ffn_prompt(x1, ada, p, wup_b, wdn_b, batch, TOKEN_TILE, tps, prompt_row_block)
        outs_p[0].append(npool)
        outs_p[1].append(nconv)
        outs_p[2].append(nh.reshape(batch, LRU_WIDTH))
        outs_p[3].append(nffn[tps - 1::tps].reshape(batch, FFN_CONV - 1, -1))

        rows_s = dec_batch * dec_seq
        y, npool, nconv, nh, h = _mix_sample(
            xs, ada, p, time_major(state_pool[l]), time_major(state_lru_conv[l]),
            state_lru_h[l], dec_batch, dec_seq, PAST_LEN)
        m, = _merge(h, y, p, rows_s, "merge_sample")
        x1 = _outproj(m, xs, ada, p, rows_s, 1, 0, dec_seq)
        xs, nffn = _ffn_sample(x1, ada, p, wup_b, wdn_b, time_major(state_ffn_conv[l]),
                               dec_batch, dec_seq)
        outs_s[0].append(seq_major(npool, POOL_BUF))
        outs_s[1].append(seq_major(nconv, LRU_CONV - 1))
        outs_s[2].append(nh)
        outs_s[3].append(nffn.reshape(dec_batch, FFN_CONV - 1, -1))

    return (xp.reshape(batch, seq, d), seq_major(xs, dec_seq),
            jnp.stack(outs_p[0]), jnp.stack(outs_p[1]), jnp.stack(outs_p[2]), jnp.stack(outs_p[3]),
            jnp.stack(outs_s[0]), jnp.stack(outs_s[1]), jnp.stack(outs_s[2]), jnp.stack(outs_s[3]))
```

```python
import functools

import jax
import jax.numpy as jnp
from jax import lax
from jax.experimental import pallas as pl
from jax.experimental.pallas import tpu as pltpu

F32 = jnp.float32
BF16 = jnp.bfloat16

D_MODEL = 2048
POOL_WINDOWS = (2, 4, 8, 16)
POOL_GROUPS = len(POOL_WINDOWS)
POOL_BUF = max(POOL_WINDOWS) - 1
LRU_CONV = 4
LRU_C = 8.0
PAST_LEN = 16384
FFN_CONV = 3
N_ADA = 6
EPS = 1e-6

CB = 256
POOL_WIDTH = POOL_GROUPS * CB
LRU_BLOCKS = 8
LRU_WIDTH = LRU_BLOCKS * CB
N_MIX_BLOCKS = POOL_GROUPS + LRU_BLOCKS
CHUNK = 64
HALO = CHUNK
SUBLANES = 8
V7X_VMEM_BYTES = 64 * 1024 * 1024
VMEM_LIMIT = V7X_VMEM_BYTES - 4 * 1024 * 1024


def _cparams(n_axes):
    return pltpu.CompilerParams(
        dimension_semantics=("arbitrary",) * n_axes, vmem_limit_bytes=VMEM_LIMIT)


def _dot(a, b):
    return jnp.dot(a, b, preferred_element_type=F32)


def _softplus(z):
    return jnp.maximum(z, 0.0) + jnp.log1p(jnp.exp(-jnp.abs(z)))


def _lru_coeffs(xc, r_pre, i_pre, neg_c_sp):
    r = jax.nn.sigmoid(r_pre)
    i = jax.nn.sigmoid(i_pre)
    log_a = r * neg_c_sp
    a = jnp.exp(log_a)
    m2 = -jnp.tanh(log_a) * (a * a + 1.0)
    root = jnp.where(m2 > 0.0, m2 * lax.rsqrt(m2), 0.0)
    return a, root * (i * xc)


ADA_K = 256
ADA_N = 1024


def _ada_kernel(c_ref, w_ref, b_ref, o_ref):
    rows, n_out = o_ref.shape

    @pl.when(pl.program_id(0) == 0)
    def _():
        o_ref[...] = jnp.broadcast_to(b_ref[...], (rows, n_out))

    c = c_ref[...]
    s = (c * jax.nn.sigmoid(c)).astype(BF16)
    for n0 in range(0, n_out, ADA_N):
        o_ref[:, n0:n0 + ADA_N] += _dot(s, w_ref[:, n0:n0 + ADA_N].astype(BF16))


def _ada(c_all, w_ada, b_ada):
    rows = c_all.shape[0]
    n_out = N_ADA * D_MODEL
    return pl.pallas_call(
        _ada_kernel,
        grid=(D_MODEL // ADA_K,),
        in_specs=[pl.BlockSpec((rows, ADA_K), lambda k: (0, k)),
                  pl.BlockSpec((ADA_K, n_out), lambda k: (k, 0)),
                  pl.BlockSpec((1, n_out), lambda k: (0, 0))],
        out_specs=pl.BlockSpec((rows, n_out), lambda k: (0, 0)),
        out_shape=jax.ShapeDtypeStruct((rows, n_out), F32),
        compiler_params=_cparams(1),
        name="ada",
    )(c_all, w_ada, b_ada)


NORM_ROWS = 32
NORM_UNROLL = 8
OUT_SUBTILE = 256


def _unit_rms(x):
    return x * lax.rsqrt(jnp.mean(x * x, axis=-1, keepdims=True) + EPS)


def _build_h_prompt(x_ref, h_ref, g, scale1p, shift, n_rows):
    gs = g * scale1p

    def body(i, carry):
        r0 = pl.multiple_of(i * NORM_ROWS, NORM_ROWS)
        x = x_ref[pl.ds(r0, NORM_ROWS), :]
        h_ref[pl.ds(r0, NORM_ROWS), :] = (_unit_rms(x) * gs + shift).astype(BF16)
        return carry
    lax.fori_loop(0, n_rows // NORM_ROWS, body, 0, unroll=NORM_UNROLL)


def _residual_norm_prompt(x_ref, acc_ref, o_ref, g, gate, n_rows):
    gg = g * gate
    for r0 in range(0, n_rows, NORM_ROWS):
        sl = slice(r0, r0 + NORM_ROWS)
        o_ref[sl, :] = x_ref[sl, :] + _unit_rms(acc_ref[sl, :]) * gg


def _build_h_sample(x_ref, h_ref, g, sc_ref, sh_ref, n_seq, n_t):
    for s0 in range(0, n_seq, NORM_ROWS):
        gs = g * (1.0 + sc_ref[s0:s0 + NORM_ROWS, :])
        shift = sh_ref[s0:s0 + NORM_ROWS, :]
        for t in range(n_t):
            sl = slice(t * n_seq + s0, t * n_seq + s0 + NORM_ROWS)
            h_ref[sl, :] = (_unit_rms(x_ref[sl, :]) * gs + shift).astype(BF16)


MIX_SUBTILE = 512
POOL_ROWS = 128


def _mix_prompt_kernel(x_hbm, sh_ref, sc_ref, g_ref, win_ref, wgrp_ref, pscale_ref,
                       wconv_ref, bconv_ref, wrg_ref, brg_ref, wig_ref, big_ref, lam_ref, wdn_ref,
                       y_ref, npool_ref, nconv_ref, nh_ref, h_ref, wdnb_ref,
                       x_ref, x_sem, ext_ref, xb_ref, r_ref, i_ref, a_ref, *, seq, n_seq):
    q = pl.program_id(0)
    j = pl.program_id(1)
    _cast_slab(wdn_ref, wdnb_ref)

    def x_copy(s):
        rows = pl.ds(pl.multiple_of(s * seq, seq), seq)
        return pltpu.make_async_copy(x_hbm.at[rows, :], x_ref, x_sem)

    @pl.when(jnp.logical_and(q == 0, j == 0))
    def _():
        x_copy(0).start()

    @pl.when(jnp.logical_and(j == 1, q + 1 < n_seq))
    def _():
        x_copy(q + 1).start()

    @pl.when(j == 0)
    def _():
        x_copy(q).wait()
        _build_h_prompt(x_ref, h_ref, g_ref[...], 1.0 + sc_ref[pl.ds(q, 1), :],
                        sh_ref[pl.ds(q, 1), :], seq)
        for ref in (ext_ref, r_ref, i_ref, a_ref):
            ref[0:HALO, :] = jnp.zeros((HALO, CB), F32)

    win = win_ref[...].astype(BF16)

    def rows_of(r0, n, shift=0):
        return slice(HALO + r0 - shift, HALO + r0 - shift + n)

    def up_proj(m0):
        ext_ref[rows_of(m0, MIX_SUBTILE), :] = _dot(h_ref[m0:m0 + MIX_SUBTILE, :], win)

    def pool_branch(w):
        wg = wgrp_ref[0].astype(BF16)
        ps = pscale_ref[...]
        partial = {2: r_ref, 4: i_ref, 8: a_ref}
        up_proj(0)
        for m0 in range(0, seq, MIX_SUBTILE):
            if m0 + MIX_SUBTILE < seq:
                up_proj(m0 + MIX_SUBTILE)
            for r0 in range(m0, m0 + MIX_SUBTILE, POOL_ROWS):
                u = ext_ref[rows_of(r0, POOL_ROWS), :]
                s, src, width = u, ext_ref, 1
                while width < w:
                    s = s + src[rows_of(r0, POOL_ROWS, width), :]
                    width *= 2
                    if width < w:
                        src = partial[width]
                        src[rows_of(r0, POOL_ROWS), :] = s
                if r0 < w:
                    pos = r0 + lax.broadcasted_iota(jnp.int32, (POOL_ROWS, 1), 0)
                    cnt = jnp.minimum(w, pos + 1).astype(F32)
                else:
                    cnt = float(w)
                xb_ref[r0:r0 + POOL_ROWS, :] = (s / cnt - u).astype(BF16)
            sub = slice(m0, m0 + MIX_SUBTILE)
            y_ref[sub, :] = (_dot(xb_ref[sub, :], wg) * ps).astype(BF16)
        npool_ref[0] = ext_ref[pl.ds(HALO + seq - POOL_BUF, POOL_BUF), :]

    for g, w in enumerate(POOL_WINDOWS):
        pl.when(j == g)(functools.partial(pool_branch, w))

    @pl.when(j >= POOL_GROUPS)
    def _():
        wc = wconv_ref[...]
        bc = bconv_ref[...]
        wrg = wrg_ref[0].astype(BF16)
        wig = wig_ref[0].astype(BF16)
        neg_c_sp = (-LRU_C) * _softplus(-lam_ref[...])
        brg = brg_ref[...]
        big = big_ref[...]
        nk = LRU_CONV - 1
        group = lax.broadcasted_iota(jnp.int32, (SUBLANES, CB), 0)
        w_tap = [jnp.broadcast_to(wc[k:k + 1, :], (SUBLANES, CB)) for k in range(LRU_CONV)]
        b_tap = jnp.broadcast_to(bc, (SUBLANES, CB))

        def regroup(v):
            return jnp.swapaxes(v.reshape(SUBLANES, SUBLANES, CB), 0, 1).reshape(CHUNK, CB)

        def split(v):
            return [v[SUBLANES * p:SUBLANES * (p + 1), :] for p in range(SUBLANES)]

        def group_before(cur, prev):
            return jnp.where(group == 0, pltpu.roll(prev, 1, 0), pltpu.roll(cur, 1, 0))

        h_carry = jnp.zeros((1, CB), F32)
        up_proj(0)
        for m0 in range(0, seq, MIX_SUBTILE):
            sub = slice(m0, m0 + MIX_SUBTILE)
            if m0 + MIX_SUBTILE < seq:
                up_proj(m0 + MIX_SUBTILE)
            else:
                nconv_ref[0] = ext_ref[pl.ds(HALO + seq - nk, nk), :]
            for r0 in range(m0, m0 + MIX_SUBTILE, CHUNK):
                ext_ref[rows_of(r0, CHUNK), :] = regroup(ext_ref[rows_of(r0, CHUNK), :])
            for r0 in range(m0, m0 + MIX_SUBTILE, CHUNK):
                cur = split(ext_ref[rows_of(r0, CHUNK), :])
                prev_tail = split(ext_ref[rows_of(r0, CHUNK, CHUNK), :])[SUBLANES - nk:]
                wrapped = [group_before(cur[SUBLANES - nk + t], prev_tail[t]) for t in range(nk)]
                rows = []
                for p in range(SUBLANES):
                    xc = w_tap[nk] * cur[p] + b_tap
                    for k in range(1, LRU_CONV):
                        src = cur[p - k] if p >= k else wrapped[nk + p - k]
                        xc = xc + w_tap[nk - k] * src
                    rows.append(xc)
                xc = jnp.concatenate(rows, axis=0)
                a_ref[rows_of(r0, CHUNK), :] = xc
                xb_ref[r0:r0 + CHUNK, :] = xc.astype(BF16)
            r_ref[rows_of(m0, MIX_SUBTILE), :] = _dot(xb_ref[sub, :], wrg)
            i_ref[rows_of(m0, MIX_SUBTILE), :] = _dot(xb_ref[sub, :], wig)
            for r0 in range(m0, m0 + MIX_SUBTILE, CHUNK):
                sl = rows_of(r0, CHUNK)
                a, b = _lru_coeffs(a_ref[sl, :], r_ref[sl, :] + brg, i_ref[sl, :] + big, neg_c_sp)
                a, b = split(a), split(b)
                hs, ps = [b[0]], [a[0]]
                for p in range(1, SUBLANES):
                    hs.append(a[p] * hs[-1] + b[p])
                    ps.append(a[p] * ps[-1])
                e_p, e_h = ps[-1], hs[-1]
                for k in (1, 2, 4):
                    p_sh = jnp.where(group >= k, pltpu.roll(e_p, k, 0), 1.0)
                    h_sh = jnp.where(group >= k, pltpu.roll(e_h, k, 0), 0.0)
                    e_h = e_h + e_p * h_sh
                    e_p = e_p * p_sh
                ends = e_p * h_carry + e_h
                h_in = jnp.where(group == 0, h_carry, pltpu.roll(ends, 1, 0))
                h = jnp.concatenate([hs[p] + ps[p] * h_in for p in range(SUBLANES)], axis=0)
                y_ref[r0:r0 + CHUNK, :] = regroup(h).astype(BF16)
                h_carry = ends[SUBLANES - 1:SUBLANES, :]
        nh_ref[0] = h_carry


def _mix_sample_kernel(x_ref, sh_ref, sc_ref, g_ref, win_ref, wgrp_ref, pscale_ref,
                       wconv_ref, bconv_ref, wrg_ref, brg_ref, wig_ref, big_ref, lam_ref,
                       spool_ref, sconv_ref, sh0_ref,
                       y_ref, npool_ref, nconv_ref, nh_ref, h_ref,
                       u_ref, d_ref, *, n_seq, n_t, start):
    j = pl.program_id(1)

    @pl.when(j == 0)
    def _():
        _build_h_sample(x_ref, h_ref, g_ref[...], sc_ref, sh_ref, n_seq, n_t)

    u_ref[...] = _dot(h_ref[...], win_ref[...].astype(BF16))

    def u_slab(t):
        return u_ref[t * n_seq:(t + 1) * n_seq, :]

    def pool_branch(w):
        e = [spool_ref[k * n_seq:(k + 1) * n_seq, :] for k in range(POOL_BUF)]
        e += [u_slab(t) for t in range(n_t)]
        for k in range(POOL_BUF):
            npool_ref[k * n_seq:(k + 1) * n_seq, :] = e[n_t + k]
        for t in range(n_t):
            s = e[POOL_BUF + t]
            for k in range(1, w):
                s = s + e[POOL_BUF + t - k]
            cnt = float(min(w, start + t + 1))
            d_ref[t * n_seq:(t + 1) * n_seq, :] = (s / cnt - e[POOL_BUF + t]).astype(BF16)
        y = _dot(d_ref[...], wgrp_ref[0].astype(BF16)) * pscale_ref[...]
        y_ref[...] = y.astype(BF16)

    for g, w in enumerate(POOL_WINDOWS):
        pl.when(j == g)(functools.partial(pool_branch, w))

    @pl.when(j >= POOL_GROUPS)
    def _():
        nk = LRU_CONV - 1
        e = [sconv_ref[k * n_seq:(k + 1) * n_seq, :] for k in range(nk)]
        e += [u_slab(t) for t in range(n_t)]
        for k in range(nk):
            nconv_ref[k * n_seq:(k + 1) * n_seq, :] = e[n_t + k]
        wc = wconv_ref[...]
        bc = bconv_ref[...]
        wrg = wrg_ref[0].astype(BF16)
        wig = wig_ref[0].astype(BF16)
        neg_c_sp = (-LRU_C) * _softplus(-lam_ref[...])
        h = sh0_ref[...]
        for t in range(n_t):
            xc = bc
            for k in range(LRU_CONV):
                xc = xc + wc[k:k + 1, :] * e[t + k]
            xb = xc.astype(BF16)
            a, b = _lru_coeffs(xc, _dot(xb, wrg) + brg_ref[...], _dot(xb, wig) + big_ref[...],
                               neg_c_sp)
            h = a * h + b
            y_ref[t * n_seq:(t + 1) * n_seq, :] = h.astype(BF16)
        nh_ref[...] = h


def _mix_weight_specs():
    pj = lambda j: jnp.minimum(j, POOL_GROUPS - 1)
    lj = lambda j: jnp.maximum(j - POOL_GROUPS, 0)
    return [
        pl.BlockSpec((1, D_MODEL), lambda q, j: (0, 0)),
        pl.BlockSpec((D_MODEL, CB), lambda q, j: (0, j)),
        pl.BlockSpec((1, CB, CB), lambda q, j: (pj(j), 0, 0)),
        pl.BlockSpec((1, CB), lambda q, j: (0, pj(j))),
        pl.BlockSpec((LRU_CONV, CB), lambda q, j: (0, lj(j))),
        pl.BlockSpec((1, CB), lambda q, j: (0, lj(j))),
        pl.BlockSpec((1, CB, CB), lambda q, j: (lj(j), 0, 0)),
        pl.BlockSpec((1, CB), lambda q, j: (0, lj(j))),
        pl.BlockSpec((1, CB, CB), lambda q, j: (lj(j), 0, 0)),
        pl.BlockSpec((1, CB), lambda q, j: (0, lj(j))),
        pl.BlockSpec((1, CB), lambda q, j: (0, lj(j))),
    ], pj, lj


def _mix_prompt(x2, ada, p, n_seq, seq, ada_row_block):
    wspecs, pj, lj = _mix_weight_specs()
    in_specs = [
        pl.BlockSpec(memory_space=pl.ANY),
        pl.BlockSpec((SUBLANES, D_MODEL), lambda q, j: (ada_row_block, 0)),
        pl.BlockSpec((SUBLANES, D_MODEL), lambda q, j: (ada_row_block, 1)),
    ] + wspecs
    d_ff = p["w_ffn_down"].shape[0]
    dn_rows = d_ff // (n_seq * N_MIX_BLOCKS)
    assert dn_rows * n_seq * N_MIX_BLOCKS == d_ff and dn_rows % BF16_ROWS == 0
    slab_spec = pl.BlockSpec((dn_rows, D_MODEL), lambda q, j: (q * N_MIX_BLOCKS + j, 0))
    in_specs.append(slab_spec)
    out_specs = [
        pl.BlockSpec((seq, CB), lambda q, j: (q, j)),
        pl.BlockSpec((1, POOL_BUF, CB), lambda q, j: (q, 0, pj(j))),
        pl.BlockSpec((1, LRU_CONV - 1, CB), lambda q, j: (q, 0, lj(j))),
        pl.BlockSpec((1, 1, CB), lambda q, j: (q, 0, lj(j))),
        pl.BlockSpec((seq, D_MODEL), lambda q, j: (q, 0)),
        slab_spec,
    ]
    out_shape = [
        jax.ShapeDtypeStruct((n_seq * seq, N_MIX_BLOCKS * CB), BF16),
        jax.ShapeDtypeStruct((n_seq, POOL_BUF, POOL_WIDTH), F32),
        jax.ShapeDtypeStruct((n_seq, LRU_CONV - 1, LRU_WIDTH), F32),
        jax.ShapeDtypeStruct((n_seq, 1, LRU_WIDTH), F32),
        jax.ShapeDtypeStruct((n_seq * seq, D_MODEL), BF16),
        jax.ShapeDtypeStruct((d_ff, D_MODEL), BF16),
    ]
    scratch = [
        pltpu.VMEM((seq, D_MODEL), F32),
        pltpu.SemaphoreType.DMA(()),
        pltpu.VMEM((HALO + seq, CB), F32),
        pltpu.VMEM((seq, CB), BF16),
        pltpu.VMEM((HALO + seq, CB), F32),
        pltpu.VMEM((HALO + seq, CB), F32),
        pltpu.VMEM((HALO + seq, CB), F32),
    ]
    return pl.pallas_call(
        functools.partial(_mix_prompt_kernel, seq=seq, n_seq=n_seq),
        grid=(n_seq, N_MIX_BLOCKS),
        in_specs=in_specs, out_specs=out_specs, out_shape=out_shape,
        scratch_shapes=scratch, compiler_params=_cparams(2), name="mix_prompt",
    )(x2, ada, ada, p["g_pre1"], p["w_in"], p["w_pool_grp"], p["pool_scale"],
      p["w_lru_conv"], p["b_lru_conv"], p["w_rg"], p["b_rg"], p["w_ig"], p["b_ig"],
      p["lru_lambda"], p["w_ffn_down"])


def _mix_sample(x2, ada, p, spool2, sconv2, sh0, n_seq, n_t, start):
    wspecs, pj, lj = _mix_weight_specs()
    rows = n_seq * n_t
    in_specs = [
        pl.BlockSpec((rows, D_MODEL), lambda q, j: (0, 0), pipeline_mode=pl.Buffered(1)),
        pl.BlockSpec((n_seq, D_MODEL), lambda q, j: (0, 0)),
        pl.BlockSpec((n_seq, D_MODEL), lambda q, j: (0, 1)),
    ] + wspecs + [
        pl.BlockSpec((n_seq * POOL_BUF, CB), lambda q, j: (0, pj(j))),
        pl.BlockSpec((n_seq * (LRU_CONV - 1), CB), lambda q, j: (0, lj(j))),
        pl.BlockSpec((n_seq, CB), lambda q, j: (0, lj(j))),
    ]
    out_specs = [
        pl.BlockSpec((rows, CB), lambda q, j: (0, j)),
        pl.BlockSpec((n_seq * POOL_BUF, CB), lambda q, j: (0, pj(j))),
        pl.BlockSpec((n_seq * (LRU_CONV - 1), CB), lambda q, j: (0, lj(j))),
        pl.BlockSpec((n_seq, CB), lambda q, j: (0, lj(j))),
        pl.BlockSpec((rows, D_MODEL), lambda q, j: (0, 0)),
    ]
    out_shape = [
        jax.ShapeDtypeStruct((rows, N_MIX_BLOCKS * CB), BF16),
        jax.ShapeDtypeStruct((n_seq * POOL_BUF, POOL_WIDTH), F32),
        jax.ShapeDtypeStruct((n_seq * (LRU_CONV - 1), LRU_WIDTH), F32),
        jax.ShapeDtypeStruct((n_seq, LRU_WIDTH), F32),
        jax.ShapeDtypeStruct((rows, D_MODEL), BF16),
    ]
    scratch = [
        pltpu.VMEM((rows, CB), F32),
        pltpu.VMEM((rows, CB), BF16),
    ]
    return pl.pallas_call(
        functools.partial(_mix_sample_kernel, n_seq=n_seq, n_t=n_t, start=start),
        grid=(1, N_MIX_BLOCKS),
        in_specs=in_specs, out_specs=out_specs, out_shape=out_shape,
        scratch_shapes=scratch, compiler_params=_cparams(2), name="mix_sample",
    )(x2, ada, ada, p["g_pre1"], p["w_in"], p["w_pool_grp"], p["pool_scale"],
      p["w_lru_conv"], p["b_lru_conv"], p["w_rg"], p["b_rg"], p["w_ig"], p["b_ig"],
      p["lru_lambda"], spool2, sconv2, sh0)


MERGE_ROWS = 64
MERGE_SUBTILE = 256
CAST_ROWS = 32
BF16_ROWS = 16


def _cast_ffn_up_slab(wup_ref, wupb_ref):
    n_blocks, _, cb2 = wupb_ref.shape
    cb = cb2 // 2
    for blk in range(n_blocks):
        wupb_ref[blk, :, 0:cb] = wup_ref[:, blk * cb:(blk + 1) * cb].astype(BF16)
        wupb_ref[blk, :, cb:cb2] = wup_ref[:, (n_blocks + blk) * cb:(n_blocks + blk + 1) * cb].astype(BF16)


def _cast_slab(w_ref, wb_ref):
    for r0 in range(0, w_ref.shape[0], CAST_ROWS):
        wb_ref[r0:r0 + CAST_ROWS, :] = w_ref[r0:r0 + CAST_ROWS, :].astype(BF16)


def _merge_kernel(h_ref, y_ref, wgp_ref, wgl_ref, wpu_ref, wlu_ref, *rest, tm, cast_ffn):
    if cast_ffn:
        wup_ref, m_ref, wupb_ref, wg_ref, g_ref, pu_ref, lu_ref = rest
    else:
        m_ref, wg_ref, g_ref, pu_ref, lu_ref = rest
    wg_ref[:, 0:CB] = wgp_ref[...].astype(BF16)
    wg_ref[:, CB:2 * CB] = wgl_ref[...].astype(BF16)
    wpu = wpu_ref[...].astype(BF16)
    wlu = wlu_ref[...].astype(BF16)

    def slot(m0):
        s0 = (m0 // MERGE_SUBTILE) % 2 * MERGE_SUBTILE
        return slice(s0, s0 + MERGE_SUBTILE)

    def proj(m0):
        rows = slice(m0, m0 + MERGE_SUBTILE)
        g_ref[slot(m0), :] = _dot(h_ref[rows, :], wg_ref[...])
        pu_ref[slot(m0), :] = _dot(y_ref[rows, 0:POOL_WIDTH], wpu)
        lu_ref[slot(m0), :] = _dot(y_ref[rows, POOL_WIDTH:POOL_WIDTH + LRU_WIDTH], wlu)

    proj(0)
    for m0 in range(0, tm, MERGE_SUBTILE):
        if m0 + MERGE_SUBTILE < tm:
            proj(m0 + MERGE_SUBTILE)
        s0 = slot(m0).start
        for r0 in range(0, MERGE_SUBTILE, MERGE_ROWS):
            sl = slice(s0 + r0, s0 + r0 + MERGE_ROWS)
            m = (jax.nn.sigmoid(g_ref[sl, 0:CB]) * pu_ref[sl, :]
                 + jax.nn.sigmoid(g_ref[sl, CB:2 * CB]) * lu_ref[sl, :])
            m_ref[m0 + r0:m0 + r0 + MERGE_ROWS, :] = m.astype(BF16)
    if cast_ffn:
        _cast_ffn_up_slab(wup_ref, wupb_ref)


def _merge(h, y, p, tm, name, cast_ffn_cb=0):
    rows = h.shape[0]
    n_col = D_MODEL // CB
    n_gate0 = N_MIX_BLOCKS
    n_gate1 = N_MIX_BLOCKS + n_col
    in_specs = [
        pl.BlockSpec((tm, D_MODEL), lambda i, c: (i, 0), pipeline_mode=pl.Buffered(1)),
        pl.BlockSpec((tm, N_MIX_BLOCKS * CB), lambda i, c: (i, 0), pipeline_mode=pl.Buffered(1)),
        pl.BlockSpec((D_MODEL, CB), lambda i, c: (0, n_gate0 + c)),
        pl.BlockSpec((D_MODEL, CB), lambda i, c: (0, n_gate1 + c)),
        pl.BlockSpec((POOL_WIDTH, CB), lambda i, c: (0, c)),
        pl.BlockSpec((LRU_WIDTH, CB), lambda i, c: (0, c)),
    ]
    out_specs = [pl.BlockSpec((tm, CB), lambda i, c: (i, c))]
    out_shape = [jax.ShapeDtypeStruct((rows, D_MODEL), BF16)]
    args = [h, y, p["w_in"], p["w_in"], p["w_pool_up"], p["w_lru_up"]]
    if cast_ffn_cb:
        n_steps = (rows // tm) * n_col
        d_ff = p["w_ffn_down"].shape[0]
        n_blocks = d_ff // cast_ffn_cb
        up_rows = D_MODEL // n_steps
        assert up_rows * n_steps == D_MODEL and up_rows % BF16_ROWS == 0
        step = lambda i, c: i * n_col + c
        in_specs += [pl.BlockSpec((up_rows, 2 * d_ff), lambda i, c: (step(i, c), 0))]
        out_specs += [pl.BlockSpec((n_blocks, up_rows, 2 * cast_ffn_cb),
                                   lambda i, c: (0, step(i, c), 0))]
        out_shape += [jax.ShapeDtypeStruct((n_blocks, D_MODEL, 2 * cast_ffn_cb), BF16)]
        args += [p["w_ffn_up"]]
    ring = 2 * MERGE_SUBTILE
    return pl.pallas_call(
        functools.partial(_merge_kernel, tm=tm, cast_ffn=bool(cast_ffn_cb)),
        grid=(rows // tm, n_col),
        in_specs=in_specs, out_specs=out_specs, out_shape=out_shape,
        scratch_shapes=[pltpu.VMEM((D_MODEL, 2 * CB), BF16), pltpu.VMEM((ring, 2 * CB), F32),
                        pltpu.VMEM((ring, CB), F32), pltpu.VMEM((ring, CB), F32)],
        compiler_params=_cparams(2), name=name,
    )(*args)


def _outproj_kernel(m_ref, x_ref, gate_ref, g_ref, w_ref, o_ref, wb_ref, acc_ref,
                    *, tm, tiles_per_seq, sample_t):
    i = pl.program_id(0)

    @pl.when(i == 0)
    def _():
        _cast_slab(w_ref, wb_ref)

    g = g_ref[...]
    if sample_t:
        n_seq = tm // sample_t
    else:
        gg_tile = g * gate_ref[pl.ds(i // tiles_per_seq, 1), :]

    def proj(m0):
        acc_ref[m0:m0 + OUT_SUBTILE, :] = _dot(m_ref[m0:m0 + OUT_SUBTILE, :], wb_ref[...])

    proj(0)
    for m0 in range(0, tm, OUT_SUBTILE):
        if m0 + OUT_SUBTILE < tm:
            proj(m0 + OUT_SUBTILE)
        for r0 in range(m0, m0 + OUT_SUBTILE, NORM_ROWS):
            sl = slice(r0, r0 + NORM_ROWS)
            if sample_t:
                s0 = r0 % n_seq
                gg = g * gate_ref[s0:s0 + NORM_ROWS, :]
            else:
                gg = gg_tile
            o_ref[sl, :] = x_ref[sl, :] + _unit_rms(acc_ref[sl, :]) * gg


def _outproj(m, x2, ada, p, tm, tiles_per_seq, ada_row_block, sample_t):
    rows = x2.shape[0]
    if sample_t:
        gate_spec = pl.BlockSpec((tm // sample_t, D_MODEL), lambda i: (0, 2))
    else:
        gate_spec = pl.BlockSpec((SUBLANES, D_MODEL), lambda i: (ada_row_block, 2))
    return pl.pallas_call(
        functools.partial(_outproj_kernel, tm=tm, tiles_per_seq=tiles_per_seq, sample_t=sample_t),
        grid=(rows // tm,),
        in_specs=[pl.BlockSpec((tm, D_MODEL), lambda i: (i, 0)),
                  pl.BlockSpec((tm, D_MODEL), lambda i: (i, 0)),
                  gate_spec,
                  pl.BlockSpec((1, D_MODEL), lambda i: (0, 0)),
                  pl.BlockSpec((D_MODEL, D_MODEL), lambda i: (0, 0), pipeline_mode=pl.Buffered(1))],
        out_specs=pl.BlockSpec((tm, D_MODEL), lambda i: (i, 0)),
        out_shape=jax.ShapeDtypeStruct((rows, D_MODEL), F32),
        scratch_shapes=[pltpu.VMEM((D_MODEL, D_MODEL), BF16), pltpu.VMEM((tm, D_MODEL), F32)],
        compiler_params=_cparams(1),
        name="outproj_sample" if sample_t else "outproj_prompt",
    )(m, x2, ada, p["g_post1"], p["w_out"])


GELU_C0 = 0.7978845608028654
GELU_C1 = GELU_C0 * 0.044715
FFN_ROWS = 32
FFN_SUBTILE = 256
FFN_AHEAD = 2
FFN_CB = 768


def _ffn_conv_gate(eg, ev, wcg, wcv, bcg, bcv):
    last = FFN_CONV - 1
    g = wcg[last:, :] * eg[last] + bcg
    v = wcv[last:, :] * ev[last] + bcv
    for k in range(last):
        g = g + wcg[k:k + 1, :] * eg[k]
        v = v + wcv[k:k + 1, :] * ev[k]
    t = jnp.tanh(g * (GELU_C0 + GELU_C1 * (g * g)))
    return ((g * v) * (0.5 + 0.5 * t)).astype(BF16)


def _ffn_prompt_kernel(x_ref, sh_ref, sc_ref, gate_ref, gpre_ref, gpost_ref,
                       wup_ref, wcg_ref, wcv_ref, bcg_ref, bcv_ref, wdn_ref,
                       o_ref, nst_ref,
                       h_ref, ext_ref, f_ref, carry_ref, *, tm, tiles_per_seq, n_blocks, cb):
    i = pl.program_id(0)
    c = pl.program_id(1)
    q = i // tiles_per_seq
    first = (i % tiles_per_seq) == 0
    nk = FFN_CONV - 1

    @pl.when(c == 0)
    def _():
        _build_h_prompt(x_ref, h_ref, gpre_ref[...], 1.0 + sc_ref[pl.ds(q, 1), :],
                        sh_ref[pl.ds(q, 1), :], tm)
        o_ref[...] = jnp.zeros((tm, D_MODEL), F32)

    @pl.when(first)
    def _():
        ext_ref[0:SUBLANES, :] = jnp.zeros((SUBLANES, 2 * cb), F32)

    @pl.when(jnp.logical_not(first))
    def _():
        ext_ref[0:SUBLANES, :] = carry_ref[c]

    wcg = wcg_ref[...]
    wcv = wcv_ref[...]
    bcg = bcg_ref[...]
    bcv = bcv_ref[...]

    def up_proj(m0):
        ext_ref[SUBLANES + m0:SUBLANES + m0 + FFN_SUBTILE, :] = _dot(
            h_ref[m0:m0 + FFN_SUBTILE, :], wup_ref[0])

    for m0 in range(0, min(FFN_AHEAD * FFN_SUBTILE, tm), FFN_SUBTILE):
        up_proj(m0)
    for m0 in range(0, tm, FFN_SUBTILE):
        rows = slice(m0, m0 + FFN_SUBTILE)
        for r0 in range(m0, m0 + FFN_SUBTILE, FFN_ROWS):
            shifted = [slice(SUBLANES + r0 - nk + k, SUBLANES + r0 - nk + k + FFN_ROWS)
                       for k in range(FFN_CONV)]
            eg = [ext_ref[s, 0:cb] for s in shifted]
            ev = [ext_ref[s, cb:2 * cb] for s in shifted]
            f_ref[r0:r0 + FFN_ROWS, :] = _ffn_conv_gate(eg, ev, wcg, wcv, bcg, bcv)
        if m0 + FFN_AHEAD * FFN_SUBTILE < tm:
            up_proj(m0 + FFN_AHEAD * FFN_SUBTILE)
        o_ref[rows, :] += _dot(f_ref[rows, :], wdn_ref[...])

    carry_ref[c] = ext_ref[tm:tm + SUBLANES, :]
    nst_ref[0, :, 0, :] = ext_ref[pl.ds(SUBLANES + tm - nk, nk), 0:cb]
    nst_ref[0, :, 1, :] = ext_ref[pl.ds(SUBLANES + tm - nk, nk), cb:2 * cb]

    @pl.when(c == n_blocks - 1)
    def _():
        _residual_norm_prompt(x_ref, o_ref, o_ref, gpost_ref[...], gate_ref[pl.ds(q, 1), :], tm)


def _ffn_sample_kernel(x_ref, sh_ref, sc_ref, gate_ref, gpre_ref, gpost_ref,
                       wup_ref, wcg_ref, wcv_ref, bcg_ref, bcv_ref, wdn_ref,
                       sg_ref, sv_ref,
                       o_ref, nst_ref,
                       h_ref, ext_ref, f_ref, *, n_seq, n_t, n_blocks, cb):
    c = pl.program_id(1)
    nk = FFN_CONV - 1

    @pl.when(c == 0)
    def _():
        _build_h_sample(x_ref, h_ref, gpre_ref[...], sc_ref, sh_ref, n_seq, n_t)
        o_ref[...] = jnp.zeros((n_seq * n_t, D_MODEL), F32)

    wcg = wcg_ref[...]
    wcv = wcv_ref[...]
    bcg = bcg_ref[...]
    bcv = bcv_ref[...]

    def view(state_ref, half, t, s0, n):
        if t < 0:
            return state_ref[(nk + t) * n_seq + s0:(nk + t) * n_seq + s0 + n, :]
        return ext_ref[t * n_seq + s0:t * n_seq + s0 + n, half * cb:(half + 1) * cb]

    t_sub = max(FFN_SUBTILE // n_seq, 1)

    def up_proj(t0):
        rows = slice(t0 * n_seq, (t0 + t_sub) * n_seq)
        ext_ref[rows, :] = _dot(h_ref[rows, :], wup_ref[0])

    for t0 in range(0, min(FFN_AHEAD * t_sub, n_t), t_sub):
        up_proj(t0)
    for t0 in range(0, n_t, t_sub):
        rows = slice(t0 * n_seq, (t0 + t_sub) * n_seq)
        for t in range(t0, t0 + t_sub):
            for s0 in range(0, n_seq, FFN_ROWS):
                eg = [view(sg_ref, 0, t - nk + k, s0, FFN_ROWS) for k in range(FFN_CONV)]
                ev = [view(sv_ref, 1, t - nk + k, s0, FFN_ROWS) for k in range(FFN_CONV)]
                f_ref[t * n_seq + s0:t * n_seq + s0 + FFN_ROWS, :] = _ffn_conv_gate(
                    eg, ev, wcg, wcv, bcg, bcv)
        if t0 + FFN_AHEAD * t_sub < n_t:
            up_proj(t0 + FFN_AHEAD * t_sub)
        o_ref[rows, :] += _dot(f_ref[rows, :], wdn_ref[...])

    for k in range(nk):
        nst_ref[:, k, 0, :] = view(sg_ref, 0, n_t - nk + k, 0, n_seq)
        nst_ref[:, k, 1, :] = view(sv_ref, 1, n_t - nk + k, 0, n_seq)

    @pl.when(c == n_blocks - 1)
    def _():
        g = gpost_ref[...]
        for s0 in range(0, n_seq, NORM_ROWS):
            gg = g * gate_ref[s0:s0 + NORM_ROWS, :]
            for t in range(n_t):
                sl = slice(t * n_seq + s0, t * n_seq + s0 + NORM_ROWS)
                o_ref[sl, :] = x_ref[sl, :] + _unit_rms(o_ref[sl, :]) * gg


def _ffn_weight_specs(n_blocks, cb):
    return [
        pl.BlockSpec((1, D_MODEL), lambda i, c: (0, 0)),
        pl.BlockSpec((1, D_MODEL), lambda i, c: (0, 0)),
        pl.BlockSpec((1, D_MODEL, 2 * cb), lambda i, c: (c, 0, 0)),
        pl.BlockSpec((FFN_CONV, cb), lambda i, c: (0, c)),
        pl.BlockSpec((FFN_CONV, cb), lambda i, c: (0, n_blocks + c)),
        pl.BlockSpec((1, cb), lambda i, c: (0, c)),
        pl.BlockSpec((1, cb), lambda i, c: (0, n_blocks + c)),
        pl.BlockSpec((cb, D_MODEL), lambda i, c: (c, 0)),
    ]


def _ffn_prompt(x1, ada, p, wup_b, wdn_b, n_seq, tm, tiles_per_seq, ada_row_block):
    rows = x1.shape[0]
    n_blocks, _, cb2 = wup_b.shape
    cb = cb2 // 2
    ada_spec = lambda k: pl.BlockSpec((SUBLANES, D_MODEL), lambda i, c: (ada_row_block, k))
    in_specs = [pl.BlockSpec((tm, D_MODEL), lambda i, c: (i, 0), pipeline_mode=pl.Buffered(1)),
                ada_spec(3), ada_spec(4), ada_spec(5)] + _ffn_weight_specs(n_blocks, cb)
    out_specs = [
        pl.BlockSpec((tm, D_MODEL), lambda i, c: (i, 0)),
        pl.BlockSpec((1, FFN_CONV - 1, 2, cb), lambda i, c: (i, 0, 0, c)),
    ]
    out_shape = [jax.ShapeDtypeStruct((rows, D_MODEL), F32),
                 jax.ShapeDtypeStruct((rows // tm, FFN_CONV - 1, 2, n_blocks * cb), F32)]
    scratch = [
        pltpu.VMEM((tm, D_MODEL), BF16),
        pltpu.VMEM((SUBLANES + tm, 2 * cb), F32),
        pltpu.VMEM((tm, cb), BF16),
        pltpu.VMEM((n_blocks, SUBLANES, 2 * cb), F32),
    ]
    return pl.pallas_call(
        functools.partial(_ffn_prompt_kernel, tm=tm, tiles_per_seq=tiles_per_seq,
                          n_blocks=n_blocks, cb=cb),
        grid=(rows // tm, n_blocks),
        in_specs=in_specs, out_specs=out_specs, out_shape=out_shape,
        scratch_shapes=scratch, compiler_params=_cparams(2), name="ffn_prompt",
    )(x1, ada, ada, ada, p["g_pre2"], p["g_post2"], wup_b, p["w_ffn_conv"], p["w_ffn_conv"],
      p["b_ffn_conv"], p["b_ffn_conv"], wdn_b)


def _ffn_sample(x1, ada, p, wup_b, wdn_b, sffn2, n_seq, n_t):
    rows = n_seq * n_t
    n_blocks, _, cb2 = wup_b.shape
    cb = cb2 // 2
    nk = FFN_CONV - 1
    ada_spec = lambda k: pl.BlockSpec((n_seq, D_MODEL), lambda i, c: (0, k))
    in_specs = [pl.BlockSpec((rows, D_MODEL), lambda i, c: (0, 0), pipeline_mode=pl.Buffered(1)),
                ada_spec(3), ada_spec(4), ada_spec(5)] + _ffn_weight_specs(n_blocks, cb) + [
        pl.BlockSpec((n_seq * nk, cb), lambda i, c: (0, c)),
        pl.BlockSpec((n_seq * nk, cb), lambda i, c: (0, n_blocks + c)),
    ]
    out_specs = [
        pl.BlockSpec((rows, D_MODEL), lambda i, c: (0, 0)),
        pl.BlockSpec((n_seq, nk, 2, cb), lambda i, c: (0, 0, 0, c)),
    ]
    out_shape = [jax.ShapeDtypeStruct((rows, D_MODEL), F32),
                 jax.ShapeDtypeStruct((n_seq, nk, 2, n_blocks * cb), F32)]
    scratch = [
        pltpu.VMEM((rows, D_MODEL), BF16),
        pltpu.VMEM((rows, 2 * cb), F32),
        pltpu.VMEM((rows, cb), BF16),
    ]
    return pl.pallas_call(
        functools.partial(_ffn_sample_kernel, n_seq=n_seq, n_t=n_t, n_blocks=n_blocks, cb=cb),
        grid=(1, n_blocks),
        in_specs=in_specs, out_specs=out_specs, out_shape=out_shape,
        scratch_shapes=scratch, compiler_params=_cparams(2), name="ffn_sample",
    )(x1, ada, ada, ada, p["g_pre2"], p["g_post2"], wup_b, p["w_ffn_conv"], p["w_ffn_conv"],
      p["b_ffn_conv"], p["b_ffn_conv"], wdn_b, sffn2, sffn2)


TOKEN_TILE = 1024
MERGE_TILE = 2048
OUTPROJ_TILE = 512


def kernel(x_prompt, x_sample, c_prompt, c_sample, state_pool, state_lru_conv, state_lru_h, state_ffn_conv, w_ada, b_ada, g_pre1, g_post1, g_pre2, g_post2, w_in, w_pool_grp, pool_scale, w_lru_conv, b_lru_conv, w_rg, b_rg, w_ig, b_ig, lru_lambda, w_pool_up, w_lru_up, w_out, w_ffn_up, w_ffn_conv, b_ffn_conv, w_ffn_down):
    batch, seq, d = x_prompt.shape
    dec_batch, dec_seq, _ = x_sample.shape
    depth = w_ada.shape[0]
    assert d == D_MODEL and dec_batch % SUBLANES == 0 and seq % TOKEN_TILE == 0
    assert w_in.shape[2] == N_MIX_BLOCKS * CB + 2 * D_MODEL

    pad = (-batch) % SUBLANES
    c_all = jnp.concatenate([c_sample, c_prompt, jnp.zeros((pad, d), c_prompt.dtype)], axis=0)
    prompt_row_block = dec_batch // SUBLANES

    vec_names = ("g_pre1", "g_post1", "g_pre2", "g_post2", "pool_scale", "b_lru_conv", "b_rg",
                 "b_ig", "lru_lambda", "b_ffn_conv")
    weights = dict(w_ada=w_ada, b_ada=b_ada, g_pre1=g_pre1, g_post1=g_post1, g_pre2=g_pre2,
                   g_post2=g_post2, w_in=w_in, w_pool_grp=w_pool_grp, pool_scale=pool_scale,
                   w_lru_conv=w_lru_conv, b_lru_conv=b_lru_conv, w_rg=w_rg, b_rg=b_rg, w_ig=w_ig,
                   b_ig=b_ig, lru_lambda=lru_lambda, w_pool_up=w_pool_up, w_lru_up=w_lru_up,
                   w_out=w_out, w_ffn_up=w_ffn_up, w_ffn_conv=w_ffn_conv, b_ffn_conv=b_ffn_conv,
                   w_ffn_down=w_ffn_down)

    def time_major(a):
        return jnp.swapaxes(a, 0, 1).reshape(-1, a.shape[-1])

    def seq_major(a2, n_rows):
        return jnp.swapaxes(a2.reshape(n_rows, dec_batch, -1), 0, 1)

    xp = x_prompt.reshape(batch * seq, d)
    xs = time_major(x_sample)
    tps = seq // TOKEN_TILE
    outs_p = ([], [], [], [])
    outs_s = ([], [], [], [])
    for l in range(depth):
        p = {k: v[l] for k, v in weights.items()}
        for k in vec_names + ("b_ada",):
            p[k] = p[k].reshape(1, -1)
        ada = _ada(c_all, p["w_ada"], p["b_ada"])

        y, npool, nconv, nh, h, wdn_b = _mix_prompt(xp, ada, p, batch, seq, prompt_row_block)
        m, wup_b = _merge(h, y, p, MERGE_TILE, "merge_prompt", cast_ffn_cb=FFN_CB)
        x1 = _outproj(m, xp, ada, p, OUTPROJ_TILE, seq // OUTPROJ_TILE, prompt_row_block, 0)
        xp, nffn = _ffn_prompt(x1, ada, p, wup_b, wdn_b, batch, TOKEN_TILE, tps, prompt_row_block)
        outs_p[0].append(npool)
        outs_p[1].append(nconv)
        outs_p[2].append(nh.reshape(batch, LRU_WIDTH))
        outs_p[3].append(nffn[tps - 1::tps].reshape(batch, FFN_CONV - 1, -1))

        rows_s = dec_batch * dec_seq
        y, npool, nconv, nh, h = _mix_sample(
            xs, ada, p, time_major(state_pool[l]), time_major(state_lru_conv[l]),
            state_lru_h[l], dec_batch, dec_seq, PAST_LEN)
        m, = _merge(h, y, p, rows_s, "merge_sample")
        x1 = _outproj(m, xs, ada, p, rows_s, 1, 0, dec_seq)
        xs, nffn = _ffn_sample(x1, ada, p, wup_b, wdn_b, time_major(state_ffn_conv[l]),
                               dec_batch, dec_seq)
        outs_s[0].append(seq_major(npool, POOL_BUF))
        outs_s[1].append(seq_major(nconv, LRU_CONV - 1))
        outs_s[2].append(nh)
        outs_s[3].append(nffn.reshape(dec_batch, FFN_CONV - 1, -1))

    return (xp.reshape(batch, seq, d), seq_major(xs, dec_seq),
            jnp.stack(outs_p[0]), jnp.stack(outs_p[1]), jnp.stack(outs_p[2]), jnp.stack(outs_p[3]),
            jnp.stack(outs_s[0]), jnp.stack(outs_s[1]), jnp.stack(outs_s[2]), jnp.stack(outs_s[3]))
```

```python
import functools

import jax
import jax.numpy as jnp
from jax import lax
from jax.experimental import pallas as pl
from jax.experimental.pallas import tpu as pltpu

F32 = jnp.float32
BF16 = jnp.bfloat16

D_MODEL = 2048
POOL_WINDOWS = (2, 4, 8, 16)
POOL_GROUPS = len(POOL_WINDOWS)
POOL_BUF = max(POOL_WINDOWS) - 1
LRU_CONV = 4
LRU_C = 8.0
PAST_LEN = 16384
FFN_CONV = 3
N_ADA = 6
EPS = 1e-6

CB = 256
POOL_WIDTH = POOL_GROUPS * CB
LRU_BLOCKS = 8
LRU_WIDTH = LRU_BLOCKS * CB
N_MIX_BLOCKS = POOL_GROUPS + LRU_BLOCKS
CHUNK = 64
HALO = CHUNK
SUBLANES = 8
V7X_VMEM_BYTES = 64 * 1024 * 1024
VMEM_LIMIT = V7X_VMEM_BYTES - 4 * 1024 * 1024


def _cparams(n_axes):
    return pltpu.CompilerParams(
        dimension_semantics=("arbitrary",) * n_axes, vmem_limit_bytes=VMEM_LIMIT)


def _dot(a, b):
    return jnp.dot(a, b, preferred_element_type=F32)


def _softplus(z):
    return jnp.maximum(z, 0.0) + jnp.log1p(jnp.exp(-jnp.abs(z)))


def _lru_coeffs(xc, r_pre, i_pre, neg_c_sp):
    r = jax.nn.sigmoid(r_pre)
    i = jax.nn.sigmoid(i_pre)
    log_a = r * neg_c_sp
    a = jnp.exp(log_a)
    m2 = -jnp.tanh(log_a) * (a * a + 1.0)
    root = jnp.where(m2 > 0.0, m2 * lax.rsqrt(m2), 0.0)
    return a, root * (i * xc)


ADA_K = 256
ADA_N = 1024


def _ada_kernel(c_ref, w_ref, b_ref, o_ref):
    rows, n_out = o_ref.shape

    @pl.when(pl.program_id(0) == 0)
    def _():
        o_ref[...] = jnp.broadcast_to(b_ref[...], (rows, n_out))

    c = c_ref[...]
    s = (c * jax.nn.sigmoid(c)).astype(BF16)
    for n0 in range(0, n_out, ADA_N):
        o_ref[:, n0:n0 + ADA_N] += _dot(s, w_ref[:, n0:n0 + ADA_N].astype(BF16))


def _ada(c_all, w_ada, b_ada):
    rows = c_all.shape[0]
    n_out = N_ADA * D_MODEL
    return pl.pallas_call(
        _ada_kernel,
        grid=(D_MODEL // ADA_K,),
        in_specs=[pl.BlockSpec((rows, ADA_K), lambda k: (0, k)),
                  pl.BlockSpec((ADA_K, n_out), lambda k: (k, 0)),
                  pl.BlockSpec((1, n_out), lambda k: (0, 0))],
        out_specs=pl.BlockSpec((rows, n_out), lambda k: (0, 0)),
        out_shape=jax.ShapeDtypeStruct((rows, n_out), F32),
        compiler_params=_cparams(1),
        name="ada",
    )(c_all, w_ada, b_ada)


NORM_ROWS = 32
NORM_UNROLL = 8
OUT_SUBTILE = 256


def _unit_rms(x):
    return x * lax.rsqrt(jnp.mean(x * x, axis=-1, keepdims=True) + EPS)


def _build_h_prompt(x_ref, h_ref, g, scale1p, shift, n_rows):
    gs = g * scale1p

    def body(i, carry):
        r0 = pl.multiple_of(i * NORM_ROWS, NORM_ROWS)
        x = x_ref[pl.ds(r0, NORM_ROWS), :]
        h_ref[pl.ds(r0, NORM_ROWS), :] = (_unit_rms(x) * gs + shift).astype(BF16)
        return carry
    lax.fori_loop(0, n_rows // NORM_ROWS, body, 0, unroll=NORM_UNROLL)


def _residual_norm_prompt(x_ref, acc_ref, o_ref, g, gate, n_rows):
    gg = g * gate
    for r0 in range(0, n_rows, NORM_ROWS):
        sl = slice(r0, r0 + NORM_ROWS)
        o_ref[sl, :] = x_ref[sl, :] + _unit_rms(acc_ref[sl, :]) * gg


def _build_h_sample(x_ref, h_ref, g, sc_ref, sh_ref, n_seq, n_t):
    for s0 in range(0, n_seq, NORM_ROWS):
        gs = g * (1.0 + sc_ref[s0:s0 + NORM_ROWS, :])
        shift = sh_ref[s0:s0 + NORM_ROWS, :]
        for t in range(n_t):
            sl = slice(t * n_seq + s0, t * n_seq + s0 + NORM_ROWS)
            h_ref[sl, :] = (_unit_rms(x_ref[sl, :]) * gs + shift).astype(BF16)


MIX_SUBTILE = 512
POOL_ROWS = 128


def _mix_prompt_kernel(x_hbm, sh_ref, sc_ref, g_ref, win_ref, wgrp_ref, pscale_ref,
                       wconv_ref, bconv_ref, wrg_ref, brg_ref, wig_ref, big_ref, lam_ref, wdn_ref,
                       y_ref, npool_ref, nconv_ref, nh_ref, h_ref, wdnb_ref,
                       x_ref, x_sem, ext_ref, xb_ref, r_ref, i_ref, a_ref, *, seq, n_seq):
    q = pl.program_id(0)
    j = pl.program_id(1)
    _cast_slab(wdn_ref, wdnb_ref)

    def x_copy(s):
        rows = pl.ds(pl.multiple_of(s * seq, seq), seq)
        return pltpu.make_async_copy(x_hbm.at[rows, :], x_ref, x_sem)

    @pl.when(jnp.logical_and(q == 0, j == 0))
    def _():
        x_copy(0).start()

    @pl.when(jnp.logical_and(j == 1, q + 1 < n_seq))
    def _():
        x_copy(q + 1).start()

    @pl.when(j == 0)
    def _():
        x_copy(q).wait()
        _build_h_prompt(x_ref, h_ref, g_ref[...], 1.0 + sc_ref[pl.ds(q, 1), :],
                        sh_ref[pl.ds(q, 1), :], seq)
        for ref in (ext_ref, r_ref, i_ref, a_ref):
            ref[0:HALO, :] = jnp.zeros((HALO, CB), F32)

    win = win_ref[...].astype(BF16)

    def rows_of(r0, n, shift=0):
        return slice(HALO + r0 - shift, HALO + r0 - shift + n)

    def up_proj(m0):
        ext_ref[rows_of(m0, MIX_SUBTILE), :] = _dot(h_ref[m0:m0 + MIX_SUBTILE, :], win)

    def pool_branch(w):
        wg = wgrp_ref[0].astype(BF16)
        ps = pscale_ref[...]
        partial = {2: r_ref, 4: i_ref, 8: a_ref}
        up_proj(0)
        for m0 in range(0, seq, MIX_SUBTILE):
            if m0 + MIX_SUBTILE < seq:
                up_proj(m0 + MIX_SUBTILE)
            for r0 in range(m0, m0 + MIX_SUBTILE, POOL_ROWS):
                u = ext_ref[rows_of(r0, POOL_ROWS), :]
                s, src, width = u, ext_ref, 1
                while width < w:
                    s = s + src[rows_of(r0, POOL_ROWS, width), :]
                    width *= 2
                    if width < w:
                        src = partial[width]
                        src[rows_of(r0, POOL_ROWS), :] = s
                if r0 < w:
                    pos = r0 + lax.broadcasted_iota(jnp.int32, (POOL_ROWS, 1), 0)
                    cnt = jnp.minimum(w, pos + 1).astype(F32)
                else:
                    cnt = float(w)
                xb_ref[r0:r0 + POOL_ROWS, :] = (s / cnt - u).astype(BF16)
            sub = slice(m0, m0 + MIX_SUBTILE)
            y_ref[sub, :] = (_dot(xb_ref[sub, :], wg) * ps).astype(BF16)
        npool_ref[0] = ext_ref[pl.ds(HALO + seq - POOL_BUF, POOL_BUF), :]

    for g, w in enumerate(POOL_WINDOWS):
        pl.when(j == g)(functools.partial(pool_branch, w))

    @pl.when(j >= POOL_GROUPS)
    def _():
        wc = wconv_ref[...]
        bc = bconv_ref[...]
        wrg = wrg_ref[0].astype(BF16)
        wig = wig_ref[0].astype(BF16)
        neg_c_sp = (-LRU_C) * _softplus(-lam_ref[...])
        brg = brg_ref[...]
        big = big_ref[...]
        nk = LRU_CONV - 1
        group = lax.broadcasted_iota(jnp.int32, (SUBLANES, CB), 0)
        w_tap = [jnp.broadcast_to(wc[k:k + 1, :], (SUBLANES, CB)) for k in range(LRU_CONV)]
        b_tap = jnp.broadcast_to(bc, (SUBLANES, CB))

        def regroup(v):
            return jnp.swapaxes(v.reshape(SUBLANES, SUBLANES, CB), 0, 1).reshape(CHUNK, CB)

        def split(v):
            return [v[SUBLANES * p:SUBLANES * (p + 1), :] for p in range(SUBLANES)]

        def group_before(cur, prev):
            return jnp.where(group == 0, pltpu.roll(prev, 1, 0), pltpu.roll(cur, 1, 0))

        h_carry = jnp.zeros((1, CB), F32)
        up_proj(0)
        for m0 in range(0, seq, MIX_SUBTILE):
            sub = slice(m0, m0 + MIX_SUBTILE)
            if m0 + MIX_SUBTILE < seq:
                up_proj(m0 + MIX_SUBTILE)
            else:
                nconv_ref[0] = ext_ref[pl.ds(HALO + seq - nk, nk), :]
            for r0 in range(m0, m0 + MIX_SUBTILE, CHUNK):
                ext_ref[rows_of(r0, CHUNK), :] = regroup(ext_ref[rows_of(r0, CHUNK), :])
            for r0 in range(m0, m0 + MIX_SUBTILE, CHUNK):
                cur = split(ext_ref[rows_of(r0, CHUNK), :])
                prev_tail = split(ext_ref[rows_of(r0, CHUNK, CHUNK), :])[SUBLANES - nk:]
                wrapped = [group_before(cur[SUBLANES - nk + t], prev_tail[t]) for t in range(nk)]
                rows = []
                for p in range(SUBLANES):
                    xc = w_tap[nk] * cur[p] + b_tap
                    for k in range(1, LRU_CONV):
                        src = cur[p - k] if p >= k else wrapped[nk + p - k]
                        xc = xc + w_tap[nk - k] * src
                    rows.append(xc)
                xc = jnp.concatenate(rows, axis=0)
                a_ref[rows_of(r0, CHUNK), :] = xc
                xb_ref[r0:r0 + CHUNK, :] = xc.astype(BF16)
            r_ref[rows_of(m0, MIX_SUBTILE), :] = _dot(xb_ref[sub, :], wrg)
            i_ref[rows_of(m0, MIX_SUBTILE), :] = _dot(xb_ref[sub, :], wig)
            for r0 in range(m0, m0 + MIX_SUBTILE, CHUNK):
                sl = rows_of(r0, CHUNK)
                a, b = _lru_coeffs(a_ref[sl, :], r_ref[sl, :] + brg, i_ref[sl, :] + big, neg_c_sp)
                a, b = split(a), split(b)
                hs, ps = [b[0]], [a[0]]
                for p in range(1, SUBLANES):
                    hs.append(a[p] * hs[-1] + b[p])
                    ps.append(a[p] * ps[-1])
                e_p, e_h = ps[-1], hs[-1]
                for k in (1, 2, 4):
                    p_sh = jnp.where(group >= k, pltpu.roll(e_p, k, 0), 1.0)
                    h_sh = jnp.where(group >= k, pltpu.roll(e_h, k, 0), 0.0)
                    e_h = e_h + e_p * h_sh
                    e_p = e_p * p_sh
                ends = e_p * h_carry + e_h
                h_in = jnp.where(group == 0, h_carry, pltpu.roll(ends, 1, 0))
                h = jnp.concatenate([hs[p] + ps[p] * h_in for p in range(SUBLANES)], axis=0)
                y_ref[r0:r0 + CHUNK, :] = regroup(h).astype(BF16)
                h_carry = ends[SUBLANES - 1:SUBLANES, :]
        nh_ref[0] = h_carry


def _mix_sample_kernel(x_ref, sh_ref, sc_ref, g_ref, win_ref, wgrp_ref, pscale_ref,
                       wconv_ref, bconv_ref, wrg_ref, brg_ref, wig_ref, big_ref, lam_ref,
                       spool_ref, sconv_ref, sh0_ref,
                       y_ref, npool_ref, nconv_ref, nh_ref, h_ref,
                       u_ref, d_ref, *, n_seq, n_t, start):
    j = pl.program_id(1)

    @pl.when(j == 0)
    def _():
        _build_h_sample(x_ref, h_ref, g_ref[...], sc_ref, sh_ref, n_seq, n_t)

    u_ref[...] = _dot(h_ref[...], win_ref[...].astype(BF16))

    def u_slab(t):
        return u_ref[t * n_seq:(t + 1) * n_seq, :]

    def pool_branch(w):
        e = [spool_ref[k * n_seq:(k + 1) * n_seq, :] for k in range(POOL_BUF)]
        e += [u_slab(t) for t in range(n_t)]
        for k in range(POOL_BUF):
            npool_ref[k * n_seq:(k + 1) * n_seq, :] = e[n_t + k]
        for t in range(n_t):
            s = e[POOL_BUF + t]
            for k in range(1, w):
                s = s + e[POOL_BUF + t - k]
            cnt = float(min(w, start + t + 1))
            d_ref[t * n_seq:(t + 1) * n_seq, :] = (s / cnt - e[POOL_BUF + t]).astype(BF16)
        y = _dot(d_ref[...], wgrp_ref[0].astype(BF16)) * pscale_ref[...]
        y_ref[...] = y.astype(BF16)

    for g, w in enumerate(POOL_WINDOWS):
        pl.when(j == g)(functools.partial(pool_branch, w))

    @pl.when(j >= POOL_GROUPS)
    def _():
        nk = LRU_CONV - 1
        e = [sconv_ref[k * n_seq:(k + 1) * n_seq, :] for k in range(nk)]
        e += [u_slab(t) for t in range(n_t)]
        for k in range(nk):
            nconv_ref[k * n_seq:(k + 1) * n_seq, :] = e[n_t + k]
        wc = wconv_ref[...]
        bc = bconv_ref[...]
        wrg = wrg_ref[0].astype(BF16)
        wig = wig_ref[0].astype(BF16)
        neg_c_sp = (-LRU_C) * _softplus(-lam_ref[...])
        h = sh0_ref[...]
        for t in range(n_t):
            xc = bc
            for k in range(LRU_CONV):
                xc = xc + wc[k:k + 1, :] * e[t + k]
            xb = xc.astype(BF16)
            a, b = _lru_coeffs(xc, _dot(xb, wrg) + brg_ref[...], _dot(xb, wig) + big_ref[...],
                               neg_c_sp)
            h = a * h + b
            y_ref[t * n_seq:(t + 1) * n_seq, :] = h.astype(BF16)
        nh_ref[...] = h


def _mix_weight_specs():
    pj = lambda j: jnp.minimum(j, POOL_GROUPS - 1)
    lj = lambda j: jnp.maximum(j - POOL_GROUPS, 0)
    return [
        pl.BlockSpec((1, D_MODEL), lambda q, j: (0, 0)),
        pl.BlockSpec((D_MODEL, CB), lambda q, j: (0, j)),
        pl.BlockSpec((1, CB, CB), lambda q, j: (pj(j), 0, 0)),
        pl.BlockSpec((1, CB), lambda q, j: (0, pj(j))),
        pl.BlockSpec((LRU_CONV, CB), lambda q, j: (0, lj(j))),
        pl.BlockSpec((1, CB), lambda q, j: (0, lj(j))),
        pl.BlockSpec((1, CB, CB), lambda q, j: (lj(j), 0, 0)),
        pl.BlockSpec((1, CB), lambda q, j: (0, lj(j))),
        pl.BlockSpec((1, CB, CB), lambda q, j: (lj(j), 0, 0)),
        pl.BlockSpec((1, CB), lambda q, j: (0, lj(j))),
        pl.BlockSpec((1, CB), lambda q, j: (0, lj(j))),
    ], pj, lj


def _mix_prompt(x2, ada, p, n_seq, seq, ada_row_block):
    wspecs, pj, lj = _mix_weight_specs()
    in_specs = [
        pl.BlockSpec(memory_space=pl.ANY),
        pl.BlockSpec((SUBLANES, D_MODEL), lambda q, j: (ada_row_block, 0)),
        pl.BlockSpec((SUBLANES, D_MODEL), lambda q, j: (ada_row_block, 1)),
    ] + wspecs
    d_ff = p["w_ffn_down"].shape[0]
    dn_rows = d_ff // (n_seq * N_MIX_BLOCKS)
    assert dn_rows * n_seq * N_MIX_BLOCKS == d_ff and dn_rows % BF16_ROWS == 0
    slab_spec = pl.BlockSpec((dn_rows, D_MODEL), lambda q, j: (q * N_MIX_BLOCKS + j, 0))
    in_specs.append(slab_spec)
    out_specs = [
        pl.BlockSpec((seq, CB), lambda q, j: (q, j)),
        pl.BlockSpec((1, POOL_BUF, CB), lambda q, j: (q, 0, pj(j))),
        pl.BlockSpec((1, LRU_CONV - 1, CB), lambda q, j: (q, 0, lj(j))),
        pl.BlockSpec((1, 1, CB), lambda q, j: (q, 0, lj(j))),
        pl.BlockSpec((seq, D_MODEL), lambda q, j: (q, 0)),
        slab_spec,
    ]
    out_shape = [
        jax.ShapeDtypeStruct((n_seq * seq, N_MIX_BLOCKS * CB), BF16),
        jax.ShapeDtypeStruct((n_seq, POOL_BUF, POOL_WIDTH), F32),
        jax.ShapeDtypeStruct((n_seq, LRU_CONV - 1, LRU_WIDTH), F32),
        jax.ShapeDtypeStruct((n_seq, 1, LRU_WIDTH), F32),
        jax.ShapeDtypeStruct((n_seq * seq, D_MODEL), BF16),
        jax.ShapeDtypeStruct((d_ff, D_MODEL), BF16),
    ]
    scratch = [
        pltpu.VMEM((seq, D_MODEL), F32),
        pltpu.SemaphoreType.DMA(()),
        pltpu.VMEM((HALO + seq, CB), F32),
        pltpu.VMEM((seq, CB), BF16),
        pltpu.VMEM((HALO + seq, CB), F32),
        pltpu.VMEM((HALO + seq, CB), F32),
        pltpu.VMEM((HALO + seq, CB), F32),
    ]
    return pl.pallas_call(
        functools.partial(_mix_prompt_kernel, seq=seq, n_seq=n_seq),
        grid=(n_seq, N_MIX_BLOCKS),
        in_specs=in_specs, out_specs=out_specs, out_shape=out_shape,
        scratch_shapes=scratch, compiler_params=_cparams(2), name="mix_prompt",
    )(x2, ada, ada, p["g_pre1"], p["w_in"], p["w_pool_grp"], p["pool_scale"],
      p["w_lru_conv"], p["b_lru_conv"], p["w_rg"], p["b_rg"], p["w_ig"], p["b_ig"],
      p["lru_lambda"], p["w_ffn_down"])


def _mix_sample(x2, ada, p, spool2, sconv2, sh0, n_seq, n_t, start):
    wspecs, pj, lj = _mix_weight_specs()
    rows = n_seq * n_t
    in_specs = [
        pl.BlockSpec((rows, D_MODEL), lambda q, j: (0, 0), pipeline_mode=pl.Buffered(1)),
        pl.BlockSpec((n_seq, D_MODEL), lambda q, j: (0, 0)),
        pl.BlockSpec((n_seq, D_MODEL), lambda q, j: (0, 1)),
    ] + wspecs + [
        pl.BlockSpec((n_seq * POOL_BUF, CB), lambda q, j: (0, pj(j))),
        pl.BlockSpec((n_seq * (LRU_CONV - 1), CB), lambda q, j: (0, lj(j))),
        pl.BlockSpec((n_seq, CB), lambda q, j: (0, lj(j))),
    ]
    out_specs = [
        pl.BlockSpec((rows, CB), lambda q, j: (0, j)),
        pl.BlockSpec((n_seq * POOL_BUF, CB), lambda q, j: (0, pj(j))),
        pl.BlockSpec((n_seq * (LRU_CONV - 1), CB), lambda q, j: (0, lj(j))),
        pl.BlockSpec((n_seq, CB), lambda q, j: (0, lj(j))),
        pl.BlockSpec((rows, D_MODEL), lambda q, j: (0, 0)),
    ]
    out_shape = [
        jax.ShapeDtypeStruct((rows, N_MIX_BLOCKS * CB), BF16),
        jax.ShapeDtypeStruct((n_seq * POOL_BUF, POOL_WIDTH), F32),
        jax.ShapeDtypeStruct((n_seq * (LRU_CONV - 1), LRU_WIDTH), F32),
        jax.ShapeDtypeStruct((n_seq, LRU_WIDTH), F32),
        jax.ShapeDtypeStruct((rows, D_MODEL), BF16),
    ]
    scratch = [
        pltpu.VMEM((rows, CB), F32),
        pltpu.VMEM((rows, CB), BF16),
    ]
    return pl.pallas_call(
        functools.partial(_mix_sample_kernel, n_seq=n_seq, n_t=n_t, start=start),
        grid=(1, N_MIX_BLOCKS),
        in_specs=in_specs, out_specs=out_specs, out_shape=out_shape,
        scratch_shapes=scratch, compiler_params=_cparams(2), name="mix_sample",
    )(x2, ada, ada, p["g_pre1"], p["w_in"], p["w_pool_grp"], p["pool_scale"],
      p["w_lru_conv"], p["b_lru_conv"], p["w_rg"], p["b_rg"], p["w_ig"], p["b_ig"],
      p["lru_lambda"], spool2, sconv2, sh0)


MERGE_ROWS = 64
MERGE_SUBTILE = 256
CAST_ROWS = 32
BF16_ROWS = 16


def _cast_ffn_up_slab(wup_ref, wupb_ref):
    n_blocks, _, cb2 = wupb_ref.shape
    cb = cb2 // 2
    for blk in range(n_blocks):
        wupb_ref[blk, :, 0:cb] = wup_ref[:, blk * cb:(blk + 1) * cb].astype(BF16)
        wupb_ref[blk, :, cb:cb2] = wup_ref[:, (n_blocks + blk) * cb:(n_blocks + blk + 1) * cb].astype(BF16)


def _cast_slab(w_ref, wb_ref):
    for r0 in range(0, w_ref.shape[0], CAST_ROWS):
        wb_ref[r0:r0 + CAST_ROWS, :] = w_ref[r0:r0 + CAST_ROWS, :].astype(BF16)


def _merge_kernel(h_ref, y_ref, wgp_ref, wgl_ref, wpu_ref, wlu_ref, *rest, tm, cast_ffn):
    if cast_ffn:
        wup_ref, m_ref, wupb_ref, wg_ref, g_ref, pu_ref, lu_ref = rest
    else:
        m_ref, wg_ref, g_ref, pu_ref, lu_ref = rest
    wg_ref[:, 0:CB] = wgp_ref[...].astype(BF16)
    wg_ref[:, CB:2 * CB] = wgl_ref[...].astype(BF16)
    wpu = wpu_ref[...].astype(BF16)
    wlu = wlu_ref[...].astype(BF16)

    def slot(m0):
        s0 = (m0 // MERGE_SUBTILE) % 2 * MERGE_SUBTILE
        return slice(s0, s0 + MERGE_SUBTILE)

    def proj(m0):
        rows = slice(m0, m0 + MERGE_SUBTILE)
        g_ref[slot(m0), :] = _dot(h_ref[rows, :], wg_ref[...])
        pu_ref[slot(m0), :] = _dot(y_ref[rows, 0:POOL_WIDTH], wpu)
        lu_ref[slot(m0), :] = _dot(y_ref[rows, POOL_WIDTH:POOL_WIDTH + LRU_WIDTH], wlu)

    proj(0)
    for m0 in range(0, tm, MERGE_SUBTILE):
        if m0 + MERGE_SUBTILE < tm:
            proj(m0 + MERGE_SUBTILE)
        s0 = slot(m0).start
        for r0 in range(0, MERGE_SUBTILE, MERGE_ROWS):
            sl = slice(s0 + r0, s0 + r0 + MERGE_ROWS)
            m = (jax.nn.sigmoid(g_ref[sl, 0:CB]) * pu_ref[sl, :]
                 + jax.nn.sigmoid(g_ref[sl, CB:2 * CB]) * lu_ref[sl, :])
            m_ref[m0 + r0:m0 + r0 + MERGE_ROWS, :] = m.astype(BF16)
    if cast_ffn:
        _cast_ffn_up_slab(wup_ref, wupb_ref)


def _merge(h, y, p, tm, name, cast_ffn_cb=0):
    rows = h.shape[0]
    n_col = D_MODEL // CB
    n_gate0 = N_MIX_BLOCKS
    n_gate1 = N_MIX_BLOCKS + n_col
    in_specs = [
        pl.BlockSpec((tm, D_MODEL), lambda i, c: (i, 0), pipeline_mode=pl.Buffered(1)),
        pl.BlockSpec((tm, N_MIX_BLOCKS * CB), lambda i, c: (i, 0), pipeline_mode=pl.Buffered(1)),
        pl.BlockSpec((D_MODEL, CB), lambda i, c: (0, n_gate0 + c)),
        pl.BlockSpec((D_MODEL, CB), lambda i, c: (0, n_gate1 + c)),
        pl.BlockSpec((POOL_WIDTH, CB), lambda i, c: (0, c)),
        pl.BlockSpec((LRU_WIDTH, CB), lambda i, c: (0, c)),
    ]
    out_specs = [pl.BlockSpec((tm, CB), lambda i, c: (i, c))]
    out_shape = [jax.ShapeDtypeStruct((rows, D_MODEL), BF16)]
    args = [h, y, p["w_in"], p["w_in"], p["w_pool_up"], p["w_lru_up"]]
    if cast_ffn_cb:
        n_steps = (rows // tm) * n_col
        d_ff = p["w_ffn_down"].shape[0]
        n_blocks = d_ff // cast_ffn_cb
        up_rows = D_MODEL // n_steps
        assert up_rows * n_steps == D_MODEL and up_rows % BF16_ROWS == 0
        step = lambda i, c: i * n_col + c
        in_specs += [pl.BlockSpec((up_rows, 2 * d_ff), lambda i, c: (step(i, c), 0))]
        out_specs += [pl.BlockSpec((n_blocks, up_rows, 2 * cast_ffn_cb),
                                   lambda i, c: (0, step(i, c), 0))]
        out_shape += [jax.ShapeDtypeStruct((n_blocks, D_MODEL, 2 * cast_ffn_cb), BF16)]
        args += [p["w_ffn_up"]]
    ring = 2 * MERGE_SUBTILE
    return pl.pallas_call(
        functools.partial(_merge_kernel, tm=tm, cast_ffn=bool(cast_ffn_cb)),
        grid=(rows // tm, n_col),
        in_specs=in_specs, out_specs=out_specs, out_shape=out_shape,
        scratch_shapes=[pltpu.VMEM((D_MODEL, 2 * CB), BF16), pltpu.VMEM((ring, 2 * CB), F32),
                        pltpu.VMEM((ring, CB), F32), pltpu.VMEM((ring, CB), F32)],
        compiler_params=_cparams(2), name=name,
    )(*args)


def _outproj_kernel(m_ref, x_ref, gate_ref, g_ref, w_ref, o_ref, wb_ref, acc_ref,
                    *, tm, tiles_per_seq, sample_t):
    i = pl.program_id(0)

    @pl.when(i == 0)
    def _():
        _cast_slab(w_ref, wb_ref)

    g = g_ref[...]
    if sample_t:
        n_seq = tm // sample_t
    else:
        gg_tile = g * gate_ref[pl.ds(i // tiles_per_seq, 1), :]

    def proj(m0):
        acc_ref[m0:m0 + OUT_SUBTILE, :] = _dot(m_ref[m0:m0 + OUT_SUBTILE, :], wb_ref[...])

    proj(0)
    for m0 in range(0, tm, OUT_SUBTILE):
        if m0 + OUT_SUBTILE < tm:
            proj(m0 + OUT_SUBTILE)
        for r0 in range(m0, m0 + OUT_SUBTILE, NORM_ROWS):
            sl = slice(r0, r0 + NORM_ROWS)
            if sample_t:
                s0 = r0 % n_seq
                gg = g * gate_ref[s0:s0 + NORM_ROWS, :]
            else:
                gg = gg_tile
            o_ref[sl, :] = x_ref[sl, :] + _unit_rms(acc_ref[sl, :]) * gg


def _outproj(m, x2, ada, p, tm, tiles_per_seq, ada_row_block, sample_t):
    rows = x2.shape[0]
    if sample_t:
        gate_spec = pl.BlockSpec((tm // sample_t, D_MODEL), lambda i: (0, 2))
    else:
        gate_spec = pl.BlockSpec((SUBLANES, D_MODEL), lambda i: (ada_row_block, 2))
    return pl.pallas_call(
        functools.partial(_outproj_kernel, tm=tm, tiles_per_seq=tiles_per_seq, sample_t=sample_t),
        grid=(rows // tm,),
        in_specs=[pl.BlockSpec((tm, D_MODEL), lambda i: (i, 0)),
                  pl.BlockSpec((tm, D_MODEL), lambda i: (i, 0)),
                  gate_spec,
                  pl.BlockSpec((1, D_MODEL), lambda i: (0, 0)),
                  pl.BlockSpec((D_MODEL, D_MODEL), lambda i: (0, 0), pipeline_mode=pl.Buffered(1))],
        out_specs=pl.BlockSpec((tm, D_MODEL), lambda i: (i, 0)),
        out_shape=jax.ShapeDtypeStruct((rows, D_MODEL), F32),
        scratch_shapes=[pltpu.VMEM((D_MODEL, D_MODEL), BF16), pltpu.VMEM((tm, D_MODEL), F32)],
        compiler_params=_cparams(1),
        name="outproj_sample" if sample_t else "outproj_prompt",
    )(m, x2, ada, p["g_post1"], p["w_out"])


GELU_C0 = 0.7978845608028654
GELU_C1 = GELU_C0 * 0.044715
FFN_ROWS = 32
FFN_SUBTILE = 256
FFN_AHEAD = 3
FFN_CB = 768


def _ffn_conv_gate(eg, ev, wcg, wcv, bcg, bcv):
    last = FFN_CONV - 1
    g = wcg[last:, :] * eg[last] + bcg
    v = wcv[last:, :] * ev[last] + bcv
    for k in range(last):
        g = g + wcg[k:k + 1, :] * eg[k]
        v = v + wcv[k:k + 1, :] * ev[k]
    t = jnp.tanh(g * (GELU_C0 + GELU_C1 * (g * g)))
    return ((g * v) * (0.5 + 0.5 * t)).astype(BF16)


def _ffn_prompt_kernel(x_ref, sh_ref, sc_ref, gate_ref, gpre_ref, gpost_ref,
                       wup_ref, wcg_ref, wcv_ref, bcg_ref, bcv_ref, wdn_ref,
                       o_ref, nst_ref,
                       h_ref, ext_ref, f_ref, carry_ref, *, tm, tiles_per_seq, n_blocks, cb):
    i = pl.program_id(0)
    c = pl.program_id(1)
    q = i // tiles_per_seq
    first = (i % tiles_per_seq) == 0
    nk = FFN_CONV - 1

    @pl.when(c == 0)
    def _():
        _build_h_prompt(x_ref, h_ref, gpre_ref[...], 1.0 + sc_ref[pl.ds(q, 1), :],
                        sh_ref[pl.ds(q, 1), :], tm)
        o_ref[...] = jnp.zeros((tm, D_MODEL), F32)

    @pl.when(first)
    def _():
        ext_ref[0:SUBLANES, :] = jnp.zeros((SUBLANES, 2 * cb), F32)

    @pl.when(jnp.logical_not(first))
    def _():
        ext_ref[0:SUBLANES, :] = carry_ref[c]

    wcg = wcg_ref[...]
    wcv = wcv_ref[...]
    bcg = bcg_ref[...]
    bcv = bcv_ref[...]

    def up_proj(m0):
        ext_ref[SUBLANES + m0:SUBLANES + m0 + FFN_SUBTILE, :] = _dot(
            h_ref[m0:m0 + FFN_SUBTILE, :], wup_ref[0])

    for m0 in range(0, min(FFN_AHEAD * FFN_SUBTILE, tm), FFN_SUBTILE):
        up_proj(m0)
    for m0 in range(0, tm, FFN_SUBTILE):
        rows = slice(m0, m0 + FFN_SUBTILE)
        for r0 in range(m0, m0 + FFN_SUBTILE, FFN_ROWS):
            shifted = [slice(SUBLANES + r0 - nk + k, SUBLANES + r0 - nk + k + FFN_ROWS)
                       for k in range(FFN_CONV)]
            eg = [ext_ref[s, 0:cb] for s in shifted]
            ev = [ext_ref[s, cb:2 * cb] for s in shifted]
            f_ref[r0:r0 + FFN_ROWS, :] = _ffn_conv_gate(eg, ev, wcg, wcv, bcg, bcv)
        if m0 + FFN_AHEAD * FFN_SUBTILE < tm:
            up_proj(m0 + FFN_AHEAD * FFN_SUBTILE)
        o_ref[rows, :] += _dot(f_ref[rows, :], wdn_ref[...])

    carry_ref[c] = ext_ref[tm:tm + SUBLANES, :]
    nst_ref[0, :, 0, :] = ext_ref[pl.ds(SUBLANES + tm - nk, nk), 0:cb]
    nst_ref[0, :, 1, :] = ext_ref[pl.ds(SUBLANES + tm - nk, nk), cb:2 * cb]

    @pl.when(c == n_blocks - 1)
    def _():
        _residual_norm_prompt(x_ref, o_ref, o_ref, gpost_ref[...], gate_ref[pl.ds(q, 1), :], tm)


def _ffn_sample_kernel(x_ref, sh_ref, sc_ref, gate_ref, gpre_ref, gpost_ref,
                       wup_ref, wcg_ref, wcv_ref, bcg_ref, bcv_ref, wdn_ref,
                       sg_ref, sv_ref,
                       o_ref, nst_ref,
                       h_ref, ext_ref, f_ref, *, n_seq, n_t, n_blocks, cb):
    c = pl.program_id(1)
    nk = FFN_CONV - 1

    @pl.when(c == 0)
    def _():
        _build_h_sample(x_ref, h_ref, gpre_ref[...], sc_ref, sh_ref, n_seq, n_t)
        o_ref[...] = jnp.zeros((n_seq * n_t, D_MODEL), F32)

    wcg = wcg_ref[...]
    wcv = wcv_ref[...]
    bcg = bcg_ref[...]
    bcv = bcv_ref[...]

    def view(state_ref, half, t, s0, n):
        if t < 0:
            return state_ref[(nk + t) * n_seq + s0:(nk + t) * n_seq + s0 + n, :]
        return ext_ref[t * n_seq + s0:t * n_seq + s0 + n, half * cb:(half + 1) * cb]

    t_sub = max(FFN_SUBTILE // n_seq, 1)

    def up_proj(t0):
        rows = slice(t0 * n_seq, (t0 + t_sub) * n_seq)
        ext_ref[rows, :] = _dot(h_ref[rows, :], wup_ref[0])

    for t0 in range(0, min(FFN_AHEAD * t_sub, n_t), t_sub):
        up_proj(t0)
    for t0 in range(0, n_t, t_sub):
        rows = slice(t0 * n_seq, (t0 + t_sub) * n_seq)
        for t in range(t0, t0 + t_sub):
            for s0 in range(0, n_seq, FFN_ROWS):
                eg = [view(sg_ref, 0, t - nk + k, s0, FFN_ROWS) for k in range(FFN_CONV)]
                ev = [view(sv_ref, 1, t - nk + k, s0, FFN_ROWS) for k in range(FFN_CONV)]
                f_ref[t * n_seq + s0:t * n_seq + s0 + FFN_ROWS, :] = _ffn_conv_gate(
                    eg, ev, wcg, wcv, bcg, bcv)
        if t0 + FFN_AHEAD * t_sub < n_t:
            up_proj(t0 + FFN_AHEAD * t_sub)
        o_ref[rows, :] += _dot(f_ref[rows, :], wdn_ref[...])

    for k in range(nk):
        nst_ref[:, k, 0, :] = view(sg_ref, 0, n_t - nk + k, 0, n_seq)
        nst_ref[:, k, 1, :] = view(sv_ref, 1, n_t - nk + k, 0, n_seq)

    @pl.when(c == n_blocks - 1)
    def _():
        g = gpost_ref[...]
        for s0 in range(0, n_seq, NORM_ROWS):
            gg = g * gate_ref[s0:s0 + NORM_ROWS, :]
            for t in range(n_t):
                sl = slice(t * n_seq + s0, t * n_seq + s0 + NORM_ROWS)
                o_ref[sl, :] = x_ref[sl, :] + _unit_rms(o_ref[sl, :]) * gg


def _ffn_weight_specs(n_blocks, cb):
    return [
        pl.BlockSpec((1, D_MODEL), lambda i, c: (0, 0)),
        pl.BlockSpec((1, D_MODEL), lambda i, c: (0, 0)),
        pl.BlockSpec((1, D_MODEL, 2 * cb), lambda i, c: (c, 0, 0)),
        pl.BlockSpec((FFN_CONV, cb), lambda i, c: (0, c)),
        pl.BlockSpec((FFN_CONV, cb), lambda i, c: (0, n_blocks + c)),
        pl.BlockSpec((1, cb), lambda i, c: (0, c)),
        pl.BlockSpec((1, cb), lambda i, c: (0, n_blocks + c)),
        pl.BlockSpec((cb, D_MODEL), lambda i, c: (c, 0)),
    ]


def _ffn_prompt(x1, ada, p, wup_b, wdn_b, n_seq, tm, tiles_per_seq, ada_row_block):
    rows = x1.shape[0]
    n_blocks, _, cb2 = wup_b.shape
    cb = cb2 // 2
    ada_spec = lambda k: pl.BlockSpec((SUBLANES, D_MODEL), lambda i, c: (ada_row_block, k))
    in_specs = [pl.BlockSpec((tm, D_MODEL), lambda i, c: (i, 0), pipeline_mode=pl.Buffered(1)),
                ada_spec(3), ada_spec(4), ada_spec(5)] + _ffn_weight_specs(n_blocks, cb)
    out_specs = [
        pl.BlockSpec((tm, D_MODEL), lambda i, c: (i, 0)),
        pl.BlockSpec((1, FFN_CONV - 1, 2, cb), lambda i, c: (i, 0, 0, c)),
    ]
    out_shape = [jax.ShapeDtypeStruct((rows, D_MODEL), F32),
                 jax.ShapeDtypeStruct((rows // tm, FFN_CONV - 1, 2, n_blocks * cb), F32)]
    scratch = [
        pltpu.VMEM((tm, D_MODEL), BF16),
        pltpu.VMEM((SUBLANES + tm, 2 * cb), F32),
        pltpu.VMEM((tm, cb), BF16),
        pltpu.VMEM((n_blocks, SUBLANES, 2 * cb), F32),
    ]
    return pl.pallas_call(
        functools.partial(_ffn_prompt_kernel, tm=tm, tiles_per_seq=tiles_per_seq,
                          n_blocks=n_blocks, cb=cb),
        grid=(rows // tm, n_blocks),
        in_specs=in_specs, out_specs=out_specs, out_shape=out_shape,
        scratch_shapes=scratch, compiler_params=_cparams(2), name="ffn_prompt",
    )(x1, ada, ada, ada, p["g_pre2"], p["g_post2"], wup_b, p["w_ffn_conv"], p["w_ffn_conv"],
      p["b_ffn_conv"], p["b_ffn_conv"], wdn_b)


def _ffn_sample(x1, ada, p, wup_b, wdn_b, sffn2, n_seq, n_t):
    rows = n_seq * n_t
    n_blocks, _, cb2 = wup_b.shape
    cb = cb2 // 2
    nk = FFN_CONV - 1
    ada_spec = lambda k: pl.BlockSpec((n_seq, D_MODEL), lambda i, c: (0, k))
    in_specs = [pl.BlockSpec((rows, D_MODEL), lambda i, c: (0, 0), pipeline_mode=pl.Buffered(1)),
                ada_spec(3), ada_spec(4), ada_spec(5)] + _ffn_weight_specs(n_blocks, cb) + [
        pl.BlockSpec((n_seq * nk, cb), lambda i, c: (0, c)),
        pl.BlockSpec((n_seq * nk, cb), lambda i, c: (0, n_blocks + c)),
    ]
    out_specs = [
        pl.BlockSpec((rows, D_MODEL), lambda i, c: (0, 0)),
        pl.BlockSpec((n_seq, nk, 2, cb), lambda i, c: (0, 0, 0, c)),
    ]
    out_shape = [jax.ShapeDtypeStruct((rows, D_MODEL), F32),
                 jax.ShapeDtypeStruct((n_seq, nk, 2, n_blocks * cb), F32)]
    scratch = [
        pltpu.VMEM((rows, D_MODEL), BF16),
        pltpu.VMEM((rows, 2 * cb), F32),
        pltpu.VMEM((rows, cb), BF16),
    ]
    return pl.pallas_call(
        functools.partial(_ffn_sample_kernel, n_seq=n_seq, n_t=n_t, n_blocks=n_blocks, cb=cb),
        grid=(1, n_blocks),
        in_specs=in_specs, out_specs=out_specs, out_shape=out_shape,
        scratch_shapes=scratch, compiler_params=_cparams(2), name="ffn_sample",
    )(x1, ada, ada, ada, p["g_pre2"], p["g_post2"], wup_b, p["w_ffn_conv"], p["w_ffn_conv"],
      p["b_ffn_conv"], p["b_ffn_conv"], wdn_b, sffn2, sffn2)


TOKEN_TILE = 1024
MERGE_TILE = 2048
OUTPROJ_TILE = 512


def kernel(x_prompt, x_sample, c_prompt, c_sample, state_pool, state_lru_conv, state_lru_h, state_ffn_conv, w_ada, b_ada, g_pre1, g_post1, g_pre2, g_post2, w_in, w_pool_grp, pool_scale, w_lru_conv, b_lru_conv, w_rg, b_rg, w_ig, b_ig, lru_lambda, w_pool_up, w_lru_up, w_out, w_ffn_up, w_ffn_conv, b_ffn_conv, w_ffn_down):
    batch, seq, d = x_prompt.shape
    dec_batch, dec_seq, _ = x_sample.shape
    depth = w_ada.shape[0]
    assert d == D_MODEL and dec_batch % SUBLANES == 0 and seq % TOKEN_TILE == 0
    assert w_in.shape[2] == N_MIX_BLOCKS * CB + 2 * D_MODEL

    pad = (-batch) % SUBLANES
    c_all = jnp.concatenate([c_sample, c_prompt, jnp.zeros((pad, d), c_prompt.dtype)], axis=0)
    prompt_row_block = dec_batch // SUBLANES

    vec_names = ("g_pre1", "g_post1", "g_pre2", "g_post2", "pool_scale", "b_lru_conv", "b_rg",
                 "b_ig", "lru_lambda", "b_ffn_conv")
    weights = dict(w_ada=w_ada, b_ada=b_ada, g_pre1=g_pre1, g_post1=g_post1, g_pre2=g_pre2,
                   g_post2=g_post2, w_in=w_in, w_pool_grp=w_pool_grp, pool_scale=pool_scale,
                   w_lru_conv=w_lru_conv, b_lru_conv=b_lru_conv, w_rg=w_rg, b_rg=b_rg, w_ig=w_ig,
                   b_ig=b_ig, lru_lambda=lru_lambda, w_pool_up=w_pool_up, w_lru_up=w_lru_up,
                   w_out=w_out, w_ffn_up=w_ffn_up, w_ffn_conv=w_ffn_conv, b_ffn_conv=b_ffn_conv,
                   w_ffn_down=w_ffn_down)

    def time_major(a):
        return jnp.swapaxes(a, 0, 1).reshape(-1, a.shape[-1])

    def seq_major(a2, n_rows):
        return jnp.swapaxes(a2.reshape(n_rows, dec_batch, -1), 0, 1)

    xp = x_prompt.reshape(batch * seq, d)
    xs = time_major(x_sample)
    tps = seq // TOKEN_TILE
    outs_p = ([], [], [], [])
    outs_s = ([], [], [], [])
    for l in range(depth):
        p = {k: v[l] for k, v in weights.items()}
        for k in vec_names + ("b_ada",):
            p[k] = p[k].reshape(1, -1)
        ada = _ada(c_all, p["w_ada"], p["b_ada"])

        y, npool, nconv, nh, h, wdn_b = _mix_prompt(xp, ada, p, batch, seq, prompt_row_block)
        m, wup_b = _merge(h, y, p, MERGE_TILE, "merge_prompt", cast_ffn_cb=FFN_CB)
        x1 = _outproj(m, xp, ada, p, OUTPROJ_TILE, seq // OUTPROJ_TILE, prompt_row_block, 0)
        xp, nffn = _ffn_prompt(x1, ada, p, wup_b, wdn_b, batch, TOKEN_TILE, tps, prompt_row_block)
        outs_p[0].append(npool)
        outs_p[1].append(nconv)
        outs_p[2].append(nh.reshape(batch, LRU_WIDTH))
        outs_p[3].append(nffn[tps - 1::tps].reshape(batch, FFN_CONV - 1, -1))

        rows_s = dec_batch * dec_seq
        y, npool, nconv, nh, h = _mix_sample(
            xs, ada, p, time_major(state_pool[l]), time_major(state_lru_conv[l]),
            state_lru_h[l], dec_batch, dec_seq, PAST_LEN)
        m, = _merge(h, y, p, rows_s, "merge_sample")
        x1 = _outproj(m, xs, ada, p, rows_s, 1, 0, dec_seq)
        xs, nffn = _ffn_sample(x1, ada, p, wup_b, wdn_b, time_major(state_ffn_conv[l]),
                               dec_batch, dec_seq)
        outs_s[0].append(seq_major(npool, POOL_BUF))
        outs_s[1].append(seq_major(nconv, LRU_CONV - 1))
        outs_s[2].append(nh)
        outs_s[3].append(nffn.reshape(dec_batch, FFN_CONV - 1, -1))

    return (xp.reshape(batch, seq, d), seq_major(xs, dec_seq),
            jnp.stack(outs_p[0]), jnp.stack(outs_p[1]), jnp.stack(outs_p[2]), jnp.stack(outs_p[3]),
            jnp.stack(outs_s[0]), jnp.stack(outs_s[1]), jnp.stack(outs_s[2]), jnp.stack(outs_s[3]))
```

```python
import functools

import jax
import jax.numpy as jnp
from jax import lax
from jax.experimental import pallas as pl
from jax.experimental.pallas import tpu as pltpu

F32 = jnp.float32
BF16 = jnp.bfloat16

D_MODEL = 2048
POOL_WINDOWS = (2, 4, 8, 16)
POOL_GROUPS = len(POOL_WINDOWS)
POOL_BUF = max(POOL_WINDOWS) - 1
LRU_CONV = 4
LRU_C = 8.0
PAST_LEN = 16384
FFN_CONV = 3
N_ADA = 6
EPS = 1e-6

CB = 256
POOL_WIDTH = POOL_GROUPS * CB
LRU_BLOCKS = 8
LRU_WIDTH = LRU_BLOCKS * CB
N_MIX_BLOCKS = POOL_GROUPS + LRU_BLOCKS
CHUNK = 64
HALO = CHUNK
SUBLANES = 8
V7X_VMEM_BYTES = 64 * 1024 * 1024
VMEM_LIMIT = V7X_VMEM_BYTES - 4 * 1024 * 1024


def _cparams(n_axes):
    return pltpu.CompilerParams(
        dimension_semantics=("arbitrary",) * n_axes, vmem_limit_bytes=VMEM_LIMIT)


def _dot(a, b):
    return jnp.dot(a, b, preferred_element_type=F32)


def _softplus(z):
    return jnp.maximum(z, 0.0) + jnp.log1p(jnp.exp(-jnp.abs(z)))


def _lru_coeffs(xc, r_pre, i_pre, neg_c_sp):
    r = jax.nn.sigmoid(r_pre)
    i = jax.nn.sigmoid(i_pre)
    log_a = r * neg_c_sp
    a = jnp.exp(log_a)
    m2 = -jnp.tanh(log_a) * (a * a + 1.0)
    root = jnp.where(m2 > 0.0, m2 * lax.rsqrt(m2), 0.0)
    return a, root * (i * xc)


ADA_K = 256
ADA_N = 1024


def _ada_kernel(c_ref, w_ref, b_ref, o_ref):
    rows, n_out = o_ref.shape

    @pl.when(pl.program_id(0) == 0)
    def _():
        o_ref[...] = jnp.broadcast_to(b_ref[...], (rows, n_out))

    c = c_ref[...]
    s = (c * jax.nn.sigmoid(c)).astype(BF16)
    for n0 in range(0, n_out, ADA_N):
        o_ref[:, n0:n0 + ADA_N] += _dot(s, w_ref[:, n0:n0 + ADA_N].astype(BF16))


def _ada(c_all, w_ada, b_ada):
    rows = c_all.shape[0]
    n_out = N_ADA * D_MODEL
    return pl.pallas_call(
        _ada_kernel,
        grid=(D_MODEL // ADA_K,),
        in_specs=[pl.BlockSpec((rows, ADA_K), lambda k: (0, k)),
                  pl.BlockSpec((ADA_K, n_out), lambda k: (k, 0)),
                  pl.BlockSpec((1, n_out), lambda k: (0, 0))],
        out_specs=pl.BlockSpec((rows, n_out), lambda k: (0, 0)),
        out_shape=jax.ShapeDtypeStruct((rows, n_out), F32),
        compiler_params=_cparams(1),
        name="ada",
    )(c_all, w_ada, b_ada)


NORM_ROWS = 32
NORM_UNROLL = 8
OUT_SUBTILE = 256


def _unit_rms(x):
    return x * lax.rsqrt(jnp.mean(x * x, axis=-1, keepdims=True) + EPS)


def _build_h_prompt(x_ref, h_ref, g, scale1p, shift, n_rows):
    gs = g * scale1p

    def body(i, carry):
        r0 = pl.multiple_of(i * NORM_ROWS, NORM_ROWS)
        x = x_ref[pl.ds(r0, NORM_ROWS), :]
        h_ref[pl.ds(r0, NORM_ROWS), :] = (_unit_rms(x) * gs + shift).astype(BF16)
        return carry
    lax.fori_loop(0, n_rows // NORM_ROWS, body, 0, unroll=NORM_UNROLL)


def _residual_norm_prompt(x_ref, acc_ref, o_ref, g, gate, n_rows):
    gg = g * gate
    for r0 in range(0, n_rows, NORM_ROWS):
        sl = slice(r0, r0 + NORM_ROWS)
        o_ref[sl, :] = x_ref[sl, :] + _unit_rms(acc_ref[sl, :]) * gg


def _build_h_sample(x_ref, h_ref, g, sc_ref, sh_ref, n_seq, n_t):
    for s0 in range(0, n_seq, NORM_ROWS):
        gs = g * (1.0 + sc_ref[s0:s0 + NORM_ROWS, :])
        shift = sh_ref[s0:s0 + NORM_ROWS, :]
        for t in range(n_t):
            sl = slice(t * n_seq + s0, t * n_seq + s0 + NORM_ROWS)
            h_ref[sl, :] = (_unit_rms(x_ref[sl, :]) * gs + shift).astype(BF16)


MIX_SUBTILE = 512
POOL_ROWS = 128


def _mix_prompt_kernel(x_hbm, sh_ref, sc_ref, g_ref, win_ref, wgrp_ref, pscale_ref,
                       wconv_ref, bconv_ref, wrg_ref, brg_ref, wig_ref, big_ref, lam_ref, wdn_ref,
                       y_ref, npool_ref, nconv_ref, nh_ref, h_ref, wdnb_ref,
                       x_ref, x_sem, ext_ref, xb_ref, r_ref, i_ref, a_ref, *, seq, n_seq):
    q = pl.program_id(0)
    j = pl.program_id(1)
    _cast_slab(wdn_ref, wdnb_ref)

    def x_copy(s):
        rows = pl.ds(pl.multiple_of(s * seq, seq), seq)
        return pltpu.make_async_copy(x_hbm.at[rows, :], x_ref, x_sem)

    @pl.when(jnp.logical_and(q == 0, j == 0))
    def _():
        x_copy(0).start()

    @pl.when(jnp.logical_and(j == 1, q + 1 < n_seq))
    def _():
        x_copy(q + 1).start()

    @pl.when(j == 0)
    def _():
        x_copy(q).wait()
        _build_h_prompt(x_ref, h_ref, g_ref[...], 1.0 + sc_ref[pl.ds(q, 1), :],
                        sh_ref[pl.ds(q, 1), :], seq)
        for ref in (ext_ref, r_ref, i_ref, a_ref):
            ref[0:HALO, :] = jnp.zeros((HALO, CB), F32)

    win = win_ref[...].astype(BF16)

    def rows_of(r0, n, shift=0):
        return slice(HALO + r0 - shift, HALO + r0 - shift + n)

    def up_proj(m0):
        ext_ref[rows_of(m0, MIX_SUBTILE), :] = _dot(h_ref[m0:m0 + MIX_SUBTILE, :], win)

    def pool_branch(w):
        wg = wgrp_ref[0].astype(BF16)
        ps = pscale_ref[...]
        partial = {2: r_ref, 4: i_ref, 8: a_ref}
        up_proj(0)
        for m0 in range(0, seq, MIX_SUBTILE):
            if m0 + MIX_SUBTILE < seq:
                up_proj(m0 + MIX_SUBTILE)
            for r0 in range(m0, m0 + MIX_SUBTILE, POOL_ROWS):
                u = ext_ref[rows_of(r0, POOL_ROWS), :]
                s, src, width = u, ext_ref, 1
                while width < w:
                    s = s + src[rows_of(r0, POOL_ROWS, width), :]
                    width *= 2
                    if width < w:
                        src = partial[width]
                        src[rows_of(r0, POOL_ROWS), :] = s
                if r0 < w:
                    pos = r0 + lax.broadcasted_iota(jnp.int32, (POOL_ROWS, 1), 0)
                    cnt = jnp.minimum(w, pos + 1).astype(F32)
                else:
                    cnt = float(w)
                xb_ref[r0:r0 + POOL_ROWS, :] = (s / cnt - u).astype(BF16)
            sub = slice(m0, m0 + MIX_SUBTILE)
            y_ref[sub, :] = (_dot(xb_ref[sub, :], wg) * ps).astype(BF16)
        npool_ref[0] = ext_ref[pl.ds(HALO + seq - POOL_BUF, POOL_BUF), :]

    for g, w in enumerate(POOL_WINDOWS):
        pl.when(j == g)(functools.partial(pool_branch, w))

    @pl.when(j >= POOL_GROUPS)
    def _():
        wc = wconv_ref[...]
        bc = bconv_ref[...]
        wrg = wrg_ref[0].astype(BF16)
        wig = wig_ref[0].astype(BF16)
        neg_c_sp = (-LRU_C) * _softplus(-lam_ref[...])
        brg = brg_ref[...]
        big = big_ref[...]
        nk = LRU_CONV - 1
        group = lax.broadcasted_iota(jnp.int32, (SUBLANES, CB), 0)
        w_tap = [jnp.broadcast_to(wc[k:k + 1, :], (SUBLANES, CB)) for k in range(LRU_CONV)]
        b_tap = jnp.broadcast_to(bc, (SUBLANES, CB))

        def regroup(v):
            return jnp.swapaxes(v.reshape(SUBLANES, SUBLANES, CB), 0, 1).reshape(CHUNK, CB)

        def split(v):
            return [v[SUBLANES * p:SUBLANES * (p + 1), :] for p in range(SUBLANES)]

        def group_before(cur, prev):
            return jnp.where(group == 0, pltpu.roll(prev, 1, 0), pltpu.roll(cur, 1, 0))

        h_carry = jnp.zeros((1, CB), F32)
        up_proj(0)
        for m0 in range(0, seq, MIX_SUBTILE):
            sub = slice(m0, m0 + MIX_SUBTILE)
            if m0 + MIX_SUBTILE < seq:
                up_proj(m0 + MIX_SUBTILE)
            else:
                nconv_ref[0] = ext_ref[pl.ds(HALO + seq - nk, nk), :]
            for r0 in range(m0, m0 + MIX_SUBTILE, CHUNK):
                ext_ref[rows_of(r0, CHUNK), :] = regroup(ext_ref[rows_of(r0, CHUNK), :])
            for r0 in range(m0, m0 + MIX_SUBTILE, CHUNK):
                cur = split(ext_ref[rows_of(r0, CHUNK), :])
                prev_tail = split(ext_ref[rows_of(r0, CHUNK, CHUNK), :])[SUBLANES - nk:]
                wrapped = [group_before(cur[SUBLANES - nk + t], prev_tail[t]) for t in range(nk)]
                rows = []
                for p in range(SUBLANES):
                    xc = w_tap[nk] * cur[p] + b_tap
                    for k in range(1, LRU_CONV):
                        src = cur[p - k] if p >= k else wrapped[nk + p - k]
                        xc = xc + w_tap[nk - k] * src
                    rows.append(xc)
                xc = jnp.concatenate(rows, axis=0)
                a_ref[rows_of(r0, CHUNK), :] = xc
                xb_ref[r0:r0 + CHUNK, :] = xc.astype(BF16)
            r_ref[rows_of(m0, MIX_SUBTILE), :] = _dot(xb_ref[sub, :], wrg)
            i_ref[rows_of(m0, MIX_SUBTILE), :] = _dot(xb_ref[sub, :], wig)
            for r0 in range(m0, m0 + MIX_SUBTILE, CHUNK):
                sl = rows_of(r0, CHUNK)
                a, b = _lru_coeffs(a_ref[sl, :], r_ref[sl, :] + brg, i_ref[sl, :] + big, neg_c_sp)
                a, b = split(a), split(b)
                hs, ps = [b[0]], [a[0]]
                for p in range(1, SUBLANES):
                    hs.append(a[p] * hs[-1] + b[p])
                    ps.append(a[p] * ps[-1])
                e_p, e_h = ps[-1], hs[-1]
                for k in (1, 2, 4):
                    p_sh = jnp.where(group >= k, pltpu.roll(e_p, k, 0), 1.0)
                    h_sh = jnp.where(group >= k, pltpu.roll(e_h, k, 0), 0.0)
                    e_h = e_h + e_p * h_sh
                    e_p = e_p * p_sh
                ends = e_p * h_carry + e_h
                h_in = jnp.where(group == 0, h_carry, pltpu.roll(ends, 1, 0))
                h = jnp.concatenate([hs[p] + ps[p] * h_in for p in range(SUBLANES)], axis=0)
                y_ref[r0:r0 + CHUNK, :] = regroup(h).astype(BF16)
                h_carry = ends[SUBLANES - 1:SUBLANES, :]
        nh_ref[0] = h_carry


def _mix_sample_kernel(x_ref, sh_ref, sc_ref, g_ref, win_ref, wgrp_ref, pscale_ref,
                       wconv_ref, bconv_ref, wrg_ref, brg_ref, wig_ref, big_ref, lam_ref,
                       spool_ref, sconv_ref, sh0_ref,
                       y_ref, npool_ref, nconv_ref, nh_ref, h_ref,
                       u_ref, d_ref, *, n_seq, n_t, start):
    j = pl.program_id(1)

    @pl.when(j == 0)
    def _():
        _build_h_sample(x_ref, h_ref, g_ref[...], sc_ref, sh_ref, n_seq, n_t)

    u_ref[...] = _dot(h_ref[...], win_ref[...].astype(BF16))

    def u_slab(t):
        return u_ref[t * n_seq:(t + 1) * n_seq, :]

    def pool_branch(w):
        e = [spool_ref[k * n_seq:(k + 1) * n_seq, :] for k in range(POOL_BUF)]
        e += [u_slab(t) for t in range(n_t)]
        for k in range(POOL_BUF):
            npool_ref[k * n_seq:(k + 1) * n_seq, :] = e[n_t + k]
        for t in range(n_t):
            s = e[POOL_BUF + t]
            for k in range(1, w):
                s = s + e[POOL_BUF + t - k]
            cnt = float(min(w, start + t + 1))
            d_ref[t * n_seq:(t + 1) * n_seq, :] = (s / cnt - e[POOL_BUF + t]).astype(BF16)
        y = _dot(d_ref[...], wgrp_ref[0].astype(BF16)) * pscale_ref[...]
        y_ref[...] = y.astype(BF16)

    for g, w in enumerate(POOL_WINDOWS):
        pl.when(j == g)(functools.partial(pool_branch, w))

    @pl.when(j >= POOL_GROUPS)
    def _():
        nk = LRU_CONV - 1
        e = [sconv_ref[k * n_seq:(k + 1) * n_seq, :] for k in range(nk)]
        e += [u_slab(t) for t in range(n_t)]
        for k in range(nk):
            nconv_ref[k * n_seq:(k + 1) * n_seq, :] = e[n_t + k]
        wc = wconv_ref[...]
        bc = bconv_ref[...]
        wrg = wrg_ref[0].astype(BF16)
        wig = wig_ref[0].astype(BF16)
        neg_c_sp = (-LRU_C) * _softplus(-lam_ref[...])
        h = sh0_ref[...]
        for t in range(n_t):
            xc = bc
            for k in range(LRU_CONV):
                xc = xc + wc[k:k + 1, :] * e[t + k]
            xb = xc.astype(BF16)
            a, b = _lru_coeffs(xc, _dot(xb, wrg) + brg_ref[...], _dot(xb, wig) + big_ref[...],
                               neg_c_sp)
            h = a * h + b
            y_ref[t * n_seq:(t + 1) * n_seq, :] = h.astype(BF16)
        nh_ref[...] = h


def _mix_weight_specs():
    pj = lambda j: jnp.minimum(j, POOL_GROUPS - 1)
    lj = lambda j: jnp.maximum(j - POOL_GROUPS, 0)
    return [
        pl.BlockSpec((1, D_MODEL), lambda q, j: (0, 0)),
        pl.BlockSpec((D_MODEL, CB), lambda q, j: (0, j)),
        pl.BlockSpec((1, CB, CB), lambda q, j: (pj(j), 0, 0)),
        pl.BlockSpec((1, CB), lambda q, j: (0, pj(j))),
        pl.BlockSpec((LRU_CONV, CB), lambda q, j: (0, lj(j))),
        pl.BlockSpec((1, CB), lambda q, j: (0, lj(j))),
        pl.BlockSpec((1, CB, CB), lambda q, j: (lj(j), 0, 0)),
        pl.BlockSpec((1, CB), lambda q, j: (0, lj(j))),
        pl.BlockSpec((1, CB, CB), lambda q, j: (lj(j), 0, 0)),
        pl.BlockSpec((1, CB), lambda q, j: (0, lj(j))),
        pl.BlockSpec((1, CB), lambda q, j: (0, lj(j))),
    ], pj, lj


def _mix_prompt(x2, ada, p, n_seq, seq, ada_row_block):
    wspecs, pj, lj = _mix_weight_specs()
    in_specs = [
        pl.BlockSpec(memory_space=pl.ANY),
        pl.BlockSpec((SUBLANES, D_MODEL), lambda q, j: (ada_row_block, 0)),
        pl.BlockSpec((SUBLANES, D_MODEL), lambda q, j: (ada_row_block, 1)),
    ] + wspecs
    d_ff = p["w_ffn_down"].shape[0]
    dn_rows = d_ff // (n_seq * N_MIX_BLOCKS)
    assert dn_rows * n_seq * N_MIX_BLOCKS == d_ff and dn_rows % BF16_ROWS == 0
    slab_spec = pl.BlockSpec((dn_rows, D_MODEL), lambda q, j: (q * N_MIX_BLOCKS + j, 0))
    in_specs.append(slab_spec)
    out_specs = [
        pl.BlockSpec((seq, CB), lambda q, j: (q, j)),
        pl.BlockSpec((1, POOL_BUF, CB), lambda q, j: (q, 0, pj(j))),
        pl.BlockSpec((1, LRU_CONV - 1, CB), lambda q, j: (q, 0, lj(j))),
        pl.BlockSpec((1, 1, CB), lambda q, j: (q, 0, lj(j))),
        pl.BlockSpec((seq, D_MODEL), lambda q, j: (q, 0)),
        slab_spec,
    ]
    out_shape = [
        jax.ShapeDtypeStruct((n_seq * seq, N_MIX_BLOCKS * CB), BF16),
        jax.ShapeDtypeStruct((n_seq, POOL_BUF, POOL_WIDTH), F32),
        jax.ShapeDtypeStruct((n_seq, LRU_CONV - 1, LRU_WIDTH), F32),
        jax.ShapeDtypeStruct((n_seq, 1, LRU_WIDTH), F32),
        jax.ShapeDtypeStruct((n_seq * seq, D_MODEL), BF16),
        jax.ShapeDtypeStruct((d_ff, D_MODEL), BF16),
    ]
    scratch = [
        pltpu.VMEM((seq, D_MODEL), F32),
        pltpu.SemaphoreType.DMA(()),
        pltpu.VMEM((HALO + seq, CB), F32),
        pltpu.VMEM((seq, CB), BF16),
        pltpu.VMEM((HALO + seq, CB), F32),
        pltpu.VMEM((HALO + seq, CB), F32),
        pltpu.VMEM((HALO + seq, CB), F32),
    ]
    return pl.pallas_call(
        functools.partial(_mix_prompt_kernel, seq=seq, n_seq=n_seq),
        grid=(n_seq, N_MIX_BLOCKS),
        in_specs=in_specs, out_specs=out_specs, out_shape=out_shape,
        scratch_shapes=scratch, compiler_params=_cparams(2), name="mix_prompt",
    )(x2, ada, ada, p["g_pre1"], p["w_in"], p["w_pool_grp"], p["pool_scale"],
      p["w_lru_conv"], p["b_lru_conv"], p["w_rg"], p["b_rg"], p["w_ig"], p["b_ig"],
      p["lru_lambda"], p["w_ffn_down"])


def _mix_sample(x2, ada, p, spool2, sconv2, sh0, n_seq, n_t, start):
    wspecs, pj, lj = _mix_weight_specs()
    rows = n_seq * n_t
    in_specs = [
        pl.BlockSpec((rows, D_MODEL), lambda q, j: (0, 0), pipeline_mode=pl.Buffered(1)),
        pl.BlockSpec((n_seq, D_MODEL), lambda q, j: (0, 0)),
        pl.BlockSpec((n_seq, D_MODEL), lambda q, j: (0, 1)),
    ] + wspecs + [
        pl.BlockSpec((n_seq * POOL_BUF, CB), lambda q, j: (0, pj(j))),
        pl.BlockSpec((n_seq * (LRU_CONV - 1), CB), lambda q, j: (0, lj(j))),
        pl.BlockSpec((n_seq, CB), lambda q, j: (0, lj(j))),
    ]
    out_specs = [
        pl.BlockSpec((rows, CB), lambda q, j: (0, j)),
        pl.BlockSpec((n_seq * POOL_BUF, CB), lambda q, j: (0, pj(j))),
        pl.BlockSpec((n_seq * (LRU_CONV - 1), CB), lambda q, j: (0, lj(j))),
        pl.BlockSpec((n_seq, CB), lambda q, j: (0, lj(j))),
        pl.BlockSpec((rows, D_MODEL), lambda q, j: (0, 0)),
    ]
    out_shape = [
        jax.ShapeDtypeStruct((rows, N_MIX_BLOCKS * CB), BF16),
        jax.ShapeDtypeStruct((n_seq * POOL_BUF, POOL_WIDTH), F32),
        jax.ShapeDtypeStruct((n_seq * (LRU_CONV - 1), LRU_WIDTH), F32),
        jax.ShapeDtypeStruct((n_seq, LRU_WIDTH), F32),
        jax.ShapeDtypeStruct((rows, D_MODEL), BF16),
    ]
    scratch = [
        pltpu.VMEM((rows, CB), F32),
        pltpu.VMEM((rows, CB), BF16),
    ]
    return pl.pallas_call(
        functools.partial(_mix_sample_kernel, n_seq=n_seq, n_t=n_t, start=start),
        grid=(1, N_MIX_BLOCKS),
        in_specs=in_specs, out_specs=out_specs, out_shape=out_shape,
        scratch_shapes=scratch, compiler_params=_cparams(2), name="mix_sample",
    )(x2, ada, ada, p["g_pre1"], p["w_in"], p["w_pool_grp"], p["pool_scale"],
      p["w_lru_conv"], p["b_lru_conv"], p["w_rg"], p["b_rg"], p["w_ig"], p["b_ig"],
      p["lru_lambda"], spool2, sconv2, sh0)


MERGE_ROWS = 64
MERGE_SUBTILE = 256
CAST_ROWS = 32
BF16_ROWS = 16


def _cast_ffn_up_slab(wup_ref, wupb_ref):
    n_blocks, _, cb2 = wupb_ref.shape
    cb = cb2 // 2
    for blk in range(n_blocks):
        wupb_ref[blk, :, 0:cb] = wup_ref[:, blk * cb:(blk + 1) * cb].astype(BF16)
        wupb_ref[blk, :, cb:cb2] = wup_ref[:, (n_blocks + blk) * cb:(n_blocks + blk + 1) * cb].astype(BF16)


def _cast_slab(w_ref, wb_ref):
    for r0 in range(0, w_ref.shape[0], CAST_ROWS):
        wb_ref[r0:r0 + CAST_ROWS, :] = w_ref[r0:r0 + CAST_ROWS, :].astype(BF16)


def _merge_kernel(h_ref, y_ref, wgp_ref, wgl_ref, wpu_ref, wlu_ref, *rest, tm, cast_ffn):
    if cast_ffn:
        wup_ref, m_ref, wupb_ref, wg_ref, g_ref, pu_ref, lu_ref = rest
    else:
        m_ref, wg_ref, g_ref, pu_ref, lu_ref = rest
    wg_ref[:, 0:CB] = wgp_ref[...].astype(BF16)
    wg_ref[:, CB:2 * CB] = wgl_ref[...].astype(BF16)
    wpu = wpu_ref[...].astype(BF16)
    wlu = wlu_ref[...].astype(BF16)

    def slot(m0):
        s0 = (m0 // MERGE_SUBTILE) % 2 * MERGE_SUBTILE
        return slice(s0, s0 + MERGE_SUBTILE)

    def proj(m0):
        rows = slice(m0, m0 + MERGE_SUBTILE)
        g_ref[slot(m0), :] = _dot(h_ref[rows, :], wg_ref[...])
        pu_ref[slot(m0), :] = _dot(y_ref[rows, 0:POOL_WIDTH], wpu)
        lu_ref[slot(m0), :] = _dot(y_ref[rows, POOL_WIDTH:POOL_WIDTH + LRU_WIDTH], wlu)

    proj(0)
    for m0 in range(0, tm, MERGE_SUBTILE):
        if m0 + MERGE_SUBTILE < tm:
            proj(m0 + MERGE_SUBTILE)
        s0 = slot(m0).start
        for r0 in range(0, MERGE_SUBTILE, MERGE_ROWS):
            sl = slice(s0 + r0, s0 + r0 + MERGE_ROWS)
            m = (jax.nn.sigmoid(g_ref[sl, 0:CB]) * pu_ref[sl, :]
                 + jax.nn.sigmoid(g_ref[sl, CB:2 * CB]) * lu_ref[sl, :])
            m_ref[m0 + r0:m0 + r0 + MERGE_ROWS, :] = m.astype(BF16)
    if cast_ffn:
        _cast_ffn_up_slab(wup_ref, wupb_ref)


def _merge(h, y, p, tm, name, cast_ffn_cb=0):
    rows = h.shape[0]
    n_col = D_MODEL // CB
    n_gate0 = N_MIX_BLOCKS
    n_gate1 = N_MIX_BLOCKS + n_col
    in_specs = [
        pl.BlockSpec((tm, D_MODEL), lambda i, c: (i, 0), pipeline_mode=pl.Buffered(1)),
        pl.BlockSpec((tm, N_MIX_BLOCKS * CB), lambda i, c: (i, 0), pipeline_mode=pl.Buffered(1)),
        pl.BlockSpec((D_MODEL, CB), lambda i, c: (0, n_gate0 + c)),
        pl.BlockSpec((D_MODEL, CB), lambda i, c: (0, n_gate1 + c)),
        pl.BlockSpec((POOL_WIDTH, CB), lambda i, c: (0, c)),
        pl.BlockSpec((LRU_WIDTH, CB), lambda i, c: (0, c)),
    ]
    out_specs = [pl.BlockSpec((tm, CB), lambda i, c: (i, c))]
    out_shape = [jax.ShapeDtypeStruct((rows, D_MODEL), BF16)]
    args = [h, y, p["w_in"], p["w_in"], p["w_pool_up"], p["w_lru_up"]]
    if cast_ffn_cb:
        n_steps = (rows // tm) * n_col
        d_ff = p["w_ffn_down"].shape[0]
        n_blocks = d_ff // cast_ffn_cb
        up_rows = D_MODEL // n_steps
        assert up_rows * n_steps == D_MODEL and up_rows % BF16_ROWS == 0
        step = lambda i, c: i * n_col + c
        in_specs += [pl.BlockSpec((up_rows, 2 * d_ff), lambda i, c: (step(i, c), 0))]
        out_specs += [pl.BlockSpec((n_blocks, up_rows, 2 * cast_ffn_cb),
                                   lambda i, c: (0, step(i, c), 0))]
        out_shape += [jax.ShapeDtypeStruct((n_blocks, D_MODEL, 2 * cast_ffn_cb), BF16)]
        args += [p["w_ffn_up"]]
    ring = 2 * MERGE_SUBTILE
    return pl.pallas_call(
        functools.partial(_merge_kernel, tm=tm, cast_ffn=bool(cast_ffn_cb)),
        grid=(rows // tm, n_col),
        in_specs=in_specs, out_specs=out_specs, out_shape=out_shape,
        scratch_shapes=[pltpu.VMEM((D_MODEL, 2 * CB), BF16), pltpu.VMEM((ring, 2 * CB), F32),
                        pltpu.VMEM((ring, CB), F32), pltpu.VMEM((ring, CB), F32)],
        compiler_params=_cparams(2), name=name,
    )(*args)


def _outproj_kernel(m_ref, x_ref, gate_ref, g_ref, w_ref, o_ref, wb_ref, acc_ref,
                    *, tm, tiles_per_seq, sample_t):
    i = pl.program_id(0)

    @pl.when(i == 0)
    def _():
        _cast_slab(w_ref, wb_ref)

    g = g_ref[...]
    if sample_t:
        n_seq = tm // sample_t
    else:
        gg_tile = g * gate_ref[pl.ds(i // tiles_per_seq, 1), :]

    def proj(m0):
        acc_ref[m0:m0 + OUT_SUBTILE, :] = _dot(m_ref[m0:m0 + OUT_SUBTILE, :], wb_ref[...])

    proj(0)
    for m0 in range(0, tm, OUT_SUBTILE):
        if m0 + OUT_SUBTILE < tm:
            proj(m0 + OUT_SUBTILE)
        for r0 in range(m0, m0 + OUT_SUBTILE, NORM_ROWS):
            sl = slice(r0, r0 + NORM_ROWS)
            if sample_t:
                s0 = r0 % n_seq
                gg = g * gate_ref[s0:s0 + NORM_ROWS, :]
            else:
                gg = gg_tile
            o_ref[sl, :] = x_ref[sl, :] + _unit_rms(acc_ref[sl, :]) * gg


def _outproj(m, x2, ada, p, tm, tiles_per_seq, ada_row_block, sample_t):
    rows = x2.shape[0]
    if sample_t:
        gate_spec = pl.BlockSpec((tm // sample_t, D_MODEL), lambda i: (0, 2))
    else:
        gate_spec = pl.BlockSpec((SUBLANES, D_MODEL), lambda i: (ada_row_block, 2))
    return pl.pallas_call(
        functools.partial(_outproj_kernel, tm=tm, tiles_per_seq=tiles_per_seq, sample_t=sample_t),
        grid=(rows // tm,),
        in_specs=[pl.BlockSpec((tm, D_MODEL), lambda i: (i, 0)),
                  pl.BlockSpec((tm, D_MODEL), lambda i: (i, 0)),
                  gate_spec,
                  pl.BlockSpec((1, D_MODEL), lambda i: (0, 0)),
                  pl.BlockSpec((D_MODEL, D_MODEL), lambda i: (0, 0), pipeline_mode=pl.Buffered(1))],
        out_specs=pl.BlockSpec((tm, D_MODEL), lambda i: (i, 0)),
        out_shape=jax.ShapeDtypeStruct((rows, D_MODEL), F32),
        scratch_shapes=[pltpu.VMEM((D_MODEL, D_MODEL), BF16), pltpu.VMEM((tm, D_MODEL), F32)],
        compiler_params=_cparams(1),
        name="outproj_sample" if sample_t else "outproj_prompt",
    )(m, x2, ada, p["g_post1"], p["w_out"])


GELU_C0 = 0.7978845608028654
GELU_C1 = GELU_C0 * 0.044715
FFN_ROWS = 32
FFN_SUBTILE = 512
FFN_AHEAD = 2
FFN_CB = 768


def _ffn_conv_gate(eg, ev, wcg, wcv, bcg, bcv):
    last = FFN_CONV - 1
    g = wcg[last:, :] * eg[last] + bcg
    v = wcv[last:, :] * ev[last] + bcv
    for k in range(last):
        g = g + wcg[k:k + 1, :] * eg[k]
        v = v + wcv[k:k + 1, :] * ev[k]
    t = jnp.tanh(g * (GELU_C0 + GELU_C1 * (g * g)))
    return ((g * v) * (0.5 + 0.5 * t)).astype(BF16)


def _ffn_prompt_kernel(x_ref, sh_ref, sc_ref, gate_ref, gpre_ref, gpost_ref,
                       wup_ref, wcg_ref, wcv_ref, bcg_ref, bcv_ref, wdn_ref,
                       o_ref, nst_ref,
                       h_ref, ext_ref, f_ref, carry_ref, *, tm, tiles_per_seq, n_blocks, cb):
    i = pl.program_id(0)
    c = pl.program_id(1)
    q = i // tiles_per_seq
    first = (i % tiles_per_seq) == 0
    nk = FFN_CONV - 1

    @pl.when(c == 0)
    def _():
        _build_h_prompt(x_ref, h_ref, gpre_ref[...], 1.0 + sc_ref[pl.ds(q, 1), :],
                        sh_ref[pl.ds(q, 1), :], tm)
        o_ref[...] = jnp.zeros((tm, D_MODEL), F32)

    @pl.when(first)
    def _():
        ext_ref[0:SUBLANES, :] = jnp.zeros((SUBLANES, 2 * cb), F32)

    @pl.when(jnp.logical_not(first))
    def _():
        ext_ref[0:SUBLANES, :] = carry_ref[c]

    wcg = wcg_ref[...]
    wcv = wcv_ref[...]
    bcg = bcg_ref[...]
    bcv = bcv_ref[...]

    def up_proj(m0):
        ext_ref[SUBLANES + m0:SUBLANES + m0 + FFN_SUBTILE, :] = _dot(
            h_ref[m0:m0 + FFN_SUBTILE, :], wup_ref[0])

    for m0 in range(0, min(FFN_AHEAD * FFN_SUBTILE, tm), FFN_SUBTILE):
        up_proj(m0)
    for m0 in range(0, tm, FFN_SUBTILE):
        rows = slice(m0, m0 + FFN_SUBTILE)
        for r0 in range(m0, m0 + FFN_SUBTILE, FFN_ROWS):
            shifted = [slice(SUBLANES + r0 - nk + k, SUBLANES + r0 - nk + k + FFN_ROWS)
                       for k in range(FFN_CONV)]
            eg = [ext_ref[s, 0:cb] for s in shifted]
            ev = [ext_ref[s, cb:2 * cb] for s in shifted]
            f_ref[r0:r0 + FFN_ROWS, :] = _ffn_conv_gate(eg, ev, wcg, wcv, bcg, bcv)
        if m0 + FFN_AHEAD * FFN_SUBTILE < tm:
            up_proj(m0 + FFN_AHEAD * FFN_SUBTILE)
        o_ref[rows, :] += _dot(f_ref[rows, :], wdn_ref[...])

    carry_ref[c] = ext_ref[tm:tm + SUBLANES, :]
    nst_ref[0, :, 0, :] = ext_ref[pl.ds(SUBLANES + tm - nk, nk), 0:cb]
    nst_ref[0, :, 1, :] = ext_ref[pl.ds(SUBLANES + tm - nk, nk), cb:2 * cb]

    @pl.when(c == n_blocks - 1)
    def _():
        _residual_norm_prompt(x_ref, o_ref, o_ref, gpost_ref[...], gate_ref[pl.ds(q, 1), :], tm)


def _ffn_sample_kernel(x_ref, sh_ref, sc_ref, gate_ref, gpre_ref, gpost_ref,
                       wup_ref, wcg_ref, wcv_ref, bcg_ref, bcv_ref, wdn_ref,
                       sg_ref, sv_ref,
                       o_ref, nst_ref,
                       h_ref, ext_ref, f_ref, *, n_seq, n_t, n_blocks, cb):
    c = pl.program_id(1)
    nk = FFN_CONV - 1

    @pl.when(c == 0)
    def _():
        _build_h_sample(x_ref, h_ref, gpre_ref[...], sc_ref, sh_ref, n_seq, n_t)
        o_ref[...] = jnp.zeros((n_seq * n_t, D_MODEL), F32)

    wcg = wcg_ref[...]
    wcv = wcv_ref[...]
    bcg = bcg_ref[...]
    bcv = bcv_ref[...]

    def view(state_ref, half, t, s0, n):
        if t < 0:
            return state_ref[(nk + t) * n_seq + s0:(nk + t) * n_seq + s0 + n, :]
        return ext_ref[t * n_seq + s0:t * n_seq + s0 + n, half * cb:(half + 1) * cb]

    t_sub = max(FFN_SUBTILE // n_seq, 1)

    def up_proj(t0):
        rows = slice(t0 * n_seq, (t0 + t_sub) * n_seq)
        ext_ref[rows, :] = _dot(h_ref[rows, :], wup_ref[0])

    for t0 in range(0, min(FFN_AHEAD * t_sub, n_t), t_sub):
        up_proj(t0)
    for t0 in range(0, n_t, t_sub):
        rows = slice(t0 * n_seq, (t0 + t_sub) * n_seq)
        for t in range(t0, t0 + t_sub):
            for s0 in range(0, n_seq, FFN_ROWS):
                eg = [view(sg_ref, 0, t - nk + k, s0, FFN_ROWS) for k in range(FFN_CONV)]
                ev = [view(sv_ref, 1, t - nk + k, s0, FFN_ROWS) for k in range(FFN_CONV)]
                f_ref[t * n_seq + s0:t * n_seq + s0 + FFN_ROWS, :] = _ffn_conv_gate(
                    eg, ev, wcg, wcv, bcg, bcv)
        if t0 + FFN_AHEAD * t_sub < n_t:
            up_proj(t0 + FFN_AHEAD * t_sub)
        o_ref[rows, :] += _dot(f_ref[rows, :], wdn_ref[...])

    for k in range(nk):
        nst_ref[:, k, 0, :] = view(sg_ref, 0, n_t - nk + k, 0, n_seq)
        nst_ref[:, k, 1, :] = view(sv_ref, 1, n_t - nk + k, 0, n_seq)

    @pl.when(c == n_blocks - 1)
    def _():
        g = gpost_ref[...]
        for s0 in range(0, n_seq, NORM_ROWS):
            gg = g * gate_ref[s0:s0 + NORM_ROWS, :]
            for t in range(n_t):
                sl = slice(t * n_seq + s0, t * n_seq + s0 + NORM_ROWS)
                o_ref[sl, :] = x_ref[sl, :] + _unit_rms(o_ref[sl, :]) * gg


def _ffn_weight_specs(n_blocks, cb):
    return [
        pl.BlockSpec((1, D_MODEL), lambda i, c: (0, 0)),
        pl.BlockSpec((1, D_MODEL), lambda i, c: (0, 0)),
        pl.BlockSpec((1, D_MODEL, 2 * cb), lambda i, c: (c, 0, 0)),
        pl.BlockSpec((FFN_CONV, cb), lambda i, c: (0, c)),
        pl.BlockSpec((FFN_CONV, cb), lambda i, c: (0, n_blocks + c)),
        pl.BlockSpec((1, cb), lambda i, c: (0, c)),
        pl.BlockSpec((1, cb), lambda i, c: (0, n_blocks + c)),
        pl.BlockSpec((cb, D_MODEL), lambda i, c: (c, 0)),
    ]


def _ffn_prompt(x1, ada, p, wup_b, wdn_b, n_seq, tm, tiles_per_seq, ada_row_block):
    rows = x1.shape[0]
    n_blocks, _, cb2 = wup_b.shape
    cb = cb2 // 2
    ada_spec = lambda k: pl.BlockSpec((SUBLANES, D_MODEL), lambda i, c: (ada_row_block, k))
    in_specs = [pl.BlockSpec((tm, D_MODEL), lambda i, c: (i, 0), pipeline_mode=pl.Buffered(1)),
                ada_spec(3), ada_spec(4), ada_spec(5)] + _ffn_weight_specs(n_blocks, cb)
    out_specs = [
        pl.BlockSpec((tm, D_MODEL), lambda i, c: (i, 0)),
        pl.BlockSpec((1, FFN_CONV - 1, 2, cb), lambda i, c: (i, 0, 0, c)),
    ]
    out_shape = [jax.ShapeDtypeStruct((rows, D_MODEL), F32),
                 jax.ShapeDtypeStruct((rows // tm, FFN_CONV - 1, 2, n_blocks * cb), F32)]
    scratch = [
        pltpu.VMEM((tm, D_MODEL), BF16),
        pltpu.VMEM((SUBLANES + tm, 2 * cb), F32),
        pltpu.VMEM((tm, cb), BF16),
        pltpu.VMEM((n_blocks, SUBLANES, 2 * cb), F32),
    ]
    return pl.pallas_call(
        functools.partial(_ffn_prompt_kernel, tm=tm, tiles_per_seq=tiles_per_seq,
                          n_blocks=n_blocks, cb=cb),
        grid=(rows // tm, n_blocks),
        in_specs=in_specs, out_specs=out_specs, out_shape=out_shape,
        scratch_shapes=scratch, compiler_params=_cparams(2), name="ffn_prompt",
    )(x1, ada, ada, ada, p["g_pre2"], p["g_post2"], wup_b, p["w_ffn_conv"], p["w_ffn_conv"],
      p["b_ffn_conv"], p["b_ffn_conv"], wdn_b)


def _ffn_sample(x1, ada, p, wup_b, wdn_b, sffn2, n_seq, n_t):
    rows = n_seq * n_t
    n_blocks, _, cb2 = wup_b.shape
    cb = cb2 // 2
    nk = FFN_CONV - 1
    ada_spec = lambda k: pl.BlockSpec((n_seq, D_MODEL), lambda i, c: (0, k))
    in_specs = [pl.BlockSpec((rows, D_MODEL), lambda i, c: (0, 0), pipeline_mode=pl.Buffered(1)),
                ada_spec(3), ada_spec(4), ada_spec(5)] + _ffn_weight_specs(n_blocks, cb) + [
        pl.BlockSpec((n_seq * nk, cb), lambda i, c: (0, c)),
        pl.BlockSpec((n_seq * nk, cb), lambda i, c: (0, n_blocks + c)),
    ]
    out_specs = [
        pl.BlockSpec((rows, D_MODEL), lambda i, c: (0, 0)),
        pl.BlockSpec((n_seq, nk, 2, cb), lambda i, c: (0, 0, 0, c)),
    ]
    out_shape = [jax.ShapeDtypeStruct((rows, D_MODEL), F32),
                 jax.ShapeDtypeStruct((n_seq, nk, 2, n_blocks * cb), F32)]
    scratch = [
        pltpu.VMEM((rows, D_MODEL), BF16),
        pltpu.VMEM((rows, 2 * cb), F32),
        pltpu.VMEM((rows, cb), BF16),
    ]
    return pl.pallas_call(
        functools.partial(_ffn_sample_kernel, n_seq=n_seq, n_t=n_t, n_blocks=n_blocks, cb=cb),
        grid=(1, n_blocks),
        in_specs=in_specs, out_specs=out_specs, out_shape=out_shape,
        scratch_shapes=scratch, compiler_params=_cparams(2), name="ffn_sample",
    )(x1, ada, ada, ada, p["g_pre2"], p["g_post2"], wup_b, p["w_ffn_conv"], p["w_ffn_conv"],
      p["b_ffn_conv"], p["b_ffn_conv"], wdn_b, sffn2, sffn2)


TOKEN_TILE = 1024
MERGE_TILE = 2048
OUTPROJ_TILE = 512


def kernel(x_prompt, x_sample, c_prompt, c_sample, state_pool, state_lru_conv, state_lru_h, state_ffn_conv, w_ada, b_ada, g_pre1, g_post1, g_pre2, g_post2, w_in, w_pool_grp, pool_scale, w_lru_conv, b_lru_conv, w_rg, b_rg, w_ig, b_ig, lru_lambda, w_pool_up, w_lru_up, w_out, w_ffn_up, w_ffn_conv, b_ffn_conv, w_ffn_down):
    batch, seq, d = x_prompt.shape
    dec_batch, dec_seq, _ = x_sample.shape
    depth = w_ada.shape[0]
    assert d == D_MODEL and dec_batch % SUBLANES == 0 and seq % TOKEN_TILE == 0
    assert w_in.shape[2] == N_MIX_BLOCKS * CB + 2 * D_MODEL

    pad = (-batch) % SUBLANES
    c_all = jnp.concatenate([c_sample, c_prompt, jnp.zeros((pad, d), c_prompt.dtype)], axis=0)
    prompt_row_block = dec_batch // SUBLANES

    vec_names = ("g_pre1", "g_post1", "g_pre2", "g_post2", "pool_scale", "b_lru_conv", "b_rg",
                 "b_ig", "lru_lambda", "b_ffn_conv")
    weights = dict(w_ada=w_ada, b_ada=b_ada, g_pre1=g_pre1, g_post1=g_post1, g_pre2=g_pre2,
                   g_post2=g_post2, w_in=w_in, w_pool_grp=w_pool_grp, pool_scale=pool_scale,
                   w_lru_conv=w_lru_conv, b_lru_conv=b_lru_conv, w_rg=w_rg, b_rg=b_rg, w_ig=w_ig,
                   b_ig=b_ig, lru_lambda=lru_lambda, w_pool_up=w_pool_up, w_lru_up=w_lru_up,
                   w_out=w_out, w_ffn_up=w_ffn_up, w_ffn_conv=w_ffn_conv, b_ffn_conv=b_ffn_conv,
                   w_ffn_down=w_ffn_down)

    def time_major(a):
        return jnp.swapaxes(a, 0, 1).reshape(-1, a.shape[-1])

    def seq_major(a2, n_rows):
        return jnp.swapaxes(a2.reshape(n_rows, dec_batch, -1), 0, 1)

    xp = x_prompt.reshape(batch * seq, d)
    xs = time_major(x_sample)
    tps = seq // TOKEN_TILE
    outs_p = ([], [], [], [])
    outs_s = ([], [], [], [])
    for l in range(depth):
        p = {k: v[l] for k, v in weights.items()}
        for k in vec_names + ("b_ada",):
            p[k] = p[k].reshape(1, -1)
        ada = _ada(c_all, p["w_ada"], p["b_ada"])

        y, npool, nconv, nh, h, wdn_b = _mix_prompt(xp, ada, p, batch, seq, prompt_row_block)
        m, wup_b = _merge(h, y, p, MERGE_TILE, "merge_prompt", cast_ffn_cb=FFN_CB)
        x1 = _outproj(m, xp, ada, p, OUTPROJ_TILE, seq // OUTPROJ_TILE, prompt_row_block, 0)
        xp, nffn = _ffn_prompt(x1, ada, p, wup_b, wdn_b, batch, TOKEN_TILE, tps, prompt_row_block)
        outs_p[0].append(npool)
        outs_p[1].append(nconv)
        outs_p[2].append(nh.reshape(batch, LRU_WIDTH))
        outs_p[3].append(nffn[tps - 1::tps].reshape(batch, FFN_CONV - 1, -1))

        rows_s = dec_batch * dec_seq
        y, npool, nconv, nh, h = _mix_sample(
            xs, ada, p, time_major(state_pool[l]), time_major(state_lru_conv[l]),
            state_lru_h[l], dec_batch, dec_seq, PAST_LEN)
        m, = _merge(h, y, p, rows_s, "merge_sample")
        x1 = _outproj(m, xs, ada, p, rows_s, 1, 0, dec_seq)
        xs, nffn = _ffn_sample(x1, ada, p, wup_b, wdn_b, time_major(state_ffn_conv[l]),
                               dec_batch, dec_seq)
        outs_s[0].append(seq_major(npool, POOL_BUF))
        outs_s[1].append(seq_major(nconv, LRU_CONV - 1))
        outs_s[2].append(nh)
        outs_s[3].append(nffn.reshape(dec_batch, FFN_CONV - 1, -1))

    return (xp.reshape(batch, seq, d), seq_major(xs, dec_seq),
            jnp.stack(outs_p[0]), jnp.stack(outs_p[1]), jnp.stack(outs_p[2]), jnp.stack(outs_p[3]),
            jnp.stack(outs_s[0]), jnp.stack(outs_s[1]), jnp.stack(outs_s[2]), jnp.stack(outs_s[3]))
```

```python
import functools

import jax
import jax.numpy as jnp
from jax import lax
from jax.experimental import pallas as pl
from jax.experimental.pallas import tpu as pltpu

F32 = jnp.float32
BF16 = jnp.bfloat16

D_MODEL = 2048
POOL_WINDOWS = (2, 4, 8, 16)
POOL_GROUPS = len(POOL_WINDOWS)
POOL_BUF = max(POOL_WINDOWS) - 1
LRU_CONV = 4
LRU_C = 8.0
PAST_LEN = 16384
FFN_CONV = 3
N_ADA = 6
EPS = 1e-6

CB = 256
POOL_WIDTH = POOL_GROUPS * CB
LRU_BLOCKS = 8
LRU_WIDTH = LRU_BLOCKS * CB
N_MIX_BLOCKS = POOL_GROUPS + LRU_BLOCKS
CHUNK = 64
HALO = CHUNK
SUBLANES = 8
V7X_VMEM_BYTES = 64 * 1024 * 1024
VMEM_LIMIT = V7X_VMEM_BYTES - 4 * 1024 * 1024


def _cparams(n_axes):
    return pltpu.CompilerParams(
        dimension_semantics=("arbitrary",) * n_axes, vmem_limit_bytes=VMEM_LIMIT)


def _dot(a, b):
    return jnp.dot(a, b, preferred_element_type=F32)


def _softplus(z):
    return jnp.maximum(z, 0.0) + jnp.log1p(jnp.exp(-jnp.abs(z)))


def _lru_coeffs(xc, r_pre, i_pre, neg_c_sp):
    r = jax.nn.sigmoid(r_pre)
    i = jax.nn.sigmoid(i_pre)
    log_a = r * neg_c_sp
    a = jnp.exp(log_a)
    m2 = -jnp.tanh(log_a) * (a * a + 1.0)
    root = jnp.where(m2 > 0.0, m2 * lax.rsqrt(m2), 0.0)
    return a, root * (i * xc)


ADA_K = 256
ADA_N = 1024


def _ada_kernel(c_ref, w_ref, b_ref, o_ref):
    rows, n_out = o_ref.shape

    @pl.when(pl.program_id(0) == 0)
    def _():
        o_ref[...] = jnp.broadcast_to(b_ref[...], (rows, n_out))

    c = c_ref[...]
    s = (c * jax.nn.sigmoid(c)).astype(BF16)
    for n0 in range(0, n_out, ADA_N):
        o_ref[:, n0:n0 + ADA_N] += _dot(s, w_ref[:, n0:n0 + ADA_N].astype(BF16))


def _ada(c_all, w_ada, b_ada):
    rows = c_all.shape[0]
    n_out = N_ADA * D_MODEL
    return pl.pallas_call(
        _ada_kernel,
        grid=(D_MODEL // ADA_K,),
        in_specs=[pl.BlockSpec((rows, ADA_K), lambda k: (0, k)),
                  pl.BlockSpec((ADA_K, n_out), lambda k: (k, 0)),
                  pl.BlockSpec((1, n_out), lambda k: (0, 0))],
        out_specs=pl.BlockSpec((rows, n_out), lambda k: (0, 0)),
        out_shape=jax.ShapeDtypeStruct((rows, n_out), F32),
        compiler_params=_cparams(1),
        name="ada",
    )(c_all, w_ada, b_ada)


NORM_ROWS = 32
NORM_UNROLL = 8
OUT_SUBTILE = 256


def _unit_rms(x):
    return x * lax.rsqrt(jnp.mean(x * x, axis=-1, keepdims=True) + EPS)


def _build_h_prompt(x_ref, h_ref, g, scale1p, shift, n_rows):
    gs = g * scale1p

    def body(i, carry):
        r0 = pl.multiple_of(i * NORM_ROWS, NORM_ROWS)
        x = x_ref[pl.ds(r0, NORM_ROWS), :]
        h_ref[pl.ds(r0, NORM_ROWS), :] = (_unit_rms(x) * gs + shift).astype(BF16)
        return carry
    lax.fori_loop(0, n_rows // NORM_ROWS, body, 0, unroll=NORM_UNROLL)


def _residual_norm_prompt(x_ref, acc_ref, o_ref, g, gate, n_rows):
    gg = g * gate
    for r0 in range(0, n_rows, NORM_ROWS):
        sl = slice(r0, r0 + NORM_ROWS)
        o_ref[sl, :] = x_ref[sl, :] + _unit_rms(acc_ref[sl, :]) * gg


def _build_h_sample(x_ref, h_ref, g, sc_ref, sh_ref, n_seq, n_t):
    for s0 in range(0, n_seq, NORM_ROWS):
        gs = g * (1.0 + sc_ref[s0:s0 + NORM_ROWS, :])
        shift = sh_ref[s0:s0 + NORM_ROWS, :]
        for t in range(n_t):
            sl = slice(t * n_seq + s0, t * n_seq + s0 + NORM_ROWS)
            h_ref[sl, :] = (_unit_rms(x_ref[sl, :]) * gs + shift).astype(BF16)


MIX_SUBTILE = 512
POOL_ROWS = 128


def _mix_prompt_kernel(x_hbm, sh_ref, sc_ref, g_ref, win_ref, wgrp_ref, pscale_ref,
                       wconv_ref, bconv_ref, wrg_ref, brg_ref, wig_ref, big_ref, lam_ref, wdn_ref,
                       y_ref, npool_ref, nconv_ref, nh_ref, h_ref, wdnb_ref,
                       x_ref, x_sem, ext_ref, xb_ref, r_ref, i_ref, a_ref, *, seq, n_seq):
    q = pl.program_id(0)
    j = pl.program_id(1)
    _cast_slab(wdn_ref, wdnb_ref)

    def x_copy(s):
        rows = pl.ds(pl.multiple_of(s * seq, seq), seq)
        return pltpu.make_async_copy(x_hbm.at[rows, :], x_ref, x_sem)

    @pl.when(jnp.logical_and(q == 0, j == 0))
    def _():
        x_copy(0).start()

    @pl.when(jnp.logical_and(j == 1, q + 1 < n_seq))
    def _():
        x_copy(q + 1).start()

    @pl.when(j == 0)
    def _():
        x_copy(q).wait()
        _build_h_prompt(x_ref, h_ref, g_ref[...], 1.0 + sc_ref[pl.ds(q, 1), :],
                        sh_ref[pl.ds(q, 1), :], seq)
        for ref in (ext_ref, r_ref, i_ref, a_ref):
            ref[0:HALO, :] = jnp.zeros((HALO, CB), F32)

    win = win_ref[...].astype(BF16)

    def rows_of(r0, n, shift=0):
        return slice(HALO + r0 - shift, HALO + r0 - shift + n)

    def up_proj(m0):
        ext_ref[rows_of(m0, MIX_SUBTILE), :] = _dot(h_ref[m0:m0 + MIX_SUBTILE, :], win)

    def pool_branch(w):
        wg = wgrp_ref[0].astype(BF16)
        ps = pscale_ref[...]
        partial = {2: r_ref, 4: i_ref, 8: a_ref}
        up_proj(0)
        for m0 in range(0, seq, MIX_SUBTILE):
            if m0 + MIX_SUBTILE < seq:
                up_proj(m0 + MIX_SUBTILE)
            for r0 in range(m0, m0 + MIX_SUBTILE, POOL_ROWS):
                u = ext_ref[rows_of(r0, POOL_ROWS), :]
                s, src, width = u, ext_ref, 1
                while width < w:
                    s = s + src[rows_of(r0, POOL_ROWS, width), :]
                    width *= 2
                    if width < w:
                        src = partial[width]
                        src[rows_of(r0, POOL_ROWS), :] = s
                if r0 < w:
                    pos = r0 + lax.broadcasted_iota(jnp.int32, (POOL_ROWS, 1), 0)
                    cnt = jnp.minimum(w, pos + 1).astype(F32)
                else:
                    cnt = float(w)
                xb_ref[r0:r0 + POOL_ROWS, :] = (s / cnt - u).astype(BF16)
            sub = slice(m0, m0 + MIX_SUBTILE)
            y_ref[sub, :] = (_dot(xb_ref[sub, :], wg) * ps).astype(BF16)
        npool_ref[0] = ext_ref[pl.ds(HALO + seq - POOL_BUF, POOL_BUF), :]

    for g, w in enumerate(POOL_WINDOWS):
        pl.when(j == g)(functools.partial(pool_branch, w))

    @pl.when(j >= POOL_GROUPS)
    def _():
        wc = wconv_ref[...]
        bc = bconv_ref[...]
        wrg = wrg_ref[0].astype(BF16)
        wig = wig_ref[0].astype(BF16)
        neg_c_sp = (-LRU_C) * _softplus(-lam_ref[...])
        brg = brg_ref[...]
        big = big_ref[...]
        nk = LRU_CONV - 1
        group = lax.broadcasted_iota(jnp.int32, (SUBLANES, CB), 0)
        w_tap = [jnp.broadcast_to(wc[k:k + 1, :], (SUBLANES, CB)) for k in range(LRU_CONV)]
        b_tap = jnp.broadcast_to(bc, (SUBLANES, CB))

        def regroup(v):
            return jnp.swapaxes(v.reshape(SUBLANES, SUBLANES, CB), 0, 1).reshape(CHUNK, CB)

        def split(v):
            return [v[SUBLANES * p:SUBLANES * (p + 1), :] for p in range(SUBLANES)]

        def group_before(cur, prev):
            return jnp.where(group == 0, pltpu.roll(prev, 1, 0), pltpu.roll(cur, 1, 0))

        h_carry = jnp.zeros((1, CB), F32)
        up_proj(0)
        for m0 in range(0, seq, MIX_SUBTILE):
            sub = slice(m0, m0 + MIX_SUBTILE)
            if m0 + MIX_SUBTILE < seq:
                up_proj(m0 + MIX_SUBTILE)
            else:
                nconv_ref[0] = ext_ref[pl.ds(HALO + seq - nk, nk), :]
            for r0 in range(m0, m0 + MIX_SUBTILE, CHUNK):
                ext_ref[rows_of(r0, CHUNK), :] = regroup(ext_ref[rows_of(r0, CHUNK), :])
            for r0 in range(m0, m0 + MIX_SUBTILE, CHUNK):
                cur = split(ext_ref[rows_of(r0, CHUNK), :])
                prev_tail = split(ext_ref[rows_of(r0, CHUNK, CHUNK), :])[SUBLANES - nk:]
                wrapped = [group_before(cur[SUBLANES - nk + t], prev_tail[t]) for t in range(nk)]
                rows = []
                for p in range(SUBLANES):
                    xc = w_tap[nk] * cur[p] + b_tap
                    for k in range(1, LRU_CONV):
                        src = cur[p - k] if p >= k else wrapped[nk + p - k]
                        xc = xc + w_tap[nk - k] * src
                    rows.append(xc)
                xc = jnp.concatenate(rows, axis=0)
                a_ref[rows_of(r0, CHUNK), :] = xc
                xb_ref[r0:r0 + CHUNK, :] = xc.astype(BF16)
            r_ref[rows_of(m0, MIX_SUBTILE), :] = _dot(xb_ref[sub, :], wrg)
            i_ref[rows_of(m0, MIX_SUBTILE), :] = _dot(xb_ref[sub, :], wig)
            for r0 in range(m0, m0 + MIX_SUBTILE, CHUNK):
                sl = rows_of(r0, CHUNK)
                a, b = _lru_coeffs(a_ref[sl, :], r_ref[sl, :] + brg, i_ref[sl, :] + big, neg_c_sp)
                a, b = split(a), split(b)
                hs, ps = [b[0]], [a[0]]
                for p in range(1, SUBLANES):
                    hs.append(a[p] * hs[-1] + b[p])
                    ps.append(a[p] * ps[-1])
                e_p, e_h = ps[-1], hs[-1]
                for k in (1, 2, 4):
                    p_sh = jnp.where(group >= k, pltpu.roll(e_p, k, 0), 1.0)
                    h_sh = jnp.where(group >= k, pltpu.roll(e_h, k, 0), 0.0)
                    e_h = e_h + e_p * h_sh
                    e_p = e_p * p_sh
                ends = e_p * h_carry + e_h
                h_in = jnp.where(group == 0, h_carry, pltpu.roll(ends, 1, 0))
                h = jnp.concatenate([hs[p] + ps[p] * h_in for p in range(SUBLANES)], axis=0)
                y_ref[r0:r0 + CHUNK, :] = regroup(h).astype(BF16)
                h_carry = ends[SUBLANES - 1:SUBLANES, :]
        nh_ref[0] = h_carry


def _mix_sample_kernel(x_ref, sh_ref, sc_ref, g_ref, win_ref, wgrp_ref, pscale_ref,
                       wconv_ref, bconv_ref, wrg_ref, brg_ref, wig_ref, big_ref, lam_ref,
                       spool_ref, sconv_ref, sh0_ref,
                       y_ref, npool_ref, nconv_ref, nh_ref, h_ref,
                       u_ref, d_ref, *, n_seq, n_t, start):
    j = pl.program_id(1)

    @pl.when(j == 0)
    def _():
        _build_h_sample(x_ref, h_ref, g_ref[...], sc_ref, sh_ref, n_seq, n_t)

    u_ref[...] = _dot(h_ref[...], win_ref[...].astype(BF16))

    def u_slab(t):
        return u_ref[t * n_seq:(t + 1) * n_seq, :]

    def pool_branch(w):
        e = [spool_ref[k * n_seq:(k + 1) * n_seq, :] for k in range(POOL_BUF)]
        e += [u_slab(t) for t in range(n_t)]
        for k in range(POOL_BUF):
            npool_ref[k * n_seq:(k + 1) * n_seq, :] = e[n_t + k]
        for t in range(n_t):
            s = e[POOL_BUF + t]
            for k in range(1, w):
                s = s + e[POOL_BUF + t - k]
            cnt = float(min(w, start + t + 1))
            d_ref[t * n_seq:(t + 1) * n_seq, :] = (s / cnt - e[POOL_BUF + t]).astype(BF16)
        y = _dot(d_ref[...], wgrp_ref[0].astype(BF16)) * pscale_ref[...]
        y_ref[...] = y.astype(BF16)

    for g, w in enumerate(POOL_WINDOWS):
        pl.when(j == g)(functools.partial(pool_branch, w))

    @pl.when(j >= POOL_GROUPS)
    def _():
        nk = LRU_CONV - 1
        e = [sconv_ref[k * n_seq:(k + 1) * n_seq, :] for k in range(nk)]
        e += [u_slab(t) for t in range(n_t)]
        for k in range(nk):
            nconv_ref[k * n_seq:(k + 1) * n_seq, :] = e[n_t + k]
        wc = wconv_ref[...]
        bc = bconv_ref[...]
        wrg = wrg_ref[0].astype(BF16)
        wig = wig_ref[0].astype(BF16)
        neg_c_sp = (-LRU_C) * _softplus(-lam_ref[...])
        h = sh0_ref[...]
        for t in range(n_t):
            xc = bc
            for k in range(LRU_CONV):
                xc = xc + wc[k:k + 1, :] * e[t + k]
            xb = xc.astype(BF16)
            a, b = _lru_coeffs(xc, _dot(xb, wrg) + brg_ref[...], _dot(xb, wig) + big_ref[...],
                               neg_c_sp)
            h = a * h + b
            y_ref[t * n_seq:(t + 1) * n_seq, :] = h.astype(BF16)
        nh_ref[...] = h


def _mix_weight_specs():
    pj = lambda j: jnp.minimum(j, POOL_GROUPS - 1)
    lj = lambda j: jnp.maximum(j - POOL_GROUPS, 0)
    return [
        pl.BlockSpec((1, D_MODEL), lambda q, j: (0, 0)),
        pl.BlockSpec((D_MODEL, CB), lambda q, j: (0, j)),
        pl.BlockSpec((1, CB, CB), lambda q, j: (pj(j), 0, 0)),
        pl.BlockSpec((1, CB), lambda q, j: (0, pj(j))),
        pl.BlockSpec((LRU_CONV, CB), lambda q, j: (0, lj(j))),
        pl.BlockSpec((1, CB), lambda q, j: (0, lj(j))),
        pl.BlockSpec((1, CB, CB), lambda q, j: (lj(j), 0, 0)),
        pl.BlockSpec((1, CB), lambda q, j: (0, lj(j))),
        pl.BlockSpec((1, CB, CB), lambda q, j: (lj(j), 0, 0)),
        pl.BlockSpec((1, CB), lambda q, j: (0, lj(j))),
        pl.BlockSpec((1, CB), lambda q, j: (0, lj(j))),
    ], pj, lj


def _mix_prompt(x2, ada, p, n_seq, seq, ada_row_block):
    wspecs, pj, lj = _mix_weight_specs()
    in_specs = [
        pl.BlockSpec(memory_space=pl.ANY),
        pl.BlockSpec((SUBLANES, D_MODEL), lambda q, j: (ada_row_block, 0)),
        pl.BlockSpec((SUBLANES, D_MODEL), lambda q, j: (ada_row_block, 1)),
    ] + wspecs
    d_ff = p["w_ffn_down"].shape[0]
    dn_rows = d_ff // (n_seq * N_MIX_BLOCKS)
    assert dn_rows * n_seq * N_MIX_BLOCKS == d_ff and dn_rows % BF16_ROWS == 0
    slab_spec = pl.BlockSpec((dn_rows, D_MODEL), lambda q, j: (q * N_MIX_BLOCKS + j, 0))
    in_specs.append(slab_spec)
    out_specs = [
        pl.BlockSpec((seq, CB), lambda q, j: (q, j)),
        pl.BlockSpec((1, POOL_BUF, CB), lambda q, j: (q, 0, pj(j))),
        pl.BlockSpec((1, LRU_CONV - 1, CB), lambda q, j: (q, 0, lj(j))),
        pl.BlockSpec((1, 1, CB), lambda q, j: (q, 0, lj(j))),
        pl.BlockSpec((seq, D_MODEL), lambda q, j: (q, 0)),
        slab_spec,
    ]
    out_shape = [
        jax.ShapeDtypeStruct((n_seq * seq, N_MIX_BLOCKS * CB), BF16),
        jax.ShapeDtypeStruct((n_seq, POOL_BUF, POOL_WIDTH), F32),
        jax.ShapeDtypeStruct((n_seq, LRU_CONV - 1, LRU_WIDTH), F32),
        jax.ShapeDtypeStruct((n_seq, 1, LRU_WIDTH), F32),
        jax.ShapeDtypeStruct((n_seq * seq, D_MODEL), BF16),
        jax.ShapeDtypeStruct((d_ff, D_MODEL), BF16),
    ]
    scratch = [
        pltpu.VMEM((seq, D_MODEL), F32),
        pltpu.SemaphoreType.DMA(()),
        pltpu.VMEM((HALO + seq, CB), F32),
        pltpu.VMEM((seq, CB), BF16),
        pltpu.VMEM((HALO + seq, CB), F32),
        pltpu.VMEM((HALO + seq, CB), F32),
        pltpu.VMEM((HALO + seq, CB), F32),
    ]
    return pl.pallas_call(
        functools.partial(_mix_prompt_kernel, seq=seq, n_seq=n_seq),
        grid=(n_seq, N_MIX_BLOCKS),
        in_specs=in_specs, out_specs=out_specs, out_shape=out_shape,
        scratch_shapes=scratch, compiler_params=_cparams(2), name="mix_prompt",
    )(x2, ada, ada, p["g_pre1"], p["w_in"], p["w_pool_grp"], p["pool_scale"],
      p["w_lru_conv"], p["b_lru_conv"], p["w_rg"], p["b_rg"], p["w_ig"], p["b_ig"],
      p["lru_lambda"], p["w_ffn_down"])


def _mix_sample(x2, ada, p, spool2, sconv2, sh0, n_seq, n_t, start):
    wspecs, pj, lj = _mix_weight_specs()
    rows = n_seq * n_t
    in_specs = [
        pl.BlockSpec((rows, D_MODEL), lambda q, j: (0, 0), pipeline_mode=pl.Buffered(1)),
        pl.BlockSpec((n_seq, D_MODEL), lambda q, j: (0, 0)),
        pl.BlockSpec((n_seq, D_MODEL), lambda q, j: (0, 1)),
    ] + wspecs + [
        pl.BlockSpec((n_seq * POOL_BUF, CB), lambda q, j: (0, pj(j))),
        pl.BlockSpec((n_seq * (LRU_CONV - 1), CB), lambda q, j: (0, lj(j))),
        pl.BlockSpec((n_seq, CB), lambda q, j: (0, lj(j))),
    ]
    out_specs = [
        pl.BlockSpec((rows, CB), lambda q, j: (0, j)),
        pl.BlockSpec((n_seq * POOL_BUF, CB), lambda q, j: (0, pj(j))),
        pl.BlockSpec((n_seq * (LRU_CONV - 1), CB), lambda q, j: (0, lj(j))),
        pl.BlockSpec((n_seq, CB), lambda q, j: (0, lj(j))),
        pl.BlockSpec((rows, D_MODEL), lambda q, j: (0, 0)),
    ]
    out_shape = [
        jax.ShapeDtypeStruct((rows, N_MIX_BLOCKS * CB), BF16),
        jax.ShapeDtypeStruct((n_seq * POOL_BUF, POOL_WIDTH), F32),
        jax.ShapeDtypeStruct((n_seq * (LRU_CONV - 1), LRU_WIDTH), F32),
        jax.ShapeDtypeStruct((n_seq, LRU_WIDTH), F32),
        jax.ShapeDtypeStruct((rows, D_MODEL), BF16),
    ]
    scratch = [
        pltpu.VMEM((rows, CB), F32),
        pltpu.VMEM((rows, CB), BF16),
    ]
    return pl.pallas_call(
        functools.partial(_mix_sample_kernel, n_seq=n_seq, n_t=n_t, start=start),
        grid=(1, N_MIX_BLOCKS),
        in_specs=in_specs, out_specs=out_specs, out_shape=out_shape,
        scratch_shapes=scratch, compiler_params=_cparams(2), name="mix_sample",
    )(x2, ada, ada, p["g_pre1"], p["w_in"], p["w_pool_grp"], p["pool_scale"],
      p["w_lru_conv"], p["b_lru_conv"], p["w_rg"], p["b_rg"], p["w_ig"], p["b_ig"],
      p["lru_lambda"], spool2, sconv2, sh0)


MERGE_ROWS = 64
MERGE_SUBTILE = 256
CAST_ROWS = 32
BF16_ROWS = 16


def _cast_ffn_up_slab(wup_ref, wupb_ref):
    n_blocks, _, cb2 = wupb_ref.shape
    cb = cb2 // 2
    for blk in range(n_blocks):
        wupb_ref[blk, :, 0:cb] = wup_ref[:, blk * cb:(blk + 1) * cb].astype(BF16)
        wupb_ref[blk, :, cb:cb2] = wup_ref[:, (n_blocks + blk) * cb:(n_blocks + blk + 1) * cb].astype(BF16)


def _cast_slab(w_ref, wb_ref):
    for r0 in range(0, w_ref.shape[0], CAST_ROWS):
        wb_ref[r0:r0 + CAST_ROWS, :] = w_ref[r0:r0 + CAST_ROWS, :].astype(BF16)


def _merge_kernel(h_ref, y_ref, wgp_ref, wgl_ref, wpu_ref, wlu_ref, *rest, tm, cast_ffn):
    if cast_ffn:
        wup_ref, m_ref, wupb_ref, wg_ref, g_ref, pu_ref, lu_ref = rest
    else:
        m_ref, wg_ref, g_ref, pu_ref, lu_ref = rest
    wg_ref[:, 0:CB] = wgp_ref[...].astype(BF16)
    wg_ref[:, CB:2 * CB] = wgl_ref[...].astype(BF16)
    wpu = wpu_ref[...].astype(BF16)
    wlu = wlu_ref[...].astype(BF16)

    def slot(m0):
        s0 = (m0 // MERGE_SUBTILE) % 2 * MERGE_SUBTILE
        return slice(s0, s0 + MERGE_SUBTILE)

    def proj(m0):
        rows = slice(m0, m0 + MERGE_SUBTILE)
        g_ref[slot(m0), :] = _dot(h_ref[rows, :], wg_ref[...])
        pu_ref[slot(m0), :] = _dot(y_ref[rows, 0:POOL_WIDTH], wpu)
        lu_ref[slot(m0), :] = _dot(y_ref[rows, POOL_WIDTH:POOL_WIDTH + LRU_WIDTH], wlu)

    proj(0)
    for m0 in range(0, tm, MERGE_SUBTILE):
        if m0 + MERGE_SUBTILE < tm:
            proj(m0 + MERGE_SUBTILE)
        s0 = slot(m0).start
        for r0 in range(0, MERGE_SUBTILE, MERGE_ROWS):
            sl = slice(s0 + r0, s0 + r0 + MERGE_ROWS)
            m = (jax.nn.sigmoid(g_ref[sl, 0:CB]) * pu_ref[sl, :]
                 + jax.nn.sigmoid(g_ref[sl, CB:2 * CB]) * lu_ref[sl, :])
            m_ref[m0 + r0:m0 + r0 + MERGE_ROWS, :] = m.astype(BF16)
    if cast_ffn:
        _cast_ffn_up_slab(wup_ref, wupb_ref)


def _merge(h, y, p, tm, name, cast_ffn_cb=0):
    rows = h.shape[0]
    n_col = D_MODEL // CB
    n_gate0 = N_MIX_BLOCKS
    n_gate1 = N_MIX_BLOCKS + n_col
    in_specs = [
        pl.BlockSpec((tm, D_MODEL), lambda i, c: (i, 0), pipeline_mode=pl.Buffered(1)),
        pl.BlockSpec((tm, N_MIX_BLOCKS * CB), lambda i, c: (i, 0), pipeline_mode=pl.Buffered(1)),
        pl.BlockSpec((D_MODEL, CB), lambda i, c: (0, n_gate0 + c)),
        pl.BlockSpec((D_MODEL, CB), lambda i, c: (0, n_gate1 + c)),
        pl.BlockSpec((POOL_WIDTH, CB), lambda i, c: (0, c)),
        pl.BlockSpec((LRU_WIDTH, CB), lambda i, c: (0, c)),
    ]
    out_specs = [pl.BlockSpec((tm, CB), lambda i, c: (i, c))]
    out_shape = [jax.ShapeDtypeStruct((rows, D_MODEL), BF16)]
    args = [h, y, p["w_in"], p["w_in"], p["w_pool_up"], p["w_lru_up"]]
    if cast_ffn_cb:
        n_steps = (rows // tm) * n_col
        d_ff = p["w_ffn_down"].shape[0]
        n_blocks = d_ff // cast_ffn_cb
        up_rows = D_MODEL // n_steps
        assert up_rows * n_steps == D_MODEL and up_rows % BF16_ROWS == 0
        step = lambda i, c: i * n_col + c
        in_specs += [pl.BlockSpec((up_rows, 2 * d_ff), lambda i, c: (step(i, c), 0))]
        out_specs += [pl.BlockSpec((n_blocks, up_rows, 2 * cast_ffn_cb),
                                   lambda i, c: (0, step(i, c), 0))]
        out_shape += [jax.ShapeDtypeStruct((n_blocks, D_MODEL, 2 * cast_ffn_cb), BF16)]
        args += [p["w_ffn_up"]]
    ring = 2 * MERGE_SUBTILE
    return pl.pallas_call(
        functools.partial(_merge_kernel, tm=tm, cast_ffn=bool(cast_ffn_cb)),
        grid=(rows // tm, n_col),
        in_specs=in_specs, out_specs=out_specs, out_shape=out_shape,
        scratch_shapes=[pltpu.VMEM((D_MODEL, 2 * CB), BF16), pltpu.VMEM((ring, 2 * CB), F32),
                        pltpu.VMEM((ring, CB), F32), pltpu.VMEM((ring, CB), F32)],
        compiler_params=_cparams(2), name=name,
    )(*args)


def _outproj_kernel(m_ref, x_ref, gate_ref, g_ref, w_ref, o_ref, wb_ref, acc_ref,
                    *, tm, tiles_per_seq, sample_t):
    i = pl.program_id(0)

    @pl.when(i == 0)
    def _():
        _cast_slab(w_ref, wb_ref)

    g = g_ref[...]
    if sample_t:
        n_seq = tm // sample_t
    else:
        gg_tile = g * gate_ref[pl.ds(i // tiles_per_seq, 1), :]

    def proj(m0):
        acc_ref[m0:m0 + OUT_SUBTILE, :] = _dot(m_ref[m0:m0 + OUT_SUBTILE, :], wb_ref[...])

    proj(0)
    for m0 in range(0, tm, OUT_SUBTILE):
        if m0 + OUT_SUBTILE < tm:
            proj(m0 + OUT_SUBTILE)
        for r0 in range(m0, m0 + OUT_SUBTILE, NORM_ROWS):
            sl = slice(r0, r0 + NORM_ROWS)
            if sample_t:
                s0 = r0 % n_seq
                gg = g * gate_ref[s0:s0 + NORM_ROWS, :]
            else:
                gg = gg_tile
            o_ref[sl, :] = x_ref[sl, :] + _unit_rms(acc_ref[sl, :]) * gg


def _outproj(m, x2, ada, p, tm, tiles_per_seq, ada_row_block, sample_t):
    rows = x2.shape[0]
    if sample_t:
        gate_spec = pl.BlockSpec((tm // sample_t, D_MODEL), lambda i: (0, 2))
    else:
        gate_spec = pl.BlockSpec((SUBLANES, D_MODEL), lambda i: (ada_row_block, 2))
    return pl.pallas_call(
        functools.partial(_outproj_kernel, tm=tm, tiles_per_seq=tiles_per_seq, sample_t=sample_t),
        grid=(rows // tm,),
        in_specs=[pl.BlockSpec((tm, D_MODEL), lambda i: (i, 0)),
                  pl.BlockSpec((tm, D_MODEL), lambda i: (i, 0)),
                  gate_spec,
                  pl.BlockSpec((1, D_MODEL), lambda i: (0, 0)),
                  pl.BlockSpec((D_MODEL, D_MODEL), lambda i: (0, 0), pipeline_mode=pl.Buffered(1))],
        out_specs=pl.BlockSpec((tm, D_MODEL), lambda i: (i, 0)),
        out_shape=jax.ShapeDtypeStruct((rows, D_MODEL), F32),
        scratch_shapes=[pltpu.VMEM((D_MODEL, D_MODEL), BF16), pltpu.VMEM((tm, D_MODEL), F32)],
        compiler_params=_cparams(1),
        name="outproj_sample" if sample_t else "outproj_prompt",
    )(m, x2, ada, p["g_post1"], p["w_out"])


GELU_C0 = 0.7978845608028654
GELU_C1 = GELU_C0 * 0.044715
FFN_ROWS = 32
FFN_SUBTILE = 512
FFN_AHEAD = 2
FFN_CB = 768


def _ffn_conv_gate(eg, ev, wcg, wcv, bcg, bcv):
    last = FFN_CONV - 1
    g = wcg[last:, :] * eg[last] + bcg
    v = wcv[last:, :] * ev[last] + bcv
    for k in range(last):
        g = g + wcg[k:k + 1, :] * eg[k]
        v = v + wcv[k:k + 1, :] * ev[k]
    t = jnp.tanh(g * (GELU_C0 + GELU_C1 * (g * g)))
    return ((g * v) * (0.5 + 0.5 * t)).astype(BF16)


def _ffn_prompt_kernel(x_ref, sh_ref, sc_ref, gate_ref, gpre_ref, gpost_ref,
                       wup_ref, wcg_ref, wcv_ref, bcg_ref, bcv_ref, wdn_ref,
                       o_ref, nst_ref,
                       h_ref, ext_ref, f_ref, carry_ref, *, tm, tiles_per_seq, n_blocks, cb):
    i = pl.program_id(0)
    c = pl.program_id(1)
    q = i // tiles_per_seq
    first = (i % tiles_per_seq) == 0
    nk = FFN_CONV - 1

    @pl.when(c == 0)
    def _():
        _build_h_prompt(x_ref, h_ref, gpre_ref[...], 1.0 + sc_ref[pl.ds(q, 1), :],
                        sh_ref[pl.ds(q, 1), :], tm)
        o_ref[...] = jnp.zeros((tm, D_MODEL), F32)

    @pl.when(first)
    def _():
        ext_ref[0:SUBLANES, :] = jnp.zeros((SUBLANES, 2 * cb), F32)

    @pl.when(jnp.logical_not(first))
    def _():
        ext_ref[0:SUBLANES, :] = carry_ref[c]

    wcg = wcg_ref[...]
    wcv = wcv_ref[...]
    bcg = bcg_ref[...]
    bcv = bcv_ref[...]

    def up_proj(m0):
        ext_ref[SUBLANES + m0:SUBLANES + m0 + FFN_SUBTILE, :] = _dot(
            h_ref[m0:m0 + FFN_SUBTILE, :], wup_ref[0])

    for m0 in range(0, min(FFN_AHEAD * FFN_SUBTILE, tm), FFN_SUBTILE):
        up_proj(m0)
    for m0 in range(0, tm, FFN_SUBTILE):
        rows = slice(m0, m0 + FFN_SUBTILE)
        for r0 in range(m0, m0 + FFN_SUBTILE, FFN_ROWS):
            shifted = [slice(SUBLANES + r0 - nk + k, SUBLANES + r0 - nk + k + FFN_ROWS)
                       for k in range(FFN_CONV)]
            eg = [ext_ref[s, 0:cb] for s in shifted]
            ev = [ext_ref[s, cb:2 * cb] for s in shifted]
            f_ref[r0:r0 + FFN_ROWS, :] = _ffn_conv_gate(eg, ev, wcg, wcv, bcg, bcv)
        if m0 + FFN_AHEAD * FFN_SUBTILE < tm:
            up_proj(m0 + FFN_AHEAD * FFN_SUBTILE)
        o_ref[rows, :] += _dot(f_ref[rows, :], wdn_ref[...])

    carry_ref[c] = ext_ref[tm:tm + SUBLANES, :]
    nst_ref[0, :, 0, :] = ext_ref[pl.ds(SUBLANES + tm - nk, nk), 0:cb]
    nst_ref[0, :, 1, :] = ext_ref[pl.ds(SUBLANES + tm - nk, nk), cb:2 * cb]

    @pl.when(c == n_blocks - 1)
    def _():
        _residual_norm_prompt(x_ref, o_ref, o_ref, gpost_ref[...], gate_ref[pl.ds(q, 1), :], tm)


def _ffn_sample_kernel(x_ref, sh_ref, sc_ref, gate_ref, gpre_ref, gpost_ref,
                       wup_ref, wcg_ref, wcv_ref, bcg_ref, bcv_ref, wdn_ref,
                       sg_ref, sv_ref,
                       o_ref, nst_ref,
                       h_ref, ext_ref, f_ref, *, n_seq, n_t, n_blocks, cb):
    c = pl.program_id(1)
    nk = FFN_CONV - 1

    @pl.when(c == 0)
    def _():
        _build_h_sample(x_ref, h_ref, gpre_ref[...], sc_ref, sh_ref, n_seq, n_t)
        o_ref[...] = jnp.zeros((n_seq * n_t, D_MODEL), F32)

    wcg = wcg_ref[...]
    wcv = wcv_ref[...]
    bcg = bcg_ref[...]
    bcv = bcv_ref[...]

    def view(state_ref, half, t, s0, n):
        if t < 0:
            return state_ref[(nk + t) * n_seq + s0:(nk + t) * n_seq + s0 + n, :]
        return ext_ref[t * n_seq + s0:t * n_seq + s0 + n, half * cb:(half + 1) * cb]

    t_sub = max(FFN_SUBTILE // n_seq, 1)

    def up_proj(t0):
        rows = slice(t0 * n_seq, (t0 + t_sub) * n_seq)
        ext_ref[rows, :] = _dot(h_ref[rows, :], wup_ref[0])

    for t0 in range(0, min(FFN_AHEAD * t_sub, n_t), t_sub):
        up_proj(t0)
    for t0 in range(0, n_t, t_sub):
        rows = slice(t0 * n_seq, (t0 + t_sub) * n_seq)
        for t in range(t0, t0 + t_sub):
            for s0 in range(0, n_seq, FFN_ROWS):
                eg = [view(sg_ref, 0, t - nk + k, s0, FFN_ROWS) for k in range(FFN_CONV)]
                ev = [view(sv_ref, 1, t - nk + k, s0, FFN_ROWS) for k in range(FFN_CONV)]
                f_ref[t * n_seq + s0:t * n_seq + s0 + FFN_ROWS, :] = _ffn_conv_gate(
                    eg, ev, wcg, wcv, bcg, bcv)
        if t0 + FFN_AHEAD * t_sub < n_t:
            up_proj(t0 + FFN_AHEAD * t_sub)
        o_ref[rows, :] += _dot(f_ref[rows, :], wdn_ref[...])

    for k in range(nk):
        nst_ref[:, k, 0, :] = view(sg_ref, 0, n_t - nk + k, 0, n_seq)
        nst_ref[:, k, 1, :] = view(sv_ref, 1, n_t - nk + k, 0, n_seq)

    @pl.when(c == n_blocks - 1)
    def _():
        g = gpost_ref[...]
        for s0 in range(0, n_seq, NORM_ROWS):
            gg = g * gate_ref[s0:s0 + NORM_ROWS, :]
            for t in range(n_t):
                sl = slice(t * n_seq + s0, t * n_seq + s0 + NORM_ROWS)
                o_ref[sl, :] = x_ref[sl, :] + _unit_rms(o_ref[sl, :]) * gg


def _ffn_weight_specs(n_blocks, cb):
    return [
        pl.BlockSpec((1, D_MODEL), lambda i, c: (0, 0)),
        pl.BlockSpec((1, D_MODEL), lambda i, c: (0, 0)),
        pl.BlockSpec((1, D_MODEL, 2 * cb), lambda i, c: (c, 0, 0)),
        pl.BlockSpec((FFN_CONV, cb), lambda i, c: (0, c)),
        pl.BlockSpec((FFN_CONV, cb), lambda i, c: (0, n_blocks + c)),
        pl.BlockSpec((1, cb), lambda i, c: (0, c)),
        pl.BlockSpec((1, cb), lambda i, c: (0, n_blocks + c)),
        pl.BlockSpec((cb, D_MODEL), lambda i, c: (c, 0)),
    ]


def _ffn_prompt(x1, ada, p, wup_b, wdn_b, n_seq, tm, tiles_per_seq, ada_row_block):
    rows = x1.shape[0]
    n_blocks, _, cb2 = wup_b.shape
    cb = cb2 // 2
    ada_spec = lambda k: pl.BlockSpec((SUBLANES, D_MODEL), lambda i, c: (ada_row_block, k))
    in_specs = [pl.BlockSpec((tm, D_MODEL), lambda i, c: (i, 0)),
                ada_spec(3), ada_spec(4), ada_spec(5)] + _ffn_weight_specs(n_blocks, cb)
    out_specs = [
        pl.BlockSpec((tm, D_MODEL), lambda i, c: (i, 0), pipeline_mode=pl.Buffered(1)),
        pl.BlockSpec((1, FFN_CONV - 1, 2, cb), lambda i, c: (i, 0, 0, c)),
    ]
    out_shape = [jax.ShapeDtypeStruct((rows, D_MODEL), F32),
                 jax.ShapeDtypeStruct((rows // tm, FFN_CONV - 1, 2, n_blocks * cb), F32)]
    scratch = [
        pltpu.VMEM((tm, D_MODEL), BF16),
        pltpu.VMEM((SUBLANES + tm, 2 * cb), F32),
        pltpu.VMEM((tm, cb), BF16),
        pltpu.VMEM((n_blocks, SUBLANES, 2 * cb), F32),
    ]
    return pl.pallas_call(
        functools.partial(_ffn_prompt_kernel, tm=tm, tiles_per_seq=tiles_per_seq,
                          n_blocks=n_blocks, cb=cb),
        grid=(rows // tm, n_blocks),
        in_specs=in_specs, out_specs=out_specs, out_shape=out_shape,
        scratch_shapes=scratch, compiler_params=_cparams(2), name="ffn_prompt",
    )(x1, ada, ada, ada, p["g_pre2"], p["g_post2"], wup_b, p["w_ffn_conv"], p["w_ffn_conv"],
      p["b_ffn_conv"], p["b_ffn_conv"], wdn_b)


def _ffn_sample(x1, ada, p, wup_b, wdn_b, sffn2, n_seq, n_t):
    rows = n_seq * n_t
    n_blocks, _, cb2 = wup_b.shape
    cb = cb2 // 2
    nk = FFN_CONV - 1
    ada_spec = lambda k: pl.BlockSpec((n_seq, D_MODEL), lambda i, c: (0, k))
    in_specs = [pl.BlockSpec((rows, D_MODEL), lambda i, c: (0, 0), pipeline_mode=pl.Buffered(1)),
                ada_spec(3), ada_spec(4), ada_spec(5)] + _ffn_weight_specs(n_blocks, cb) + [
        pl.BlockSpec((n_seq * nk, cb), lambda i, c: (0, c)),
        pl.BlockSpec((n_seq * nk, cb), lambda i, c: (0, n_blocks + c)),
    ]
    out_specs = [
        pl.BlockSpec((rows, D_MODEL), lambda i, c: (0, 0)),
        pl.BlockSpec((n_seq, nk, 2, cb), lambda i, c: (0, 0, 0, c)),
    ]
    out_shape = [jax.ShapeDtypeStruct((rows, D_MODEL), F32),
                 jax.ShapeDtypeStruct((n_seq, nk, 2, n_blocks * cb), F32)]
    scratch = [
        pltpu.VMEM((rows, D_MODEL), BF16),
        pltpu.VMEM((rows, 2 * cb), F32),
        pltpu.VMEM((rows, cb), BF16),
    ]
    return pl.pallas_call(
        functools.partial(_ffn_sample_kernel, n_seq=n_seq, n_t=n_t, n_blocks=n_blocks, cb=cb),
        grid=(1, n_blocks),
        in_specs=in_specs, out_specs=out_specs, out_shape=out_shape,
        scratch_shapes=scratch, compiler_params=_cparams(2), name="ffn_sample",
    )(x1, ada, ada, ada, p["g_pre2"], p["g_post2"], wup_b, p["w_ffn_conv"], p["w_ffn_conv"],
      p["b_ffn_conv"], p["b_ffn_conv"], wdn_b, sffn2, sffn2)


TOKEN_TILE = 1024
MERGE_TILE = 2048
OUTPROJ_TILE = 512


def kernel(x_prompt, x_sample, c_prompt, c_sample, state_pool, state_lru_conv, state_lru_h, state_ffn_conv, w_ada, b_ada, g_pre1, g_post1, g_pre2, g_post2, w_in, w_pool_grp, pool_scale, w_lru_conv, b_lru_conv, w_rg, b_rg, w_ig, b_ig, lru_lambda, w_pool_up, w_lru_up, w_out, w_ffn_up, w_ffn_conv, b_ffn_conv, w_ffn_down):
    batch, seq, d = x_prompt.shape
    dec_batch, dec_seq, _ = x_sample.shape
    depth = w_ada.shape[0]
    assert d == D_MODEL and dec_batch % SUBLANES == 0 and seq % TOKEN_TILE == 0
    assert w_in.shape[2] == N_MIX_BLOCKS * CB + 2 * D_MODEL

    pad = (-batch) % SUBLANES
    c_all = jnp.concatenate([c_sample, c_prompt, jnp.zeros((pad, d), c_prompt.dtype)], axis=0)
    prompt_row_block = dec_batch // SUBLANES

    vec_names = ("g_pre1", "g_post1", "g_pre2", "g_post2", "pool_scale", "b_lru_conv", "b_rg",
                 "b_ig", "lru_lambda", "b_ffn_conv")
    weights = dict(w_ada=w_ada, b_ada=b_ada, g_pre1=g_pre1, g_post1=g_post1, g_pre2=g_pre2,
                   g_post2=g_post2, w_in=w_in, w_pool_grp=w_pool_grp, pool_scale=pool_scale,
                   w_lru_conv=w_lru_conv, b_lru_conv=b_lru_conv, w_rg=w_rg, b_rg=b_rg, w_ig=w_ig,
                   b_ig=b_ig, lru_lambda=lru_lambda, w_pool_up=w_pool_up, w_lru_up=w_lru_up,
                   w_out=w_out, w_ffn_up=w_ffn_up, w_ffn_conv=w_ffn_conv, b_ffn_conv=b_ffn_conv,
                   w_ffn_down=w_ffn_down)

    def time_major(a):
        return jnp.swapaxes(a, 0, 1).reshape(-1, a.shape[-1])

    def seq_major(a2, n_rows):
        return jnp.swapaxes(a2.reshape(n_rows, dec_batch, -1), 0, 1)

    xp = x_prompt.reshape(batch * seq, d)
    xs = time_major(x_sample)
    tps = seq // TOKEN_TILE
    outs_p = ([], [], [], [])
    outs_s = ([], [], [], [])
    for l in range(depth):
        p = {k: v[l] for k, v in weights.items()}
        for k in vec_names + ("b_ada",):
            p[k] = p[k].reshape(1, -1)
        ada = _ada(c_all, p["w_ada"], p["b_ada"])

        y, npool, nconv, nh, h, wdn_b = _mix_prompt(xp, ada, p, batch, seq, prompt_row_block)
        m, wup_b = _merge(h, y, p, MERGE_TILE, "merge_prompt", cast_ffn_cb=FFN_CB)
        x1 = _outproj(m, xp, ada, p, OUTPROJ_TILE, seq // OUTPROJ_TILE, prompt_row_block, 0)
        xp, nffn = _ffn_prompt(x1, ada, p, wup_b, wdn_b, batch, TOKEN_TILE, tps, prompt_row_block)
        outs_p[0].append(npool)
        outs_p[1].append(nconv)
        outs_p[2].append(nh.reshape(batch, LRU_WIDTH))
        outs_p[3].append(nffn[tps - 1::tps].reshape(batch, FFN_CONV - 1, -1))

        rows_s = dec_batch * dec_seq
        y, npool, nconv, nh, h = _mix_sample(
            xs, ada, p, time_major(state_pool[l]), time_major(state_lru_conv[l]),
            state_lru_h[l], dec_batch, dec_seq, PAST_LEN)
        m, = _merge(h, y, p, rows_s, "merge_sample")
        x1 = _outproj(m, xs, ada, p, rows_s, 1, 0, dec_seq)
        xs, nffn = _ffn_sample(x1, ada, p, wup_b, wdn_b, time_major(state_ffn_conv[l]),
                               dec_batch, dec_seq)
        outs_s[0].append(seq_major(npool, POOL_BUF))
        outs_s[1].append(seq_major(nconv, LRU_CONV - 1))
        outs_s[2].append(nh)
        outs_s[3].append(nffn.reshape(dec_batch, FFN_CONV - 1, -1))

    return (xp.reshape(batch, seq, d), seq_major(xs, dec_seq),
            jnp.stack(outs_p[0]), jnp.stack(outs_p[1]), jnp.stack(outs_p[2]), jnp.stack(outs_p[3]),
            jnp.stack(outs_s[0]), jnp.stack(outs_s[1]), jnp.stack(outs_s[2]), jnp.stack(outs_s[3]))
```
